```python
import math
import jax, jax.numpy as jnp
from jax import lax
import numpy as np

D_MODEL = 1024
BATCH = 8
SEQ = 4096
DEPTH = 2

HEAD_DIM = 64
ATTN_HEADS = 8
ATTN_WIDTH = ATTN_HEADS * HEAD_DIM
DILATED_PATTERNS = ((128, 1), (512, 4), (2048, 16))
ATTN_BLOCK = 128
ROPE_THETA = 10000.0

SSD_HEADS = 8
SSD_HEAD_DIM = 64
SSD_WIDTH = SSD_HEADS * SSD_HEAD_DIM
SSD_GROUPS = 2
SSD_STATE = 128
SSD_CONV = 4
SSD_CHUNK = 128
SSD_CONV_DIM = SSD_WIDTH + 2 * SSD_GROUPS * SSD_STATE

SC_WIDTH = 512
SC_CONV = 3

MIX_WIDTH = ATTN_WIDTH + SSD_WIDTH + SC_WIDTH
IN_SPLITS = (ATTN_WIDTH, ATTN_WIDTH, ATTN_WIDTH,
             SSD_WIDTH, SSD_WIDTH, SSD_GROUPS * SSD_STATE, SSD_GROUPS * SSD_STATE, SSD_HEADS,
             SC_WIDTH, SC_WIDTH, SC_WIDTH)
IN_WIDTH = sum(IN_SPLITS)

N_EXPERTS = 32
TOP_K = 4
EXPERT_FF = D_MODEL
SWIGLU_ALPHA = 1.702
SWIGLU_LIMIT = 7.0
MOE_BLOCK = 128

ALPHA = (2.0 * DEPTH) ** 0.25
BETA = (8.0 * DEPTH) ** -0.25
LN_EPS = 1e-5
RMS_EPS = 1e-5

kernel_name = "hybrid_dilated_ssd_shortconv_moe_deepnorm"


def layer_norm(x, g, b):
    xf = x.astype(jnp.float32)
    mu = jnp.mean(xf, -1, keepdims=True)
    var = jnp.mean(jnp.square(xf - mu), -1, keepdims=True)
    return ((xf - mu) * lax.rsqrt(var + LN_EPS) * g + b).astype(x.dtype)


def rotary(x, pos):
    half = x.shape[-1] // 2
    inv = ROPE_THETA ** (-jnp.arange(half, dtype=jnp.float32) / half)
    ang = pos.astype(jnp.float32)[:, None] * inv[None, :]
    cos = jnp.cos(ang)[None, :, None, :]
    sin = jnp.sin(ang)[None, :, None, :]
    x1, x2 = x[..., :half], x[..., half:]
    return jnp.concatenate([x1 * cos - x2 * sin, x2 * cos + x1 * sin], -1).astype(x.dtype)


def causal_depthwise_conv(x, w):
    width = w.shape[0]
    return lax.conv_general_dilated(
        x, w.astype(x.dtype)[:, None, :], window_strides=(1,), padding=[(width - 1, 0)],
        dimension_numbers=("NWC", "WIO", "NWC"), feature_group_count=x.shape[-1])


def dilated_window_attention(q, k, v, dilation, span):
    B, S, H, hd = q.shape
    L = S // dilation
    nb = L // ATTN_BLOCK

    def strided(t):
        return (t.reshape(B, L, dilation, H, hd).transpose(0, 2, 1, 3, 4)
                .reshape(B * dilation, nb, ATTN_BLOCK, H, hd))

    def with_prev(t):
        prev = jnp.pad(t[:, :-1], ((0, 0), (1, 0), (0, 0), (0, 0), (0, 0)))
        return jnp.concatenate([prev, t], axis=2)

    qb = strided(q)
    kk = with_prev(strided(k))
    vv = with_prev(strided(v))
    s = jnp.einsum("bnqhd,bnkhd->bnhqk", qb, kk).astype(jnp.float32) * (hd ** -0.5)
    qi = jnp.arange(ATTN_BLOCK)[:, None]
    kj = jnp.arange(2 * ATTN_BLOCK)[None, :]
    dist = qi + ATTN_BLOCK - kj
    band = (dist >= 0) & (dist <= span)
    has_prev = (jnp.arange(nb)[:, None, None] > 0) | (kj >= ATTN_BLOCK)[None]
    valid = band[None] & has_prev
    s = jnp.where(valid[None, :, None], s, -jnp.inf)
    m = jnp.max(s, -1, keepdims=True)
    p = jnp.exp(s - m)
    l = jnp.sum(p, -1, keepdims=True)
    o = jnp.einsum("bnhqk,bnkhd->bnqhd", p, vv.astype(jnp.float32)) / l.transpose(0, 1, 3, 2, 4)
    lse = (m + jnp.log(l))[..., 0].transpose(0, 1, 3, 2)
    o = o.reshape(B, dilation, L, H, hd).transpose(0, 2, 1, 3, 4).reshape(B, S, H, hd)
    lse = lse.reshape(B, dilation, L, H).transpose(0, 2, 1, 3).reshape(B, S, H)
    return o, lse


def dilated_mixture_attention(q, k, v):
    B, S, H, hd = q.shape
    mult = ATTN_BLOCK * max(d for _, d in DILATED_PATTERNS)
    s_pad = -(-S // mult) * mult
    padw = ((0, 0), (0, s_pad - S), (0, 0), (0, 0))
    qp, kp, vp = jnp.pad(q, padw), jnp.pad(k, padw), jnp.pad(v, padw)
    outs, lses = [], []
    for window, dilation in DILATED_PATTERNS:
        o, lse = dilated_window_attention(qp, kp, vp, dilation, window // dilation)
        outs.append(o)
        lses.append(lse)
    w = jax.nn.softmax(jnp.stack(lses, 0), axis=0)
    o = jnp.sum(w[..., None] * jnp.stack(outs, 0), axis=0)
    return o[:, :S].astype(q.dtype)


def ssd_chunked(x, dt, a, bm, cm):
    B, S, H, P = x.shape
    N = bm.shape[-1]
    nc = S // SSD_CHUNK
    X = (x * dt[..., None]).reshape(B, nc, SSD_CHUNK, H, P)
    A = (dt * a).reshape(B, nc, SSD_CHUNK, H).transpose(0, 3, 1, 2)
    Bc = bm.reshape(B, nc, SSD_CHUNK, H, N)
    Cc = cm.reshape(B, nc, SSD_CHUNK, H, N)
    a_cum = jnp.cumsum(A, axis=-1)
    seg = a_cum[..., :, None] - a_cum[..., None, :]
    causal = jnp.tril(jnp.ones((SSD_CHUNK, SSD_CHUNK), dtype=bool))
    Lmat = jnp.exp(jnp.where(causal, seg, -jnp.inf))
    scores = jnp.einsum("bclhn,bcshn->bhcls", Cc, Bc) * Lmat
    y_diag = jnp.einsum("bhcls,bcshp->bclhp", scores, X)
    decay_states = jnp.exp(a_cum[..., -1:] - a_cum)
    states = jnp.einsum("bclhn,bhcl,bclhp->bchpn", Bc, decay_states, X)
    chunk_decay = jnp.exp(a_cum[..., -1])

    def step(h, inp):
        st, dec = inp
        return h * dec[..., None, None] + st, h

    h0 = jnp.zeros((B, H, P, N), jnp.float32)
    _, h_in = lax.scan(step, h0, (states.transpose(1, 0, 2, 3, 4), chunk_decay.transpose(2, 0, 1)))
    h_in = h_in.transpose(1, 0, 2, 3, 4)
    y_off = jnp.einsum("bclhn,bchpn,bhcl->bclhp", Cc, h_in, jnp.exp(a_cum))
    return (y_diag + y_off).reshape(B, S, H, P)


def mamba2_mixer(z, xs, bs, cs, dt_raw, conv_w, conv_b, dt_bias, a_log, d_skip, norm_g):
    B, S, _ = xs.shape
    xbc = causal_depthwise_conv(jnp.concatenate([xs, bs, cs], -1), conv_w) + conv_b.astype(xs.dtype)
    xbc = jax.nn.silu(xbc.astype(jnp.float32))
    gn = SSD_GROUPS * SSD_STATE
    xh = xbc[..., :SSD_WIDTH].reshape(B, S, SSD_HEADS, SSD_HEAD_DIM)
    rep = SSD_HEADS // SSD_GROUPS
    bm = jnp.repeat(xbc[..., SSD_WIDTH:SSD_WIDTH + gn].reshape(B, S, SSD_GROUPS, SSD_STATE), rep, axis=2)
    cm = jnp.repeat(xbc[..., SSD_WIDTH + gn:].reshape(B, S, SSD_GROUPS, SSD_STATE), rep, axis=2)
    dt = jax.nn.softplus(dt_raw.astype(jnp.float32) + dt_bias.astype(jnp.float32))
    a = -jnp.exp(a_log.astype(jnp.float32))
    y = ssd_chunked(xh, dt, a, bm, cm) + d_skip.astype(jnp.float32)[:, None] * xh
    y = y.reshape(B, S, SSD_WIDTH) * jax.nn.silu(z.astype(jnp.float32))
    yg = y.reshape(B, S, SSD_GROUPS, SSD_WIDTH // SSD_GROUPS)
    yg = yg * lax.rsqrt(jnp.mean(jnp.square(yg), -1, keepdims=True) + RMS_EPS)
    return (yg.reshape(B, S, SSD_WIDTH) * norm_g).astype(xs.dtype)


def short_gated_conv(b_gate, c_gate, h, conv_w):
    return b_gate * causal_depthwise_conv(c_gate * h, conv_w)


def mixing_sublayer(x, w_in, ssd_conv_w, ssd_conv_b, ssd_dt_bias, ssd_a_log, ssd_d, ssd_norm_g,
                    sc_conv_w, w_out):
    B, S, _ = x.shape
    proj = x @ w_in.astype(x.dtype)
    cuts = np.cumsum(IN_SPLITS)[:-1].tolist()
    q, k, v, z, xs, bs, cs, dt_raw, sb, sc, sh = jnp.split(proj, cuts, axis=-1)
    pos = jnp.arange(S)
    q = rotary(q.reshape(B, S, ATTN_HEADS, HEAD_DIM), pos)
    k = rotary(k.reshape(B, S, ATTN_HEADS, HEAD_DIM), pos)
    v = v.reshape(B, S, ATTN_HEADS, HEAD_DIM)
    attn = dilated_mixture_attention(q, k, v).reshape(B, S, ATTN_WIDTH)
    ssd = mamba2_mixer(z, xs, bs, cs, dt_raw, ssd_conv_w, ssd_conv_b, ssd_dt_bias, ssd_a_log,
                       ssd_d, ssd_norm_g)
    conv = short_gated_conv(sb, sc, sh, sc_conv_w)
    mixed = jnp.concatenate([attn.astype(x.dtype), ssd, conv.astype(x.dtype)], -1)
    return mixed @ w_out.astype(x.dtype)


def moe_sublayer(x, router_w, router_b, w_gu, b_gu, w_down, b_down):
    Bsz, S, D = x.shape
    N = Bsz * S
    xt = x.reshape(N, D)
    logits = (xt @ router_w.astype(x.dtype) + router_b.astype(x.dtype)).astype(jnp.float32)
    top_val, top_idx = lax.top_k(logits, TOP_K)
    gates = jax.nn.softmax(top_val, axis=-1)
    e_flat = top_idx.reshape(-1).astype(jnp.int32)
    tok_flat = jnp.arange(N * TOP_K, dtype=jnp.int32) // TOP_K
    g_flat = gates.reshape(-1)
    order = jnp.argsort(e_flat)
    e_sorted = e_flat[order]
    counts = jnp.zeros((N_EXPERTS,), jnp.int32).at[e_flat].add(1)
    padded = (counts + MOE_BLOCK - 1) // MOE_BLOCK * MOE_BLOCK
    start = jnp.cumsum(counts) - counts
    pend = jnp.cumsum(padded)
    pstart = pend - padded
    dest = pstart[e_sorted] + (jnp.arange(N * TOP_K, dtype=jnp.int32) - start[e_sorted])
    P = N * TOP_K + N_EXPERTS * MOE_BLOCK
    nblk = P // MOE_BLOCK
    buf_tok = jnp.full((P,), N, jnp.int32).at[dest].set(tok_flat[order])
    buf_gate = jnp.zeros((P,), jnp.float32).at[dest].set(g_flat[order])
    blk_expert = jnp.minimum(
        jnp.searchsorted(pend, jnp.arange(nblk, dtype=jnp.int32) * MOE_BLOCK, side="right"),
        N_EXPERTS - 1).astype(jnp.int32)
    x_pad = jnp.concatenate([xt, jnp.zeros((1, D), xt.dtype)], 0)

    def expert_block(args):
        tok, e = args
        xb = x_pad[tok]
        gu = xb @ w_gu[e].astype(xb.dtype) + b_gu[e].astype(xb.dtype)
        gate = jnp.minimum(gu[:, :EXPERT_FF], SWIGLU_LIMIT)
        up = jnp.clip(gu[:, EXPERT_FF:], -SWIGLU_LIMIT, SWIGLU_LIMIT)
        h = (up + 1.0) * gate * jax.nn.sigmoid(SWIGLU_ALPHA * gate)
        return h @ w_down[e].astype(xb.dtype) + b_down[e].astype(xb.dtype)

    y = lax.map(expert_block, (buf_tok.reshape(nblk, MOE_BLOCK), blk_expert))
    y = y.reshape(P, D) * buf_gate[:, None].astype(x.dtype)
    out = jnp.zeros((N + 1, D), x.dtype).at[buf_tok].add(y)
    return out[:N].reshape(Bsz, S, D)


def setup_inputs(seed: int = 0) -> dict:
    key = jax.random.key(seed)
    ks = jax.random.split(key, 20)
    f32 = jnp.float32
    L = DEPTH

    def nrm(k, shape, scale):
        return jax.random.normal(k, shape, f32) * scale

    x = nrm(ks[0], (BATCH, SEQ, D_MODEL), 1.0)
    w_in = nrm(ks[1], (L, D_MODEL, IN_WIDTH), D_MODEL ** -0.5)
    w_in = w_in.at[:, :, 2 * ATTN_WIDTH:3 * ATTN_WIDTH].multiply(BETA)
    ssd_conv_w = nrm(ks[2], (L, SSD_CONV, SSD_CONV_DIM), SSD_CONV ** -0.5)
    ssd_conv_b = nrm(ks[3], (L, SSD_CONV_DIM), 0.02)
    dt0 = jnp.exp(jax.random.uniform(ks[4], (L, SSD_HEADS), f32, math.log(1e-3), math.log(1e-1)))
    ssd_dt_bias = dt0 + jnp.log(-jnp.expm1(-dt0))
    ssd_a_log = jnp.log(jax.random.uniform(ks[5], (L, SSD_HEADS), f32, 1.0, 16.0))
    ssd_d = 1.0 + nrm(ks[6], (L, SSD_HEADS), 0.1)
    ssd_norm_g = 1.0 + nrm(ks[7], (L, SSD_WIDTH), 0.05)
    sc_conv_w = nrm(ks[8], (L, SC_CONV, SC_WIDTH), SC_CONV ** -0.5)
    w_out = nrm(ks[9], (L, MIX_WIDTH, D_MODEL), MIX_WIDTH ** -0.5 * BETA)
    ln1_g = 1.0 + nrm(ks[10], (L, D_MODEL), 0.05)
    ln1_b = nrm(ks[11], (L, D_MODEL), 0.02)
    router_w = nrm(ks[12], (L, D_MODEL, N_EXPERTS), D_MODEL ** -0.5)
    router_b = nrm(ks[13], (L, N_EXPERTS), 0.01)
    exp_w_gu = nrm(ks[14], (L, N_EXPERTS, D_MODEL, 2 * EXPERT_FF), D_MODEL ** -0.5)
    exp_b_gu = nrm(ks[15], (L, N_EXPERTS, 2 * EXPERT_FF), 0.01)
    exp_w_down = nrm(ks[16], (L, N_EXPERTS, EXPERT_FF, D_MODEL), EXPERT_FF ** -0.5 * BETA)
    exp_b_down = nrm(ks[17], (L, N_EXPERTS, D_MODEL), 0.01)
    ln2_g = 1.0 + nrm(ks[18], (L, D_MODEL), 0.05)
    ln2_b = nrm(ks[19], (L, D_MODEL), 0.02)
    return {"x": x, "w_in": w_in, "ssd_conv_w": ssd_conv_w, "ssd_conv_b": ssd_conv_b,
            "ssd_dt_bias": ssd_dt_bias, "ssd_a_log": ssd_a_log, "ssd_d": ssd_d,
            "ssd_norm_g": ssd_norm_g, "sc_conv_w": sc_conv_w, "w_out": w_out,
            "ln1_g": ln1_g, "ln1_b": ln1_b, "router_w": router_w, "router_b": router_b,
            "exp_w_gu": exp_w_gu, "exp_b_gu": exp_b_gu, "exp_w_down": exp_w_down,
            "exp_b_down": exp_b_down, "ln2_g": ln2_g, "ln2_b": ln2_b}


def reference(x, w_in, ssd_conv_w, ssd_conv_b, ssd_dt_bias, ssd_a_log, ssd_d, ssd_norm_g,
              sc_conv_w, w_out, ln1_g, ln1_b, router_w, router_b, exp_w_gu, exp_b_gu,
              exp_w_down, exp_b_down, ln2_g, ln2_b):
    for layer in range(DEPTH):
        mix = mixing_sublayer(x, w_in[layer], ssd_conv_w[layer], ssd_conv_b[layer],
                              ssd_dt_bias[layer], ssd_a_log[layer], ssd_d[layer],
                              ssd_norm_g[layer], sc_conv_w[layer], w_out[layer])
        x = layer_norm(ALPHA * x + mix, ln1_g[layer], ln1_b[layer])
        ffn = moe_sublayer(x, router_w[layer], router_b[layer], exp_w_gu[layer], exp_b_gu[layer],
                           exp_w_down[layer], exp_b_down[layer])
        x = layer_norm(ALPHA * x + ffn, ln2_g[layer], ln2_b[layer])
    return x
```

```python
import functools

import jax
import jax.numpy as jnp
from jax import lax
from jax.experimental import pallas as pl
from jax.experimental.pallas import tpu as pltpu

F32 = jnp.float32
BF16 = jnp.bfloat16

D_MODEL = 1024
HEAD_DIM = 64
ATTN_HEADS = 8
ATTN_WIDTH = 512
DILATIONS = (1, 4, 16)
ATTN_SPAN = 128
ATTN_BLOCK = 128
ROPE_THETA = 10000.0

SSD_HEADS = 8
SSD_WIDTH = 512
SSD_GROUPS = 2
SSD_STATE = 128
SSD_CONV = 4
SSD_CHUNK = 128
SSD_CONV_DIM = 1024

SC_WIDTH = 512
SC_CONV = 3
MIX_WIDTH = 1536

N_EXPERTS = 32
TOP_K = 4
EXPERT_FF = 1024
SWIGLU_ALPHA = 1.702
SWIGLU_LIMIT = 7.0

DEPTH = 2
ALPHA = (2.0 * DEPTH) ** 0.25
LN_EPS = 1e-5
RMS_EPS = 1e-5

LANES = 128
HALO = 16
NEG = -1e30
VMEM_LIMIT = 56 * 1024 * 1024

PROJ_ROWS = 512
ROUTE_ROWS = 256
MOE_BLOCK = 256


def _cparams(n_axes):
    return pltpu.CompilerParams(dimension_semantics=("arbitrary",) * n_axes,
                                vmem_limit_bytes=VMEM_LIMIT)


def _sigmoid(x):
    return 1.0 / (1.0 + jnp.exp(-x))


def _pair_cols(lo, tile, h):
    return jnp.where(lo, tile[:, h:h + 1], tile[:, h + 1:h + 2])


def _rope_kernel(inv_ref, cos_ref, sin_ref):
    rows = cos_ref.shape[0]
    base = pl.program_id(0) * rows
    pos = (lax.broadcasted_iota(jnp.int32, cos_ref.shape, 0) + base).astype(F32)
    lane = lax.broadcasted_iota(jnp.int32, cos_ref.shape, 1)
    ang = pos * inv_ref[...]
    cos_ref[...] = jnp.cos(ang)
    sin_ref[...] = jnp.where((lane & 32) == 0, -jnp.sin(ang), jnp.sin(ang))


def rope_tables(seq):
    half = HEAD_DIM // 2
    inv = ROPE_THETA ** (-jnp.arange(half, dtype=F32) / half)
    inv = jnp.tile(inv, LANES // half)[None, :]
    rows = 512
    return pl.pallas_call(
        _rope_kernel,
        out_shape=(jax.ShapeDtypeStruct((seq, LANES), F32),) * 2,
        grid=(seq // rows,),
        in_specs=[pl.BlockSpec((1, LANES), lambda i: (0, 0))],
        out_specs=(pl.BlockSpec((rows, LANES), lambda i: (i, 0)),) * 2,
        compiler_params=_cparams(1),
        name="rope_tables",
    )(inv)


def _inproj_kernel(x_ref, wa_ref, ws_ref, wdt_ref, wc_ref, cos_ref, sin_ref,
                   q_ref, k_ref, v_ref, z_ref, xbc_ref, dt_ref, sb_ref, u_ref):
    xb = x_ref[...].astype(BF16)
    cos = cos_ref[...]
    sin = sin_ref[...]
    lane = lax.broadcasted_iota(jnp.int32, cos.shape, 1)
    first_half = (lane & 32) == 0

    def proj(w_ref, lo, hi):
        return jnp.dot(xb, w_ref[:, lo:hi], preferred_element_type=F32)

    def rope(t):
        rot = jnp.where(first_half, pltpu.roll(t, LANES - 32, 1), pltpu.roll(t, 32, 1))
        return t * cos + rot * sin

    w = ATTN_WIDTH
    q = proj(wa_ref, 0, w)
    k = proj(wa_ref, w, 2 * w)
    for j in range(w // LANES):
        sl = slice(j * LANES, (j + 1) * LANES)
        q_ref[:, sl] = (rope(q[:, sl]) * (HEAD_DIM ** -0.5)).astype(BF16)
        k_ref[:, sl] = rope(k[:, sl]).astype(BF16)
    v_ref[...] = proj(wa_ref, 2 * w, 3 * w).astype(BF16)
    z_ref[...] = proj(ws_ref, 0, SSD_WIDTH).astype(BF16)
    xbc_ref[...] = proj(ws_ref, SSD_WIDTH, SSD_WIDTH + SSD_CONV_DIM).astype(BF16)
    dt_ref[...] = jnp.dot(xb, wdt_ref[...], preferred_element_type=F32)
    sb_ref[...] = proj(wc_ref, 0, SC_WIDTH).astype(BF16)
    c = proj(wc_ref, SC_WIDTH, 2 * SC_WIDTH)
    h = proj(wc_ref, 2 * SC_WIDTH, 3 * SC_WIDTH)
    u_ref[...] = (c * h).astype(BF16)


def in_projection(x, w_attn, w_ssd, w_dt, w_sc, cos_t, sin_t, seq):
    n = x.shape[0]
    tm = PROJ_ROWS
    tiles_per_seq = seq // tm
    row = lambda i: (i, 0)
    full = lambda i: (0, 0)
    tab = lambda i: (i % tiles_per_seq, 0)
    widths = (ATTN_WIDTH, ATTN_WIDTH, ATTN_WIDTH, SSD_WIDTH, SSD_CONV_DIM, LANES, SC_WIDTH, SC_WIDTH)
    dtypes = (BF16, BF16, BF16, BF16, BF16, F32, BF16, BF16)
    return pl.pallas_call(
        _inproj_kernel,
        out_shape=tuple(jax.ShapeDtypeStruct((n, wd), dt) for wd, dt in zip(widths, dtypes)),
        grid=(n // tm,),
        in_specs=[pl.BlockSpec((tm, D_MODEL), row),
                  pl.BlockSpec(w_attn.shape, full),
                  pl.BlockSpec(w_ssd.shape, full),
                  pl.BlockSpec(w_dt.shape, full),
                  pl.BlockSpec(w_sc.shape, full),
                  pl.BlockSpec((tm, LANES), tab),
                  pl.BlockSpec((tm, LANES), tab)],
        out_specs=tuple(pl.BlockSpec((tm, wd), row) for wd in widths),
        compiler_params=_cparams(1),
        name="in_projection",
    )(x, w_attn, w_ssd, w_dt, w_sc, cos_t, sin_t)


def _attn_kernel(q_ref, kc_ref, kp_ref, vc_ref, vp_ref, o_ref, lse_ref):
    blk = ATTN_BLOCK
    n = pl.program_id(2)
    lane = lax.broadcasted_iota(jnp.int32, (blk, LANES), 1)
    lo = lane < HEAD_DIM
    qi = lax.broadcasted_iota(jnp.int32, (2 * blk, 2 * blk), 0) & (blk - 1)
    kj = lax.broadcasted_iota(jnp.int32, (2 * blk, 2 * blk), 1)
    dist = qi + blk - kj
    in_band = jnp.where(dist >= 0, jnp.where(dist <= ATTN_SPAN, 1, 0), 0)
    has_key = jnp.where(kj >= blk, 1, jnp.where(n > 0, 1, 0))
    valid = (in_band * has_key) > 0
    lse_tile = jnp.zeros((blk, LANES), F32)
    for hp in range(ATTN_WIDTH // LANES):
        sl = slice(hp * LANES, (hp + 1) * LANES)
        q2 = q_ref[:, sl].astype(F32)
        qq = jnp.concatenate([jnp.where(lo, q2, 0.0), jnp.where(lo, 0.0, q2)], axis=0).astype(BF16)
        k2 = jnp.concatenate([kp_ref[:, sl], kc_ref[:, sl]], axis=0)
        v2 = jnp.concatenate([vp_ref[:, sl], vc_ref[:, sl]], axis=0)
        s = lax.dot_general(qq, k2, (((1,), (1,)), ((), ())), preferred_element_type=F32)
        s = jnp.where(valid, s, NEG)
        m = jnp.max(s, axis=1, keepdims=True)
        p = jnp.exp(s - m)
        l = jnp.sum(p, axis=1, keepdims=True)
        pv = jnp.dot(p.astype(BF16), v2, preferred_element_type=F32)
        o = pv / l
        o_ref[:, sl] = jnp.where(lo, o[:blk], o[blk:]).astype(BF16)
        lse = m + jnp.log(l)
        lse_tile = jnp.where(lane == 2 * hp, lse[:blk],
                             jnp.where(lane == 2 * hp + 1, lse[blk:], lse_tile))
    lse_ref[...] = lse_tile


def dilated_attention(q, k, v, batch, seq, dilation):
    length = seq // dilation
    w = ATTN_WIDTH
    qv, kv, vv = (t.reshape(batch, length, dilation * w) for t in (q, k, v))
    cur = lambda b, r, n: (b, n, r)
    prev = lambda b, r, n: (b, jnp.maximum(n - 1, 0), r)
    blk = ATTN_BLOCK
    o, lse = pl.pallas_call(
        _attn_kernel,
        out_shape=(jax.ShapeDtypeStruct((batch, length, dilation * w), BF16),
                   jax.ShapeDtypeStruct((batch, length, dilation * LANES), F32)),
        grid=(batch, dilation, length // blk),
        in_specs=[pl.BlockSpec((None, blk, w), cur),
                  pl.BlockSpec((None, blk, w), cur),
                  pl.BlockSpec((None, blk, w), prev),
                  pl.BlockSpec((None, blk, w), cur),
                  pl.BlockSpec((None, blk, w), prev)],
        out_specs=(pl.BlockSpec((None, blk, w), cur),
                   pl.BlockSpec((None, blk, LANES), cur)),
        compiler_params=_cparams(3),
        name=f"dilated_attention_d{dilation}",
    )(qv, kv, kv, vv, vv)
    return o.reshape(batch * seq, w), lse.reshape(batch * seq, LANES)


def _ssd_kernel(z_ref, xbc_ref, halo_ref, dt_ref, cw_ref, cb_ref, dtb_ref, alog_ref, dskip_ref,
                g_ref, y_ref, ext_ref, st_ref):
    ch = SSD_CHUNK
    c = pl.program_id(1)

    @pl.when(c == 0)
    def _():
        st_ref[...] = jnp.zeros_like(st_ref)

    ext_ref[0:HALO, :] = jnp.where(c > 0, halo_ref[...].astype(F32), 0.0)
    ext_ref[HALO:HALO + ch, :] = xbc_ref[...].astype(F32)
    conv = cb_ref[...]
    for t in range(SSD_CONV):
        off = HALO - (SSD_CONV - 1) + t
        conv = conv + cw_ref[t:t + 1, :] * ext_ref[off:off + ch, :]
    xbc = conv * _sigmoid(conv)
    xh = xbc[:, :SSD_WIDTH]
    bm = xbc[:, SSD_WIDTH:SSD_WIDTH + SSD_GROUPS * SSD_STATE]
    cm = xbc[:, SSD_WIDTH + SSD_GROUPS * SSD_STATE:]

    lane = lax.broadcasted_iota(jnp.int32, (ch, LANES), 1)
    row = lax.broadcasted_iota(jnp.int32, (ch, LANES), 0)
    lo = lane < HEAD_DIM
    lo_row = lo[0:1, :]
    head_lane = lane < SSD_HEADS

    xdt = dt_ref[...] + dtb_ref[...]
    dtv = jnp.maximum(xdt, 0.0) + jnp.log(1.0 + jnp.exp(-jnp.abs(xdt)))
    a = jnp.where(head_lane[0:1, :], -jnp.exp(alog_ref[...]), 0.0)
    acum = dtv * a
    shift = 1
    while shift < ch:
        acum = acum + jnp.where(row >= shift, pltpu.roll(acum, shift, 0), 0.0)
        shift *= 2
    acum_t = acum.T
    tot = acum[ch - 1:ch, :]
    e_in = jnp.exp(acum)
    e_out = jnp.exp(tot - acum)
    e_tot = jnp.exp(tot)
    causal = row >= lane

    ys = []
    for g in range(SSD_GROUPS):
        bg = bm[:, g * SSD_STATE:(g + 1) * SSD_STATE]
        cg = cm[:, g * SSD_STATE:(g + 1) * SSD_STATE].astype(BF16)
        cb = lax.dot_general(cg, bg.astype(BF16), (((1,), (1,)), ((), ())),
                             preferred_element_type=F32)
        bg_t = bg.T.astype(BF16)
        for pp in range(2):
            pair = 2 * g + pp
            ha = 2 * pair
            sl = slice(pair * LANES, (pair + 1) * LANES)
            xp = xh[:, sl]
            xdt_pair = xp * _pair_cols(lo, dtv, ha)
            xb16 = xdt_pair.astype(BF16)
            diag = []
            for h in (ha, ha + 1):
                seg = acum[:, h:h + 1] - acum_t[h:h + 1, :]
                decay = jnp.exp(jnp.where(causal, seg, NEG))
                diag.append(jnp.dot((cb * decay).astype(BF16), xb16, preferred_element_type=F32))
            y = jnp.where(lo, diag[0], diag[1])
            state = st_ref[pair]
            y = y + jnp.dot(cg, state.astype(BF16), preferred_element_type=F32) * _pair_cols(lo, e_in, ha)
            xout = (xdt_pair * _pair_cols(lo, e_out, ha)).astype(BF16)
            st_ref[pair] = (state * _pair_cols(lo_row, e_tot, ha)
                            + jnp.dot(bg_t, xout, preferred_element_type=F32))
            ys.append(y + dskip_ref[:, sl] * xp)
    y = jnp.concatenate(ys, axis=1)
    zz = z_ref[...].astype(F32)
    y = y * (zz * _sigmoid(zz))
    gw = SSD_WIDTH // SSD_GROUPS
    outs = []
    for g in range(SSD_GROUPS):
        yg = y[:, g * gw:(g + 1) * gw]
        ms = jnp.mean(yg * yg, axis=1, keepdims=True)
        outs.append(yg * lax.rsqrt(ms + RMS_EPS))
    y_ref[...] = (jnp.concatenate(outs, axis=1) * g_ref[...]).astype(BF16)


def ssd_mixer(z, xbc, dt_raw, conv_w, conv_b, dt_bias, a_log, d_skip, norm_g, batch, seq):
    n = batch * seq
    ch = SSD_CHUNK
    nc = seq // ch
    pad = lambda t: jnp.pad(t.astype(F32), (0, LANES - t.shape[0]))[None, :]
    cur = lambda b, c: (b * nc + c, 0)
    halo = lambda b, c: (jnp.maximum((b * nc + c) * (ch // HALO) - 1, 0), 0)
    full = lambda b, c: (0, 0)
    return pl.pallas_call(
        _ssd_kernel,
        out_shape=jax.ShapeDtypeStruct((n, SSD_WIDTH), BF16),
        grid=(batch, nc),
        in_specs=[pl.BlockSpec((ch, SSD_WIDTH), cur),
                  pl.BlockSpec((ch, SSD_CONV_DIM), cur),
                  pl.BlockSpec((HALO, SSD_CONV_DIM), halo),
                  pl.BlockSpec((ch, LANES), cur),
                  pl.BlockSpec((SSD_CONV, SSD_CONV_DIM), full),
                  pl.BlockSpec((1, SSD_CONV_DIM), full),
                  pl.BlockSpec((1, LANES), full),
                  pl.BlockSpec((1, LANES), full),
                  pl.BlockSpec((1, SSD_WIDTH), full),
                  pl.BlockSpec((1, SSD_WIDTH), full)],
        out_specs=pl.BlockSpec((ch, SSD_WIDTH), cur),
        scratch_shapes=[pltpu.VMEM((HALO + ch, SSD_CONV_DIM), F32),
                        pltpu.VMEM((SSD_HEADS // 2, SSD_STATE, LANES), F32)],
        compiler_params=_cparams(2),
        name="ssd_mixer",
    )(z, xbc, xbc, dt_raw, conv_w.astype(F32), conv_b.astype(F32)[None, :], pad(dt_bias), pad(a_log),
      jnp.repeat(d_skip.astype(F32), HEAD_DIM)[None, :], norm_g.astype(F32)[None, :])


def _layer_norm(r, g, b):
    mu = jnp.mean(r, axis=1, keepdims=True)
    d = r - mu
    var = jnp.mean(d * d, axis=1, keepdims=True)
    return d * lax.rsqrt(var + LN_EPS) * g + b


def _split3(x):
    hi = x.astype(BF16)
    r1 = x - hi.astype(F32)
    mid = r1.astype(BF16)
    lo = (r1 - mid.astype(F32)).astype(BF16)
    return hi, mid, lo


def _outproj_kernel(tiles_per_seq, o1_ref, o4_ref, o16_ref, l1_ref, l4_ref, l16_ref, ssd_ref, sb_ref,
                    u_ref, uh_ref, x_ref, cw_ref, wout_ref, g_ref, b_ref, rwh_ref, rwm_ref, rwl_ref,
                    rb_ref, x1_ref, logit_ref, ext_ref):
    tm = x_ref.shape[0]
    i = pl.program_id(0)
    lane = lax.broadcasted_iota(jnp.int32, (tm, LANES), 1)
    lo = lane < HEAD_DIM

    lses = (l1_ref[...], l4_ref[...], l16_ref[...])
    outs = (o1_ref, o4_ref, o16_ref)
    top = jnp.maximum(jnp.maximum(lses[0], lses[1]), lses[2])
    es = [jnp.exp(l - top) for l in lses]
    den = es[0] + es[1] + es[2]
    ws = [e / den for e in es]
    attn = []
    for hp in range(ATTN_WIDTH // LANES):
        sl = slice(hp * LANES, (hp + 1) * LANES)
        acc = jnp.zeros((tm, LANES), F32)
        for w, o_ref in zip(ws, outs):
            acc = acc + _pair_cols(lo, w, 2 * hp) * o_ref[:, sl].astype(F32)
        attn.append(acc.astype(BF16))

    seq_start = (i % tiles_per_seq) == 0
    ext_ref[0:HALO, :] = jnp.where(seq_start, 0.0, uh_ref[...].astype(F32))
    ext_ref[HALO:HALO + tm, :] = u_ref[...].astype(F32)
    conv = jnp.zeros((tm, SC_WIDTH), F32)
    for t in range(SC_CONV):
        off = HALO - (SC_CONV - 1) + t
        conv = conv + cw_ref[t:t + 1, :] * ext_ref[off:off + tm, :]
    gated = (sb_ref[...].astype(F32) * conv).astype(BF16)

    mixed = jnp.concatenate(attn + [ssd_ref[...], gated], axis=1)
    mix = jnp.dot(mixed, wout_ref[...], preferred_element_type=F32)
    x1 = _layer_norm(ALPHA * x_ref[...] + mix, g_ref[...], b_ref[...])
    x1_ref[...] = x1

    xh, xm, xl = _split3(x1)
    dot = lambda a, w_ref: jnp.dot(a, w_ref[...], preferred_element_type=F32)
    logits = (dot(xl, rwh_ref) + dot(xm, rwm_ref) + dot(xh, rwl_ref)
              + dot(xm, rwh_ref) + dot(xh, rwm_ref) + dot(xh, rwh_ref))
    logit_ref[...] = logits + rb_ref[...]


def out_projection(o_list, lse_list, ssd, sb, u, x, sc_conv_w, w_out, ln_g, ln_b, router_w, router_b, seq):
    n = x.shape[0]
    tm = PROJ_ROWS
    tiles_per_seq = seq // tm
    row = lambda i: (i, 0)
    full = lambda i: (0, 0)
    halo = lambda i: (jnp.maximum(i * (tm // HALO) - 1, 0), 0)
    rw = jnp.pad(router_w.astype(F32), ((0, 0), (0, LANES - N_EXPERTS)))
    rwh, rwm, rwl = _split3(rw)
    rb = jnp.pad(router_b.astype(F32), (0, LANES - N_EXPERTS))[None, :]
    wide = lambda wd: pl.BlockSpec((tm, wd), row)
    const = lambda a: pl.BlockSpec(a.shape, full)
    wout = w_out.astype(BF16)
    cw = sc_conv_w.astype(F32)
    g = ln_g.astype(F32)[None, :]
    b = ln_b.astype(F32)[None, :]
    return pl.pallas_call(
        functools.partial(_outproj_kernel, tiles_per_seq),
        out_shape=(jax.ShapeDtypeStruct((n, D_MODEL), F32), jax.ShapeDtypeStruct((n, LANES), F32)),
        grid=(n // tm,),
        in_specs=[wide(ATTN_WIDTH)] * 3 + [wide(LANES)] * 3
                 + [wide(SSD_WIDTH), wide(SC_WIDTH), wide(SC_WIDTH), pl.BlockSpec((HALO, SC_WIDTH), halo),
                    wide(D_MODEL), const(cw), const(wout), const(g), const(b),
                    const(rwh), const(rwm), const(rwl), const(rb)],
        out_specs=(wide(D_MODEL), wide(LANES)),
        scratch_shapes=[pltpu.VMEM((HALO + tm, SC_WIDTH), F32)],
        compiler_params=_cparams(1),
        name="out_projection",
    )(*o_list, *lse_list, ssd, sb, u, u, x, cw, wout, g, b, rwh, rwm, rwl, rb)


INFO_IDX, INFO_RANK, INFO_GATE = 0, TOP_K, 2 * TOP_K


def _route_kernel(logit_ref, tri_ref, info_ref, cnt_ref, run_ref):
    tm = logit_ref.shape[0]

    @pl.when(pl.program_id(0) == 0)
    def _():
        run_ref[...] = jnp.zeros_like(run_ref)

    lane = lax.broadcasted_iota(jnp.int32, (tm, LANES), 1)
    lane_f = lane.astype(F32)
    work = jnp.where(lane < N_EXPERTS, logit_ref[...], NEG)
    sel = jnp.zeros((tm, LANES), F32)
    vals, idxs = [], []
    for _ in range(TOP_K):
        m = jnp.max(work, axis=1, keepdims=True)
        idx = jnp.min(jnp.where(work == m, lane_f, float(LANES)), axis=1, keepdims=True)
        hit = lane_f == idx
        work = jnp.where(hit, 2 * NEG, work)
        sel = sel + jnp.where(hit, 1.0, 0.0)
        vals.append(m)
        idxs.append(idx)
    es = [jnp.exp(v - vals[0]) for v in vals]
    den = es[0] + es[1] + es[2] + es[3]
    before = jnp.dot(tri_ref[...], sel.astype(BF16), preferred_element_type=F32) + run_ref[...]
    info = jnp.zeros((tm, LANES), F32)
    for k in range(TOP_K):
        rank = jnp.sum(jnp.where(lane_f == idxs[k], before, 0.0), axis=1, keepdims=True)
        info = jnp.where(lane == INFO_IDX + k, idxs[k], info)
        info = jnp.where(lane == INFO_RANK + k, rank, info)
        info = jnp.where(lane == INFO_GATE + k, es[k] / den, info)
    info_ref[...] = info
    run_ref[...] = run_ref[...] + jnp.sum(sel, axis=0, keepdims=True)
    cnt_ref[...] = jnp.broadcast_to(run_ref[...], cnt_ref.shape)


def route(logits):
    n = logits.shape[0]
    tm = ROUTE_ROWS
    r = lax.broadcasted_iota(jnp.int32, (tm, tm), 0)
    c = lax.broadcasted_iota(jnp.int32, (tm, tm), 1)
    tri = jnp.where(r > c, 1.0, 0.0).astype(BF16)
    return pl.pallas_call(
        _route_kernel,
        out_shape=(jax.ShapeDtypeStruct((n, LANES), F32), jax.ShapeDtypeStruct((8, LANES), F32)),
        grid=(n // tm,),
        in_specs=[pl.BlockSpec((tm, LANES), lambda i: (i, 0)),
                  pl.BlockSpec((tm, tm), lambda i: (0, 0))],
        out_specs=(pl.BlockSpec((tm, LANES), lambda i: (i, 0)),
                   pl.BlockSpec((8, LANES), lambda i: (0, 0))),
        scratch_shapes=[pltpu.VMEM((1, LANES), F32)],
        compiler_params=_cparams(1),
        name="moe_route",
    )(logits, tri)


def _dispatch_kernel(dest_ref, x_ref, buf_in_ref, buf_ref, sem):
    del buf_in_ref
    tm = x_ref.shape[0]

    def row_copy(r, k):
        return pltpu.make_async_copy(x_ref.at[pl.ds(r, 1)], buf_ref.at[pl.ds(dest_ref[r * TOP_K + k], 1)], sem)

    def start(r, carry):
        for k in range(TOP_K):
            row_copy(r, k).start()
        return carry

    def wait(r, carry):
        for k in range(TOP_K):
            row_copy(r, k).wait()
        return carry

    lax.fori_loop(0, tm, start, 0)
    lax.fori_loop(0, tm, wait, 0)


def dispatch(x1, dest_flat, padded_rows):
    n = x1.shape[0]
    tm = ROUTE_ROWS
    zeros = jnp.zeros((padded_rows, D_MODEL), F32)
    return pl.pallas_call(
        _dispatch_kernel,
        out_shape=jax.ShapeDtypeStruct((padded_rows, D_MODEL), F32),
        grid=(n // tm,),
        in_specs=[pl.BlockSpec((tm * TOP_K,), lambda i: (i,), memory_space=pltpu.SMEM),
                  pl.BlockSpec((tm, D_MODEL), lambda i: (i, 0)),
                  pl.BlockSpec(memory_space=pl.ANY)],
        out_specs=pl.BlockSpec(memory_space=pl.ANY),
        scratch_shapes=[pltpu.SemaphoreType.DMA(())],
        input_output_aliases={2: 0},
        compiler_params=_cparams(1),
        name="moe_dispatch",
    )(dest_flat, x1, zeros)


def _expert_kernel(blk_expert_ref, n_used_ref, xs_ref, wgu_ref, bgu_ref, wd_ref, bd_ref, y_ref):
    del blk_expert_ref
    used = pl.program_id(0) < n_used_ref[0]

    @pl.when(jnp.logical_not(used))
    def _():
        y_ref[...] = jnp.zeros_like(y_ref)

    @pl.when(used)
    def _():
        xb = xs_ref[...].astype(BF16)
        gu = jnp.dot(xb, wgu_ref[...], preferred_element_type=F32) + bgu_ref[...]
        gate = jnp.minimum(gu[:, :EXPERT_FF], SWIGLU_LIMIT)
        up = jnp.clip(gu[:, EXPERT_FF:], -SWIGLU_LIMIT, SWIGLU_LIMIT)
        h = (up + 1.0) * gate * _sigmoid(SWIGLU_ALPHA * gate)
        y_ref[...] = jnp.dot(h.astype(BF16), wd_ref[...], preferred_element_type=F32) + bd_ref[...]


def expert_mlp(xs, blk_expert, n_used, w_gu, b_gu, w_down, b_down):
    padded_rows = xs.shape[0]
    blk = MOE_BLOCK
    rows = lambda i, be, nu: (jnp.minimum(i, nu[0] - 1), 0)
    per_expert = lambda i, be, nu: (be[i], 0, 0)
    return pl.pallas_call(
        _expert_kernel,
        out_shape=jax.ShapeDtypeStruct((padded_rows, D_MODEL), F32),
        grid_spec=pltpu.PrefetchScalarGridSpec(
            num_scalar_prefetch=2,
            grid=(padded_rows // blk,),
            in_specs=[pl.BlockSpec((blk, D_MODEL), rows),
                      pl.BlockSpec((None, D_MODEL, 2 * EXPERT_FF), per_expert),
                      pl.BlockSpec((None, 1, 2 * EXPERT_FF), per_expert),
                      pl.BlockSpec((None, EXPERT_FF, D_MODEL), per_expert),
                      pl.BlockSpec((None, 1, D_MODEL), per_expert)],
            out_specs=pl.BlockSpec((blk, D_MODEL), lambda i, be, nu: (i, 0))),
        compiler_params=_cparams(1),
        name="expert_mlp",
    )(blk_expert, n_used, xs, w_gu, b_gu, w_down, b_down)


def _combine_kernel(dest_ref, x_ref, info_ref, g_ref, b_ref, y_ref, out_ref, rows_ref, sem):
    tm = x_ref.shape[0]

    def row_copy(r, k):
        return pltpu.make_async_copy(y_ref.at[pl.ds(dest_ref[r * TOP_K + k], 1)],
                                     rows_ref.at[k, pl.ds(r, 1)], sem)

    def start(r, carry):
        for k in range(TOP_K):
            row_copy(r, k).start()
        return carry

    def wait(r, carry):
        for k in range(TOP_K):
            row_copy(r, k).wait()
        return carry

    lax.fori_loop(0, tm, start, 0)
    lax.fori_loop(0, tm, wait, 0)
    info = info_ref[...]
    ffn = jnp.zeros((tm, D_MODEL), F32)
    for k in range(TOP_K):
        ffn = ffn + info[:, INFO_GATE + k:INFO_GATE + k + 1] * rows_ref[k]
    out_ref[...] = _layer_norm(ALPHA * x_ref[...] + ffn, g_ref[...], b_ref[...])


def combine(x1, info, dest_flat, y, ln_g, ln_b):
    n = x1.shape[0]
    tm = ROUTE_ROWS
    row = lambda i: (i, 0)
    full = lambda i: (0, 0)
    return pl.pallas_call(
        _combine_kernel,
        out_shape=jax.ShapeDtypeStruct((n, D_MODEL), F32),
        grid=(n // tm,),
        in_specs=[pl.BlockSpec((tm * TOP_K,), lambda i: (i,), memory_space=pltpu.SMEM),
                  pl.BlockSpec((tm, D_MODEL), row),
                  pl.BlockSpec((tm, LANES), row),
                  pl.BlockSpec((1, D_MODEL), full),
                  pl.BlockSpec((1, D_MODEL), full),
                  pl.BlockSpec(memory_space=pl.ANY)],
        out_specs=pl.BlockSpec((tm, D_MODEL), row),
        scratch_shapes=[pltpu.VMEM((TOP_K, tm, D_MODEL), F32), pltpu.SemaphoreType.DMA(())],
        compiler_params=_cparams(1),
        name="moe_combine",
    )(dest_flat, x1, info, ln_g.astype(F32)[None, :], ln_b.astype(F32)[None, :], y)


def moe_sublayer(x1, logits, w_gu, b_gu, w_down, b_down, ln_g, ln_b):
    n = x1.shape[0]
    blk = MOE_BLOCK
    padded_rows = n * TOP_K + N_EXPERTS * blk
    info, cnt = route(logits)
    idx = info[:, INFO_IDX:INFO_IDX + TOP_K].astype(jnp.int32)
    rank = info[:, INFO_RANK:INFO_RANK + TOP_K].astype(jnp.int32)
    counts = cnt[0, :N_EXPERTS].astype(jnp.int32)
    padded = (counts + blk - 1) // blk * blk
    seg_end = jnp.cumsum(padded)
    seg_start = seg_end - padded
    dest = (seg_start[idx] + rank).reshape(-1)
    n_blocks = padded_rows // blk
    blk_expert = jnp.minimum(
        jnp.searchsorted(seg_end, jnp.arange(n_blocks, dtype=jnp.int32) * blk, side="right"),
        N_EXPERTS - 1).astype(jnp.int32)
    n_used = (seg_end[-1:] // blk).astype(jnp.int32)
    xs = dispatch(x1, dest, padded_rows)
    y = expert_mlp(xs, blk_expert, n_used, w_gu.astype(BF16), b_gu.astype(F32)[:, None, :],
                   w_down.astype(BF16), b_down.astype(F32)[:, None, :])
    return combine(x1, info, dest, y, ln_g, ln_b)


def kernel(x, w_in, ssd_conv_w, ssd_conv_b, ssd_dt_bias, ssd_a_log, ssd_d, ssd_norm_g, sc_conv_w, w_out,
           ln1_g, ln1_b, router_w, router_b, exp_w_gu, exp_b_gu, exp_w_down, exp_b_down, ln2_g, ln2_b):
    batch, seq, _ = x.shape
    n = batch * seq
    cos_t, sin_t = rope_tables(seq)
    xf = x.reshape(n, D_MODEL).astype(F32)
    a_end = 3 * ATTN_WIDTH
    s_end = a_end + SSD_WIDTH + SSD_CONV_DIM
    d_end = s_end + SSD_HEADS
    for layer in range(w_in.shape[0]):
        w = w_in[layer]
        w_attn = w[:, :a_end].astype(BF16)
        w_ssd = w[:, a_end:s_end].astype(BF16)
        w_dt = jnp.pad(w[:, s_end:d_end], ((0, 0), (0, LANES - SSD_HEADS))).astype(BF16)
        w_sc = w[:, d_end:].astype(BF16)
        q, k, v, z, xbc, dt_raw, sb, u = in_projection(xf, w_attn, w_ssd, w_dt, w_sc, cos_t, sin_t, seq)
        attn = [dilated_attention(q, k, v, batch, seq, d) for d in DILATIONS]
        ssd = ssd_mixer(z, xbc, dt_raw, ssd_conv_w[layer], ssd_conv_b[layer], ssd_dt_bias[layer],
                        ssd_a_log[layer], ssd_d[layer], ssd_norm_g[layer], batch, seq)
        x1, logits = out_projection([o for o, _ in attn], [l for _, l in attn], ssd, sb, u, xf,
                                    sc_conv_w[layer], w_out[layer], ln1_g[layer], ln1_b[layer],
                                    router_w[layer], router_b[layer], seq)
        xf = moe_sublayer(x1, logits, exp_w_gu[layer], exp_b_gu[layer], exp_w_down[layer],
                          exp_b_down[layer], ln2_g[layer], ln2_b[layer])
    return xf.reshape(batch, seq, D_MODEL).astype(x.dtype)
```

```python
import functools

import jax
import jax.numpy as jnp
from jax import lax
from jax.experimental import pallas as pl
from jax.experimental.pallas import tpu as pltpu

F32 = jnp.float32
BF16 = jnp.bfloat16
U32 = jnp.uint32

D_MODEL = 1024
HEAD_DIM = 64
ATTN_HEADS = 8
ATTN_WIDTH = 512
DILATIONS = (1, 4, 16)
ATTN_SPAN = 128
ATTN_BLOCK = 128
ATTN_Q_ROWS = 512
ROPE_THETA = 10000.0

SSD_HEADS = 8
SSD_WIDTH = 512
SSD_GROUPS = 2
SSD_STATE = 128
SSD_CONV = 4
SSD_CHUNK = 128
SSD_CONV_DIM = 1024

SC_WIDTH = 512
SC_CONV = 3
MIX_WIDTH = 1536

N_EXPERTS = 32
TOP_K = 4
EXPERT_FF = 1024
SWIGLU_ALPHA = 1.702
SWIGLU_LIMIT = 7.0

DEPTH = 2
ALPHA = (2.0 * DEPTH) ** 0.25
LN_EPS = 1e-5
RMS_EPS = 1e-5

LANES = 128
SUBLANES = 8
HALO = 16
NEG = -1e30
VMEM_LIMIT = 56 * 1024 * 1024

PROJ_ROWS = 512
ROUTE_ROWS = 256
MOE_BLOCK = 256
CHUNK = 32
N_SLOTS = N_EXPERTS + ROUTE_ROWS * TOP_K // CHUNK
SLOT_ROWS = N_SLOTS * CHUNK
PACKED = D_MODEL // 2
HI_MASK = 0xFFFF0000


def _cparams(n_axes):
    return pltpu.CompilerParams(dimension_semantics=("arbitrary",) * n_axes,
                                vmem_limit_bytes=VMEM_LIMIT)


def _sigmoid(x):
    return 1.0 / (1.0 + jnp.exp(-x))


def _pair_cols(lo, tile, h):
    return jnp.where(lo, tile[:, h:h + 1], tile[:, h + 1:h + 2])


def _pack_rows(v):
    return pltpu.bitcast(v[:, :PACKED], U32) | (pltpu.bitcast(v[:, PACKED:], U32) >> 16)


def _unpack_rows(w):
    a = pltpu.bitcast(w & U32(HI_MASK), F32).astype(BF16)
    b = pltpu.bitcast(w << 16, F32).astype(BF16)
    return jnp.concatenate([a, b], axis=1)


def _rope_kernel(inv_ref, cos_ref, sin_ref):
    rows = cos_ref.shape[0]
    base = pl.program_id(0) * rows
    pos = (lax.broadcasted_iota(jnp.int32, cos_ref.shape, 0) + base).astype(F32)
    lane = lax.broadcasted_iota(jnp.int32, cos_ref.shape, 1)
    ang = pos * inv_ref[...]
    cos_ref[...] = jnp.cos(ang)
    sin_ref[...] = jnp.where((lane & 32) == 0, -jnp.sin(ang), jnp.sin(ang))


def rope_tables(seq):
    half = HEAD_DIM // 2
    inv = ROPE_THETA ** (-jnp.arange(half, dtype=F32) / half)
    inv = jnp.tile(inv, LANES // half)[None, :]
    rows = 512
    return pl.pallas_call(
        _rope_kernel,
        out_shape=(jax.ShapeDtypeStruct((seq, LANES), F32),) * 2,
        grid=(seq // rows,),
        in_specs=[pl.BlockSpec((1, LANES), lambda i: (0, 0))],
        out_specs=(pl.BlockSpec((rows, LANES), lambda i: (i, 0)),) * 2,
        compiler_params=_cparams(1),
        name="rope_tables",
    )(inv)


def _inproj_kernel(x_ref, wa_ref, ws_ref, wdt_ref, wc_ref, cos_ref, sin_ref,
                   q1_ref, k1_ref, v1_ref, q4_ref, k4_ref, v4_ref, q16_ref, k16_ref, v16_ref,
                   z_ref, xbc_ref, dt_ref, sb_ref, u_ref, perm_ref):
    tm = x_ref.shape[0]
    xb = x_ref[...].astype(BF16)
    cos = cos_ref[...]
    sin = sin_ref[...]
    lane = lax.broadcasted_iota(jnp.int32, cos.shape, 1)
    first_half = (lane & 32) == 0
    n_slab = ATTN_WIDTH // LANES

    def proj(w_ref, lo, hi):
        return jnp.dot(xb, w_ref[:, lo:hi], preferred_element_type=F32)

    def rope(t):
        rot = jnp.where(first_half, pltpu.roll(t, LANES - 32, 1), pltpu.roll(t, 32, 1))
        return t * cos + rot * sin

    def emit(slabs, nat_ref, strided_refs):
        for j, t in enumerate(slabs):
            perm_ref[j] = t
            nat_ref[:, j * LANES:(j + 1) * LANES] = t.astype(BF16)
        for d, ref in strided_refs:
            for r in range(d):
                for j in range(n_slab):
                    ref[r, :, j * LANES:(j + 1) * LANES] = (
                        perm_ref[j, pl.ds(r, tm // d, stride=d), :].astype(BF16))

    w = ATTN_WIDTH
    q = proj(wa_ref, 0, w)
    emit([rope(q[:, j * LANES:(j + 1) * LANES]) * (HEAD_DIM ** -0.5) for j in range(n_slab)],
         q1_ref, ((4, q4_ref), (16, q16_ref)))
    k = proj(wa_ref, w, 2 * w)
    emit([rope(k[:, j * LANES:(j + 1) * LANES]) for j in range(n_slab)], k1_ref, ((4, k4_ref), (16, k16_ref)))
    v = proj(wa_ref, 2 * w, 3 * w)
    emit([v[:, j * LANES:(j + 1) * LANES] for j in range(n_slab)], v1_ref, ((4, v4_ref), (16, v16_ref)))
    z_ref[...] = proj(ws_ref, 0, SSD_WIDTH).astype(BF16)
    xbc_ref[...] = proj(ws_ref, SSD_WIDTH, SSD_WIDTH + SSD_CONV_DIM).astype(BF16)
    dt_ref[...] = jnp.dot(xb, wdt_ref[...], preferred_element_type=F32)
    sb_ref[...] = proj(wc_ref, 0, SC_WIDTH).astype(BF16)
    c = proj(wc_ref, SC_WIDTH, 2 * SC_WIDTH)
    h = proj(wc_ref, 2 * SC_WIDTH, 3 * SC_WIDTH)
    u_ref[...] = (c * h).astype(BF16)


def in_projection(x, w_attn, w_ssd, w_dt, w_sc, cos_t, sin_t, batch, seq):
    n = x.shape[0]
    tm = PROJ_ROWS
    tps = seq // tm
    row = lambda i: (i, 0)
    full = lambda i: (0, 0)
    tab = lambda i: (i % tps, 0)
    strided = lambda i: (i // tps, 0, i % tps, 0)
    w = ATTN_WIDTH
    nat = jax.ShapeDtypeStruct((n, w), BF16)
    nat_spec = pl.BlockSpec((tm, w), row)
    shapes, specs = [nat] * 3, [nat_spec] * 3
    for d in DILATIONS[1:]:
        shapes += [jax.ShapeDtypeStruct((batch, d, seq // d, w), BF16)] * 3
        specs += [pl.BlockSpec((None, d, tm // d, w), strided)] * 3
    widths = (SSD_WIDTH, SSD_CONV_DIM, LANES, SC_WIDTH, SC_WIDTH)
    dtypes = (BF16, BF16, F32, BF16, BF16)
    shapes += [jax.ShapeDtypeStruct((n, wd), dt) for wd, dt in zip(widths, dtypes)]
    specs += [pl.BlockSpec((tm, wd), row) for wd in widths]
    return pl.pallas_call(
        _inproj_kernel,
        out_shape=tuple(shapes),
        grid=(n // tm,),
        in_specs=[pl.BlockSpec((tm, D_MODEL), row),
                  pl.BlockSpec(w_attn.shape, full),
                  pl.BlockSpec(w_ssd.shape, full),
                  pl.BlockSpec(w_dt.shape, full),
                  pl.BlockSpec(w_sc.shape, full),
                  pl.BlockSpec((tm, LANES), tab),
                  pl.BlockSpec((tm, LANES), tab)],
        out_specs=tuple(specs),
        scratch_shapes=[pltpu.VMEM((w // LANES, tm, LANES), F32)],
        compiler_params=_cparams(1),
        name="in_projection",
    )(x, w_attn, w_ssd, w_dt, w_sc, cos_t, sin_t)


def _attn_kernel(q_ref, kc_ref, kp_ref, vc_ref, vp_ref, o_ref, lse_ref):
    blk = ATTN_BLOCK
    n = pl.program_id(2)
    lane = lax.broadcasted_iota(jnp.int32, (blk, LANES), 1)
    lo = lane < HEAD_DIM
    qi = lax.broadcasted_iota(jnp.int32, (2 * blk, 2 * blk), 0) & (blk - 1)
    kj = lax.broadcasted_iota(jnp.int32, (2 * blk, 2 * blk), 1)
    dist = qi + blk - kj
    in_band = jnp.where(dist >= 0, jnp.where(dist <= ATTN_SPAN, 1, 0), 0)
    band = in_band > 0
    band_first = (in_band * jnp.where(kj >= blk, 1, jnp.where(n > 0, 1, 0))) > 0
    for j in range(q_ref.shape[0] // blk):
        rows = slice(j * blk, (j + 1) * blk)
        prev = slice((j - 1) * blk, j * blk)
        valid = band_first if j == 0 else band
        lse_tile = jnp.zeros((blk, LANES), F32)
        for hp in range(ATTN_WIDTH // LANES):
            sl = slice(hp * LANES, (hp + 1) * LANES)
            q2 = q_ref[rows, sl].astype(F32)
            qq = jnp.concatenate([jnp.where(lo, q2, 0.0), jnp.where(lo, 0.0, q2)], axis=0).astype(BF16)
            k_prev = kp_ref[:, sl] if j == 0 else kc_ref[prev, sl]
            v_prev = vp_ref[:, sl] if j == 0 else vc_ref[prev, sl]
            k2 = jnp.concatenate([k_prev, kc_ref[rows, sl]], axis=0)
            v2 = jnp.concatenate([v_prev, vc_ref[rows, sl]], axis=0)
            s = lax.dot_general(qq, k2, (((1,), (1,)), ((), ())), preferred_element_type=F32)
            s = jnp.where(valid, s, NEG)
            m = jnp.max(s, axis=1, keepdims=True)
            p = jnp.exp(s - m)
            l = jnp.sum(p, axis=1, keepdims=True)
            pv = jnp.dot(p.astype(BF16), v2, preferred_element_type=F32)
            o = pv / l
            o_ref[rows, sl] = jnp.where(lo, o[:blk], o[blk:]).astype(BF16)
            lse = m + jnp.log(l)
            lse_tile = jnp.where(lane == 2 * hp, lse[:blk],
                                 jnp.where(lane == 2 * hp + 1, lse[blk:], lse_tile))
        lse_ref[rows, :] = lse_tile


def dilated_attention(q, k, v, batch, seq, dilation):
    length = seq // dilation
    w = ATTN_WIDTH
    blk = ATTN_BLOCK
    tq = min(ATTN_Q_ROWS, length)
    cur = lambda b, r, n: (b, r, n, 0)
    prev = lambda b, r, n: (b, r, jnp.maximum(n * (tq // blk) - 1, 0), 0)
    big = pl.BlockSpec((None, None, tq, w), cur)
    small = pl.BlockSpec((None, None, blk, w), prev)
    return pl.pallas_call(
        _attn_kernel,
        out_shape=(jax.ShapeDtypeStruct((batch, dilation, length, w), BF16),
                   jax.ShapeDtypeStruct((batch, dilation, length, LANES), F32)),
        grid=(batch, dilation, length // tq),
        in_specs=[big, big, small, big, small],
        out_specs=(big, pl.BlockSpec((None, None, tq, LANES), cur)),
        compiler_params=_cparams(3),
        name=f"dilated_attention_d{dilation}",
    )(q, k, k, v, v)


def _ssd_kernel(z_ref, xbc_ref, halo_ref, dt_ref, cw_ref, cb_ref, dtb_ref, alog_ref, dskip_ref,
                g_ref, y_ref, ext_ref, st_ref):
    ch = SSD_CHUNK
    c = pl.program_id(1)

    @pl.when(c == 0)
    def _():
        st_ref[...] = jnp.zeros_like(st_ref)

    ext_ref[0:HALO, :] = jnp.where(c > 0, halo_ref[...].astype(F32), 0.0)
    ext_ref[HALO:HALO + ch, :] = xbc_ref[...].astype(F32)
    conv = cb_ref[...]
    for t in range(SSD_CONV):
        off = HALO - (SSD_CONV - 1) + t
        conv = conv + cw_ref[t:t + 1, :] * ext_ref[off:off + ch, :]
    xbc = conv * _sigmoid(conv)
    xh = xbc[:, :SSD_WIDTH]
    bm = xbc[:, SSD_WIDTH:SSD_WIDTH + SSD_GROUPS * SSD_STATE]
    cm = xbc[:, SSD_WIDTH + SSD_GROUPS * SSD_STATE:]

    lane = lax.broadcasted_iota(jnp.int32, (ch, LANES), 1)
    row = lax.broadcasted_iota(jnp.int32, (ch, LANES), 0)
    lo = lane < HEAD_DIM
    lo_row = lo[0:1, :]
    head_lane = lane < SSD_HEADS

    xdt = dt_ref[...] + dtb_ref[...]
    dtv = jnp.maximum(xdt, 0.0) + jnp.log(1.0 + jnp.exp(-jnp.abs(xdt)))
    a = jnp.where(head_lane[0:1, :], -jnp.exp(alog_ref[...]), 0.0)
    acum = dtv * a
    shift = 1
    while shift < ch:
        acum = acum + jnp.where(row >= shift, pltpu.roll(acum, shift, 0), 0.0)
        shift *= 2
    acum_t = acum.T
    tot = acum[ch - 1:ch, :]
    e_in = jnp.exp(acum)
    e_out = jnp.exp(tot - acum)
    e_tot = jnp.exp(tot)
    causal = row >= lane

    ys = []
    for g in range(SSD_GROUPS):
        bg = bm[:, g * SSD_STATE:(g + 1) * SSD_STATE]
        cg = cm[:, g * SSD_STATE:(g + 1) * SSD_STATE].astype(BF16)
        cb = lax.dot_general(cg, bg.astype(BF16), (((1,), (1,)), ((), ())),
                             preferred_element_type=F32)
        bg_t = bg.T.astype(BF16)
        for pp in range(2):
            pair = 2 * g + pp
            ha = 2 * pair
            sl = slice(pair * LANES, (pair + 1) * LANES)
            xp = xh[:, sl]
            xdt_pair = xp * _pair_cols(lo, dtv, ha)
            xb16 = xdt_pair.astype(BF16)
            diag = []
            for h in (ha, ha + 1):
                seg = acum[:, h:h + 1] - acum_t[h:h + 1, :]
                decay = jnp.exp(jnp.where(causal, seg, NEG))
                diag.append(jnp.dot((cb * decay).astype(BF16), xb16, preferred_element_type=F32))
            y = jnp.where(lo, diag[0], diag[1])
            state = st_ref[pair]
            y = y + jnp.dot(cg, state.astype(BF16), preferred_element_type=F32) * _pair_cols(lo, e_in, ha)
            xout = (xdt_pair * _pair_cols(lo, e_out, ha)).astype(BF16)
            st_ref[pair] = (state * _pair_cols(lo_row, e_tot, ha)
                            + jnp.dot(bg_t, xout, preferred_element_type=F32))
            ys.append(y + dskip_ref[:, sl] * xp)
    y = jnp.concatenate(ys, axis=1)
    zz = z_ref[...].astype(F32)
    y = y * (zz * _sigmoid(zz))
    gw = SSD_WIDTH // SSD_GROUPS
    outs = []
    for g in range(SSD_GROUPS):
        yg = y[:, g * gw:(g + 1) * gw]
        ms = jnp.mean(yg * yg, axis=1, keepdims=True)
        outs.append(yg * lax.rsqrt(ms + RMS_EPS))
    y_ref[...] = (jnp.concatenate(outs, axis=1) * g_ref[...]).astype(BF16)


def ssd_mixer(z, xbc, dt_raw, conv_w, conv_b, dt_bias, a_log, d_skip, norm_g, batch, seq):
    n = batch * seq
    ch = SSD_CHUNK
    nc = seq // ch
    pad = lambda t: jnp.pad(t.astype(F32), (0, LANES - t.shape[0]))[None, :]
    cur = lambda b, c: (b * nc + c, 0)
    halo = lambda b, c: (jnp.maximum((b * nc + c) * (ch // HALO) - 1, 0), 0)
    full = lambda b, c: (0, 0)
    return pl.pallas_call(
        _ssd_kernel,
        out_shape=jax.ShapeDtypeStruct((n, SSD_WIDTH), BF16),
        grid=(batch, nc),
        in_specs=[pl.BlockSpec((ch, SSD_WIDTH), cur),
                  pl.BlockSpec((ch, SSD_CONV_DIM), cur),
                  pl.BlockSpec((HALO, SSD_CONV_DIM), halo),
                  pl.BlockSpec((ch, LANES), cur),
                  pl.BlockSpec((SSD_CONV, SSD_CONV_DIM), full),
                  pl.BlockSpec((1, SSD_CONV_DIM), full),
                  pl.BlockSpec((1, LANES), full),
                  pl.BlockSpec((1, LANES), full),
                  pl.BlockSpec((1, SSD_WIDTH), full),
                  pl.BlockSpec((1, SSD_WIDTH), full)],
        out_specs=pl.BlockSpec((ch, SSD_WIDTH), cur),
        scratch_shapes=[pltpu.VMEM((HALO + ch, SSD_CONV_DIM), F32),
                        pltpu.VMEM((SSD_HEADS // 2, SSD_STATE, LANES), F32)],
        compiler_params=_cparams(2),
        name="ssd_mixer",
    )(z, xbc, xbc, dt_raw, conv_w.astype(F32), conv_b.astype(F32)[None, :], pad(dt_bias), pad(a_log),
      jnp.repeat(d_skip.astype(F32), HEAD_DIM)[None, :], norm_g.astype(F32)[None, :])


def _layer_norm(r, g, b):
    mu = jnp.mean(r, axis=1, keepdims=True)
    d = r - mu
    var = jnp.mean(d * d, axis=1, keepdims=True)
    return d * lax.rsqrt(var + LN_EPS) * g + b


def _split3(x):
    hi = x.astype(BF16)
    r1 = x - hi.astype(F32)
    mid = r1.astype(BF16)
    lo = (r1 - mid.astype(F32)).astype(BF16)
    return hi, mid, lo


def _outproj_kernel(tiles_per_seq, o1_ref, o4_ref, o16_ref, l1_ref, l4_ref, l16_ref, ssd_ref, sb_ref,
                    u_ref, uh_ref, x_ref, cw_ref, wout_ref, g_ref, b_ref, rwh_ref, rwm_ref, rwl_ref,
                    rb_ref, x1_ref, logit_ref, ext_ref, operm_ref, lperm_ref):
    tm = x_ref.shape[0]
    i = pl.program_id(0)
    lane = lax.broadcasted_iota(jnp.int32, (tm, LANES), 1)
    lo = lane < HEAD_DIM
    n_slab = ATTN_WIDTH // LANES

    for idx, (d, o_ref, l_ref) in enumerate(((4, o4_ref, l4_ref), (16, o16_ref, l16_ref))):
        for r in range(d):
            rows = pl.ds(r, tm // d, stride=d)
            lperm_ref[idx, rows, :] = l_ref[r]
            for j in range(n_slab):
                operm_ref[idx * n_slab + j, rows, :] = o_ref[r, :, j * LANES:(j + 1) * LANES].astype(F32)

    lses = (l1_ref[...], lperm_ref[0], lperm_ref[1])
    top = jnp.maximum(jnp.maximum(lses[0], lses[1]), lses[2])
    es = [jnp.exp(l - top) for l in lses]
    den = es[0] + es[1] + es[2]
    ws = [e / den for e in es]
    attn = []
    for hp in range(n_slab):
        sl = slice(hp * LANES, (hp + 1) * LANES)
        acc = _pair_cols(lo, ws[0], 2 * hp) * o1_ref[:, sl].astype(F32)
        acc = acc + _pair_cols(lo, ws[1], 2 * hp) * operm_ref[hp]
        acc = acc + _pair_cols(lo, ws[2], 2 * hp) * operm_ref[n_slab + hp]
        attn.append(acc.astype(BF16))

    seq_start = (i % tiles_per_seq) == 0
    ext_ref[0:HALO, :] = jnp.where(seq_start, 0.0, uh_ref[...].astype(F32))
    ext_ref[HALO:HALO + tm, :] = u_ref[...].astype(F32)
    conv = jnp.zeros((tm, SC_WIDTH), F32)
    for t in range(SC_CONV):
        off = HALO - (SC_CONV - 1) + t
        conv = conv + cw_ref[t:t + 1, :] * ext_ref[off:off + tm, :]
    gated = (sb_ref[...].astype(F32) * conv).astype(BF16)

    mixed = jnp.concatenate(attn + [ssd_ref[...], gated], axis=1)
    mix = jnp.dot(mixed, wout_ref[...], preferred_element_type=F32)
    x1 = _layer_norm(ALPHA * x_ref[...] + mix, g_ref[...], b_ref[...])
    x1_ref[...] = x1

    xh, xm, xl = _split3(x1)
    dot = lambda a, w_ref: jnp.dot(a, w_ref[...], preferred_element_type=F32)
    logits = (dot(xl, rwh_ref) + dot(xm, rwm_ref) + dot(xh, rwl_ref)
              + dot(xm, rwh_ref) + dot(xh, rwm_ref) + dot(xh, rwh_ref))
    logit_ref[...] = logits + rb_ref[...]


def out_projection(o_list, lse_list, ssd, sb, u, x, sc_conv_w, w_out, ln_g, ln_b, router_w, router_b, seq):
    n = x.shape[0]
    tm = PROJ_ROWS
    tps = seq // tm
    row = lambda i: (i, 0)
    full = lambda i: (0, 0)
    strided = lambda i: (i // tps, 0, i % tps, 0)
    halo = lambda i: (jnp.maximum(i * (tm // HALO) - 1, 0), 0)
    rw = jnp.pad(router_w.astype(F32), ((0, 0), (0, LANES - N_EXPERTS)))
    rwh, rwm, rwl = _split3(rw)
    rb = jnp.pad(router_b.astype(F32), (0, LANES - N_EXPERTS))[None, :]
    wide = lambda wd: pl.BlockSpec((tm, wd), row)
    perm = lambda d, wd: pl.BlockSpec((None, d, tm // d, wd), strided)
    const = lambda a: pl.BlockSpec(a.shape, full)
    wout = w_out.astype(BF16)
    cw = sc_conv_w.astype(F32)
    g = ln_g.astype(F32)[None, :]
    b = ln_b.astype(F32)[None, :]
    n_slab = ATTN_WIDTH // LANES
    return pl.pallas_call(
        functools.partial(_outproj_kernel, tps),
        out_shape=(jax.ShapeDtypeStruct((n, D_MODEL), F32), jax.ShapeDtypeStruct((n, LANES), F32)),
        grid=(n // tm,),
        in_specs=[wide(ATTN_WIDTH), perm(4, ATTN_WIDTH), perm(16, ATTN_WIDTH),
                  wide(LANES), perm(4, LANES), perm(16, LANES),
                  wide(SSD_WIDTH), wide(SC_WIDTH), wide(SC_WIDTH), pl.BlockSpec((HALO, SC_WIDTH), halo),
                  wide(D_MODEL), const(cw), const(wout), const(g), const(b),
                  const(rwh), const(rwm), const(rwl), const(rb)],
        out_specs=(wide(D_MODEL), wide(LANES)),
        scratch_shapes=[pltpu.VMEM((HALO + tm, SC_WIDTH), F32),
                        pltpu.VMEM((2 * n_slab, tm, LANES), F32),
                        pltpu.VMEM((2, tm, LANES), F32)],
        compiler_params=_cparams(1),
        name="out_projection",
    )(*o_list, *lse_list, ssd, sb, u, u, x, cw, wout, g, b, rwh, rwm, rwl, rb)


INFO_IDX, INFO_SLOT, INFO_GATE = 0, TOP_K, 2 * TOP_K
TAB_RUN, TAB_BASE, TAB_CHUNKS = 0, 1, 2


def _route_kernel(logit_ref, tri_ref, upper_ref, info_ref, tab_ref, cnt_ref, run_ref):
    tm = logit_ref.shape[0]

    @pl.when(pl.program_id(0) == 0)
    def _():
        run_ref[...] = jnp.zeros_like(run_ref)

    lane = lax.broadcasted_iota(jnp.int32, (tm, LANES), 1)
    lane_f = lane.astype(F32)
    work = jnp.where(lane < N_EXPERTS, logit_ref[...], NEG)
    sel = jnp.zeros((tm, LANES), F32)
    vals, idxs = [], []
    for _ in range(TOP_K):
        m = jnp.max(work, axis=1, keepdims=True)
        idx = jnp.min(jnp.where(work == m, lane_f, float(LANES)), axis=1, keepdims=True)
        hit = lane_f == idx
        work = jnp.where(hit, 2 * NEG, work)
        sel = sel + jnp.where(hit, 1.0, 0.0)
        vals.append(m)
        idxs.append(idx)
    es = [jnp.exp(v - vals[0]) for v in vals]
    den = es[0] + es[1] + es[2] + es[3]
    before = jnp.dot(tri_ref[...], sel.astype(BF16), preferred_element_type=F32)
    cnt = jnp.sum(sel, axis=0, keepdims=True)
    chunks = jnp.floor((cnt + (CHUNK - 1)) * (1.0 / CHUNK))
    base = jnp.dot(jnp.broadcast_to(chunks, (SUBLANES, LANES)).astype(BF16), upper_ref[...],
                   preferred_element_type=F32)[0:1, :]
    info = jnp.zeros((tm, LANES), F32)
    for k in range(TOP_K):
        mine = lane_f == idxs[k]
        rank = jnp.sum(jnp.where(mine, before, 0.0), axis=1, keepdims=True)
        first = jnp.sum(jnp.where(mine, base, 0.0), axis=1, keepdims=True)
        slot_row = first * CHUNK + rank
        info = jnp.where(lane == INFO_IDX + k, idxs[k], info)
        info = jnp.where(lane == INFO_SLOT + k, slot_row, info)
        info = jnp.where(lane == INFO_GATE + k, es[k] / den, info)
    info_ref[...] = info
    sub = lax.broadcasted_iota(jnp.int32, (SUBLANES, LANES), 0)
    tab_ref[...] = jnp.where(sub == TAB_RUN, run_ref[...],
                             jnp.where(sub == TAB_BASE, base, jnp.where(sub == TAB_CHUNKS, chunks, 0.0)))
    run_ref[...] = run_ref[...] + jnp.floor((cnt + (SUBLANES - 1)) * (1.0 / SUBLANES)) * SUBLANES
    cnt_ref[...] = jnp.broadcast_to(run_ref[...], cnt_ref.shape)


def route(logits):
    n = logits.shape[0]
    tm = ROUTE_ROWS
    r = lax.broadcasted_iota(jnp.int32, (tm, tm), 0)
    c = lax.broadcasted_iota(jnp.int32, (tm, tm), 1)
    tri = jnp.where(r > c, 1.0, 0.0).astype(BF16)
    upper = jnp.where(r[:LANES, :LANES] < c[:LANES, :LANES], 1.0, 0.0).astype(BF16)
    return pl.pallas_call(
        _route_kernel,
        out_shape=(jax.ShapeDtypeStruct((n, LANES), F32),
                   jax.ShapeDtypeStruct((n // tm * SUBLANES, LANES), F32),
                   jax.ShapeDtypeStruct((SUBLANES, LANES), F32)),
        grid=(n // tm,),
        in_specs=[pl.BlockSpec((tm, LANES), lambda i: (i, 0)),
                  pl.BlockSpec((tm, tm), lambda i: (0, 0)),
                  pl.BlockSpec((LANES, LANES), lambda i: (0, 0))],
        out_specs=(pl.BlockSpec((tm, LANES), lambda i: (i, 0)),
                   pl.BlockSpec((SUBLANES, LANES), lambda i: (i, 0)),
                   pl.BlockSpec((SUBLANES, LANES), lambda i: (0, 0))),
        scratch_shapes=[pltpu.VMEM((1, LANES), F32)],
        compiler_params=_cparams(1),
        name="moe_route",
    )(logits, tri, upper)


def _chunk_copies(n_chunks, make_copy):
    def start(s, carry):
        make_copy(s).start()
        return carry

    def wait(s, carry):
        make_copy(s).wait()
        return carry

    lax.fori_loop(0, n_chunks, start, 0)
    lax.fori_loop(0, n_chunks, wait, 0)


def _dispatch_kernel(dst_ref, nslot_ref, zdst_ref, x_ref, info_ref, buf_ref,
                     slots_ref, zero_ref, sem, zsem):
    i = pl.program_id(0)
    tm = x_ref.shape[0]

    @pl.when(i == 0)
    def _():
        zero_ref[...] = jnp.zeros_like(zero_ref)

        def zero_copy(e):
            row = pl.multiple_of(jnp.maximum(zdst_ref[e], 0), MOE_BLOCK)
            return pltpu.make_async_copy(zero_ref, buf_ref.at[pl.ds(row, MOE_BLOCK)], zsem)

        def start(e, carry):
            @pl.when(zdst_ref[e] >= 0)
            def _():
                zero_copy(e).start()
            return carry

        def wait(e, carry):
            @pl.when(zdst_ref[e] >= 0)
            def _():
                zero_copy(e).wait()
            return carry

        lax.fori_loop(0, zdst_ref.shape[0], start, 0)
        lax.fori_loop(0, zdst_ref.shape[0], wait, 0)

    xb = x_ref[...].astype(BF16)
    info_t = info_ref[...].T
    for c in range(SLOT_ROWS // tm):
        pos = (lax.broadcasted_iota(jnp.int32, (tm, tm), 0) + c * tm).astype(F32)
        onehot = jnp.zeros((tm, tm), F32)
        for k in range(TOP_K):
            onehot = onehot + jnp.where(pos == info_t[INFO_SLOT + k:INFO_SLOT + k + 1, :], 1.0, 0.0)
        rows = jnp.dot(onehot.astype(BF16), xb, preferred_element_type=F32)
        slots_ref[c * tm:(c + 1) * tm, :] = _pack_rows(rows)

    _chunk_copies(nslot_ref[i], lambda s: pltpu.make_async_copy(
        slots_ref.at[pl.ds(pl.multiple_of(s * CHUNK, CHUNK), CHUNK)],
        buf_ref.at[pl.ds(pl.multiple_of(dst_ref[i * N_SLOTS + s], SUBLANES), CHUNK)], sem))


def dispatch(x1, info, dst, nslot, zdst, padded_rows):
    n = x1.shape[0]
    tm = ROUTE_ROWS
    row = lambda i, *_: (i, 0)
    return pl.pallas_call(
        _dispatch_kernel,
        out_shape=jax.ShapeDtypeStruct((padded_rows, PACKED), U32),
        grid_spec=pltpu.PrefetchScalarGridSpec(
            num_scalar_prefetch=3,
            grid=(n // tm,),
            in_specs=[pl.BlockSpec((tm, D_MODEL), row), pl.BlockSpec((tm, LANES), row)],
            out_specs=pl.BlockSpec(memory_space=pl.ANY),
            scratch_shapes=[pltpu.VMEM((SLOT_ROWS, PACKED), U32), pltpu.VMEM((MOE_BLOCK, PACKED), U32),
                            pltpu.SemaphoreType.DMA(()), pltpu.SemaphoreType.DMA(())]),
        compiler_params=_cparams(1),
        name="moe_dispatch",
    )(dst, nslot, zdst, x1, info)


def _expert_kernel(blk_expert_ref, n_used_ref, xs_ref, wgu_ref, bgu_ref, wd_ref, bd_ref, y_ref,
                   wgu_bf, wd_bf):
    i = pl.program_id(0)
    used = i < n_used_ref[0]

    @pl.when(jnp.logical_not(used))
    def _():
        y_ref[...] = jnp.zeros_like(y_ref)

    new_expert = jnp.logical_or(i == 0, blk_expert_ref[i] != blk_expert_ref[jnp.maximum(i - 1, 0)])

    @pl.when(jnp.logical_and(used, new_expert))
    def _():
        rows = 128

        def cast(c, carry):
            r = pl.multiple_of(c * rows, rows)
            wgu_bf[pl.ds(r, rows), :] = wgu_ref[pl.ds(r, rows), :].astype(BF16)
            wd_bf[pl.ds(r, rows), :] = wd_ref[pl.ds(r, rows), :].astype(BF16)
            return carry

        lax.fori_loop(0, D_MODEL // rows, cast, 0)

    @pl.when(used)
    def _():
        xb = _unpack_rows(xs_ref[...])
        gu = jnp.dot(xb, wgu_bf[...], preferred_element_type=F32) + bgu_ref[...]
        gate = jnp.minimum(gu[:, :EXPERT_FF], SWIGLU_LIMIT)
        up = jnp.clip(gu[:, EXPERT_FF:], -SWIGLU_LIMIT, SWIGLU_LIMIT)
        h = (up + 1.0) * gate * _sigmoid(SWIGLU_ALPHA * gate)
        y = jnp.dot(h.astype(BF16), wd_bf[...], preferred_element_type=F32) + bd_ref[...]
        y_ref[...] = _pack_rows(y.astype(BF16).astype(F32))


def expert_mlp(xs, blk_expert, n_used, w_gu, b_gu, w_down, b_down):
    padded_rows = xs.shape[0]
    blk = MOE_BLOCK
    rows_in = lambda i, be, nu: (jnp.minimum(i, nu[0] - 1), 0)
    per_expert = lambda i, be, nu: (be[i], 0, 0)
    return pl.pallas_call(
        _expert_kernel,
        out_shape=jax.ShapeDtypeStruct((padded_rows, PACKED), U32),
        grid_spec=pltpu.PrefetchScalarGridSpec(
            num_scalar_prefetch=2,
            grid=(padded_rows // blk,),
            in_specs=[pl.BlockSpec((blk, PACKED), rows_in),
                      pl.BlockSpec((None, D_MODEL, 2 * EXPERT_FF), per_expert),
                      pl.BlockSpec((None, 1, 2 * EXPERT_FF), per_expert),
                      pl.BlockSpec((None, EXPERT_FF, D_MODEL), per_expert),
                      pl.BlockSpec((None, 1, D_MODEL), per_expert)],
            out_specs=pl.BlockSpec((blk, PACKED), lambda i, be, nu: (i, 0)),
            scratch_shapes=[pltpu.VMEM((D_MODEL, 2 * EXPERT_FF), BF16), pltpu.VMEM((EXPERT_FF, D_MODEL), BF16)]),
        compiler_params=_cparams(1),
        name="expert_mlp",
    )(blk_expert, n_used, xs, w_gu, b_gu, w_down, b_down)


def _combine_kernel(dst_ref, nslot_ref, x_ref, info_ref, g_ref, b_ref, y_ref, out_ref, slots_ref, sem):
    i = pl.program_id(0)
    tm = x_ref.shape[0]

    @pl.when(i == 0)
    def _():
        slots_ref[...] = jnp.zeros_like(slots_ref)

    _chunk_copies(nslot_ref[i], lambda s: pltpu.make_async_copy(
        y_ref.at[pl.ds(pl.multiple_of(dst_ref[i * N_SLOTS + s], SUBLANES), CHUNK)],
        slots_ref.at[pl.ds(pl.multiple_of(s * CHUNK, CHUNK), CHUNK)], sem))

    info = info_ref[...]
    ffn = jnp.zeros((tm, D_MODEL), F32)
    for c in range(SLOT_ROWS // tm):
        pos = (lax.broadcasted_iota(jnp.int32, (tm, tm), 1) + c * tm).astype(F32)
        w = jnp.zeros((tm, tm), F32)
        for k in range(TOP_K):
            w = w + jnp.where(pos == info[:, INFO_SLOT + k:INFO_SLOT + k + 1],
                              info[:, INFO_GATE + k:INFO_GATE + k + 1], 0.0)
        w_hi = w.astype(BF16)
        w_lo = (w - w_hi.astype(F32)).astype(BF16)
        rows = _unpack_rows(slots_ref[c * tm:(c + 1) * tm, :])
        ffn = ffn + jnp.dot(w_hi, rows, preferred_element_type=F32) + jnp.dot(w_lo, rows, preferred_element_type=F32)
    out_ref[...] = _layer_norm(ALPHA * x_ref[...] + ffn, g_ref[...], b_ref[...])


def combine(x1, info, dst, nslot, y, ln_g, ln_b):
    n = x1.shape[0]
    tm = ROUTE_ROWS
    row = lambda i, *_: (i, 0)
    full = lambda i, *_: (0, 0)
    return pl.pallas_call(
        _combine_kernel,
        out_shape=jax.ShapeDtypeStruct((n, D_MODEL), F32),
        grid_spec=pltpu.PrefetchScalarGridSpec(
            num_scalar_prefetch=2,
            grid=(n // tm,),
            in_specs=[pl.BlockSpec((tm, D_MODEL), row),
                      pl.BlockSpec((tm, LANES), row),
                      pl.BlockSpec((1, D_MODEL), full),
                      pl.BlockSpec((1, D_MODEL), full),
                      pl.BlockSpec(memory_space=pl.ANY)],
            out_specs=pl.BlockSpec((tm, D_MODEL), row),
            scratch_shapes=[pltpu.VMEM((SLOT_ROWS, PACKED), U32), pltpu.SemaphoreType.DMA(())]),
        compiler_params=_cparams(1),
        name="moe_combine",
    )(dst, nslot, x1, info, ln_g.astype(F32)[None, :], ln_b.astype(F32)[None, :], y)


def moe_sublayer(x1, logits, w_gu, b_gu, w_down, b_down, ln_g, ln_b):
    n = x1.shape[0]
    blk = MOE_BLOCK
    n_tiles = n // ROUTE_ROWS
    padded_rows = n * TOP_K + (SUBLANES - 1) * N_EXPERTS * n_tiles + N_EXPERTS * (blk + CHUNK)
    padded_rows = -(-padded_rows // blk) * blk
    info, tab, cnt = route(logits)

    i32 = lambda t: t.astype(jnp.int32)
    tab = tab.reshape(n_tiles, SUBLANES, LANES)[:, :, :N_EXPERTS]
    run, base, chunks = i32(tab[:, TAB_RUN]), i32(tab[:, TAB_BASE]), i32(tab[:, TAB_CHUNKS])
    counts = i32(cnt[0, :N_EXPERTS])
    padded = (counts + CHUNK + blk - 1) // blk * blk
    seg_end = jnp.cumsum(padded)
    seg_start = seg_end - padded
    slot = jnp.arange(N_SLOTS, dtype=jnp.int32)
    ends = base + chunks
    slot_expert = jnp.minimum(jnp.sum(i32(ends[:, None, :] <= slot[None, :, None]), -1), N_EXPERTS - 1)
    onehot = i32(slot_expert[:, :, None] == jnp.arange(N_EXPERTS, dtype=jnp.int32))
    pick = lambda v: jnp.sum(onehot * v[:, None, :], -1)
    dst = (pick(seg_start[None, :] + run) + (slot[None, :] - pick(base)) * CHUNK).reshape(-1)
    nslot = ends[:, N_EXPERTS - 1]
    total = seg_end[N_EXPERTS - 1]
    n_blocks = padded_rows // blk
    blk_row = jnp.arange(n_blocks, dtype=jnp.int32) * blk
    blk_expert = jnp.minimum(jnp.sum(i32(seg_end[None, :] <= blk_row[:, None]), -1), N_EXPERTS - 1)
    n_used = (total // blk).reshape(1)
    tail0 = (seg_start + counts) // blk * blk
    tail1 = jnp.where(tail0 + blk < seg_end, tail0 + blk, -1)
    zdst = jnp.concatenate([tail0, tail1, jnp.where(blk_row >= total, blk_row, -1)])

    xs = dispatch(x1, info, dst, nslot, zdst, padded_rows)
    y = expert_mlp(xs, blk_expert, n_used, w_gu.astype(F32), b_gu.astype(F32)[:, None, :],
                   w_down.astype(F32), b_down.astype(F32)[:, None, :])
    return combine(x1, info, dst, nslot, y, ln_g, ln_b)


def kernel(x, w_in, ssd_conv_w, ssd_conv_b, ssd_dt_bias, ssd_a_log, ssd_d, ssd_norm_g, sc_conv_w, w_out,
           ln1_g, ln1_b, router_w, router_b, exp_w_gu, exp_b_gu, exp_w_down, exp_b_down, ln2_g, ln2_b):
    batch, seq, _ = x.shape
    n = batch * seq
    cos_t, sin_t = rope_tables(seq)
    xf = x.reshape(n, D_MODEL).astype(F32)
    a_end = 3 * ATTN_WIDTH
    s_end = a_end + SSD_WIDTH + SSD_CONV_DIM
    d_end = s_end + SSD_HEADS
    for layer in range(w_in.shape[0]):
        w = w_in[layer]
        w_attn = w[:, :a_end].astype(BF16)
        w_ssd = w[:, a_end:s_end].astype(BF16)
        w_dt = jnp.pad(w[:, s_end:d_end], ((0, 0), (0, LANES - SSD_HEADS))).astype(BF16)
        w_sc = w[:, d_end:].astype(BF16)
        (q1, k1, v1, q4, k4, v4, q16, k16, v16, z, xbc, dt_raw, sb, u) = in_projection(
            xf, w_attn, w_ssd, w_dt, w_sc, cos_t, sin_t, batch, seq)
        qkv = ((q1.reshape(batch, 1, seq, ATTN_WIDTH), k1.reshape(batch, 1, seq, ATTN_WIDTH),
                v1.reshape(batch, 1, seq, ATTN_WIDTH)), (q4, k4, v4), (q16, k16, v16))
        attn = [dilated_attention(*t, batch, seq, d) for t, d in zip(qkv, DILATIONS)]
        o_list = [attn[0][0].reshape(n, ATTN_WIDTH), attn[1][0], attn[2][0]]
        lse_list = [attn[0][1].reshape(n, LANES), attn[1][1], attn[2][1]]
        ssd = ssd_mixer(z, xbc, dt_raw, ssd_conv_w[layer], ssd_conv_b[layer], ssd_dt_bias[layer],
                        ssd_a_log[layer], ssd_d[layer], ssd_norm_g[layer], batch, seq)
        x1, logits = out_projection(o_list, lse_list, ssd, sb, u, xf,
                                    sc_conv_w[layer], w_out[layer], ln1_g[layer], ln1_b[layer],
                                    router_w[layer], router_b[layer], seq)
        xf = moe_sublayer(x1, logits, exp_w_gu[layer], exp_b_gu[layer], exp_w_down[layer],
                          exp_b_down[layer], ln2_g[layer], ln2_b[layer])
    return xf.reshape(batch, seq, D_MODEL).astype(x.dtype)
```

```python
import functools

import jax
import jax.numpy as jnp
from jax import lax
from jax.experimental import pallas as pl
from jax.experimental.pallas import tpu as pltpu

F32 = jnp.float32
BF16 = jnp.bfloat16
U32 = jnp.uint32

D_MODEL = 1024
HEAD_DIM = 64
ATTN_HEADS = 8
ATTN_WIDTH = 512
DILATIONS = (1, 4, 16)
ATTN_SPAN = 128
ATTN_BLOCK = 128
ATTN_Q_ROWS = 512
ROPE_THETA = 10000.0

SSD_HEADS = 8
SSD_WIDTH = 512
SSD_GROUPS = 2
SSD_STATE = 128
SSD_CONV = 4
SSD_CHUNK = 128
SSD_CONV_DIM = 1024

SC_WIDTH = 512
SC_CONV = 3
MIX_WIDTH = 1536

N_EXPERTS = 32
TOP_K = 4
EXPERT_FF = 1024
SWIGLU_ALPHA = 1.702
SWIGLU_LIMIT = 7.0

DEPTH = 2
ALPHA = (2.0 * DEPTH) ** 0.25
LN_EPS = 1e-5
RMS_EPS = 1e-5

LANES = 128
SUBLANES = 8
HALO = 16
NEG = -1e30
VMEM_LIMIT = 56 * 1024 * 1024

PROJ_ROWS = 512
ROUTE_ROWS = 256
MOE_BLOCK = 512
CHUNK = 16
N_SLOTS = N_EXPERTS + ROUTE_ROWS * TOP_K // CHUNK
SLOT_ROWS = N_SLOTS * CHUNK
PACKED = D_MODEL // 2
HI_MASK = 0xFFFF0000


def _cparams(n_axes):
    return pltpu.CompilerParams(dimension_semantics=("arbitrary",) * n_axes,
                                vmem_limit_bytes=VMEM_LIMIT)


def _sigmoid(x):
    return 1.0 / (1.0 + jnp.exp(-x))


def _pair_cols(lo, tile, h):
    return jnp.where(lo, tile[:, h:h + 1], tile[:, h + 1:h + 2])


def _pack_rows(v):
    return pltpu.bitcast(v[:, :PACKED], U32) | (pltpu.bitcast(v[:, PACKED:], U32) >> 16)


def _unpack_rows(w):
    a = pltpu.bitcast(w & U32(HI_MASK), F32).astype(BF16)
    b = pltpu.bitcast(w << 16, F32).astype(BF16)
    return jnp.concatenate([a, b], axis=1)


def _rope_kernel(inv_ref, cos_ref, sin_ref):
    rows = cos_ref.shape[0]
    base = pl.program_id(0) * rows
    pos = (lax.broadcasted_iota(jnp.int32, cos_ref.shape, 0) + base).astype(F32)
    lane = lax.broadcasted_iota(jnp.int32, cos_ref.shape, 1)
    ang = pos * inv_ref[...]
    cos_ref[...] = jnp.cos(ang)
    sin_ref[...] = jnp.where((lane & 32) == 0, -jnp.sin(ang), jnp.sin(ang))


def rope_tables(seq):
    half = HEAD_DIM // 2
    inv = ROPE_THETA ** (-jnp.arange(half, dtype=F32) / half)
    inv = jnp.tile(inv, LANES // half)[None, :]
    rows = 512
    return pl.pallas_call(
        _rope_kernel,
        out_shape=(jax.ShapeDtypeStruct((seq, LANES), F32),) * 2,
        grid=(seq // rows,),
        in_specs=[pl.BlockSpec((1, LANES), lambda i: (0, 0))],
        out_specs=(pl.BlockSpec((rows, LANES), lambda i: (i, 0)),) * 2,
        compiler_params=_cparams(1),
        name="rope_tables",
    )(inv)


def _inproj_kernel(x_ref, wa_ref, ws_ref, wdt_ref, wc_ref, cos_ref, sin_ref,
                   q1_ref, k1_ref, v1_ref, q4_ref, k4_ref, v4_ref, q16_ref, k16_ref, v16_ref,
                   z_ref, xbc_ref, dt_ref, sb_ref, u_ref, perm_ref):
    tm = x_ref.shape[0]
    xb = x_ref[...].astype(BF16)
    cos = cos_ref[...]
    sin = sin_ref[...]
    lane = lax.broadcasted_iota(jnp.int32, cos.shape, 1)
    first_half = (lane & 32) == 0
    n_slab = ATTN_WIDTH // LANES

    def proj(w_ref, lo, hi):
        return jnp.dot(xb, w_ref[:, lo:hi], preferred_element_type=F32)

    def rope(t):
        rot = jnp.where(first_half, pltpu.roll(t, LANES - 32, 1), pltpu.roll(t, 32, 1))
        return t * cos + rot * sin

    def emit(slabs, nat_ref, strided_refs):
        for j, t in enumerate(slabs):
            perm_ref[j] = t
            nat_ref[:, j * LANES:(j + 1) * LANES] = t.astype(BF16)
        for d, ref in strided_refs:
            for r in range(d):
                for j in range(n_slab):
                    ref[r, :, j * LANES:(j + 1) * LANES] = (
                        perm_ref[j, pl.ds(r, tm // d, stride=d), :].astype(BF16))

    w = ATTN_WIDTH
    q = proj(wa_ref, 0, w)
    emit([rope(q[:, j * LANES:(j + 1) * LANES]) * (HEAD_DIM ** -0.5) for j in range(n_slab)],
         q1_ref, ((4, q4_ref), (16, q16_ref)))
    k = proj(wa_ref, w, 2 * w)
    emit([rope(k[:, j * LANES:(j + 1) * LANES]) for j in range(n_slab)], k1_ref, ((4, k4_ref), (16, k16_ref)))
    v = proj(wa_ref, 2 * w, 3 * w)
    emit([v[:, j * LANES:(j + 1) * LANES] for j in range(n_slab)], v1_ref, ((4, v4_ref), (16, v16_ref)))
    z_ref[...] = proj(ws_ref, 0, SSD_WIDTH).astype(BF16)
    xbc_ref[...] = proj(ws_ref, SSD_WIDTH, SSD_WIDTH + SSD_CONV_DIM).astype(BF16)
    dt_ref[...] = jnp.dot(xb, wdt_ref[...], preferred_element_type=F32)
    sb_ref[...] = proj(wc_ref, 0, SC_WIDTH).astype(BF16)
    c = proj(wc_ref, SC_WIDTH, 2 * SC_WIDTH)
    h = proj(wc_ref, 2 * SC_WIDTH, 3 * SC_WIDTH)
    u_ref[...] = (c * h).astype(BF16)


def in_projection(x, w_attn, w_ssd, w_dt, w_sc, cos_t, sin_t, batch, seq):
    n = x.shape[0]
    tm = PROJ_ROWS
    tps = seq // tm
    row = lambda i: (i, 0)
    full = lambda i: (0, 0)
    tab = lambda i: (i % tps, 0)
    strided = lambda i: (i // tps, 0, i % tps, 0)
    w = ATTN_WIDTH
    nat = jax.ShapeDtypeStruct((n, w), BF16)
    nat_spec = pl.BlockSpec((tm, w), row)
    shapes, specs = [nat] * 3, [nat_spec] * 3
    for d in DILATIONS[1:]:
        shapes += [jax.ShapeDtypeStruct((batch, d, seq // d, w), BF16)] * 3
        specs += [pl.BlockSpec((None, d, tm // d, w), strided)] * 3
    widths = (SSD_WIDTH, SSD_CONV_DIM, LANES, SC_WIDTH, SC_WIDTH)
    dtypes = (BF16, BF16, F32, BF16, BF16)
    shapes += [jax.ShapeDtypeStruct((n, wd), dt) for wd, dt in zip(widths, dtypes)]
    specs += [pl.BlockSpec((tm, wd), row) for wd in widths]
    return pl.pallas_call(
        _inproj_kernel,
        out_shape=tuple(shapes),
        grid=(n // tm,),
        in_specs=[pl.BlockSpec((tm, D_MODEL), row),
                  pl.BlockSpec(w_attn.shape, full),
                  pl.BlockSpec(w_ssd.shape, full),
                  pl.BlockSpec(w_dt.shape, full),
                  pl.BlockSpec(w_sc.shape, full),
                  pl.BlockSpec((tm, LANES), tab),
                  pl.BlockSpec((tm, LANES), tab)],
        out_specs=tuple(specs),
        scratch_shapes=[pltpu.VMEM((w // LANES, tm, LANES), F32)],
        compiler_params=_cparams(1),
        name="in_projection",
    )(x, w_attn, w_ssd, w_dt, w_sc, cos_t, sin_t)


def _attn_kernel(q_ref, kc_ref, kp_ref, vc_ref, vp_ref, o_ref, lse_ref):
    blk = ATTN_BLOCK
    n = pl.program_id(2)
    lane = lax.broadcasted_iota(jnp.int32, (blk, LANES), 1)
    lo = lane < HEAD_DIM
    qi = lax.broadcasted_iota(jnp.int32, (2 * blk, 2 * blk), 0) & (blk - 1)
    kj = lax.broadcasted_iota(jnp.int32, (2 * blk, 2 * blk), 1)
    dist = qi + blk - kj
    in_band = jnp.where(dist >= 0, jnp.where(dist <= ATTN_SPAN, 1, 0), 0)
    band = in_band > 0
    band_first = (in_band * jnp.where(kj >= blk, 1, jnp.where(n > 0, 1, 0))) > 0
    for j in range(q_ref.shape[0] // blk):
        rows = slice(j * blk, (j + 1) * blk)
        prev = slice((j - 1) * blk, j * blk)
        valid = band_first if j == 0 else band
        lse_tile = jnp.zeros((blk, LANES), F32)
        for hp in range(ATTN_WIDTH // LANES):
            sl = slice(hp * LANES, (hp + 1) * LANES)
            q2 = q_ref[rows, sl].astype(F32)
            qq = jnp.concatenate([jnp.where(lo, q2, 0.0), jnp.where(lo, 0.0, q2)], axis=0).astype(BF16)
            k_prev = kp_ref[:, sl] if j == 0 else kc_ref[prev, sl]
            v_prev = vp_ref[:, sl] if j == 0 else vc_ref[prev, sl]
            k2 = jnp.concatenate([k_prev, kc_ref[rows, sl]], axis=0)
            v2 = jnp.concatenate([v_prev, vc_ref[rows, sl]], axis=0)
            s = lax.dot_general(qq, k2, (((1,), (1,)), ((), ())), preferred_element_type=F32)
            s = jnp.where(valid, s, NEG)
            m = jnp.max(s, axis=1, keepdims=True)
            p = jnp.exp(s - m)
            l = jnp.sum(p, axis=1, keepdims=True)
            pv = jnp.dot(p.astype(BF16), v2, preferred_element_type=F32)
            o = pv / l
            o_ref[rows, sl] = jnp.where(lo, o[:blk], o[blk:]).astype(BF16)
            lse = m + jnp.log(l)
            lse_tile = jnp.where(lane == 2 * hp, lse[:blk],
                                 jnp.where(lane == 2 * hp + 1, lse[blk:], lse_tile))
        lse_ref[rows, :] = lse_tile


def dilated_attention(q, k, v, batch, seq, dilation):
    length = seq // dilation
    w = ATTN_WIDTH
    blk = ATTN_BLOCK
    tq = min(ATTN_Q_ROWS, length)
    cur = lambda b, r, n: (b, r, n, 0)
    prev = lambda b, r, n: (b, r, jnp.maximum(n * (tq // blk) - 1, 0), 0)
    big = pl.BlockSpec((None, None, tq, w), cur)
    small = pl.BlockSpec((None, None, blk, w), prev)
    return pl.pallas_call(
        _attn_kernel,
        out_shape=(jax.ShapeDtypeStruct((batch, dilation, length, w), BF16),
                   jax.ShapeDtypeStruct((batch, dilation, length, LANES), F32)),
        grid=(batch, dilation, length // tq),
        in_specs=[big, big, small, big, small],
        out_specs=(big, pl.BlockSpec((None, None, tq, LANES), cur)),
        compiler_params=_cparams(3),
        name=f"dilated_attention_d{dilation}",
    )(q, k, k, v, v)


def _ssd_kernel(z_ref, xbc_ref, halo_ref, dt_ref, cw_ref, cb_ref, dtb_ref, alog_ref, dskip_ref,
                g_ref, y_ref, ext_ref, st_ref):
    ch = SSD_CHUNK
    c = pl.program_id(1)

    @pl.when(c == 0)
    def _():
        st_ref[...] = jnp.zeros_like(st_ref)

    ext_ref[0:HALO, :] = jnp.where(c > 0, halo_ref[...].astype(F32), 0.0)
    ext_ref[HALO:HALO + ch, :] = xbc_ref[...].astype(F32)
    conv = cb_ref[...]
    for t in range(SSD_CONV):
        off = HALO - (SSD_CONV - 1) + t
        conv = conv + cw_ref[t:t + 1, :] * ext_ref[off:off + ch, :]
    xbc = conv * _sigmoid(conv)
    xh = xbc[:, :SSD_WIDTH]
    bm = xbc[:, SSD_WIDTH:SSD_WIDTH + SSD_GROUPS * SSD_STATE]
    cm = xbc[:, SSD_WIDTH + SSD_GROUPS * SSD_STATE:]

    lane = lax.broadcasted_iota(jnp.int32, (ch, LANES), 1)
    row = lax.broadcasted_iota(jnp.int32, (ch, LANES), 0)
    lo = lane < HEAD_DIM
    lo_row = lo[0:1, :]
    head_lane = lane < SSD_HEADS

    xdt = dt_ref[...] + dtb_ref[...]
    dtv = jnp.maximum(xdt, 0.0) + jnp.log(1.0 + jnp.exp(-jnp.abs(xdt)))
    a = jnp.where(head_lane[0:1, :], -jnp.exp(alog_ref[...]), 0.0)
    acum = dtv * a
    shift = 1
    while shift < ch:
        acum = acum + jnp.where(row >= shift, pltpu.roll(acum, shift, 0), 0.0)
        shift *= 2
    acum_t = acum.T
    tot = acum[ch - 1:ch, :]
    e_in = jnp.exp(acum)
    e_out = jnp.exp(tot - acum)
    e_tot = jnp.exp(tot)
    causal = row >= lane

    ys = []
    for g in range(SSD_GROUPS):
        bg = bm[:, g * SSD_STATE:(g + 1) * SSD_STATE]
        cg = cm[:, g * SSD_STATE:(g + 1) * SSD_STATE].astype(BF16)
        cb = lax.dot_general(cg, bg.astype(BF16), (((1,), (1,)), ((), ())),
                             preferred_element_type=F32)
        bg_t = bg.T.astype(BF16)
        for pp in range(2):
            pair = 2 * g + pp
            ha = 2 * pair
            sl = slice(pair * LANES, (pair + 1) * LANES)
            xp = xh[:, sl]
            xdt_pair = xp * _pair_cols(lo, dtv, ha)
            xb16 = xdt_pair.astype(BF16)
            diag = []
            for h in (ha, ha + 1):
                seg = acum[:, h:h + 1] - acum_t[h:h + 1, :]
                decay = jnp.exp(jnp.where(causal, seg, NEG))
                diag.append(jnp.dot((cb * decay).astype(BF16), xb16, preferred_element_type=F32))
            y = jnp.where(lo, diag[0], diag[1])
            state = st_ref[pair]
            y = y + jnp.dot(cg, state.astype(BF16), preferred_element_type=F32) * _pair_cols(lo, e_in, ha)
            xout = (xdt_pair * _pair_cols(lo, e_out, ha)).astype(BF16)
            st_ref[pair] = (state * _pair_cols(lo_row, e_tot, ha)
                            + jnp.dot(bg_t, xout, preferred_element_type=F32))
            ys.append(y + dskip_ref[:, sl] * xp)
    y = jnp.concatenate(ys, axis=1)
    zz = z_ref[...].astype(F32)
    y = y * (zz * _sigmoid(zz))
    gw = SSD_WIDTH // SSD_GROUPS
    outs = []
    for g in range(SSD_GROUPS):
        yg = y[:, g * gw:(g + 1) * gw]
        ms = jnp.mean(yg * yg, axis=1, keepdims=True)
        outs.append(yg * lax.rsqrt(ms + RMS_EPS))
    y_ref[...] = (jnp.concatenate(outs, axis=1) * g_ref[...]).astype(BF16)


def ssd_mixer(z, xbc, dt_raw, conv_w, conv_b, dt_bias, a_log, d_skip, norm_g, batch, seq):
    n = batch * seq
    ch = SSD_CHUNK
    nc = seq // ch
    pad = lambda t: jnp.pad(t.astype(F32), (0, LANES - t.shape[0]))[None, :]
    cur = lambda b, c: (b * nc + c, 0)
    halo = lambda b, c: (jnp.maximum((b * nc + c) * (ch // HALO) - 1, 0), 0)
    full = lambda b, c: (0, 0)
    return pl.pallas_call(
        _ssd_kernel,
        out_shape=jax.ShapeDtypeStruct((n, SSD_WIDTH), BF16),
        grid=(batch, nc),
        in_specs=[pl.BlockSpec((ch, SSD_WIDTH), cur),
                  pl.BlockSpec((ch, SSD_CONV_DIM), cur),
                  pl.BlockSpec((HALO, SSD_CONV_DIM), halo),
                  pl.BlockSpec((ch, LANES), cur),
                  pl.BlockSpec((SSD_CONV, SSD_CONV_DIM), full),
                  pl.BlockSpec((1, SSD_CONV_DIM), full),
                  pl.BlockSpec((1, LANES), full),
                  pl.BlockSpec((1, LANES), full),
                  pl.BlockSpec((1, SSD_WIDTH), full),
                  pl.BlockSpec((1, SSD_WIDTH), full)],
        out_specs=pl.BlockSpec((ch, SSD_WIDTH), cur),
        scratch_shapes=[pltpu.VMEM((HALO + ch, SSD_CONV_DIM), F32),
                        pltpu.VMEM((SSD_HEADS // 2, SSD_STATE, LANES), F32)],
        compiler_params=_cparams(2),
        name="ssd_mixer",
    )(z, xbc, xbc, dt_raw, conv_w.astype(F32), conv_b.astype(F32)[None, :], pad(dt_bias), pad(a_log),
      jnp.repeat(d_skip.astype(F32), HEAD_DIM)[None, :], norm_g.astype(F32)[None, :])


def _layer_norm(r, g, b):
    mu = jnp.mean(r, axis=1, keepdims=True)
    d = r - mu
    var = jnp.mean(d * d, axis=1, keepdims=True)
    return d * lax.rsqrt(var + LN_EPS) * g + b


def _split3(x):
    hi = x.astype(BF16)
    r1 = x - hi.astype(F32)
    mid = r1.astype(BF16)
    lo = (r1 - mid.astype(F32)).astype(BF16)
    return hi, mid, lo


def _outproj_kernel(tiles_per_seq, o1_ref, o4_ref, o16_ref, l1_ref, l4_ref, l16_ref, ssd_ref, sb_ref,
                    u_ref, uh_ref, x_ref, cw_ref, wout_ref, g_ref, b_ref, rwh_ref, rwm_ref, rwl_ref,
                    rb_ref, x1_ref, logit_ref, ext_ref, operm_ref, lperm_ref):
    tm = x_ref.shape[0]
    i = pl.program_id(0)
    lane = lax.broadcasted_iota(jnp.int32, (tm, LANES), 1)
    lo = lane < HEAD_DIM
    n_slab = ATTN_WIDTH // LANES

    for idx, (d, o_ref, l_ref) in enumerate(((4, o4_ref, l4_ref), (16, o16_ref, l16_ref))):
        for r in range(d):
            rows = pl.ds(r, tm // d, stride=d)
            lperm_ref[idx, rows, :] = l_ref[r]
            for j in range(n_slab):
                operm_ref[idx * n_slab + j, rows, :] = o_ref[r, :, j * LANES:(j + 1) * LANES].astype(F32)

    lses = (l1_ref[...], lperm_ref[0], lperm_ref[1])
    top = jnp.maximum(jnp.maximum(lses[0], lses[1]), lses[2])
    es = [jnp.exp(l - top) for l in lses]
    den = es[0] + es[1] + es[2]
    ws = [e / den for e in es]
    attn = []
    for hp in range(n_slab):
        sl = slice(hp * LANES, (hp + 1) * LANES)
        acc = _pair_cols(lo, ws[0], 2 * hp) * o1_ref[:, sl].astype(F32)
        acc = acc + _pair_cols(lo, ws[1], 2 * hp) * operm_ref[hp]
        acc = acc + _pair_cols(lo, ws[2], 2 * hp) * operm_ref[n_slab + hp]
        attn.append(acc.astype(BF16))

    seq_start = (i % tiles_per_seq) == 0
    ext_ref[0:HALO, :] = jnp.where(seq_start, 0.0, uh_ref[...].astype(F32))
    ext_ref[HALO:HALO + tm, :] = u_ref[...].astype(F32)
    conv = jnp.zeros((tm, SC_WIDTH), F32)
    for t in range(SC_CONV):
        off = HALO - (SC_CONV - 1) + t
        conv = conv + cw_ref[t:t + 1, :] * ext_ref[off:off + tm, :]
    gated = (sb_ref[...].astype(F32) * conv).astype(BF16)

    mixed = jnp.concatenate(attn + [ssd_ref[...], gated], axis=1)
    mix = jnp.dot(mixed, wout_ref[...], preferred_element_type=F32)
    x1 = _layer_norm(ALPHA * x_ref[...] + mix, g_ref[...], b_ref[...])
    x1_ref[...] = x1

    xh, xm, xl = _split3(x1)
    dot = lambda a, w_ref: jnp.dot(a, w_ref[...], preferred_element_type=F32)
    logits = (dot(xl, rwh_ref) + dot(xm, rwm_ref) + dot(xh, rwl_ref)
              + dot(xm, rwh_ref) + dot(xh, rwm_ref) + dot(xh, rwh_ref))
    logit_ref[...] = logits + rb_ref[...]


def out_projection(o_list, lse_list, ssd, sb, u, x, sc_conv_w, w_out, ln_g, ln_b, router_w, router_b, seq):
    n = x.shape[0]
    tm = PROJ_ROWS
    tps = seq // tm
    row = lambda i: (i, 0)
    full = lambda i: (0, 0)
    strided = lambda i: (i // tps, 0, i % tps, 0)
    halo = lambda i: (jnp.maximum(i * (tm // HALO) - 1, 0), 0)
    rw = jnp.pad(router_w.astype(F32), ((0, 0), (0, LANES - N_EXPERTS)))
    rwh, rwm, rwl = _split3(rw)
    rb = jnp.pad(router_b.astype(F32), (0, LANES - N_EXPERTS))[None, :]
    wide = lambda wd: pl.BlockSpec((tm, wd), row)
    perm = lambda d, wd: pl.BlockSpec((None, d, tm // d, wd), strided)
    const = lambda a: pl.BlockSpec(a.shape, full)
    wout = w_out.astype(BF16)
    cw = sc_conv_w.astype(F32)
    g = ln_g.astype(F32)[None, :]
    b = ln_b.astype(F32)[None, :]
    n_slab = ATTN_WIDTH // LANES
    return pl.pallas_call(
        functools.partial(_outproj_kernel, tps),
        out_shape=(jax.ShapeDtypeStruct((n, D_MODEL), F32), jax.ShapeDtypeStruct((n, LANES), F32)),
        grid=(n // tm,),
        in_specs=[wide(ATTN_WIDTH), perm(4, ATTN_WIDTH), perm(16, ATTN_WIDTH),
                  wide(LANES), perm(4, LANES), perm(16, LANES),
                  wide(SSD_WIDTH), wide(SC_WIDTH), wide(SC_WIDTH), pl.BlockSpec((HALO, SC_WIDTH), halo),
                  wide(D_MODEL), const(cw), const(wout), const(g), const(b),
                  const(rwh), const(rwm), const(rwl), const(rb)],
        out_specs=(wide(D_MODEL), wide(LANES)),
        scratch_shapes=[pltpu.VMEM((HALO + tm, SC_WIDTH), F32),
                        pltpu.VMEM((2 * n_slab, tm, LANES), F32),
                        pltpu.VMEM((2, tm, LANES), F32)],
        compiler_params=_cparams(1),
        name="out_projection",
    )(*o_list, *lse_list, ssd, sb, u, u, x, cw, wout, g, b, rwh, rwm, rwl, rb)


INFO_IDX, INFO_SLOT, INFO_GATE = 0, TOP_K, 2 * TOP_K
TAB_RUN, TAB_BASE, TAB_CHUNKS = 0, 1, 2


def _route_kernel(logit_ref, tri_ref, upper_ref, info_ref, tab_ref, cnt_ref, run_ref):
    tm = logit_ref.shape[0]

    @pl.when(pl.program_id(0) == 0)
    def _():
        run_ref[...] = jnp.zeros_like(run_ref)

    lane = lax.broadcasted_iota(jnp.int32, (tm, LANES), 1)
    lane_f = lane.astype(F32)
    work = jnp.where(lane < N_EXPERTS, logit_ref[...], NEG)
    sel = jnp.zeros((tm, LANES), F32)
    vals, idxs = [], []
    for _ in range(TOP_K):
        m = jnp.max(work, axis=1, keepdims=True)
        idx = jnp.min(jnp.where(work == m, lane_f, float(LANES)), axis=1, keepdims=True)
        hit = lane_f == idx
        work = jnp.where(hit, 2 * NEG, work)
        sel = sel + jnp.where(hit, 1.0, 0.0)
        vals.append(m)
        idxs.append(idx)
    es = [jnp.exp(v - vals[0]) for v in vals]
    den = es[0] + es[1] + es[2] + es[3]
    before = jnp.dot(tri_ref[...], sel.astype(BF16), preferred_element_type=F32)
    cnt = jnp.sum(sel, axis=0, keepdims=True)
    chunks = jnp.floor((cnt + (CHUNK - 1)) * (1.0 / CHUNK))
    base = jnp.dot(jnp.broadcast_to(chunks, (SUBLANES, LANES)).astype(BF16), upper_ref[...],
                   preferred_element_type=F32)[0:1, :]
    info = jnp.zeros((tm, LANES), F32)
    for k in range(TOP_K):
        mine = lane_f == idxs[k]
        rank = jnp.sum(jnp.where(mine, before, 0.0), axis=1, keepdims=True)
        first = jnp.sum(jnp.where(mine, base, 0.0), axis=1, keepdims=True)
        slot_row = first * CHUNK + rank
        info = jnp.where(lane == INFO_IDX + k, idxs[k], info)
        info = jnp.where(lane == INFO_SLOT + k, slot_row, info)
        info = jnp.where(lane == INFO_GATE + k, es[k] / den, info)
    info_ref[...] = info
    sub = lax.broadcasted_iota(jnp.int32, (SUBLANES, LANES), 0)
    tab_ref[...] = jnp.where(sub == TAB_RUN, run_ref[...],
                             jnp.where(sub == TAB_BASE, base, jnp.where(sub == TAB_CHUNKS, chunks, 0.0)))
    run_ref[...] = run_ref[...] + jnp.floor((cnt + (SUBLANES - 1)) * (1.0 / SUBLANES)) * SUBLANES
    cnt_ref[...] = jnp.broadcast_to(run_ref[...], cnt_ref.shape)


def route(logits):
    n = logits.shape[0]
    tm = ROUTE_ROWS
    r = lax.broadcasted_iota(jnp.int32, (tm, tm), 0)
    c = lax.broadcasted_iota(jnp.int32, (tm, tm), 1)
    tri = jnp.where(r > c, 1.0, 0.0).astype(BF16)
    upper = jnp.where(r[:LANES, :LANES] < c[:LANES, :LANES], 1.0, 0.0).astype(BF16)
    return pl.pallas_call(
        _route_kernel,
        out_shape=(jax.ShapeDtypeStruct((n, LANES), F32),
                   jax.ShapeDtypeStruct((n // tm * SUBLANES, LANES), F32),
                   jax.ShapeDtypeStruct((SUBLANES, LANES), F32)),
        grid=(n // tm,),
        in_specs=[pl.BlockSpec((tm, LANES), lambda i: (i, 0)),
                  pl.BlockSpec((tm, tm), lambda i: (0, 0)),
                  pl.BlockSpec((LANES, LANES), lambda i: (0, 0))],
        out_specs=(pl.BlockSpec((tm, LANES), lambda i: (i, 0)),
                   pl.BlockSpec((SUBLANES, LANES), lambda i: (i, 0)),
                   pl.BlockSpec((SUBLANES, LANES), lambda i: (0, 0))),
        scratch_shapes=[pltpu.VMEM((1, LANES), F32)],
        compiler_params=_cparams(1),
        name="moe_route",
    )(logits, tri, upper)


def _start_all(n_chunks, make_copy):
    def start(s, carry):
        make_copy(s).start()
        return carry

    lax.fori_loop(0, n_chunks, start, 0)


def _wait_all(n_chunks, make_copy):
    def wait(s, carry):
        make_copy(s).wait()
        return carry

    lax.fori_loop(0, n_chunks, wait, 0)


def _dispatch_kernel(dst_ref, nslot_ref, zdst_ref, x_ref, info_ref, buf_ref,
                     slots_ref, zero_ref, sems, zsem):
    i = pl.program_id(0)
    tm = x_ref.shape[0]
    cur = i % 2

    def put(tile, side):
        return lambda s: pltpu.make_async_copy(
            slots_ref.at[side, pl.ds(pl.multiple_of(s * CHUNK, CHUNK), CHUNK)],
            buf_ref.at[pl.ds(pl.multiple_of(dst_ref[tile * N_SLOTS + s], SUBLANES), CHUNK)], sems.at[side])

    @pl.when(i == 0)
    def _():
        zero_ref[...] = jnp.zeros_like(zero_ref)

        def zero_copy(e):
            row = pl.multiple_of(jnp.maximum(zdst_ref[e], 0), MOE_BLOCK)
            return pltpu.make_async_copy(zero_ref, buf_ref.at[pl.ds(row, MOE_BLOCK)], zsem)

        def start(e, carry):
            @pl.when(zdst_ref[e] >= 0)
            def _():
                zero_copy(e).start()
            return carry

        def wait(e, carry):
            @pl.when(zdst_ref[e] >= 0)
            def _():
                zero_copy(e).wait()
            return carry

        lax.fori_loop(0, zdst_ref.shape[0], start, 0)
        lax.fori_loop(0, zdst_ref.shape[0], wait, 0)

    xb = x_ref[...].astype(BF16)
    info_t = info_ref[...].T
    for c in range(SLOT_ROWS // tm):
        pos = (lax.broadcasted_iota(jnp.int32, (tm, tm), 0) + c * tm).astype(F32)
        onehot = jnp.zeros((tm, tm), F32)
        for k in range(TOP_K):
            onehot = onehot + jnp.where(pos == info_t[INFO_SLOT + k:INFO_SLOT + k + 1, :], 1.0, 0.0)
        rows = jnp.dot(onehot.astype(BF16), xb, preferred_element_type=F32)
        slots_ref[cur, c * tm:(c + 1) * tm, :] = _pack_rows(rows)

    @pl.when(i > 0)
    def _():
        _wait_all(nslot_ref[i - 1], put(i - 1, 1 - cur))

    _start_all(nslot_ref[i], put(i, cur))

    @pl.when(i == pl.num_programs(0) - 1)
    def _():
        _wait_all(nslot_ref[i], put(i, cur))


def dispatch(x1, info, dst, nslot, zdst, padded_rows):
    n = x1.shape[0]
    tm = ROUTE_ROWS
    row = lambda i, *_: (i, 0)
    return pl.pallas_call(
        _dispatch_kernel,
        out_shape=jax.ShapeDtypeStruct((padded_rows, PACKED), U32),
        grid_spec=pltpu.PrefetchScalarGridSpec(
            num_scalar_prefetch=3,
            grid=(n // tm,),
            in_specs=[pl.BlockSpec((tm, D_MODEL), row), pl.BlockSpec((tm, LANES), row)],
            out_specs=pl.BlockSpec(memory_space=pl.ANY),
            scratch_shapes=[pltpu.VMEM((2, SLOT_ROWS, PACKED), U32), pltpu.VMEM((MOE_BLOCK, PACKED), U32),
                            pltpu.SemaphoreType.DMA((2,)), pltpu.SemaphoreType.DMA(())]),
        compiler_params=_cparams(1),
        name="moe_dispatch",
    )(dst, nslot, zdst, x1, info)


def _expert_kernel(blk_expert_ref, n_used_ref, xs_ref, wgu_ref, bgu_ref, wd_ref, bd_ref, y_ref,
                   wgu_bf, wd_bf):
    i = pl.program_id(0)
    used = i < n_used_ref[0]

    @pl.when(jnp.logical_not(used))
    def _():
        y_ref[...] = jnp.zeros_like(y_ref)

    new_expert = jnp.logical_or(i == 0, blk_expert_ref[i] != blk_expert_ref[jnp.maximum(i - 1, 0)])

    @pl.when(jnp.logical_and(used, new_expert))
    def _():
        rows = 128

        def cast(c, carry):
            r = pl.multiple_of(c * rows, rows)
            wgu_bf[pl.ds(r, rows), :] = wgu_ref[pl.ds(r, rows), :].astype(BF16)
            wd_bf[pl.ds(r, rows), :] = wd_ref[pl.ds(r, rows), :].astype(BF16)
            return carry

        lax.fori_loop(0, D_MODEL // rows, cast, 0)

    @pl.when(used)
    def _():
        xb = _unpack_rows(xs_ref[...])
        gu = jnp.dot(xb, wgu_bf[...], preferred_element_type=F32) + bgu_ref[...]
        gate = jnp.minimum(gu[:, :EXPERT_FF], SWIGLU_LIMIT)
        up = jnp.clip(gu[:, EXPERT_FF:], -SWIGLU_LIMIT, SWIGLU_LIMIT)
        h = (up + 1.0) * gate * _sigmoid(SWIGLU_ALPHA * gate)
        y = jnp.dot(h.astype(BF16), wd_bf[...], preferred_element_type=F32) + bd_ref[...]
        y_ref[...] = _pack_rows(y.astype(BF16).astype(F32))


def expert_mlp(xs, blk_expert, n_used, layer, w_gu, b_gu, w_down, b_down):
    padded_rows = xs.shape[0]
    blk = MOE_BLOCK
    rows_in = lambda i, be, nu: (jnp.minimum(i, nu[0] - 1), 0)
    per_expert = lambda i, be, nu: (layer, be[i], 0, 0)
    return pl.pallas_call(
        _expert_kernel,
        out_shape=jax.ShapeDtypeStruct((padded_rows, PACKED), U32),
        grid_spec=pltpu.PrefetchScalarGridSpec(
            num_scalar_prefetch=2,
            grid=(padded_rows // blk,),
            in_specs=[pl.BlockSpec((blk, PACKED), rows_in),
                      pl.BlockSpec((None, None, D_MODEL, 2 * EXPERT_FF), per_expert),
                      pl.BlockSpec((None, None, 1, 2 * EXPERT_FF), per_expert),
                      pl.BlockSpec((None, None, EXPERT_FF, D_MODEL), per_expert),
                      pl.BlockSpec((None, None, 1, D_MODEL), per_expert)],
            out_specs=pl.BlockSpec((blk, PACKED), lambda i, be, nu: (i, 0)),
            scratch_shapes=[pltpu.VMEM((D_MODEL, 2 * EXPERT_FF), BF16), pltpu.VMEM((EXPERT_FF, D_MODEL), BF16)]),
        compiler_params=_cparams(1),
        name="expert_mlp",
    )(blk_expert, n_used, xs, w_gu, b_gu, w_down, b_down)


def _combine_kernel(dst_ref, nslot_ref, x_ref, info_ref, g_ref, b_ref, y_ref, out_ref, slots_ref, sems):
    i = pl.program_id(0)
    tm = x_ref.shape[0]
    cur = i % 2

    def fetch(tile, side):
        return lambda s: pltpu.make_async_copy(
            y_ref.at[pl.ds(pl.multiple_of(dst_ref[tile * N_SLOTS + s], SUBLANES), CHUNK)],
            slots_ref.at[side, pl.ds(pl.multiple_of(s * CHUNK, CHUNK), CHUNK)], sems.at[side])

    @pl.when(i == 0)
    def _():
        slots_ref[...] = jnp.zeros_like(slots_ref)
        _start_all(nslot_ref[0], fetch(0, 0))

    @pl.when(i + 1 < pl.num_programs(0))
    def _():
        _start_all(nslot_ref[i + 1], fetch(i + 1, 1 - cur))

    _wait_all(nslot_ref[i], fetch(i, cur))

    info = info_ref[...]
    ffn = jnp.zeros((tm, D_MODEL), F32)
    for c in range(SLOT_ROWS // tm):
        pos = (lax.broadcasted_iota(jnp.int32, (tm, tm), 1) + c * tm).astype(F32)
        w = jnp.zeros((tm, tm), F32)
        for k in range(TOP_K):
            w = w + jnp.where(pos == info[:, INFO_SLOT + k:INFO_SLOT + k + 1],
                              info[:, INFO_GATE + k:INFO_GATE + k + 1], 0.0)
        w_hi = w.astype(BF16)
        w_lo = (w - w_hi.astype(F32)).astype(BF16)
        rows = _unpack_rows(slots_ref[cur, c * tm:(c + 1) * tm, :])
        ffn = ffn + jnp.dot(w_hi, rows, preferred_element_type=F32) + jnp.dot(w_lo, rows, preferred_element_type=F32)
    out_ref[...] = _layer_norm(ALPHA * x_ref[...] + ffn, g_ref[...], b_ref[...])


def combine(x1, info, dst, nslot, y, ln_g, ln_b):
    n = x1.shape[0]
    tm = ROUTE_ROWS
    row = lambda i, *_: (i, 0)
    full = lambda i, *_: (0, 0)
    return pl.pallas_call(
        _combine_kernel,
        out_shape=jax.ShapeDtypeStruct((n, D_MODEL), F32),
        grid_spec=pltpu.PrefetchScalarGridSpec(
            num_scalar_prefetch=2,
            grid=(n // tm,),
            in_specs=[pl.BlockSpec((tm, D_MODEL), row),
                      pl.BlockSpec((tm, LANES), row),
                      pl.BlockSpec((1, D_MODEL), full),
                      pl.BlockSpec((1, D_MODEL), full),
                      pl.BlockSpec(memory_space=pl.ANY)],
            out_specs=pl.BlockSpec((tm, D_MODEL), row),
            scratch_shapes=[pltpu.VMEM((2, SLOT_ROWS, PACKED), U32), pltpu.SemaphoreType.DMA((2,))]),
        compiler_params=_cparams(1),
        name="moe_combine",
    )(dst, nslot, x1, info, ln_g.astype(F32)[None, :], ln_b.astype(F32)[None, :], y)


def moe_sublayer(x1, logits, layer, w_gu, b_gu, w_down, b_down, ln_g, ln_b):
    n = x1.shape[0]
    blk = MOE_BLOCK
    n_tiles = n // ROUTE_ROWS
    padded_rows = n * TOP_K + (SUBLANES - 1) * N_EXPERTS * n_tiles + N_EXPERTS * (blk + CHUNK)
    padded_rows = -(-padded_rows // blk) * blk
    info, tab, cnt = route(logits)

    i32 = lambda t: t.astype(jnp.int32)
    tab = tab.reshape(n_tiles, SUBLANES, LANES)[:, :, :N_EXPERTS]
    run, base, chunks = i32(tab[:, TAB_RUN]), i32(tab[:, TAB_BASE]), i32(tab[:, TAB_CHUNKS])
    counts = i32(cnt[0, :N_EXPERTS])
    padded = (counts + CHUNK + blk - 1) // blk * blk
    seg_end = jnp.cumsum(padded)
    seg_start = seg_end - padded
    slot = jnp.arange(N_SLOTS, dtype=jnp.int32)
    ends = base + chunks
    slot_expert = jnp.minimum(jnp.sum(i32(ends[:, None, :] <= slot[None, :, None]), -1), N_EXPERTS - 1)
    onehot = i32(slot_expert[:, :, None] == jnp.arange(N_EXPERTS, dtype=jnp.int32))
    pick = lambda v: jnp.sum(onehot * v[:, None, :], -1)
    dst = (pick(seg_start[None, :] + run) + (slot[None, :] - pick(base)) * CHUNK).reshape(-1)
    nslot = ends[:, N_EXPERTS - 1]
    total = seg_end[N_EXPERTS - 1]
    n_blocks = padded_rows // blk
    blk_row = jnp.arange(n_blocks, dtype=jnp.int32) * blk
    blk_expert = jnp.minimum(jnp.sum(i32(seg_end[None, :] <= blk_row[:, None]), -1), N_EXPERTS - 1)
    n_used = (total // blk).reshape(1)
    tail0 = (seg_start + counts) // blk * blk
    tail1 = jnp.where(tail0 + blk < seg_end, tail0 + blk, -1)
    zdst = jnp.concatenate([tail0, tail1, jnp.where(blk_row >= total, blk_row, -1)])

    xs = dispatch(x1, info, dst, nslot, zdst, padded_rows)
    y = expert_mlp(xs, blk_expert, n_used, layer, w_gu.astype(F32), b_gu.astype(F32)[:, :, None, :],
                   w_down.astype(F32), b_down.astype(F32)[:, :, None, :])
    return combine(x1, info, dst, nslot, y, ln_g, ln_b)


def kernel(x, w_in, ssd_conv_w, ssd_conv_b, ssd_dt_bias, ssd_a_log, ssd_d, ssd_norm_g, sc_conv_w, w_out,
           ln1_g, ln1_b, router_w, router_b, exp_w_gu, exp_b_gu, exp_w_down, exp_b_down, ln2_g, ln2_b):
    batch, seq, _ = x.shape
    n = batch * seq
    cos_t, sin_t = rope_tables(seq)
    xf = x.reshape(n, D_MODEL).astype(F32)
    a_end = 3 * ATTN_WIDTH
    s_end = a_end + SSD_WIDTH + SSD_CONV_DIM
    d_end = s_end + SSD_HEADS
    for layer in range(w_in.shape[0]):
        w = w_in[layer]
        w_attn = w[:, :a_end].astype(BF16)
        w_ssd = w[:, a_end:s_end].astype(BF16)
        w_dt = jnp.pad(w[:, s_end:d_end], ((0, 0), (0, LANES - SSD_HEADS))).astype(BF16)
        w_sc = w[:, d_end:].astype(BF16)
        (q1, k1, v1, q4, k4, v4, q16, k16, v16, z, xbc, dt_raw, sb, u) = in_projection(
            xf, w_attn, w_ssd, w_dt, w_sc, cos_t, sin_t, batch, seq)
        qkv = ((q1.reshape(batch, 1, seq, ATTN_WIDTH), k1.reshape(batch, 1, seq, ATTN_WIDTH),
                v1.reshape(batch, 1, seq, ATTN_WIDTH)), (q4, k4, v4), (q16, k16, v16))
        attn = [dilated_attention(*t, batch, seq, d) for t, d in zip(qkv, DILATIONS)]
        o_list = [attn[0][0].reshape(n, ATTN_WIDTH), attn[1][0], attn[2][0]]
        lse_list = [attn[0][1].reshape(n, LANES), attn[1][1], attn[2][1]]
        ssd = ssd_mixer(z, xbc, dt_raw, ssd_conv_w[layer], ssd_conv_b[layer], ssd_dt_bias[layer],
                        ssd_a_log[layer], ssd_d[layer], ssd_norm_g[layer], batch, seq)
        x1, logits = out_projection(o_list, lse_list, ssd, sb, u, xf,
                                    sc_conv_w[layer], w_out[layer], ln1_g[layer], ln1_b[layer],
                                    router_w[layer], router_b[layer], seq)
        xf = moe_sublayer(x1, logits, layer, exp_w_gu, exp_b_gu, exp_w_down, exp_b_down,
                          ln2_g[layer], ln2_b[layer])
    return xf.reshape(batch, seq, D_MODEL).astype(x.dtype)
```

```python
import functools

import jax
import jax.numpy as jnp
from jax import lax
from jax.experimental import pallas as pl
from jax.experimental.pallas import tpu as pltpu

F32 = jnp.float32
BF16 = jnp.bfloat16
U32 = jnp.uint32

D_MODEL = 1024
HEAD_DIM = 64
ATTN_HEADS = 8
ATTN_WIDTH = 512
DILATIONS = (1, 4, 16)
ATTN_SPAN = 128
ATTN_BLOCK = 128
ATTN_Q_ROWS = 512
ROPE_THETA = 10000.0

SSD_HEADS = 8
SSD_WIDTH = 512
SSD_GROUPS = 2
SSD_STATE = 128
SSD_CONV = 4
SSD_CHUNK = 128
SSD_CONV_DIM = 1024

SC_WIDTH = 512
SC_CONV = 3
MIX_WIDTH = 1536

N_EXPERTS = 32
TOP_K = 4
EXPERT_FF = 1024
SWIGLU_ALPHA = 1.702
SWIGLU_LIMIT = 7.0

DEPTH = 2
ALPHA = (2.0 * DEPTH) ** 0.25
LN_EPS = 1e-5
RMS_EPS = 1e-5

LANES = 128
SUBLANES = 8
HALO = 16
NEG = -1e30
VMEM_LIMIT = 56 * 1024 * 1024

PROJ_ROWS = 512
ROUTE_ROWS = 256
MOE_BLOCK = 512
CHUNK = 16
N_SLOTS = N_EXPERTS + ROUTE_ROWS * TOP_K // CHUNK
SLOT_ROWS = N_SLOTS * CHUNK
PACKED = D_MODEL // 2
HI_MASK = 0xFFFF0000


def _cparams(n_axes):
    return pltpu.CompilerParams(dimension_semantics=("arbitrary",) * n_axes,
                                vmem_limit_bytes=VMEM_LIMIT)


def _sigmoid(x):
    return 1.0 / (1.0 + jnp.exp(-x))


def _pair_cols(lo, tile, h):
    return jnp.where(lo, tile[:, h:h + 1], tile[:, h + 1:h + 2])


def _pack_rows(v):
    return pltpu.bitcast(v[:, :PACKED], U32) | (pltpu.bitcast(v[:, PACKED:], U32) >> 16)


def _unpack_rows(w):
    a = pltpu.bitcast(w & U32(HI_MASK), F32).astype(BF16)
    b = pltpu.bitcast(w << 16, F32).astype(BF16)
    return jnp.concatenate([a, b], axis=1)


def _rope_kernel(inv_ref, cos_ref, sin_ref):
    rows = cos_ref.shape[0]
    base = pl.program_id(0) * rows
    pos = (lax.broadcasted_iota(jnp.int32, cos_ref.shape, 0) + base).astype(F32)
    lane = lax.broadcasted_iota(jnp.int32, cos_ref.shape, 1)
    ang = pos * inv_ref[...]
    cos_ref[...] = jnp.cos(ang)
    sin_ref[...] = jnp.where((lane & 32) == 0, -jnp.sin(ang), jnp.sin(ang))


def rope_tables(seq):
    half = HEAD_DIM // 2
    inv = ROPE_THETA ** (-jnp.arange(half, dtype=F32) / half)
    inv = jnp.tile(inv, LANES // half)[None, :]
    rows = 512
    return pl.pallas_call(
        _rope_kernel,
        out_shape=(jax.ShapeDtypeStruct((seq, LANES), F32),) * 2,
        grid=(seq // rows,),
        in_specs=[pl.BlockSpec((1, LANES), lambda i: (0, 0))],
        out_specs=(pl.BlockSpec((rows, LANES), lambda i: (i, 0)),) * 2,
        compiler_params=_cparams(1),
        name="rope_tables",
    )(inv)


def _inproj_kernel(x_ref, wa_ref, ws_ref, wdt_ref, wc_ref, cos_ref, sin_ref,
                   q1_ref, k1_ref, v1_ref, q4_ref, k4_ref, v4_ref, q16_ref, k16_ref, v16_ref,
                   z_ref, xbc_ref, dt_ref, sb_ref, u_ref, perm_ref):
    tm = x_ref.shape[0]
    xb = x_ref[...].astype(BF16)
    cos = cos_ref[...]
    sin = sin_ref[...]
    lane = lax.broadcasted_iota(jnp.int32, cos.shape, 1)
    first_half = (lane & 32) == 0
    n_slab = ATTN_WIDTH // LANES

    def proj(w_ref, lo, hi):
        return jnp.dot(xb, w_ref[:, lo:hi], preferred_element_type=F32)

    def rope(t):
        rot = jnp.where(first_half, pltpu.roll(t, LANES - 32, 1), pltpu.roll(t, 32, 1))
        return t * cos + rot * sin

    def emit(slabs, nat_ref, strided_refs):
        for j, t in enumerate(slabs):
            perm_ref[j] = t
            nat_ref[:, j * LANES:(j + 1) * LANES] = t.astype(BF16)
        for d, ref in strided_refs:
            for r in range(d):
                for j in range(n_slab):
                    ref[r, :, j * LANES:(j + 1) * LANES] = (
                        perm_ref[j, pl.ds(r, tm // d, stride=d), :].astype(BF16))

    w = ATTN_WIDTH
    q = proj(wa_ref, 0, w)
    emit([rope(q[:, j * LANES:(j + 1) * LANES]) * (HEAD_DIM ** -0.5) for j in range(n_slab)],
         q1_ref, ((4, q4_ref), (16, q16_ref)))
    k = proj(wa_ref, w, 2 * w)
    emit([rope(k[:, j * LANES:(j + 1) * LANES]) for j in range(n_slab)], k1_ref, ((4, k4_ref), (16, k16_ref)))
    v = proj(wa_ref, 2 * w, 3 * w)
    emit([v[:, j * LANES:(j + 1) * LANES] for j in range(n_slab)], v1_ref, ((4, v4_ref), (16, v16_ref)))
    z_ref[...] = proj(ws_ref, 0, SSD_WIDTH).astype(BF16)
    xbc_ref[...] = proj(ws_ref, SSD_WIDTH, SSD_WIDTH + SSD_CONV_DIM).astype(BF16)
    dt_ref[...] = jnp.dot(xb, wdt_ref[...], preferred_element_type=F32)
    sb_ref[...] = proj(wc_ref, 0, SC_WIDTH).astype(BF16)
    c = proj(wc_ref, SC_WIDTH, 2 * SC_WIDTH)
    h = proj(wc_ref, 2 * SC_WIDTH, 3 * SC_WIDTH)
    u_ref[...] = (c * h).astype(BF16)


def in_projection(x, w_attn, w_ssd, w_dt, w_sc, cos_t, sin_t, batch, seq):
    n = x.shape[0]
    tm = PROJ_ROWS
    tps = seq // tm
    row = lambda i: (i, 0)
    full = lambda i: (0, 0)
    tab = lambda i: (i % tps, 0)
    strided = lambda i: (i // tps, 0, i % tps, 0)
    w = ATTN_WIDTH
    nat = jax.ShapeDtypeStruct((n, w), BF16)
    nat_spec = pl.BlockSpec((tm, w), row)
    shapes, specs = [nat] * 3, [nat_spec] * 3
    for d in DILATIONS[1:]:
        shapes += [jax.ShapeDtypeStruct((batch, d, seq // d, w), BF16)] * 3
        specs += [pl.BlockSpec((None, d, tm // d, w), strided)] * 3
    widths = (SSD_WIDTH, SSD_CONV_DIM, LANES, SC_WIDTH, SC_WIDTH)
    dtypes = (BF16, BF16, F32, BF16, BF16)
    shapes += [jax.ShapeDtypeStruct((n, wd), dt) for wd, dt in zip(widths, dtypes)]
    specs += [pl.BlockSpec((tm, wd), row) for wd in widths]
    return pl.pallas_call(
        _inproj_kernel,
        out_shape=tuple(shapes),
        grid=(n // tm,),
        in_specs=[pl.BlockSpec((tm, D_MODEL), row),
                  pl.BlockSpec(w_attn.shape, full),
                  pl.BlockSpec(w_ssd.shape, full),
                  pl.BlockSpec(w_dt.shape, full),
                  pl.BlockSpec(w_sc.shape, full),
                  pl.BlockSpec((tm, LANES), tab),
                  pl.BlockSpec((tm, LANES), tab)],
        out_specs=tuple(specs),
        scratch_shapes=[pltpu.VMEM((w // LANES, tm, LANES), F32)],
        compiler_params=_cparams(1),
        name="in_projection",
    )(x, w_attn, w_ssd, w_dt, w_sc, cos_t, sin_t)


def _attn_kernel(q_ref, kc_ref, kp_ref, vc_ref, vp_ref, o_ref, lse_ref):
    blk = ATTN_BLOCK
    n = pl.program_id(2)
    lane = lax.broadcasted_iota(jnp.int32, (blk, LANES), 1)
    lo = lane < HEAD_DIM
    qi = lax.broadcasted_iota(jnp.int32, (2 * blk, 2 * blk), 0) & (blk - 1)
    kj = lax.broadcasted_iota(jnp.int32, (2 * blk, 2 * blk), 1)
    dist = qi + blk - kj
    in_band = jnp.where(dist >= 0, jnp.where(dist <= ATTN_SPAN, 1, 0), 0)
    band = in_band > 0
    band_first = (in_band * jnp.where(kj >= blk, 1, jnp.where(n > 0, 1, 0))) > 0
    for j in range(q_ref.shape[0] // blk):
        rows = slice(j * blk, (j + 1) * blk)
        prev = slice((j - 1) * blk, j * blk)
        valid = band_first if j == 0 else band
        lse_tile = jnp.zeros((blk, LANES), F32)
        for hp in range(ATTN_WIDTH // LANES):
            sl = slice(hp * LANES, (hp + 1) * LANES)
            q2 = q_ref[rows, sl].astype(F32)
            qq = jnp.concatenate([jnp.where(lo, q2, 0.0), jnp.where(lo, 0.0, q2)], axis=0).astype(BF16)
            k_prev = kp_ref[:, sl] if j == 0 else kc_ref[prev, sl]
            v_prev = vp_ref[:, sl] if j == 0 else vc_ref[prev, sl]
            k2 = jnp.concatenate([k_prev, kc_ref[rows, sl]], axis=0)
            v2 = jnp.concatenate([v_prev, vc_ref[rows, sl]], axis=0)
            s = lax.dot_general(qq, k2, (((1,), (1,)), ((), ())), preferred_element_type=F32)
            s = jnp.where(valid, s, NEG)
            m = jnp.max(s, axis=1, keepdims=True)
            p = jnp.exp(s - m)
            l = jnp.sum(p, axis=1, keepdims=True)
            pv = jnp.dot(p.astype(BF16), v2, preferred_element_type=F32)
            o = pv / l
            o_ref[rows, sl] = jnp.where(lo, o[:blk], o[blk:]).astype(BF16)
            lse = m + jnp.log(l)
            lse_tile = jnp.where(lane == 2 * hp, lse[:blk],
                                 jnp.where(lane == 2 * hp + 1, lse[blk:], lse_tile))
        lse_ref[rows, :] = lse_tile


def dilated_attention(q, k, v, batch, seq, dilation):
    length = seq // dilation
    w = ATTN_WIDTH
    blk = ATTN_BLOCK
    tq = min(ATTN_Q_ROWS, length)
    cur = lambda b, r, n: (b, r, n, 0)
    prev = lambda b, r, n: (b, r, jnp.maximum(n * (tq // blk) - 1, 0), 0)
    big = pl.BlockSpec((None, None, tq, w), cur)
    small = pl.BlockSpec((None, None, blk, w), prev)
    return pl.pallas_call(
        _attn_kernel,
        out_shape=(jax.ShapeDtypeStruct((batch, dilation, length, w), BF16),
                   jax.ShapeDtypeStruct((batch, dilation, length, LANES), F32)),
        grid=(batch, dilation, length // tq),
        in_specs=[big, big, small, big, small],
        out_specs=(big, pl.BlockSpec((None, None, tq, LANES), cur)),
        compiler_params=_cparams(3),
        name=f"dilated_attention_d{dilation}",
    )(q, k, k, v, v)


def _ssd_kernel(z_ref, xbc_ref, halo_ref, dt_ref, shift_ref, cw_ref, cb_ref, dtb_ref, alog_ref, dskip_ref,
                g_ref, y_ref, st_ref):
    ch = SSD_CHUNK
    c = pl.program_id(1)

    @pl.when(c == 0)
    def _():
        st_ref[...] = jnp.zeros_like(st_ref)

    halo = halo_ref[...]
    ext = jnp.concatenate([jnp.where(c > 0, halo, jnp.zeros_like(halo)), xbc_ref[...]], axis=0)
    conv = cb_ref[...] + cw_ref[SSD_CONV - 1:SSD_CONV, :] * xbc_ref[...].astype(F32)
    for t in range(SSD_CONV - 1):
        conv = conv + cw_ref[t:t + 1, :] * jnp.dot(shift_ref[t], ext, preferred_element_type=F32)
    xbc = conv * _sigmoid(conv)
    xh = xbc[:, :SSD_WIDTH]
    bm = xbc[:, SSD_WIDTH:SSD_WIDTH + SSD_GROUPS * SSD_STATE]
    cm = xbc[:, SSD_WIDTH + SSD_GROUPS * SSD_STATE:]

    lane = lax.broadcasted_iota(jnp.int32, (ch, LANES), 1)
    row = lax.broadcasted_iota(jnp.int32, (ch, LANES), 0)
    lo = lane < HEAD_DIM
    lo_row = lo[0:1, :]
    head_lane = lane < SSD_HEADS

    xdt = dt_ref[...] + dtb_ref[...]
    dtv = jnp.maximum(xdt, 0.0) + jnp.log(1.0 + jnp.exp(-jnp.abs(xdt)))
    a = jnp.where(head_lane[0:1, :], -jnp.exp(alog_ref[...]), 0.0)
    acum = dtv * a
    shift = 1
    while shift < ch:
        acum = acum + jnp.where(row >= shift, pltpu.roll(acum, shift, 0), 0.0)
        shift *= 2
    acum_t = acum.T
    tot = acum[ch - 1:ch, :]
    e_in = jnp.exp(acum)
    e_out = jnp.exp(tot - acum)
    e_tot = jnp.exp(tot)
    causal = row >= lane

    ys = []
    for g in range(SSD_GROUPS):
        bg = bm[:, g * SSD_STATE:(g + 1) * SSD_STATE]
        cg = cm[:, g * SSD_STATE:(g + 1) * SSD_STATE].astype(BF16)
        cb = lax.dot_general(cg, bg.astype(BF16), (((1,), (1,)), ((), ())),
                             preferred_element_type=F32)
        bg_t = bg.T.astype(BF16)
        for pp in range(2):
            pair = 2 * g + pp
            ha = 2 * pair
            sl = slice(pair * LANES, (pair + 1) * LANES)
            xp = xh[:, sl]
            xdt_pair = xp * _pair_cols(lo, dtv, ha)
            xb16 = xdt_pair.astype(BF16)
            diag = []
            for h in (ha, ha + 1):
                seg = acum[:, h:h + 1] - acum_t[h:h + 1, :]
                decay = jnp.exp(jnp.where(causal, seg, NEG))
                diag.append(jnp.dot((cb * decay).astype(BF16), xb16, preferred_element_type=F32))
            y = jnp.where(lo, diag[0], diag[1])
            state = st_ref[pair]
            y = y + jnp.dot(cg, state.astype(BF16), preferred_element_type=F32) * _pair_cols(lo, e_in, ha)
            xout = (xdt_pair * _pair_cols(lo, e_out, ha)).astype(BF16)
            st_ref[pair] = (state * _pair_cols(lo_row, e_tot, ha)
                            + jnp.dot(bg_t, xout, preferred_element_type=F32))
            ys.append(y + dskip_ref[:, sl] * xp)
    y = jnp.concatenate(ys, axis=1)
    zz = z_ref[...].astype(F32)
    y = y * (zz * _sigmoid(zz))
    gw = SSD_WIDTH // SSD_GROUPS
    outs = []
    for g in range(SSD_GROUPS):
        yg = y[:, g * gw:(g + 1) * gw]
        ms = jnp.mean(yg * yg, axis=1, keepdims=True)
        outs.append(yg * lax.rsqrt(ms + RMS_EPS))
    y_ref[...] = (jnp.concatenate(outs, axis=1) * g_ref[...]).astype(BF16)


def ssd_mixer(z, xbc, dt_raw, conv_w, conv_b, dt_bias, a_log, d_skip, norm_g, batch, seq):
    n = batch * seq
    ch = SSD_CHUNK
    nc = seq // ch
    pad = lambda t: jnp.pad(t.astype(F32), (0, LANES - t.shape[0]))[None, :]
    cur = lambda b, c: (b * nc + c, 0)
    halo = lambda b, c: (jnp.maximum((b * nc + c) * (ch // HALO) - 1, 0), 0)
    full = lambda b, c: (0, 0)
    r_idx = lax.broadcasted_iota(jnp.int32, (SSD_CONV - 1, ch, HALO + ch), 1)
    j_idx = lax.broadcasted_iota(jnp.int32, (SSD_CONV - 1, ch, HALO + ch), 2)
    t_idx = lax.broadcasted_iota(jnp.int32, (SSD_CONV - 1, ch, HALO + ch), 0)
    shift = jnp.where(j_idx == r_idx + HALO - (SSD_CONV - 1) + t_idx, 1.0, 0.0).astype(BF16)
    return pl.pallas_call(
        _ssd_kernel,
        out_shape=jax.ShapeDtypeStruct((n, SSD_WIDTH), BF16),
        grid=(batch, nc),
        in_specs=[pl.BlockSpec((ch, SSD_WIDTH), cur),
                  pl.BlockSpec((ch, SSD_CONV_DIM), cur),
                  pl.BlockSpec((HALO, SSD_CONV_DIM), halo),
                  pl.BlockSpec((ch, LANES), cur),
                  pl.BlockSpec(shift.shape, lambda b, c: (0, 0, 0)),
                  pl.BlockSpec((SSD_CONV, SSD_CONV_DIM), full),
                  pl.BlockSpec((1, SSD_CONV_DIM), full),
                  pl.BlockSpec((1, LANES), full),
                  pl.BlockSpec((1, LANES), full),
                  pl.BlockSpec((1, SSD_WIDTH), full),
                  pl.BlockSpec((1, SSD_WIDTH), full)],
        out_specs=pl.BlockSpec((ch, SSD_WIDTH), cur),
        scratch_shapes=[pltpu.VMEM((SSD_HEADS // 2, SSD_STATE, LANES), F32)],
        compiler_params=_cparams(2),
        name="ssd_mixer",
    )(z, xbc, xbc, dt_raw, shift, conv_w.astype(F32), conv_b.astype(F32)[None, :], pad(dt_bias), pad(a_log),
      jnp.repeat(d_skip.astype(F32), HEAD_DIM)[None, :], norm_g.astype(F32)[None, :])


def _layer_norm(r, g, b):
    mu = jnp.mean(r, axis=1, keepdims=True)
    d = r - mu
    var = jnp.mean(d * d, axis=1, keepdims=True)
    return d * lax.rsqrt(var + LN_EPS) * g + b


def _outproj_kernel(tiles_per_seq, o1_ref, o4_ref, o16_ref, l1_ref, l4_ref, l16_ref, ssd_ref, sb_ref,
                    u_ref, uh_ref, x_ref, cw_ref, wout_ref, g_ref, b_ref, spread_ref, rw_ref,
                    rb_ref, x1_ref, logit_ref, ext_ref, operm_ref, lperm_ref):
    tm = x_ref.shape[0]
    i = pl.program_id(0)
    n_slab = ATTN_WIDTH // LANES

    for idx, (d, o_ref, l_ref) in enumerate(((4, o4_ref, l4_ref), (16, o16_ref, l16_ref))):
        for r in range(d):
            rows = pl.ds(r, tm // d, stride=d)
            lperm_ref[idx, rows, :] = l_ref[r]
            for j in range(n_slab):
                operm_ref[idx * n_slab + j, rows, :] = o_ref[r, :, j * LANES:(j + 1) * LANES].astype(F32)

    lses = (l1_ref[...], lperm_ref[0], lperm_ref[1])
    top = jnp.maximum(jnp.maximum(lses[0], lses[1]), lses[2])
    es = [jnp.exp(l - top) for l in lses]
    den = es[0] + es[1] + es[2]
    wide = []
    for e in es:
        w = e / den
        w_hi = w.astype(BF16)
        w_lo = (w - w_hi.astype(F32)).astype(BF16)
        wide.append(jnp.dot(w_hi, spread_ref[...], preferred_element_type=F32)
                    + jnp.dot(w_lo, spread_ref[...], preferred_element_type=F32))
    attn = []
    for hp in range(n_slab):
        sl = slice(hp * LANES, (hp + 1) * LANES)
        acc = wide[0][:, sl] * o1_ref[:, sl].astype(F32)
        acc = acc + wide[1][:, sl] * operm_ref[hp]
        acc = acc + wide[2][:, sl] * operm_ref[n_slab + hp]
        attn.append(acc.astype(BF16))

    seq_start = (i % tiles_per_seq) == 0
    ext_ref[0:HALO, :] = jnp.where(seq_start, 0.0, uh_ref[...].astype(F32))
    ext_ref[HALO:HALO + tm, :] = u_ref[...].astype(F32)
    conv = jnp.zeros((tm, SC_WIDTH), F32)
    for t in range(SC_CONV):
        off = HALO - (SC_CONV - 1) + t
        conv = conv + cw_ref[t:t + 1, :] * ext_ref[off:off + tm, :]
    gated = (sb_ref[...].astype(F32) * conv).astype(BF16)

    mixed = jnp.concatenate(attn + [ssd_ref[...], gated], axis=1)
    mix = jnp.dot(mixed, wout_ref[...], preferred_element_type=F32)
    x1 = _layer_norm(ALPHA * x_ref[...] + mix, g_ref[...], b_ref[...])
    x1_ref[...] = x1

    xh = x1.astype(BF16)
    xm = (x1 - xh.astype(F32)).astype(BF16)
    ph = jnp.dot(xh, rw_ref[...], preferred_element_type=F32)
    pm = jnp.dot(xm, rw_ref[...], preferred_element_type=F32)
    logit_ref[...] = pm + pltpu.roll(ph, LANES - N_EXPERTS, 1) + ph + rb_ref[...]


def out_projection(o_list, lse_list, ssd, sb, u, x, sc_conv_w, w_out, ln_g, ln_b, router_w, router_b, seq):
    n = x.shape[0]
    tm = PROJ_ROWS
    tps = seq // tm
    row = lambda i: (i, 0)
    full = lambda i: (0, 0)
    strided = lambda i: (i // tps, 0, i % tps, 0)
    halo = lambda i: (jnp.maximum(i * (tm // HALO) - 1, 0), 0)
    rw_f = router_w.astype(F32)
    rw_hi = rw_f.astype(BF16)
    rw_mid = (rw_f - rw_hi.astype(F32)).astype(BF16)
    rw = jnp.pad(jnp.concatenate([rw_hi, rw_mid], axis=1), ((0, 0), (0, LANES - 2 * N_EXPERTS)))
    rb = jnp.pad(router_b.astype(F32), (0, LANES - N_EXPERTS))[None, :]
    spread = jnp.where(lax.broadcasted_iota(jnp.int32, (LANES, ATTN_WIDTH), 0)
                       == lax.broadcasted_iota(jnp.int32, (LANES, ATTN_WIDTH), 1) // HEAD_DIM, 1.0, 0.0).astype(BF16)
    wide = lambda wd: pl.BlockSpec((tm, wd), row)
    perm = lambda d, wd: pl.BlockSpec((None, d, tm // d, wd), strided)
    const = lambda a: pl.BlockSpec(a.shape, full)
    wout = w_out.astype(BF16)
    cw = sc_conv_w.astype(F32)
    g = ln_g.astype(F32)[None, :]
    b = ln_b.astype(F32)[None, :]
    n_slab = ATTN_WIDTH // LANES
    return pl.pallas_call(
        functools.partial(_outproj_kernel, tps),
        out_shape=(jax.ShapeDtypeStruct((n, D_MODEL), F32), jax.ShapeDtypeStruct((n, LANES), F32)),
        grid=(n // tm,),
        in_specs=[wide(ATTN_WIDTH), perm(4, ATTN_WIDTH), perm(16, ATTN_WIDTH),
                  wide(LANES), perm(4, LANES), perm(16, LANES),
                  wide(SSD_WIDTH), wide(SC_WIDTH), wide(SC_WIDTH), pl.BlockSpec((HALO, SC_WIDTH), halo),
                  wide(D_MODEL), const(cw), const(wout), const(g), const(b),
                  const(spread), const(rw), const(rb)],
        out_specs=(wide(D_MODEL), wide(LANES)),
        scratch_shapes=[pltpu.VMEM((HALO + tm, SC_WIDTH), F32),
                        pltpu.VMEM((2 * n_slab, tm, LANES), F32),
                        pltpu.VMEM((2, tm, LANES), F32)],
        compiler_params=_cparams(1),
        name="out_projection",
    )(*o_list, *lse_list, ssd, sb, u, u, x, cw, wout, g, b, spread, rw, rb)


INFO_IDX, INFO_SLOT, INFO_GATE = 0, TOP_K, 2 * TOP_K
TAB_RUN, TAB_BASE, TAB_CHUNKS = 0, 1, 2


def _route_kernel(logit_ref, tri_ref, upper_ref, info_ref, tab_ref, cnt_ref, run_ref):
    tm = logit_ref.shape[0]

    @pl.when(pl.program_id(0) == 0)
    def _():
        run_ref[...] = jnp.zeros_like(run_ref)

    lane = lax.broadcasted_iota(jnp.int32, (tm, LANES), 1)
    lane_f = lane.astype(F32)
    work = jnp.where(lane < N_EXPERTS, logit_ref[...], NEG)
    sel = jnp.zeros((tm, LANES), F32)
    vals, idxs = [], []
    for _ in range(TOP_K):
        m = jnp.max(work, axis=1, keepdims=True)
        idx = jnp.min(jnp.where(work == m, lane_f, float(LANES)), axis=1, keepdims=True)
        hit = lane_f == idx
        work = jnp.where(hit, 2 * NEG, work)
        sel = sel + jnp.where(hit, 1.0, 0.0)
        vals.append(m)
        idxs.append(idx)
    es = [jnp.exp(v - vals[0]) for v in vals]
    den = es[0] + es[1] + es[2] + es[3]
    before = jnp.dot(tri_ref[...], sel.astype(BF16), preferred_element_type=F32)
    cnt = jnp.sum(sel, axis=0, keepdims=True)
    chunks = jnp.floor((cnt + (CHUNK - 1)) * (1.0 / CHUNK))
    base = jnp.dot(jnp.broadcast_to(chunks, (SUBLANES, LANES)).astype(BF16), upper_ref[...],
                   preferred_element_type=F32)[0:1, :]
    info = jnp.zeros((tm, LANES), F32)
    for k in range(TOP_K):
        mine = lane_f == idxs[k]
        rank = jnp.sum(jnp.where(mine, before, 0.0), axis=1, keepdims=True)
        first = jnp.sum(jnp.where(mine, base, 0.0), axis=1, keepdims=True)
        slot_row = first * CHUNK + rank
        info = jnp.where(lane == INFO_IDX + k, idxs[k], info)
        info = jnp.where(lane == INFO_SLOT + k, slot_row, info)
        info = jnp.where(lane == INFO_GATE + k, es[k] / den, info)
    info_ref[...] = info
    sub = lax.broadcasted_iota(jnp.int32, (SUBLANES, LANES), 0)
    tab_ref[...] = jnp.where(sub == TAB_RUN, run_ref[...],
                             jnp.where(sub == TAB_BASE, base, jnp.where(sub == TAB_CHUNKS, chunks, 0.0)))
    run_ref[...] = run_ref[...] + jnp.floor((cnt + (SUBLANES - 1)) * (1.0 / SUBLANES)) * SUBLANES
    cnt_ref[...] = jnp.broadcast_to(run_ref[...], cnt_ref.shape)


def route(logits):
    n = logits.shape[0]
    tm = ROUTE_ROWS
    r = lax.broadcasted_iota(jnp.int32, (tm, tm), 0)
    c = lax.broadcasted_iota(jnp.int32, (tm, tm), 1)
    tri = jnp.where(r > c, 1.0, 0.0).astype(BF16)
    upper = jnp.where(r[:LANES, :LANES] < c[:LANES, :LANES], 1.0, 0.0).astype(BF16)
    return pl.pallas_call(
        _route_kernel,
        out_shape=(jax.ShapeDtypeStruct((n, LANES), F32),
                   jax.ShapeDtypeStruct((n // tm * SUBLANES, LANES), F32),
                   jax.ShapeDtypeStruct((SUBLANES, LANES), F32)),
        grid=(n // tm,),
        in_specs=[pl.BlockSpec((tm, LANES), lambda i: (i, 0)),
                  pl.BlockSpec((tm, tm), lambda i: (0, 0)),
                  pl.BlockSpec((LANES, LANES), lambda i: (0, 0))],
        out_specs=(pl.BlockSpec((tm, LANES), lambda i: (i, 0)),
                   pl.BlockSpec((SUBLANES, LANES), lambda i: (i, 0)),
                   pl.BlockSpec((SUBLANES, LANES), lambda i: (0, 0))),
        scratch_shapes=[pltpu.VMEM((1, LANES), F32)],
        compiler_params=_cparams(1),
        name="moe_route",
    )(logits, tri, upper)


def _start_all(n_chunks, make_copy):
    def start(s, carry):
        make_copy(s).start()
        return carry

    lax.fori_loop(0, n_chunks, start, 0)


def _wait_all(n_chunks, make_copy):
    def wait(s, carry):
        make_copy(s).wait()
        return carry

    lax.fori_loop(0, n_chunks, wait, 0)


def _dispatch_kernel(dst_ref, nslot_ref, zdst_ref, x_ref, info_ref, buf_ref,
                     slots_ref, zero_ref, sems, zsem):
    i = pl.program_id(0)
    tm = x_ref.shape[0]
    cur = i % 2

    def put(tile, side):
        return lambda s: pltpu.make_async_copy(
            slots_ref.at[side, pl.ds(pl.multiple_of(s * CHUNK, CHUNK), CHUNK)],
            buf_ref.at[pl.ds(pl.multiple_of(dst_ref[tile * N_SLOTS + s], SUBLANES), CHUNK)], sems.at[side])

    @pl.when(i == 0)
    def _():
        zero_ref[...] = jnp.zeros_like(zero_ref)

        def zero_copy(e):
            row = pl.multiple_of(jnp.maximum(zdst_ref[e], 0), MOE_BLOCK)
            return pltpu.make_async_copy(zero_ref, buf_ref.at[pl.ds(row, MOE_BLOCK)], zsem)

        def start(e, carry):
            @pl.when(zdst_ref[e] >= 0)
            def _():
                zero_copy(e).start()
            return carry

        def wait(e, carry):
            @pl.when(zdst_ref[e] >= 0)
            def _():
                zero_copy(e).wait()
            return carry

        lax.fori_loop(0, zdst_ref.shape[0], start, 0)
        lax.fori_loop(0, zdst_ref.shape[0], wait, 0)

    xb = x_ref[...].astype(BF16)
    slot_t = info_ref[...].T[INFO_SLOT:INFO_SLOT + TOP_K, :]
    local = lax.broadcasted_iota(jnp.int32, (tm, tm), 0).astype(F32).astype(BF16)
    one = jnp.ones((tm, tm), BF16)
    zero = jnp.zeros((tm, tm), BF16)
    for c in range(SLOT_ROWS // tm):
        target = (slot_t - float(c * tm)).astype(BF16)
        onehot = zero
        for k in range(TOP_K):
            onehot = onehot + jnp.where(local == target[k:k + 1, :], one, zero)
        rows = jnp.dot(onehot, xb, preferred_element_type=F32)
        slots_ref[cur, c * tm:(c + 1) * tm, :] = _pack_rows(rows)

    @pl.when(i > 0)
    def _():
        _wait_all(nslot_ref[i - 1], put(i - 1, 1 - cur))

    _start_all(nslot_ref[i], put(i, cur))

    @pl.when(i == pl.num_programs(0) - 1)
    def _():
        _wait_all(nslot_ref[i], put(i, cur))


def dispatch(x1, info, dst, nslot, zdst, padded_rows):
    n = x1.shape[0]
    tm = ROUTE_ROWS
    row = lambda i, *_: (i, 0)
    return pl.pallas_call(
        _dispatch_kernel,
        out_shape=jax.ShapeDtypeStruct((padded_rows, PACKED), U32),
        grid_spec=pltpu.PrefetchScalarGridSpec(
            num_scalar_prefetch=3,
            grid=(n // tm,),
            in_specs=[pl.BlockSpec((tm, D_MODEL), row), pl.BlockSpec((tm, LANES), row)],
            out_specs=pl.BlockSpec(memory_space=pl.ANY),
            scratch_shapes=[pltpu.VMEM((2, SLOT_ROWS, PACKED), U32), pltpu.VMEM((MOE_BLOCK, PACKED), U32),
                            pltpu.SemaphoreType.DMA((2,)), pltpu.SemaphoreType.DMA(())]),
        compiler_params=_cparams(1),
        name="moe_dispatch",
    )(dst, nslot, zdst, x1, info)


def _expert_kernel(blk_expert_ref, n_used_ref, xs_ref, wgu_ref, bgu_ref, wd_ref, bd_ref, y_ref,
                   wgu_bf, wd_bf):
    i = pl.program_id(0)
    used = i < n_used_ref[0]

    @pl.when(jnp.logical_not(used))
    def _():
        y_ref[...] = jnp.zeros_like(y_ref)

    new_expert = jnp.logical_or(i == 0, blk_expert_ref[i] != blk_expert_ref[jnp.maximum(i - 1, 0)])

    @pl.when(jnp.logical_and(used, new_expert))
    def _():
        rows = 128

        def cast(c, carry):
            r = pl.multiple_of(c * rows, rows)
            wgu_bf[pl.ds(r, rows), :] = wgu_ref[pl.ds(r, rows), :].astype(BF16)
            wd_bf[pl.ds(r, rows), :] = wd_ref[pl.ds(r, rows), :].astype(BF16)
            return carry

        lax.fori_loop(0, D_MODEL // rows, cast, 0)

    @pl.when(used)
    def _():
        xb = _unpack_rows(xs_ref[...])
        gu = jnp.dot(xb, wgu_bf[...], preferred_element_type=F32) + bgu_ref[...]
        gate = jnp.minimum(gu[:, :EXPERT_FF], SWIGLU_LIMIT)
        up = jnp.clip(gu[:, EXPERT_FF:], -SWIGLU_LIMIT, SWIGLU_LIMIT)
        h = (up + 1.0) * gate * _sigmoid(SWIGLU_ALPHA * gate)
        y = jnp.dot(h.astype(BF16), wd_bf[...], preferred_element_type=F32) + bd_ref[...]
        y_ref[...] = _pack_rows(y.astype(BF16).astype(F32))


def expert_mlp(xs, blk_expert, n_used, layer, w_gu, b_gu, w_down, b_down):
    padded_rows = xs.shape[0]
    blk = MOE_BLOCK
    rows_in = lambda i, be, nu: (jnp.minimum(i, nu[0] - 1), 0)
    per_expert = lambda i, be, nu: (layer, be[i], 0, 0)
    return pl.pallas_call(
        _expert_kernel,
        out_shape=jax.ShapeDtypeStruct((padded_rows, PACKED), U32),
        grid_spec=pltpu.PrefetchScalarGridSpec(
            num_scalar_prefetch=2,
            grid=(padded_rows // blk,),
            in_specs=[pl.BlockSpec((blk, PACKED), rows_in),
                      pl.BlockSpec((None, None, D_MODEL, 2 * EXPERT_FF), per_expert),
                      pl.BlockSpec((None, None, 1, 2 * EXPERT_FF), per_expert),
                      pl.BlockSpec((None, None, EXPERT_FF, D_MODEL), per_expert),
                      pl.BlockSpec((None, None, 1, D_MODEL), per_expert)],
            out_specs=pl.BlockSpec((blk, PACKED), lambda i, be, nu: (i, 0)),
            scratch_shapes=[pltpu.VMEM((D_MODEL, 2 * EXPERT_FF), BF16), pltpu.VMEM((EXPERT_FF, D_MODEL), BF16)]),
        compiler_params=_cparams(1),
        name="expert_mlp",
    )(blk_expert, n_used, xs, w_gu, b_gu, w_down, b_down)


def _combine_kernel(dst_ref, nslot_ref, x_ref, info_ref, g_ref, b_ref, y_ref, out_ref, slots_ref, sems):
    i = pl.program_id(0)
    tm = x_ref.shape[0]
    cur = i % 2

    def fetch(tile, side):
        return lambda s: pltpu.make_async_copy(
            y_ref.at[pl.ds(pl.multiple_of(dst_ref[tile * N_SLOTS + s], SUBLANES), CHUNK)],
            slots_ref.at[side, pl.ds(pl.multiple_of(s * CHUNK, CHUNK), CHUNK)], sems.at[side])

    @pl.when(i == 0)
    def _():
        slots_ref[...] = jnp.zeros_like(slots_ref)
        _start_all(nslot_ref[0], fetch(0, 0))

    @pl.when(i + 1 < pl.num_programs(0))
    def _():
        _start_all(nslot_ref[i + 1], fetch(i + 1, 1 - cur))

    _wait_all(nslot_ref[i], fetch(i, cur))

    info = info_ref[...]
    ffn = jnp.zeros((tm, D_MODEL), F32)
    for c in range(SLOT_ROWS // tm):
        pos = (lax.broadcasted_iota(jnp.int32, (tm, tm), 1) + c * tm).astype(F32)
        w = jnp.zeros((tm, tm), F32)
        for k in range(TOP_K):
            w = w + jnp.where(pos == info[:, INFO_SLOT + k:INFO_SLOT + k + 1],
                              info[:, INFO_GATE + k:INFO_GATE + k + 1], 0.0)
        w_hi = w.astype(BF16)
        w_lo = (w - w_hi.astype(F32)).astype(BF16)
        rows = _unpack_rows(slots_ref[cur, c * tm:(c + 1) * tm, :])
        ffn = ffn + jnp.dot(w_hi, rows, preferred_element_type=F32) + jnp.dot(w_lo, rows, preferred_element_type=F32)
    out_ref[...] = _layer_norm(ALPHA * x_ref[...] + ffn, g_ref[...], b_ref[...])


def combine(x1, info, dst, nslot, y, ln_g, ln_b):
    n = x1.shape[0]
    tm = ROUTE_ROWS
    row = lambda i, *_: (i, 0)
    full = lambda i, *_: (0, 0)
    return pl.pallas_call(
        _combine_kernel,
        out_shape=jax.ShapeDtypeStruct((n, D_MODEL), F32),
        grid_spec=pltpu.PrefetchScalarGridSpec(
            num_scalar_prefetch=2,
            grid=(n // tm,),
            in_specs=[pl.BlockSpec((tm, D_MODEL), row),
                      pl.BlockSpec((tm, LANES), row),
                      pl.BlockSpec((1, D_MODEL), full),
                      pl.BlockSpec((1, D_MODEL), full),
                      pl.BlockSpec(memory_space=pl.ANY)],
            out_specs=pl.BlockSpec((tm, D_MODEL), row),
            scratch_shapes=[pltpu.VMEM((2, SLOT_ROWS, PACKED), U32), pltpu.SemaphoreType.DMA((2,))]),
        compiler_params=_cparams(1),
        name="moe_combine",
    )(dst, nslot, x1, info, ln_g.astype(F32)[None, :], ln_b.astype(F32)[None, :], y)


def moe_sublayer(x1, logits, layer, w_gu, b_gu, w_down, b_down, ln_g, ln_b):
    n = x1.shape[0]
    blk = MOE_BLOCK
    n_tiles = n // ROUTE_ROWS
    padded_rows = n * TOP_K + (SUBLANES - 1) * N_EXPERTS * n_tiles + N_EXPERTS * (blk + CHUNK)
    padded_rows = -(-padded_rows // blk) * blk
    info, tab, cnt = route(logits)

    i32 = lambda t: t.astype(jnp.int32)
    tab = tab.reshape(n_tiles, SUBLANES, LANES)[:, :, :N_EXPERTS]
    run, base, chunks = i32(tab[:, TAB_RUN]), i32(tab[:, TAB_BASE]), i32(tab[:, TAB_CHUNKS])
    counts = i32(cnt[0, :N_EXPERTS])
    padded = (counts + CHUNK + blk - 1) // blk * blk
    seg_end = jnp.cumsum(padded)
    seg_start = seg_end - padded
    slot = jnp.arange(N_SLOTS, dtype=jnp.int32)
    ends = base + chunks
    slot_expert = jnp.minimum(jnp.sum(i32(ends[:, None, :] <= slot[None, :, None]), -1), N_EXPERTS - 1)
    onehot = i32(slot_expert[:, :, None] == jnp.arange(N_EXPERTS, dtype=jnp.int32))
    pick = lambda v: jnp.sum(onehot * v[:, None, :], -1)
    dst = (pick(seg_start[None, :] + run) + (slot[None, :] - pick(base)) * CHUNK).reshape(-1)
    nslot = ends[:, N_EXPERTS - 1]
    total = seg_end[N_EXPERTS - 1]
    n_blocks = padded_rows // blk
    blk_row = jnp.arange(n_blocks, dtype=jnp.int32) * blk
    blk_expert = jnp.minimum(jnp.sum(i32(seg_end[None, :] <= blk_row[:, None]), -1), N_EXPERTS - 1)
    n_used = (total // blk).reshape(1)
    tail0 = (seg_start + counts) // blk * blk
    tail1 = jnp.where(tail0 + blk < seg_end, tail0 + blk, -1)
    zdst = jnp.concatenate([tail0, tail1, jnp.where(blk_row >= total, blk_row, -1)])

    xs = dispatch(x1, info, dst, nslot, zdst, padded_rows)
    y = expert_mlp(xs, blk_expert, n_used, layer, w_gu.astype(F32), b_gu.astype(F32)[:, :, None, :],
                   w_down.astype(F32), b_down.astype(F32)[:, :, None, :])
    return combine(x1, info, dst, nslot, y, ln_g, ln_b)


def kernel(x, w_in, ssd_conv_w, ssd_conv_b, ssd_dt_bias, ssd_a_log, ssd_d, ssd_norm_g, sc_conv_w, w_out,
           ln1_g, ln1_b, router_w, router_b, exp_w_gu, exp_b_gu, exp_w_down, exp_b_down, ln2_g, ln2_b):
    batch, seq, _ = x.shape
    n = batch * seq
    cos_t, sin_t = rope_tables(seq)
    xf = x.reshape(n, D_MODEL).astype(F32)
    a_end = 3 * ATTN_WIDTH
    s_end = a_end + SSD_WIDTH + SSD_CONV_DIM
    d_end = s_end + SSD_HEADS
    for layer in range(w_in.shape[0]):
        w = w_in[layer]
        w_attn = w[:, :a_end].astype(BF16)
        w_ssd = w[:, a_end:s_end].astype(BF16)
        w_dt = jnp.pad(w[:, s_end:d_end], ((0, 0), (0, LANES - SSD_HEADS))).astype(BF16)
        w_sc = w[:, d_end:].astype(BF16)
        (q1, k1, v1, q4, k4, v4, q16, k16, v16, z, xbc, dt_raw, sb, u) = in_projection(
            xf, w_attn, w_ssd, w_dt, w_sc, cos_t, sin_t, batch, seq)
        qkv = ((q1.reshape(batch, 1, seq, ATTN_WIDTH), k1.reshape(batch, 1, seq, ATTN_WIDTH),
                v1.reshape(batch, 1, seq, ATTN_WIDTH)), (q4, k4, v4), (q16, k16, v16))
        attn = [dilated_attention(*t, batch, seq, d) for t, d in zip(qkv, DILATIONS)]
        o_list = [attn[0][0].reshape(n, ATTN_WIDTH), attn[1][0], attn[2][0]]
        lse_list = [attn[0][1].reshape(n, LANES), attn[1][1], attn[2][1]]
        ssd = ssd_mixer(z, xbc, dt_raw, ssd_conv_w[layer], ssd_conv_b[layer], ssd_dt_bias[layer],
                        ssd_a_log[layer], ssd_d[layer], ssd_norm_g[layer], batch, seq)
        x1, logits = out_projection(o_list, lse_list, ssd, sb, u, xf,
                                    sc_conv_w[layer], w_out[layer], ln1_g[layer], ln1_b[layer],
                                    router_w[layer], router_b[layer], seq)
        xf = moe_sublayer(x1, logits, layer, exp_w_gu, exp_b_gu, exp_w_down, exp_b_down,
                          ln2_g[layer], ln2_b[layer])
    return xf.reshape(batch, seq, D_MODEL).astype(x.dtype)
```

```python
import functools

import jax
import jax.numpy as jnp
from jax import lax
from jax.experimental import pallas as pl
from jax.experimental.pallas import tpu as pltpu

F32 = jnp.float32
BF16 = jnp.bfloat16
U32 = jnp.uint32

D_MODEL = 1024
HEAD_DIM = 64
ATTN_HEADS = 8
ATTN_WIDTH = 512
DILATIONS = (1, 4, 16)
ATTN_SPAN = 128
ATTN_BLOCK = 128
ATTN_Q_ROWS = 512
ROPE_THETA = 10000.0

SSD_HEADS = 8
SSD_WIDTH = 512
SSD_GROUPS = 2
SSD_STATE = 128
SSD_CONV = 4
SSD_CHUNK = 128
SSD_CONV_DIM = 1024

SC_WIDTH = 512
SC_CONV = 3
MIX_WIDTH = 1536

N_EXPERTS = 32
TOP_K = 4
EXPERT_FF = 1024
SWIGLU_ALPHA = 1.702
SWIGLU_LIMIT = 7.0

DEPTH = 2
ALPHA = (2.0 * DEPTH) ** 0.25
LN_EPS = 1e-5
RMS_EPS = 1e-5

LANES = 128
SUBLANES = 8
HALO = 16
NEG = -1e30
VMEM_LIMIT = 56 * 1024 * 1024

PROJ_ROWS = 512
ROUTE_ROWS = 256
MOE_BLOCK = 512
CHUNK = 16
N_SLOTS = N_EXPERTS + ROUTE_ROWS * TOP_K // CHUNK
SLOT_ROWS = N_SLOTS * CHUNK
PACKED = D_MODEL // 2
HI_MASK = 0xFFFF0000


def _cparams(n_axes):
    return pltpu.CompilerParams(dimension_semantics=("arbitrary",) * n_axes,
                                vmem_limit_bytes=VMEM_LIMIT)


def _sigmoid(x):
    return 1.0 / (1.0 + jnp.exp(-x))


def _pair_cols(lo, tile, h):
    return jnp.where(lo, tile[:, h:h + 1], tile[:, h + 1:h + 2])


def _pack_rows(v):
    return pltpu.bitcast(v[:, :PACKED], U32) | (pltpu.bitcast(v[:, PACKED:], U32) >> 16)


def _unpack_rows(w):
    a = pltpu.bitcast(w & U32(HI_MASK), F32).astype(BF16)
    b = pltpu.bitcast(w << 16, F32).astype(BF16)
    return jnp.concatenate([a, b], axis=1)


def _rope_kernel(inv_ref, cos_ref, sin_ref):
    rows = cos_ref.shape[0]
    base = pl.program_id(0) * rows
    pos = (lax.broadcasted_iota(jnp.int32, cos_ref.shape, 0) + base).astype(F32)
    lane = lax.broadcasted_iota(jnp.int32, cos_ref.shape, 1)
    ang = pos * inv_ref[...]
    cos_ref[...] = jnp.cos(ang)
    sin_ref[...] = jnp.where((lane & 32) == 0, -jnp.sin(ang), jnp.sin(ang))


def rope_tables(seq):
    half = HEAD_DIM // 2
    inv = ROPE_THETA ** (-jnp.arange(half, dtype=F32) / half)
    inv = jnp.tile(inv, LANES // half)[None, :]
    rows = 512
    return pl.pallas_call(
        _rope_kernel,
        out_shape=(jax.ShapeDtypeStruct((seq, LANES), F32),) * 2,
        grid=(seq // rows,),
        in_specs=[pl.BlockSpec((1, LANES), lambda i: (0, 0))],
        out_specs=(pl.BlockSpec((rows, LANES), lambda i: (i, 0)),) * 2,
        compiler_params=_cparams(1),
        name="rope_tables",
    )(inv)


def _inproj_kernel(x_ref, wa_ref, ws_ref, wdt_ref, wc_ref, cos_ref, sin_ref,
                   q1_ref, k1_ref, v1_ref, q4_ref, k4_ref, v4_ref, q16_ref, k16_ref, v16_ref,
                   z_ref, xbc_ref, dt_ref, sb_ref, u_ref, perm_ref):
    tm = x_ref.shape[0]
    xb = x_ref[...].astype(BF16)
    cos = cos_ref[...]
    sin = sin_ref[...]
    lane = lax.broadcasted_iota(jnp.int32, cos.shape, 1)
    first_half = (lane & 32) == 0
    n_slab = ATTN_WIDTH // LANES

    def proj(w_ref, lo, hi):
        return jnp.dot(xb, w_ref[:, lo:hi], preferred_element_type=F32)

    def rope(t):
        rot = jnp.where(first_half, pltpu.roll(t, LANES - 32, 1), pltpu.roll(t, 32, 1))
        return t * cos + rot * sin

    def emit(slabs, nat_ref, strided_refs):
        for j, t in enumerate(slabs):
            perm_ref[j] = t
            nat_ref[:, j * LANES:(j + 1) * LANES] = t.astype(BF16)
        for d, ref in strided_refs:
            for r in range(d):
                for j in range(n_slab):
                    ref[r, :, j * LANES:(j + 1) * LANES] = (
                        perm_ref[j, pl.ds(r, tm // d, stride=d), :].astype(BF16))

    w = ATTN_WIDTH
    q = proj(wa_ref, 0, w)
    emit([rope(q[:, j * LANES:(j + 1) * LANES]) * (HEAD_DIM ** -0.5) for j in range(n_slab)],
         q1_ref, ((4, q4_ref), (16, q16_ref)))
    k = proj(wa_ref, w, 2 * w)
    emit([rope(k[:, j * LANES:(j + 1) * LANES]) for j in range(n_slab)], k1_ref, ((4, k4_ref), (16, k16_ref)))
    v = proj(wa_ref, 2 * w, 3 * w)
    emit([v[:, j * LANES:(j + 1) * LANES] for j in range(n_slab)], v1_ref, ((4, v4_ref), (16, v16_ref)))
    z_ref[...] = proj(ws_ref, 0, SSD_WIDTH).astype(BF16)
    xbc_ref[...] = proj(ws_ref, SSD_WIDTH, SSD_WIDTH + SSD_CONV_DIM).astype(BF16)
    dt_ref[...] = jnp.dot(xb, wdt_ref[...], preferred_element_type=F32)
    sb_ref[...] = proj(wc_ref, 0, SC_WIDTH).astype(BF16)
    c = proj(wc_ref, SC_WIDTH, 2 * SC_WIDTH)
    h = proj(wc_ref, 2 * SC_WIDTH, 3 * SC_WIDTH)
    u_ref[...] = (c * h).astype(BF16)


def in_projection(x, w_attn, w_ssd, w_dt, w_sc, cos_t, sin_t, batch, seq):
    n = x.shape[0]
    tm = PROJ_ROWS
    tps = seq // tm
    row = lambda i: (i, 0)
    full = lambda i: (0, 0)
    tab = lambda i: (i % tps, 0)
    strided = lambda i: (i // tps, 0, i % tps, 0)
    w = ATTN_WIDTH
    nat = jax.ShapeDtypeStruct((n, w), BF16)
    nat_spec = pl.BlockSpec((tm, w), row)
    shapes, specs = [nat] * 3, [nat_spec] * 3
    for d in DILATIONS[1:]:
        shapes += [jax.ShapeDtypeStruct((batch, d, seq // d, w), BF16)] * 3
        specs += [pl.BlockSpec((None, d, tm // d, w), strided)] * 3
    widths = (SSD_WIDTH, SSD_CONV_DIM, LANES, SC_WIDTH, SC_WIDTH)
    dtypes = (BF16, BF16, F32, BF16, BF16)
    shapes += [jax.ShapeDtypeStruct((n, wd), dt) for wd, dt in zip(widths, dtypes)]
    specs += [pl.BlockSpec((tm, wd), row) for wd in widths]
    return pl.pallas_call(
        _inproj_kernel,
        out_shape=tuple(shapes),
        grid=(n // tm,),
        in_specs=[pl.BlockSpec((tm, D_MODEL), row),
                  pl.BlockSpec(w_attn.shape, full),
                  pl.BlockSpec(w_ssd.shape, full),
                  pl.BlockSpec(w_dt.shape, full),
                  pl.BlockSpec(w_sc.shape, full),
                  pl.BlockSpec((tm, LANES), tab),
                  pl.BlockSpec((tm, LANES), tab)],
        out_specs=tuple(specs),
        scratch_shapes=[pltpu.VMEM((w // LANES, tm, LANES), F32)],
        compiler_params=_cparams(1),
        name="in_projection",
    )(x, w_attn, w_ssd, w_dt, w_sc, cos_t, sin_t)


def _attn_kernel(q_ref, kc_ref, kp_ref, vc_ref, vp_ref, o_ref, lse_ref):
    blk = ATTN_BLOCK
    n = pl.program_id(2)
    lane = lax.broadcasted_iota(jnp.int32, (blk, LANES), 1)
    lo = lane < HEAD_DIM
    qi = lax.broadcasted_iota(jnp.int32, (2 * blk, 2 * blk), 0) & (blk - 1)
    kj = lax.broadcasted_iota(jnp.int32, (2 * blk, 2 * blk), 1)
    dist = qi + blk - kj
    in_band = jnp.where(dist >= 0, jnp.where(dist <= ATTN_SPAN, 1, 0), 0)
    band = in_band > 0
    band_first = (in_band * jnp.where(kj >= blk, 1, jnp.where(n > 0, 1, 0))) > 0
    for j in range(q_ref.shape[0] // blk):
        rows = slice(j * blk, (j + 1) * blk)
        prev = slice((j - 1) * blk, j * blk)
        valid = band_first if j == 0 else band
        lse_tile = jnp.zeros((blk, LANES), F32)
        for hp in range(ATTN_WIDTH // LANES):
            sl = slice(hp * LANES, (hp + 1) * LANES)
            q2 = q_ref[rows, sl].astype(F32)
            qq = jnp.concatenate([jnp.where(lo, q2, 0.0), jnp.where(lo, 0.0, q2)], axis=0).astype(BF16)
            k_prev = kp_ref[:, sl] if j == 0 else kc_ref[prev, sl]
            v_prev = vp_ref[:, sl] if j == 0 else vc_ref[prev, sl]
            k2 = jnp.concatenate([k_prev, kc_ref[rows, sl]], axis=0)
            v2 = jnp.concatenate([v_prev, vc_ref[rows, sl]], axis=0)
            s = lax.dot_general(qq, k2, (((1,), (1,)), ((), ())), preferred_element_type=F32)
            s = jnp.where(valid, s, NEG)
            m = jnp.max(s, axis=1, keepdims=True)
            p = jnp.exp(s - m)
            l = jnp.sum(p, axis=1, keepdims=True)
            pv = jnp.dot(p.astype(BF16), v2, preferred_element_type=F32)
            o = pv / l
            o_ref[rows, sl] = jnp.where(lo, o[:blk], o[blk:]).astype(BF16)
            lse = m + jnp.log(l)
            lse_tile = jnp.where(lane == 2 * hp, lse[:blk],
                                 jnp.where(lane == 2 * hp + 1, lse[blk:], lse_tile))
        lse_ref[rows, :] = lse_tile


def dilated_attention(q, k, v, batch, seq, dilation):
    length = seq // dilation
    w = ATTN_WIDTH
    blk = ATTN_BLOCK
    tq = min(ATTN_Q_ROWS, length)
    cur = lambda b, r, n: (b, r, n, 0)
    prev = lambda b, r, n: (b, r, jnp.maximum(n * (tq // blk) - 1, 0), 0)
    big = pl.BlockSpec((None, None, tq, w), cur)
    small = pl.BlockSpec((None, None, blk, w), prev)
    return pl.pallas_call(
        _attn_kernel,
        out_shape=(jax.ShapeDtypeStruct((batch, dilation, length, w), BF16),
                   jax.ShapeDtypeStruct((batch, dilation, length, LANES), F32)),
        grid=(batch, dilation, length // tq),
        in_specs=[big, big, small, big, small],
        out_specs=(big, pl.BlockSpec((None, None, tq, LANES), cur)),
        compiler_params=_cparams(3),
        name=f"dilated_attention_d{dilation}",
    )(q, k, k, v, v)


def _ssd_kernel(z_ref, xbc_ref, halo_ref, dt_ref, shift_ref, cw_ref, cb_ref, dtb_ref, alog_ref, dskip_ref,
                g_ref, y_ref, st_ref):
    ch = SSD_CHUNK
    c = pl.program_id(1)

    @pl.when(c == 0)
    def _():
        st_ref[...] = jnp.zeros_like(st_ref)

    halo = halo_ref[...]
    ext = jnp.concatenate([jnp.where(c > 0, halo, jnp.zeros_like(halo)), xbc_ref[...]], axis=0)
    conv = cb_ref[...] + cw_ref[SSD_CONV - 1:SSD_CONV, :] * xbc_ref[...].astype(F32)
    for t in range(SSD_CONV - 1):
        conv = conv + cw_ref[t:t + 1, :] * jnp.dot(shift_ref[t], ext, preferred_element_type=F32)
    xbc = conv * _sigmoid(conv)
    xh = xbc[:, :SSD_WIDTH]
    bm = xbc[:, SSD_WIDTH:SSD_WIDTH + SSD_GROUPS * SSD_STATE]
    cm = xbc[:, SSD_WIDTH + SSD_GROUPS * SSD_STATE:]

    lane = lax.broadcasted_iota(jnp.int32, (ch, LANES), 1)
    row = lax.broadcasted_iota(jnp.int32, (ch, LANES), 0)
    lo = lane < HEAD_DIM
    lo_row = lo[0:1, :]
    head_lane = lane < SSD_HEADS

    xdt = dt_ref[...] + dtb_ref[...]
    dtv = jnp.maximum(xdt, 0.0) + jnp.log(1.0 + jnp.exp(-jnp.abs(xdt)))
    a = jnp.where(head_lane[0:1, :], -jnp.exp(alog_ref[...]), 0.0)
    acum = dtv * a
    shift = 1
    while shift < ch:
        acum = acum + jnp.where(row >= shift, pltpu.roll(acum, shift, 0), 0.0)
        shift *= 2
    acum_t = acum.T
    tot = acum[ch - 1:ch, :]
    e_in = jnp.exp(acum)
    e_out = jnp.exp(tot - acum)
    e_tot = jnp.exp(tot)
    causal = row >= lane

    ys = []
    for g in range(SSD_GROUPS):
        bg = bm[:, g * SSD_STATE:(g + 1) * SSD_STATE]
        cg = cm[:, g * SSD_STATE:(g + 1) * SSD_STATE].astype(BF16)
        cb = lax.dot_general(cg, bg.astype(BF16), (((1,), (1,)), ((), ())),
                             preferred_element_type=F32)
        bg_t = bg.T.astype(BF16)
        for pp in range(2):
            pair = 2 * g + pp
            ha = 2 * pair
            sl = slice(pair * LANES, (pair + 1) * LANES)
            xp = xh[:, sl]
            xdt_pair = xp * _pair_cols(lo, dtv, ha)
            xb16 = xdt_pair.astype(BF16)
            diag = []
            for h in (ha, ha + 1):
                seg = acum[:, h:h + 1] - acum_t[h:h + 1, :]
                decay = jnp.exp(jnp.where(causal, seg, NEG))
                diag.append(jnp.dot((cb * decay).astype(BF16), xb16, preferred_element_type=F32))
            y = jnp.where(lo, diag[0], diag[1])
            state = st_ref[pair]
            y = y + jnp.dot(cg, state.astype(BF16), preferred_element_type=F32) * _pair_cols(lo, e_in, ha)
            xout = (xdt_pair * _pair_cols(lo, e_out, ha)).astype(BF16)
            st_ref[pair] = (state * _pair_cols(lo_row, e_tot, ha)
                            + jnp.dot(bg_t, xout, preferred_element_type=F32))
            ys.append(y + dskip_ref[:, sl] * xp)
    y = jnp.concatenate(ys, axis=1)
    zz = z_ref[...].astype(F32)
    y = y * (zz * _sigmoid(zz))
    gw = SSD_WIDTH // SSD_GROUPS
    outs = []
    for g in range(SSD_GROUPS):
        yg = y[:, g * gw:(g + 1) * gw]
        ms = jnp.mean(yg * yg, axis=1, keepdims=True)
        outs.append(yg * lax.rsqrt(ms + RMS_EPS))
    y_ref[...] = (jnp.concatenate(outs, axis=1) * g_ref[...]).astype(BF16)


def ssd_mixer(z, xbc, dt_raw, conv_w, conv_b, dt_bias, a_log, d_skip, norm_g, batch, seq):
    n = batch * seq
    ch = SSD_CHUNK
    nc = seq // ch
    pad = lambda t: jnp.pad(t.astype(F32), (0, LANES - t.shape[0]))[None, :]
    cur = lambda b, c: (b * nc + c, 0)
    halo = lambda b, c: (jnp.maximum((b * nc + c) * (ch // HALO) - 1, 0), 0)
    full = lambda b, c: (0, 0)
    r_idx = lax.broadcasted_iota(jnp.int32, (SSD_CONV - 1, ch, HALO + ch), 1)
    j_idx = lax.broadcasted_iota(jnp.int32, (SSD_CONV - 1, ch, HALO + ch), 2)
    t_idx = lax.broadcasted_iota(jnp.int32, (SSD_CONV - 1, ch, HALO + ch), 0)
    shift = jnp.where(j_idx == r_idx + HALO - (SSD_CONV - 1) + t_idx, 1.0, 0.0).astype(BF16)
    return pl.pallas_call(
        _ssd_kernel,
        out_shape=jax.ShapeDtypeStruct((n, SSD_WIDTH), BF16),
        grid=(batch, nc),
        in_specs=[pl.BlockSpec((ch, SSD_WIDTH), cur),
                  pl.BlockSpec((ch, SSD_CONV_DIM), cur),
                  pl.BlockSpec((HALO, SSD_CONV_DIM), halo),
                  pl.BlockSpec((ch, LANES), cur),
                  pl.BlockSpec(shift.shape, lambda b, c: (0, 0, 0)),
                  pl.BlockSpec((SSD_CONV, SSD_CONV_DIM), full),
                  pl.BlockSpec((1, SSD_CONV_DIM), full),
                  pl.BlockSpec((1, LANES), full),
                  pl.BlockSpec((1, LANES), full),
                  pl.BlockSpec((1, SSD_WIDTH), full),
                  pl.BlockSpec((1, SSD_WIDTH), full)],
        out_specs=pl.BlockSpec((ch, SSD_WIDTH), cur),
        scratch_shapes=[pltpu.VMEM((SSD_HEADS // 2, SSD_STATE, LANES), F32)],
        compiler_params=_cparams(2),
        name="ssd_mixer",
    )(z, xbc, xbc, dt_raw, shift, conv_w.astype(F32), conv_b.astype(F32)[None, :], pad(dt_bias), pad(a_log),
      jnp.repeat(d_skip.astype(F32), HEAD_DIM)[None, :], norm_g.astype(F32)[None, :])


def _layer_norm(r, g, b):
    mu = jnp.mean(r, axis=1, keepdims=True)
    d = r - mu
    var = jnp.mean(d * d, axis=1, keepdims=True)
    return d * lax.rsqrt(var + LN_EPS) * g + b


def _outproj_kernel(tiles_per_seq, o1_ref, o4_ref, o16_ref, l1_ref, l4_ref, l16_ref, ssd_ref, sb_ref,
                    u_ref, uh_ref, x_ref, cw_ref, wout_ref, g_ref, b_ref, spread_ref, rw_ref,
                    rb_ref, x1_ref, logit_ref, ext_ref, operm_ref, lperm_ref):
    tm = x_ref.shape[0]
    i = pl.program_id(0)
    n_slab = ATTN_WIDTH // LANES

    for idx, (d, o_ref, l_ref) in enumerate(((4, o4_ref, l4_ref), (16, o16_ref, l16_ref))):
        for r in range(d):
            rows = pl.ds(r, tm // d, stride=d)
            lperm_ref[idx, rows, :] = l_ref[r]
            for j in range(n_slab):
                operm_ref[idx * n_slab + j, rows, :] = o_ref[r, :, j * LANES:(j + 1) * LANES].astype(F32)

    lses = (l1_ref[...], lperm_ref[0], lperm_ref[1])
    top = jnp.maximum(jnp.maximum(lses[0], lses[1]), lses[2])
    es = [jnp.exp(l - top) for l in lses]
    den = es[0] + es[1] + es[2]
    wide = []
    for e in es:
        w = e / den
        w_hi = w.astype(BF16)
        w_lo = (w - w_hi.astype(F32)).astype(BF16)
        wide.append(jnp.dot(w_hi, spread_ref[...], preferred_element_type=F32)
                    + jnp.dot(w_lo, spread_ref[...], preferred_element_type=F32))
    attn = []
    for hp in range(n_slab):
        sl = slice(hp * LANES, (hp + 1) * LANES)
        acc = wide[0][:, sl] * o1_ref[:, sl].astype(F32)
        acc = acc + wide[1][:, sl] * operm_ref[hp]
        acc = acc + wide[2][:, sl] * operm_ref[n_slab + hp]
        attn.append(acc.astype(BF16))

    seq_start = (i % tiles_per_seq) == 0
    ext_ref[0:HALO, :] = jnp.where(seq_start, 0.0, uh_ref[...].astype(F32))
    ext_ref[HALO:HALO + tm, :] = u_ref[...].astype(F32)
    conv = jnp.zeros((tm, SC_WIDTH), F32)
    for t in range(SC_CONV):
        off = HALO - (SC_CONV - 1) + t
        conv = conv + cw_ref[t:t + 1, :] * ext_ref[off:off + tm, :]
    gated = (sb_ref[...].astype(F32) * conv).astype(BF16)

    mixed = jnp.concatenate(attn + [ssd_ref[...], gated], axis=1)
    mix = jnp.dot(mixed, wout_ref[...], preferred_element_type=F32)
    x1 = _layer_norm(ALPHA * x_ref[...] + mix, g_ref[...], b_ref[...])
    x1_ref[...] = x1

    xh = x1.astype(BF16)
    xm = (x1 - xh.astype(F32)).astype(BF16)
    ph = jnp.dot(xh, rw_ref[...], preferred_element_type=F32)
    pm = jnp.dot(xm, rw_ref[...], preferred_element_type=F32)
    logit_ref[...] = pm + pltpu.roll(ph, LANES - N_EXPERTS, 1) + ph + rb_ref[...]


def out_projection(o_list, lse_list, ssd, sb, u, x, sc_conv_w, w_out, ln_g, ln_b, router_w, router_b, seq):
    n = x.shape[0]
    tm = PROJ_ROWS
    tps = seq // tm
    row = lambda i: (i, 0)
    full = lambda i: (0, 0)
    strided = lambda i: (i // tps, 0, i % tps, 0)
    halo = lambda i: (jnp.maximum(i * (tm // HALO) - 1, 0), 0)
    rw_f = router_w.astype(F32)
    rw_hi = rw_f.astype(BF16)
    rw_mid = (rw_f - rw_hi.astype(F32)).astype(BF16)
    rw = jnp.pad(jnp.concatenate([rw_hi, rw_mid], axis=1), ((0, 0), (0, LANES - 2 * N_EXPERTS)))
    rb = jnp.pad(router_b.astype(F32), (0, LANES - N_EXPERTS))[None, :]
    spread = jnp.where(lax.broadcasted_iota(jnp.int32, (LANES, ATTN_WIDTH), 0)
                       == lax.broadcasted_iota(jnp.int32, (LANES, ATTN_WIDTH), 1) // HEAD_DIM, 1.0, 0.0).astype(BF16)
    wide = lambda wd: pl.BlockSpec((tm, wd), row)
    perm = lambda d, wd: pl.BlockSpec((None, d, tm // d, wd), strided)
    const = lambda a: pl.BlockSpec(a.shape, full)
    wout = w_out.astype(BF16)
    cw = sc_conv_w.astype(F32)
    g = ln_g.astype(F32)[None, :]
    b = ln_b.astype(F32)[None, :]
    n_slab = ATTN_WIDTH // LANES
    return pl.pallas_call(
        functools.partial(_outproj_kernel, tps),
        out_shape=(jax.ShapeDtypeStruct((n, D_MODEL), F32), jax.ShapeDtypeStruct((n, LANES), F32)),
        grid=(n // tm,),
        in_specs=[wide(ATTN_WIDTH), perm(4, ATTN_WIDTH), perm(16, ATTN_WIDTH),
                  wide(LANES), perm(4, LANES), perm(16, LANES),
                  wide(SSD_WIDTH), wide(SC_WIDTH), wide(SC_WIDTH), pl.BlockSpec((HALO, SC_WIDTH), halo),
                  wide(D_MODEL), const(cw), const(wout), const(g), const(b),
                  const(spread), const(rw), const(rb)],
        out_specs=(wide(D_MODEL), wide(LANES)),
        scratch_shapes=[pltpu.VMEM((HALO + tm, SC_WIDTH), F32),
                        pltpu.VMEM((2 * n_slab, tm, LANES), F32),
                        pltpu.VMEM((2, tm, LANES), F32)],
        compiler_params=_cparams(1),
        name="out_projection",
    )(*o_list, *lse_list, ssd, sb, u, u, x, cw, wout, g, b, spread, rw, rb)


INFO_IDX, INFO_SLOT, INFO_GATE = 0, TOP_K, 2 * TOP_K
TAB_RUN, TAB_BASE, TAB_CHUNKS = 0, 1, 2


def _route_kernel(logit_ref, upper_ref, info_ref, tab_ref, cnt_ref, run_ref):
    tm = logit_ref.shape[0]
    ne = N_EXPERTS
    rec_rows = 2 * SUBLANES

    @pl.when(pl.program_id(0) == 0)
    def _():
        run_ref[...] = jnp.zeros_like(run_ref)

    work = logit_ref[...].T[:ne, :]
    eidx = lax.broadcasted_iota(jnp.int32, (ne, tm), 0).astype(F32)
    sel = jnp.zeros((ne, tm), F32)
    vals, idxs, hits = [], [], []
    for _ in range(TOP_K):
        m = jnp.max(work, axis=0, keepdims=True)
        idx = jnp.min(jnp.where(work == m, eidx, float(ne)), axis=0, keepdims=True)
        hit = eidx == idx
        work = jnp.where(hit, NEG, work)
        sel = sel + jnp.where(hit, 1.0, 0.0)
        vals.append(m)
        idxs.append(idx)
        hits.append(hit)
    es = [jnp.exp(v - vals[0]) for v in vals]
    den = es[0] + es[1] + es[2] + es[3]
    before = jnp.dot(sel.astype(BF16), upper_ref[...], preferred_element_type=F32)
    cnt = jnp.sum(sel, axis=1, keepdims=True)
    chunks = jnp.floor((cnt + (CHUNK - 1)) * (1.0 / CHUNK))
    row = lax.broadcasted_iota(jnp.int32, (ne, LANES), 0)
    chunks_w = jnp.broadcast_to(chunks, (ne, LANES))
    incl = chunks_w
    shift = 1
    while shift < ne:
        incl = incl + jnp.where(row >= shift, pltpu.roll(incl, shift, 0), 0.0)
        shift *= 2
    base = (incl - chunks_w)[:, 0:1]

    sub = lax.broadcasted_iota(jnp.int32, (rec_rows, tm), 0)
    rec = jnp.zeros((rec_rows, tm), F32)
    for k in range(TOP_K):
        rank = jnp.sum(jnp.where(hits[k], before, 0.0), axis=0, keepdims=True)
        first = jnp.sum(jnp.where(hits[k], base, 0.0), axis=0, keepdims=True)
        rec = jnp.where(sub == INFO_IDX + k, idxs[k], rec)
        rec = jnp.where(sub == INFO_SLOT + k, first * CHUNK + rank, rec)
        rec = jnp.where(sub == INFO_GATE + k, es[k] / den, rec)
    info_ref[...] = jnp.concatenate([rec, jnp.zeros((LANES - rec_rows, tm), F32)], axis=0).T

    lane = lax.broadcasted_iota(jnp.int32, (ne, LANES), 1)
    cnt8 = jnp.floor((cnt + (SUBLANES - 1)) * (1.0 / SUBLANES)) * SUBLANES
    cols = jnp.where(lane == 0, cnt8, jnp.where(lane == 1, base, jnp.where(lane == 2, chunks, 0.0)))
    rows = jnp.concatenate([cols, jnp.zeros((LANES - ne, LANES), F32)], axis=0).T
    sub8 = lax.broadcasted_iota(jnp.int32, (SUBLANES, LANES), 0)
    tab_ref[...] = jnp.where(sub8 == TAB_RUN, run_ref[...],
                             jnp.where(sub8 == TAB_BASE, rows[1:2, :],
                                       jnp.where(sub8 == TAB_CHUNKS, rows[2:3, :], 0.0)))
    run_ref[...] = run_ref[...] + rows[0:1, :]
    cnt_ref[...] = jnp.broadcast_to(run_ref[...], cnt_ref.shape)


def route(logits):
    n = logits.shape[0]
    tm = ROUTE_ROWS
    r = lax.broadcasted_iota(jnp.int32, (tm, tm), 0)
    c = lax.broadcasted_iota(jnp.int32, (tm, tm), 1)
    upper = jnp.where(r < c, 1.0, 0.0).astype(BF16)
    return pl.pallas_call(
        _route_kernel,
        out_shape=(jax.ShapeDtypeStruct((n, LANES), F32),
                   jax.ShapeDtypeStruct((n // tm * SUBLANES, LANES), F32),
                   jax.ShapeDtypeStruct((SUBLANES, LANES), F32)),
        grid=(n // tm,),
        in_specs=[pl.BlockSpec((tm, LANES), lambda i: (i, 0)),
                  pl.BlockSpec((tm, tm), lambda i: (0, 0))],
        out_specs=(pl.BlockSpec((tm, LANES), lambda i: (i, 0)),
                   pl.BlockSpec((SUBLANES, LANES), lambda i: (i, 0)),
                   pl.BlockSpec((SUBLANES, LANES), lambda i: (0, 0))),
        scratch_shapes=[pltpu.VMEM((1, LANES), F32)],
        compiler_params=_cparams(1),
        name="moe_route",
    )(logits, upper)


def _start_all(n_chunks, make_copy):
    def start(s, carry):
        make_copy(s).start()
        return carry

    lax.fori_loop(0, n_chunks, start, 0)


def _wait_all(n_chunks, make_copy):
    def wait(s, carry):
        make_copy(s).wait()
        return carry

    lax.fori_loop(0, n_chunks, wait, 0)


def _dispatch_kernel(dst_ref, nslot_ref, zdst_ref, x_ref, info_ref, buf_ref,
                     slots_ref, zero_ref, sems, zsem):
    i = pl.program_id(0)
    tm = x_ref.shape[0]
    cur = i % 2

    def put(tile, side):
        return lambda s: pltpu.make_async_copy(
            slots_ref.at[side, pl.ds(pl.multiple_of(s * CHUNK, CHUNK), CHUNK)],
            buf_ref.at[pl.ds(pl.multiple_of(dst_ref[tile * N_SLOTS + s], SUBLANES), CHUNK)], sems.at[side])

    @pl.when(i == 0)
    def _():
        zero_ref[...] = jnp.zeros_like(zero_ref)

        def zero_copy(e):
            row = pl.multiple_of(jnp.maximum(zdst_ref[e], 0), MOE_BLOCK)
            return pltpu.make_async_copy(zero_ref, buf_ref.at[pl.ds(row, MOE_BLOCK)], zsem)

        def start(e, carry):
            @pl.when(zdst_ref[e] >= 0)
            def _():
                zero_copy(e).start()
            return carry

        def wait(e, carry):
            @pl.when(zdst_ref[e] >= 0)
            def _():
                zero_copy(e).wait()
            return carry

        lax.fori_loop(0, zdst_ref.shape[0], start, 0)
        lax.fori_loop(0, zdst_ref.shape[0], wait, 0)

    xb = x_ref[...].astype(BF16)
    slot_t = info_ref[...].T[INFO_SLOT:INFO_SLOT + TOP_K, :]
    local = lax.broadcasted_iota(jnp.int32, (tm, tm), 0).astype(F32).astype(BF16)
    one = jnp.ones((tm, tm), BF16)
    zero = jnp.zeros((tm, tm), BF16)
    for c in range(SLOT_ROWS // tm):
        target = (slot_t - float(c * tm)).astype(BF16)
        onehot = zero
        for k in range(TOP_K):
            onehot = onehot + jnp.where(local == target[k:k + 1, :], one, zero)
        rows = jnp.dot(onehot, xb, preferred_element_type=F32)
        slots_ref[cur, c * tm:(c + 1) * tm, :] = _pack_rows(rows)

    @pl.when(i > 0)
    def _():
        _wait_all(nslot_ref[i - 1], put(i - 1, 1 - cur))

    _start_all(nslot_ref[i], put(i, cur))

    @pl.when(i == pl.num_programs(0) - 1)
    def _():
        _wait_all(nslot_ref[i], put(i, cur))


def dispatch(x1, info, dst, nslot, zdst, padded_rows):
    n = x1.shape[0]
    tm = ROUTE_ROWS
    row = lambda i, *_: (i, 0)
    return pl.pallas_call(
        _dispatch_kernel,
        out_shape=jax.ShapeDtypeStruct((padded_rows, PACKED), U32),
        grid_spec=pltpu.PrefetchScalarGridSpec(
            num_scalar_prefetch=3,
            grid=(n // tm,),
            in_specs=[pl.BlockSpec((tm, D_MODEL), row), pl.BlockSpec((tm, LANES), row)],
            out_specs=pl.BlockSpec(memory_space=pl.ANY),
            scratch_shapes=[pltpu.VMEM((2, SLOT_ROWS, PACKED), U32), pltpu.VMEM((MOE_BLOCK, PACKED), U32),
                            pltpu.SemaphoreType.DMA((2,)), pltpu.SemaphoreType.DMA(())]),
        compiler_params=_cparams(1),
        name="moe_dispatch",
    )(dst, nslot, zdst, x1, info)


def _expert_kernel(blk_expert_ref, n_used_ref, xs_ref, wgu_ref, bgu_ref, wd_ref, bd_ref, y_ref,
                   wgu_bf, wd_bf):
    i = pl.program_id(0)
    used = i < n_used_ref[0]

    @pl.when(jnp.logical_not(used))
    def _():
        y_ref[...] = jnp.zeros_like(y_ref)

    new_expert = jnp.logical_or(i == 0, blk_expert_ref[i] != blk_expert_ref[jnp.maximum(i - 1, 0)])

    @pl.when(jnp.logical_and(used, new_expert))
    def _():
        rows = 128

        def cast(c, carry):
            r = pl.multiple_of(c * rows, rows)
            wgu_bf[pl.ds(r, rows), :] = wgu_ref[pl.ds(r, rows), :].astype(BF16)
            wd_bf[pl.ds(r, rows), :] = wd_ref[pl.ds(r, rows), :].astype(BF16)
            return carry

        lax.fori_loop(0, D_MODEL // rows, cast, 0)

    @pl.when(used)
    def _():
        xb = _unpack_rows(xs_ref[...])
        gu = jnp.dot(xb, wgu_bf[...], preferred_element_type=F32) + bgu_ref[...]
        gate = jnp.minimum(gu[:, :EXPERT_FF], SWIGLU_LIMIT)
        up = jnp.clip(gu[:, EXPERT_FF:], -SWIGLU_LIMIT, SWIGLU_LIMIT)
        h = (up + 1.0) * gate * _sigmoid(SWIGLU_ALPHA * gate)
        y = jnp.dot(h.astype(BF16), wd_bf[...], preferred_element_type=F32) + bd_ref[...]
        y_ref[...] = _pack_rows(y.astype(BF16).astype(F32))


def expert_mlp(xs, blk_expert, n_used, layer, w_gu, b_gu, w_down, b_down):
    padded_rows = xs.shape[0]
    blk = MOE_BLOCK
    rows_in = lambda i, be, nu: (jnp.minimum(i, nu[0] - 1), 0)
    per_expert = lambda i, be, nu: (layer, be[i], 0, 0)
    return pl.pallas_call(
        _expert_kernel,
        out_shape=jax.ShapeDtypeStruct((padded_rows, PACKED), U32),
        grid_spec=pltpu.PrefetchScalarGridSpec(
            num_scalar_prefetch=2,
            grid=(padded_rows // blk,),
            in_specs=[pl.BlockSpec((blk, PACKED), rows_in),
                      pl.BlockSpec((None, None, D_MODEL, 2 * EXPERT_FF), per_expert),
                      pl.BlockSpec((None, None, 1, 2 * EXPERT_FF), per_expert),
                      pl.BlockSpec((None, None, EXPERT_FF, D_MODEL), per_expert),
                      pl.BlockSpec((None, None, 1, D_MODEL), per_expert)],
            out_specs=pl.BlockSpec((blk, PACKED), lambda i, be, nu: (i, 0)),
            scratch_shapes=[pltpu.VMEM((D_MODEL, 2 * EXPERT_FF), BF16), pltpu.VMEM((EXPERT_FF, D_MODEL), BF16)]),
        compiler_params=_cparams(1),
        name="expert_mlp",
    )(blk_expert, n_used, xs, w_gu, b_gu, w_down, b_down)


def _combine_kernel(dst_ref, nslot_ref, x_ref, info_ref, g_ref, b_ref, y_ref, out_ref, slots_ref, sems):
    i = pl.program_id(0)
    tm = x_ref.shape[0]
    cur = i % 2

    def fetch(tile, side):
        return lambda s: pltpu.make_async_copy(
            y_ref.at[pl.ds(pl.multiple_of(dst_ref[tile * N_SLOTS + s], SUBLANES), CHUNK)],
            slots_ref.at[side, pl.ds(pl.multiple_of(s * CHUNK, CHUNK), CHUNK)], sems.at[side])

    @pl.when(i == 0)
    def _():
        slots_ref[...] = jnp.zeros_like(slots_ref)
        _start_all(nslot_ref[0], fetch(0, 0))

    @pl.when(i + 1 < pl.num_programs(0))
    def _():
        _start_all(nslot_ref[i + 1], fetch(i + 1, 1 - cur))

    _wait_all(nslot_ref[i], fetch(i, cur))

    info = info_ref[...]
    ffn = jnp.zeros((tm, D_MODEL), F32)
    local = lax.broadcasted_iota(jnp.int32, (tm, tm), 1).astype(F32).astype(BF16)
    zero = jnp.zeros((tm, tm), BF16)
    gates = [info[:, INFO_GATE + k:INFO_GATE + k + 1].astype(BF16) for k in range(TOP_K)]
    for c in range(SLOT_ROWS // tm):
        w = zero
        for k in range(TOP_K):
            target = (info[:, INFO_SLOT + k:INFO_SLOT + k + 1] - float(c * tm)).astype(BF16)
            w = w + jnp.where(local == target, gates[k], zero)
        rows = _unpack_rows(slots_ref[cur, c * tm:(c + 1) * tm, :])
        ffn = ffn + jnp.dot(w, rows, preferred_element_type=F32)
    out_ref[...] = _layer_norm(ALPHA * x_ref[...] + ffn, g_ref[...], b_ref[...])


def combine(x1, info, dst, nslot, y, ln_g, ln_b):
    n = x1.shape[0]
    tm = ROUTE_ROWS
    row = lambda i, *_: (i, 0)
    full = lambda i, *_: (0, 0)
    return pl.pallas_call(
        _combine_kernel,
        out_shape=jax.ShapeDtypeStruct((n, D_MODEL), F32),
        grid_spec=pltpu.PrefetchScalarGridSpec(
            num_scalar_prefetch=2,
            grid=(n // tm,),
            in_specs=[pl.BlockSpec((tm, D_MODEL), row),
                      pl.BlockSpec((tm, LANES), row),
                      pl.BlockSpec((1, D_MODEL), full),
                      pl.BlockSpec((1, D_MODEL), full),
                      pl.BlockSpec(memory_space=pl.ANY)],
            out_specs=pl.BlockSpec((tm, D_MODEL), row),
            scratch_shapes=[pltpu.VMEM((2, SLOT_ROWS, PACKED), U32), pltpu.SemaphoreType.DMA((2,))]),
        compiler_params=_cparams(1),
        name="moe_combine",
    )(dst, nslot, x1, info, ln_g.astype(F32)[None, :], ln_b.astype(F32)[None, :], y)


def moe_sublayer(x1, logits, layer, w_gu, b_gu, w_down, b_down, ln_g, ln_b):
    n = x1.shape[0]
    blk = MOE_BLOCK
    n_tiles = n // ROUTE_ROWS
    padded_rows = n * TOP_K + (SUBLANES - 1) * N_EXPERTS * n_tiles + N_EXPERTS * (blk + CHUNK)
    padded_rows = -(-padded_rows // blk) * blk
    info, tab, cnt = route(logits)

    i32 = lambda t: t.astype(jnp.int32)
    tab = tab.reshape(n_tiles, SUBLANES, LANES)[:, :, :N_EXPERTS]
    run, base, chunks = i32(tab[:, TAB_RUN]), i32(tab[:, TAB_BASE]), i32(tab[:, TAB_CHUNKS])
    counts = i32(cnt[0, :N_EXPERTS])
    padded = (counts + CHUNK + blk - 1) // blk * blk
    seg_end = jnp.cumsum(padded)
    seg_start = seg_end - padded
    slot = jnp.arange(N_SLOTS, dtype=jnp.int32)
    ends = base + chunks
    slot_expert = jnp.minimum(jnp.sum(i32(ends[:, None, :] <= slot[None, :, None]), -1), N_EXPERTS - 1)
    onehot = i32(slot_expert[:, :, None] == jnp.arange(N_EXPERTS, dtype=jnp.int32))
    pick = lambda v: jnp.sum(onehot * v[:, None, :], -1)
    dst = (pick(seg_start[None, :] + run) + (slot[None, :] - pick(base)) * CHUNK).reshape(-1)
    nslot = ends[:, N_EXPERTS - 1]
    total = seg_end[N_EXPERTS - 1]
    n_blocks = padded_rows // blk
    blk_row = jnp.arange(n_blocks, dtype=jnp.int32) * blk
    blk_expert = jnp.minimum(jnp.sum(i32(seg_end[None, :] <= blk_row[:, None]), -1), N_EXPERTS - 1)
    n_used = (total // blk).reshape(1)
    tail0 = (seg_start + counts) // blk * blk
    tail1 = jnp.where(tail0 + blk < seg_end, tail0 + blk, -1)
    zdst = jnp.concatenate([tail0, tail1, jnp.where(blk_row >= total, blk_row, -1)])

    xs = dispatch(x1, info, dst, nslot, zdst, padded_rows)
    y = expert_mlp(xs, blk_expert, n_used, layer, w_gu.astype(F32), b_gu.astype(F32)[:, :, None, :],
                   w_down.astype(F32), b_down.astype(F32)[:, :, None, :])
    return combine(x1, info, dst, nslot, y, ln_g, ln_b)


def kernel(x, w_in, ssd_conv_w, ssd_conv_b, ssd_dt_bias, ssd_a_log, ssd_d, ssd_norm_g, sc_conv_w, w_out,
           ln1_g, ln1_b, router_w, router_b, exp_w_gu, exp_b_gu, exp_w_down, exp_b_down, ln2_g, ln2_b):
    batch, seq, _ = x.shape
    n = batch * seq
    cos_t, sin_t = rope_tables(seq)
    xf = x.reshape(n, D_MODEL).astype(F32)
    a_end = 3 * ATTN_WIDTH
    s_end = a_end + SSD_WIDTH + SSD_CONV_DIM
    d_end = s_end + SSD_HEADS
    for layer in range(w_in.shape[0]):
        w = w_in[layer]
        w_attn = w[:, :a_end].astype(BF16)
        w_ssd = w[:, a_end:s_end].astype(BF16)
        w_dt = jnp.pad(w[:, s_end:d_end], ((0, 0), (0, LANES - SSD_HEADS))).astype(BF16)
        w_sc = w[:, d_end:].astype(BF16)
        (q1, k1, v1, q4, k4, v4, q16, k16, v16, z, xbc, dt_raw, sb, u) = in_projection(
            xf, w_attn, w_ssd, w_dt, w_sc, cos_t, sin_t, batch, seq)
        qkv = ((q1.reshape(batch, 1, seq, ATTN_WIDTH), k1.reshape(batch, 1, seq, ATTN_WIDTH),
                v1.reshape(batch, 1, seq, ATTN_WIDTH)), (q4, k4, v4), (q16, k16, v16))
        attn = [dilated_attention(*t, batch, seq, d) for t, d in zip(qkv, DILATIONS)]
        o_list = [attn[0][0].reshape(n, ATTN_WIDTH), attn[1][0], attn[2][0]]
        lse_list = [attn[0][1].reshape(n, LANES), attn[1][1], attn[2][1]]
        ssd = ssd_mixer(z, xbc, dt_raw, ssd_conv_w[layer], ssd_conv_b[layer], ssd_dt_bias[layer],
                        ssd_a_log[layer], ssd_d[layer], ssd_norm_g[layer], batch, seq)
        x1, logits = out_projection(o_list, lse_list, ssd, sb, u, xf,
                                    sc_conv_w[layer], w_out[layer], ln1_g[layer], ln1_b[layer],
                                    router_w[layer], router_b[layer], seq)
        xf = moe_sublayer(x1, logits, layer, exp_w_gu, exp_b_gu, exp_w_down, exp_b_down,
                          ln2_g[layer], ln2_b[layer])
    return xf.reshape(batch, seq, D_MODEL).astype(x.dtype)
```

```python
import functools

import jax
import jax.numpy as jnp
from jax import lax
from jax.experimental import pallas as pl
from jax.experimental.pallas import tpu as pltpu

F32 = jnp.float32
BF16 = jnp.bfloat16
U32 = jnp.uint32

D_MODEL = 1024
HEAD_DIM = 64
ATTN_HEADS = 8
ATTN_WIDTH = 512
DILATIONS = (1, 4, 16)
ATTN_SPAN = 128
ATTN_BLOCK = 128
ATTN_Q_ROWS = 1024
ROPE_THETA = 10000.0

SSD_HEADS = 8
SSD_WIDTH = 512
SSD_GROUPS = 2
SSD_STATE = 128
SSD_CONV = 4
SSD_CHUNK = 128
SSD_STEP_ROWS = 512
SSD_CONV_DIM = 1024

SC_WIDTH = 512
SC_CONV = 3
MIX_WIDTH = 1536

N_EXPERTS = 32
TOP_K = 4
EXPERT_FF = 1024
SWIGLU_ALPHA = 1.702
SWIGLU_LIMIT = 7.0

DEPTH = 2
ALPHA = (2.0 * DEPTH) ** 0.25
LN_EPS = 1e-5
RMS_EPS = 1e-5

LANES = 128
SUBLANES = 8
HALO = 16
NEG = -1e30
VMEM_LIMIT = 56 * 1024 * 1024

PROJ_ROWS = 512
ROUTE_ROWS = 256
MOE_BLOCK = 512
CHUNK = 16
N_SLOTS = N_EXPERTS + ROUTE_ROWS * TOP_K // CHUNK
SLOT_ROWS = N_SLOTS * CHUNK
PACKED = D_MODEL // 2
HI_MASK = 0xFFFF0000


def _cparams(n_axes):
    return pltpu.CompilerParams(dimension_semantics=("arbitrary",) * n_axes,
                                vmem_limit_bytes=VMEM_LIMIT)


def _sigmoid(x):
    return 1.0 / (1.0 + jnp.exp(-x))


def _pair_cols(lo, tile, h):
    return jnp.where(lo, tile[:, h:h + 1], tile[:, h + 1:h + 2])


def _pack_rows(v):
    return pltpu.bitcast(v[:, :PACKED], U32) | (pltpu.bitcast(v[:, PACKED:], U32) >> 16)


def _unpack_rows(w):
    a = pltpu.bitcast(w & U32(HI_MASK), F32).astype(BF16)
    b = pltpu.bitcast(w << 16, F32).astype(BF16)
    return jnp.concatenate([a, b], axis=1)


def _rope_kernel(inv_ref, cos_ref, sin_ref):
    rows = cos_ref.shape[0]
    base = pl.program_id(0) * rows
    pos = (lax.broadcasted_iota(jnp.int32, cos_ref.shape, 0) + base).astype(F32)
    lane = lax.broadcasted_iota(jnp.int32, cos_ref.shape, 1)
    ang = pos * inv_ref[...]
    cos_ref[...] = jnp.cos(ang)
    sin_ref[...] = jnp.where((lane & 32) == 0, -jnp.sin(ang), jnp.sin(ang))


def rope_tables(seq):
    half = HEAD_DIM // 2
    inv = ROPE_THETA ** (-jnp.arange(half, dtype=F32) / half)
    inv = jnp.tile(inv, LANES // half)[None, :]
    rows = 512
    return pl.pallas_call(
        _rope_kernel,
        out_shape=(jax.ShapeDtypeStruct((seq, LANES), F32),) * 2,
        grid=(seq // rows,),
        in_specs=[pl.BlockSpec((1, LANES), lambda i: (0, 0))],
        out_specs=(pl.BlockSpec((rows, LANES), lambda i: (i, 0)),) * 2,
        compiler_params=_cparams(1),
        name="rope_tables",
    )(inv)


def _inproj_kernel(x_ref, wa_ref, ws_ref, wdt_ref, wc_ref, cos_ref, sin_ref,
                   q1_ref, k1_ref, v1_ref, q4_ref, k4_ref, v4_ref, q16_ref, k16_ref, v16_ref,
                   z_ref, xbc_ref, dt_ref, sb_ref, u_ref, perm_ref):
    tm = x_ref.shape[0]
    xb = x_ref[...].astype(BF16)
    cos = cos_ref[...]
    sin = sin_ref[...]
    lane = lax.broadcasted_iota(jnp.int32, cos.shape, 1)
    first_half = (lane & 32) == 0
    n_slab = ATTN_WIDTH // LANES

    def proj(w_ref, lo, hi):
        return jnp.dot(xb, w_ref[:, lo:hi], preferred_element_type=F32)

    def rope(t):
        rot = jnp.where(first_half, pltpu.roll(t, LANES - 32, 1), pltpu.roll(t, 32, 1))
        return t * cos + rot * sin

    def emit(slabs, nat_ref, strided_refs):
        for j, t in enumerate(slabs):
            perm_ref[j] = t
            nat_ref[:, j * LANES:(j + 1) * LANES] = t.astype(BF16)
        for d, ref in strided_refs:
            for r in range(d):
                for j in range(n_slab):
                    ref[r, :, j * LANES:(j + 1) * LANES] = (
                        perm_ref[j, pl.ds(r, tm // d, stride=d), :].astype(BF16))

    w = ATTN_WIDTH
    q = proj(wa_ref, 0, w)
    emit([rope(q[:, j * LANES:(j + 1) * LANES]) * (HEAD_DIM ** -0.5) for j in range(n_slab)],
         q1_ref, ((4, q4_ref), (16, q16_ref)))
    k = proj(wa_ref, w, 2 * w)
    emit([rope(k[:, j * LANES:(j + 1) * LANES]) for j in range(n_slab)], k1_ref, ((4, k4_ref), (16, k16_ref)))
    v = proj(wa_ref, 2 * w, 3 * w)
    emit([v[:, j * LANES:(j + 1) * LANES] for j in range(n_slab)], v1_ref, ((4, v4_ref), (16, v16_ref)))
    z_ref[...] = proj(ws_ref, 0, SSD_WIDTH).astype(BF16)
    xbc_ref[...] = proj(ws_ref, SSD_WIDTH, SSD_WIDTH + SSD_CONV_DIM).astype(BF16)
    dt_ref[...] = jnp.dot(xb, wdt_ref[...], preferred_element_type=F32)
    sb_ref[...] = proj(wc_ref, 0, SC_WIDTH).astype(BF16)
    c = proj(wc_ref, SC_WIDTH, 2 * SC_WIDTH)
    h = proj(wc_ref, 2 * SC_WIDTH, 3 * SC_WIDTH)
    u_ref[...] = (c * h).astype(BF16)


def in_projection(x, w_attn, w_ssd, w_dt, w_sc, cos_t, sin_t, batch, seq):
    n = x.shape[0]
    tm = PROJ_ROWS
    tps = seq // tm
    row = lambda i: (i, 0)
    full = lambda i: (0, 0)
    tab = lambda i: (i % tps, 0)
    strided = lambda i: (i // tps, 0, i % tps, 0)
    w = ATTN_WIDTH
    nat = jax.ShapeDtypeStruct((n, w), BF16)
    nat_spec = pl.BlockSpec((tm, w), row)
    shapes, specs = [nat] * 3, [nat_spec] * 3
    for d in DILATIONS[1:]:
        shapes += [jax.ShapeDtypeStruct((batch, d, seq // d, w), BF16)] * 3
        specs += [pl.BlockSpec((None, d, tm // d, w), strided)] * 3
    widths = (SSD_WIDTH, SSD_CONV_DIM, LANES, SC_WIDTH, SC_WIDTH)
    dtypes = (BF16, BF16, F32, BF16, BF16)
    shapes += [jax.ShapeDtypeStruct((n, wd), dt) for wd, dt in zip(widths, dtypes)]
    specs += [pl.BlockSpec((tm, wd), row) for wd in widths]
    return pl.pallas_call(
        _inproj_kernel,
        out_shape=tuple(shapes),
        grid=(n // tm,),
        in_specs=[pl.BlockSpec((tm, D_MODEL), row),
                  pl.BlockSpec(w_attn.shape, full),
                  pl.BlockSpec(w_ssd.shape, full),
                  pl.BlockSpec(w_dt.shape, full),
                  pl.BlockSpec(w_sc.shape, full),
                  pl.BlockSpec((tm, LANES), tab),
                  pl.BlockSpec((tm, LANES), tab)],
        out_specs=tuple(specs),
        scratch_shapes=[pltpu.VMEM((w // LANES, tm, LANES), F32)],
        compiler_params=_cparams(1),
        name="in_projection",
    )(x, w_attn, w_ssd, w_dt, w_sc, cos_t, sin_t)


def _attn_kernel(q_ref, kc_ref, kp_ref, vc_ref, vp_ref, o_ref, lse_ref):
    blk = ATTN_BLOCK
    n = pl.program_id(2)
    lane = lax.broadcasted_iota(jnp.int32, (blk, LANES), 1)
    lo = lane < HEAD_DIM
    qi = lax.broadcasted_iota(jnp.int32, (2 * blk, 2 * blk), 0) & (blk - 1)
    kj = lax.broadcasted_iota(jnp.int32, (2 * blk, 2 * blk), 1)
    dist = qi + blk - kj
    in_band = jnp.where(dist >= 0, jnp.where(dist <= ATTN_SPAN, 1, 0), 0)
    band = in_band > 0
    band_first = (in_band * jnp.where(kj >= blk, 1, jnp.where(n > 0, 1, 0))) > 0
    n_res, tq = q_ref.shape[0], q_ref.shape[1]
    for r, j in ((r, j) for r in range(n_res) for j in range(tq // blk)):
        rows = slice(j * blk, (j + 1) * blk)
        prev = slice((j - 1) * blk, j * blk)
        valid = band_first if j == 0 else band
        lse_tile = jnp.zeros((blk, LANES), F32)
        for hp in range(ATTN_WIDTH // LANES):
            sl = slice(hp * LANES, (hp + 1) * LANES)
            q2 = q_ref[r, rows, sl].astype(F32)
            qq = jnp.concatenate([jnp.where(lo, q2, 0.0), jnp.where(lo, 0.0, q2)], axis=0).astype(BF16)
            k_prev = kp_ref[r, :, sl] if j == 0 else kc_ref[r, prev, sl]
            v_prev = vp_ref[r, :, sl] if j == 0 else vc_ref[r, prev, sl]
            k2 = jnp.concatenate([k_prev, kc_ref[r, rows, sl]], axis=0)
            v2 = jnp.concatenate([v_prev, vc_ref[r, rows, sl]], axis=0)
            s = lax.dot_general(qq, k2, (((1,), (1,)), ((), ())), preferred_element_type=F32)
            s = jnp.where(valid, s, NEG)
            m = jnp.max(s, axis=1, keepdims=True)
            p = jnp.exp(s - m)
            l = jnp.sum(p, axis=1, keepdims=True)
            pv = jnp.dot(p.astype(BF16), v2, preferred_element_type=F32)
            o = pv / l
            o_ref[r, rows, sl] = jnp.where(lo, o[:blk], o[blk:]).astype(BF16)
            lse = m + jnp.log(l)
            lse_tile = jnp.where(lane == 2 * hp, lse[:blk],
                                 jnp.where(lane == 2 * hp + 1, lse[blk:], lse_tile))
        lse_ref[r, rows, :] = lse_tile


def dilated_attention(q, k, v, batch, seq, dilation):
    length = seq // dilation
    w = ATTN_WIDTH
    blk = ATTN_BLOCK
    tq = min(ATTN_Q_ROWS, length)
    n_res = ATTN_Q_ROWS // tq
    cur = lambda b, r, n: (b, r, n, 0)
    prev = lambda b, r, n: (b, r, jnp.maximum(n * (tq // blk) - 1, 0), 0)
    big = pl.BlockSpec((None, n_res, tq, w), cur)
    small = pl.BlockSpec((None, n_res, blk, w), prev)
    return pl.pallas_call(
        _attn_kernel,
        out_shape=(jax.ShapeDtypeStruct((batch, dilation, length, w), BF16),
                   jax.ShapeDtypeStruct((batch, dilation, length, LANES), F32)),
        grid=(batch, dilation // n_res, length // tq),
        in_specs=[big, big, small, big, small],
        out_specs=(big, pl.BlockSpec((None, n_res, tq, LANES), cur)),
        compiler_params=_cparams(3),
        name=f"dilated_attention_d{dilation}",
    )(q, k, k, v, v)


def _ssd_kernel(z_ref, xbc_ref, halo_ref, dt_ref, shift_ref, cw_ref, cb_ref, dtb_ref, alog_ref, dskip_ref,
                g_ref, y_ref, st_ref):
    ch = SSD_CHUNK
    c = pl.program_id(1)

    @pl.when(c == 0)
    def _():
        st_ref[...] = jnp.zeros_like(st_ref)

    first_halo = halo_ref[...]
    first_halo = jnp.where(c > 0, first_halo, jnp.zeros_like(first_halo))
    for j in range(xbc_ref.shape[0] // ch):
        rows = slice(j * ch, (j + 1) * ch)
        halo = first_halo if j == 0 else xbc_ref[j * ch - HALO:j * ch, :]
        y_ref[rows, :] = _ssd_chunk(xbc_ref[rows, :], halo, z_ref[rows, :], dt_ref[rows, :], shift_ref, cw_ref,
                                    cb_ref, dtb_ref, alog_ref, dskip_ref, g_ref, st_ref)


def _ssd_chunk(xbc_in, halo, z_in, dt_in, shift_ref, cw_ref, cb_ref, dtb_ref, alog_ref, dskip_ref, g_ref, st_ref):
    ch = SSD_CHUNK
    ext = jnp.concatenate([halo, xbc_in], axis=0)
    conv = cb_ref[...] + cw_ref[SSD_CONV - 1:SSD_CONV, :] * xbc_in.astype(F32)
    for t in range(SSD_CONV - 1):
        conv = conv + cw_ref[t:t + 1, :] * jnp.dot(shift_ref[t], ext, preferred_element_type=F32)
    xbc = conv * _sigmoid(conv)
    xh = xbc[:, :SSD_WIDTH]
    bm = xbc[:, SSD_WIDTH:SSD_WIDTH + SSD_GROUPS * SSD_STATE]
    cm = xbc[:, SSD_WIDTH + SSD_GROUPS * SSD_STATE:]

    lane = lax.broadcasted_iota(jnp.int32, (ch, LANES), 1)
    row = lax.broadcasted_iota(jnp.int32, (ch, LANES), 0)
    lo = lane < HEAD_DIM
    lo_row = lo[0:1, :]
    head_lane = lane < SSD_HEADS

    xdt = dt_in + dtb_ref[...]
    dtv = jnp.maximum(xdt, 0.0) + jnp.log(1.0 + jnp.exp(-jnp.abs(xdt)))
    a = jnp.where(head_lane[0:1, :], -jnp.exp(alog_ref[...]), 0.0)
    acum = dtv * a
    shift = 1
    while shift < ch:
        acum = acum + jnp.where(row >= shift, pltpu.roll(acum, shift, 0), 0.0)
        shift *= 2
    acum_t = acum.T
    tot = acum[ch - 1:ch, :]
    e_in = jnp.exp(acum)
    e_out = jnp.exp(tot - acum)
    e_tot = jnp.exp(tot)
    causal = row >= lane

    ys = []
    for g in range(SSD_GROUPS):
        bg = bm[:, g * SSD_STATE:(g + 1) * SSD_STATE]
        cg = cm[:, g * SSD_STATE:(g + 1) * SSD_STATE].astype(BF16)
        cb = lax.dot_general(cg, bg.astype(BF16), (((1,), (1,)), ((), ())),
                             preferred_element_type=F32)
        bg_t = bg.T.astype(BF16)
        for pp in range(2):
            pair = 2 * g + pp
            ha = 2 * pair
            sl = slice(pair * LANES, (pair + 1) * LANES)
            xp = xh[:, sl]
            xdt_pair = xp * _pair_cols(lo, dtv, ha)
            xb16 = xdt_pair.astype(BF16)
            diag = []
            for h in (ha, ha + 1):
                seg = acum[:, h:h + 1] - acum_t[h:h + 1, :]
                decay = jnp.exp(jnp.where(causal, seg, NEG))
                diag.append(jnp.dot((cb * decay).astype(BF16), xb16, preferred_element_type=F32))
            y = jnp.where(lo, diag[0], diag[1])
            state = st_ref[pair]
            y = y + jnp.dot(cg, state.astype(BF16), preferred_element_type=F32) * _pair_cols(lo, e_in, ha)
            xout = (xdt_pair * _pair_cols(lo, e_out, ha)).astype(BF16)
            st_ref[pair] = (state * _pair_cols(lo_row, e_tot, ha)
                            + jnp.dot(bg_t, xout, preferred_element_type=F32))
            ys.append(y + dskip_ref[:, sl] * xp)
    y = jnp.concatenate(ys, axis=1)
    zz = z_in.astype(F32)
    y = y * (zz * _sigmoid(zz))
    gw = SSD_WIDTH // SSD_GROUPS
    outs = []
    for g in range(SSD_GROUPS):
        yg = y[:, g * gw:(g + 1) * gw]
        ms = jnp.mean(yg * yg, axis=1, keepdims=True)
        outs.append(yg * lax.rsqrt(ms + RMS_EPS))
    return (jnp.concatenate(outs, axis=1) * g_ref[...]).astype(BF16)


def ssd_mixer(z, xbc, dt_raw, conv_w, conv_b, dt_bias, a_log, d_skip, norm_g, batch, seq):
    n = batch * seq
    ch = SSD_CHUNK
    step = SSD_STEP_ROWS
    nc = seq // step
    pad = lambda t: jnp.pad(t.astype(F32), (0, LANES - t.shape[0]))[None, :]
    cur = lambda b, c: (b * nc + c, 0)
    halo = lambda b, c: (jnp.maximum((b * nc + c) * (step // HALO) - 1, 0), 0)
    full = lambda b, c: (0, 0)
    r_idx = lax.broadcasted_iota(jnp.int32, (SSD_CONV - 1, ch, HALO + ch), 1)
    j_idx = lax.broadcasted_iota(jnp.int32, (SSD_CONV - 1, ch, HALO + ch), 2)
    t_idx = lax.broadcasted_iota(jnp.int32, (SSD_CONV - 1, ch, HALO + ch), 0)
    shift = jnp.where(j_idx == r_idx + HALO - (SSD_CONV - 1) + t_idx, 1.0, 0.0).astype(BF16)
    return pl.pallas_call(
        _ssd_kernel,
        out_shape=jax.ShapeDtypeStruct((n, SSD_WIDTH), BF16),
        grid=(batch, nc),
        in_specs=[pl.BlockSpec((step, SSD_WIDTH), cur),
                  pl.BlockSpec((step, SSD_CONV_DIM), cur),
                  pl.BlockSpec((HALO, SSD_CONV_DIM), halo),
                  pl.BlockSpec((step, LANES), cur),
                  pl.BlockSpec(shift.shape, lambda b, c: (0, 0, 0)),
                  pl.BlockSpec((SSD_CONV, SSD_CONV_DIM), full),
                  pl.BlockSpec((1, SSD_CONV_DIM), full),
                  pl.BlockSpec((1, LANES), full),
                  pl.BlockSpec((1, LANES), full),
                  pl.BlockSpec((1, SSD_WIDTH), full),
                  pl.BlockSpec((1, SSD_WIDTH), full)],
        out_specs=pl.BlockSpec((step, SSD_WIDTH), cur),
        scratch_shapes=[pltpu.VMEM((SSD_HEADS // 2, SSD_STATE, LANES), F32)],
        compiler_params=_cparams(2),
        name="ssd_mixer",
    )(z, xbc, xbc, dt_raw, shift, conv_w.astype(F32), conv_b.astype(F32)[None, :], pad(dt_bias), pad(a_log),
      jnp.repeat(d_skip.astype(F32), HEAD_DIM)[None, :], norm_g.astype(F32)[None, :])


def _layer_norm(r, g, b):
    mu = jnp.mean(r, axis=1, keepdims=True)
    d = r - mu
    var = jnp.mean(d * d, axis=1, keepdims=True)
    return d * lax.rsqrt(var + LN_EPS) * g + b


def _outproj_kernel(tiles_per_seq, o1_ref, o4_ref, o16_ref, l1_ref, l4_ref, l16_ref, ssd_ref, sb_ref,
                    u_ref, uh_ref, x_ref, cw_ref, wout_ref, g_ref, b_ref, spread_ref, rw_ref,
                    rb_ref, x1_ref, logit_ref, ext_ref, operm_ref, lperm_ref):
    tm = x_ref.shape[0]
    i = pl.program_id(0)
    n_slab = ATTN_WIDTH // LANES

    for idx, (d, o_ref, l_ref) in enumerate(((4, o4_ref, l4_ref), (16, o16_ref, l16_ref))):
        for r in range(d):
            rows = pl.ds(r, tm // d, stride=d)
            lperm_ref[idx, rows, :] = l_ref[r]
            for j in range(n_slab):
                operm_ref[idx * n_slab + j, rows, :] = o_ref[r, :, j * LANES:(j + 1) * LANES].astype(F32)

    lses = (l1_ref[...], lperm_ref[0], lperm_ref[1])
    top = jnp.maximum(jnp.maximum(lses[0], lses[1]), lses[2])
    es = [jnp.exp(l - top) for l in lses]
    den = es[0] + es[1] + es[2]
    wide = []
    for e in es:
        w = e / den
        w_hi = w.astype(BF16)
        w_lo = (w - w_hi.astype(F32)).astype(BF16)
        wide.append(jnp.dot(w_hi, spread_ref[...], preferred_element_type=F32)
                    + jnp.dot(w_lo, spread_ref[...], preferred_element_type=F32))
    attn = []
    for hp in range(n_slab):
        sl = slice(hp * LANES, (hp + 1) * LANES)
        acc = wide[0][:, sl] * o1_ref[:, sl].astype(F32)
        acc = acc + wide[1][:, sl] * operm_ref[hp]
        acc = acc + wide[2][:, sl] * operm_ref[n_slab + hp]
        attn.append(acc.astype(BF16))

    seq_start = (i % tiles_per_seq) == 0
    ext_ref[0:HALO, :] = jnp.where(seq_start, 0.0, uh_ref[...].astype(F32))
    ext_ref[HALO:HALO + tm, :] = u_ref[...].astype(F32)
    conv = jnp.zeros((tm, SC_WIDTH), F32)
    for t in range(SC_CONV):
        off = HALO - (SC_CONV - 1) + t
        conv = conv + cw_ref[t:t + 1, :] * ext_ref[off:off + tm, :]
    gated = (sb_ref[...].astype(F32) * conv).astype(BF16)

    mixed = jnp.concatenate(attn + [ssd_ref[...], gated], axis=1)
    mix = jnp.dot(mixed, wout_ref[...], preferred_element_type=F32)
    x1 = _layer_norm(ALPHA * x_ref[...] + mix, g_ref[...], b_ref[...])
    x1_ref[...] = x1

    xh = x1.astype(BF16)
    xm = (x1 - xh.astype(F32)).astype(BF16)
    ph = jnp.dot(xh, rw_ref[...], preferred_element_type=F32)
    pm = jnp.dot(xm, rw_ref[...], preferred_element_type=F32)
    logit_ref[...] = pm + pltpu.roll(ph, LANES - N_EXPERTS, 1) + ph + rb_ref[...]


def out_projection(o_list, lse_list, ssd, sb, u, x, sc_conv_w, w_out, ln_g, ln_b, router_w, router_b, seq):
    n = x.shape[0]
    tm = PROJ_ROWS
    tps = seq // tm
    row = lambda i: (i, 0)
    full = lambda i: (0, 0)
    strided = lambda i: (i // tps, 0, i % tps, 0)
    halo = lambda i: (jnp.maximum(i * (tm // HALO) - 1, 0), 0)
    rw_f = router_w.astype(F32)
    rw_hi = rw_f.astype(BF16)
    rw_mid = (rw_f - rw_hi.astype(F32)).astype(BF16)
    rw = jnp.pad(jnp.concatenate([rw_hi, rw_mid], axis=1), ((0, 0), (0, LANES - 2 * N_EXPERTS)))
    rb = jnp.pad(router_b.astype(F32), (0, LANES - N_EXPERTS))[None, :]
    spread = jnp.where(lax.broadcasted_iota(jnp.int32, (LANES, ATTN_WIDTH), 0)
                       == lax.broadcasted_iota(jnp.int32, (LANES, ATTN_WIDTH), 1) // HEAD_DIM, 1.0, 0.0).astype(BF16)
    wide = lambda wd: pl.BlockSpec((tm, wd), row)
    perm = lambda d, wd: pl.BlockSpec((None, d, tm // d, wd), strided)
    const = lambda a: pl.BlockSpec(a.shape, full)
    wout = w_out.astype(BF16)
    cw = sc_conv_w.astype(F32)
    g = ln_g.astype(F32)[None, :]
    b = ln_b.astype(F32)[None, :]
    n_slab = ATTN_WIDTH // LANES
    return pl.pallas_call(
        functools.partial(_outproj_kernel, tps),
        out_shape=(jax.ShapeDtypeStruct((n, D_MODEL), F32), jax.ShapeDtypeStruct((n, LANES), F32)),
        grid=(n // tm,),
        in_specs=[wide(ATTN_WIDTH), perm(4, ATTN_WIDTH), perm(16, ATTN_WIDTH),
                  wide(LANES), perm(4, LANES), perm(16, LANES),
                  wide(SSD_WIDTH), wide(SC_WIDTH), wide(SC_WIDTH), pl.BlockSpec((HALO, SC_WIDTH), halo),
                  wide(D_MODEL), const(cw), const(wout), const(g), const(b),
                  const(spread), const(rw), const(rb)],
        out_specs=(wide(D_MODEL), wide(LANES)),
        scratch_shapes=[pltpu.VMEM((HALO + tm, SC_WIDTH), F32),
                        pltpu.VMEM((2 * n_slab, tm, LANES), F32),
                        pltpu.VMEM((2, tm, LANES), F32)],
        compiler_params=_cparams(1),
        name="out_projection",
    )(*o_list, *lse_list, ssd, sb, u, u, x, cw, wout, g, b, spread, rw, rb)


INFO_IDX, INFO_SLOT, INFO_GATE = 0, TOP_K, 2 * TOP_K
TAB_RUN, TAB_BASE, TAB_CHUNKS = 0, 1, 2


def _route_kernel(logit_ref, upper_ref, info_ref, tab_ref, cnt_ref, run_ref):
    tm = logit_ref.shape[0]
    ne = N_EXPERTS
    rec_rows = 2 * SUBLANES

    @pl.when(pl.program_id(0) == 0)
    def _():
        run_ref[...] = jnp.zeros_like(run_ref)

    work = logit_ref[...].T[:ne, :]
    eidx = lax.broadcasted_iota(jnp.int32, (ne, tm), 0).astype(F32)
    sel = jnp.zeros((ne, tm), F32)
    vals, idxs, hits = [], [], []
    for _ in range(TOP_K):
        m = jnp.max(work, axis=0, keepdims=True)
        idx = jnp.min(jnp.where(work == m, eidx, float(ne)), axis=0, keepdims=True)
        hit = eidx == idx
        work = jnp.where(hit, NEG, work)
        sel = sel + jnp.where(hit, 1.0, 0.0)
        vals.append(m)
        idxs.append(idx)
        hits.append(hit)
    es = [jnp.exp(v - vals[0]) for v in vals]
    den = es[0] + es[1] + es[2] + es[3]
    before = jnp.dot(sel.astype(BF16), upper_ref[...], preferred_element_type=F32)
    cnt = jnp.sum(sel, axis=1, keepdims=True)
    chunks = jnp.floor((cnt + (CHUNK - 1)) * (1.0 / CHUNK))
    row = lax.broadcasted_iota(jnp.int32, (ne, LANES), 0)
    chunks_w = jnp.broadcast_to(chunks, (ne, LANES))
    incl = chunks_w
    shift = 1
    while shift < ne:
        incl = incl + jnp.where(row >= shift, pltpu.roll(incl, shift, 0), 0.0)
        shift *= 2
    base = (incl - chunks_w)[:, 0:1]

    sub = lax.broadcasted_iota(jnp.int32, (rec_rows, tm), 0)
    rec = jnp.zeros((rec_rows, tm), F32)
    for k in range(TOP_K):
        rank = jnp.sum(jnp.where(hits[k], before, 0.0), axis=0, keepdims=True)
        first = jnp.sum(jnp.where(hits[k], base, 0.0), axis=0, keepdims=True)
        rec = jnp.where(sub == INFO_IDX + k, idxs[k], rec)
        rec = jnp.where(sub == INFO_SLOT + k, first * CHUNK + rank, rec)
        rec = jnp.where(sub == INFO_GATE + k, es[k] / den, rec)
    info_ref[...] = jnp.concatenate([rec, jnp.zeros((LANES - rec_rows, tm), F32)], axis=0).T

    lane = lax.broadcasted_iota(jnp.int32, (ne, LANES), 1)
    cnt8 = jnp.floor((cnt + (SUBLANES - 1)) * (1.0 / SUBLANES)) * SUBLANES
    cols = jnp.where(lane == 0, cnt8, jnp.where(lane == 1, base, jnp.where(lane == 2, chunks, 0.0)))
    rows = jnp.concatenate([cols, jnp.zeros((LANES - ne, LANES), F32)], axis=0).T
    sub8 = lax.broadcasted_iota(jnp.int32, (SUBLANES, LANES), 0)
    tab_ref[...] = jnp.where(sub8 == TAB_RUN, run_ref[...],
                             jnp.where(sub8 == TAB_BASE, rows[1:2, :],
                                       jnp.where(sub8 == TAB_CHUNKS, rows[2:3, :], 0.0)))
    run_ref[...] = run_ref[...] + rows[0:1, :]
    cnt_ref[...] = jnp.broadcast_to(run_ref[...], cnt_ref.shape)


def route(logits):
    n = logits.shape[0]
    tm = ROUTE_ROWS
    r = lax.broadcasted_iota(jnp.int32, (tm, tm), 0)
    c = lax.broadcasted_iota(jnp.int32, (tm, tm), 1)
    upper = jnp.where(r < c, 1.0, 0.0).astype(BF16)
    return pl.pallas_call(
        _route_kernel,
        out_shape=(jax.ShapeDtypeStruct((n, LANES), F32),
                   jax.ShapeDtypeStruct((n // tm * SUBLANES, LANES), F32),
                   jax.ShapeDtypeStruct((SUBLANES, LANES), F32)),
        grid=(n // tm,),
        in_specs=[pl.BlockSpec((tm, LANES), lambda i: (i, 0)),
                  pl.BlockSpec((tm, tm), lambda i: (0, 0))],
        out_specs=(pl.BlockSpec((tm, LANES), lambda i: (i, 0)),
                   pl.BlockSpec((SUBLANES, LANES), lambda i: (i, 0)),
                   pl.BlockSpec((SUBLANES, LANES), lambda i: (0, 0))),
        scratch_shapes=[pltpu.VMEM((1, LANES), F32)],
        compiler_params=_cparams(1),
        name="moe_route",
    )(logits, upper)


def _start_all(n_chunks, make_copy):
    def start(s, carry):
        make_copy(s).start()
        return carry

    lax.fori_loop(0, n_chunks, start, 0)


def _wait_all(n_chunks, make_copy):
    def wait(s, carry):
        make_copy(s).wait()
        return carry

    lax.fori_loop(0, n_chunks, wait, 0)


def _dispatch_kernel(dst_ref, nslot_ref, zdst_ref, x_ref, info_ref, buf_ref,
                     slots_ref, zero_ref, sems, zsem):
    i = pl.program_id(0)
    tm = x_ref.shape[0]
    cur = i % 2

    def put(tile, side):
        return lambda s: pltpu.make_async_copy(
            slots_ref.at[side, pl.ds(pl.multiple_of(s * CHUNK, CHUNK), CHUNK)],
            buf_ref.at[pl.ds(pl.multiple_of(dst_ref[tile * N_SLOTS + s], SUBLANES), CHUNK)], sems.at[side])

    @pl.when(i == 0)
    def _():
        zero_ref[...] = jnp.zeros_like(zero_ref)

        def zero_copy(e):
            row = pl.multiple_of(jnp.maximum(zdst_ref[e], 0), MOE_BLOCK)
            return pltpu.make_async_copy(zero_ref, buf_ref.at[pl.ds(row, MOE_BLOCK)], zsem)

        def start(e, carry):
            @pl.when(zdst_ref[e] >= 0)
            def _():
                zero_copy(e).start()
            return carry

        def wait(e, carry):
            @pl.when(zdst_ref[e] >= 0)
            def _():
                zero_copy(e).wait()
            return carry

        lax.fori_loop(0, zdst_ref.shape[0], start, 0)
        lax.fori_loop(0, zdst_ref.shape[0], wait, 0)

    xb = x_ref[...].astype(BF16)
    slot_t = info_ref[...].T[INFO_SLOT:INFO_SLOT + TOP_K, :]
    local = lax.broadcasted_iota(jnp.int32, (tm, tm), 0).astype(F32).astype(BF16)
    one = jnp.ones((tm, tm), BF16)
    zero = jnp.zeros((tm, tm), BF16)
    for c in range(SLOT_ROWS // tm):
        target = (slot_t - float(c * tm)).astype(BF16)
        onehot = zero
        for k in range(TOP_K):
            onehot = onehot + jnp.where(local == target[k:k + 1, :], one, zero)
        rows = jnp.dot(onehot, xb, preferred_element_type=F32)
        slots_ref[cur, c * tm:(c + 1) * tm, :] = _pack_rows(rows)

    @pl.when(i > 0)
    def _():
        _wait_all(nslot_ref[i - 1], put(i - 1, 1 - cur))

    _start_all(nslot_ref[i], put(i, cur))

    @pl.when(i == pl.num_programs(0) - 1)
    def _():
        _wait_all(nslot_ref[i], put(i, cur))


def dispatch(x1, info, dst, nslot, zdst, padded_rows):
    n = x1.shape[0]
    tm = ROUTE_ROWS
    row = lambda i, *_: (i, 0)
    return pl.pallas_call(
        _dispatch_kernel,
        out_shape=jax.ShapeDtypeStruct((padded_rows, PACKED), U32),
        grid_spec=pltpu.PrefetchScalarGridSpec(
            num_scalar_prefetch=3,
            grid=(n // tm,),
            in_specs=[pl.BlockSpec((tm, D_MODEL), row), pl.BlockSpec((tm, LANES), row)],
            out_specs=pl.BlockSpec(memory_space=pl.ANY),
            scratch_shapes=[pltpu.VMEM((2, SLOT_ROWS, PACKED), U32), pltpu.VMEM((MOE_BLOCK, PACKED), U32),
                            pltpu.SemaphoreType.DMA((2,)), pltpu.SemaphoreType.DMA(())]),
        compiler_params=_cparams(1),
        name="moe_dispatch",
    )(dst, nslot, zdst, x1, info)


def _expert_kernel(blk_expert_ref, n_used_ref, xs_ref, wgu_ref, bgu_ref, wd_ref, bd_ref, y_ref,
                   wgu_bf, wd_bf):
    i = pl.program_id(0)
    used = i < n_used_ref[0]

    @pl.when(jnp.logical_not(used))
    def _():
        y_ref[...] = jnp.zeros_like(y_ref)

    new_expert = jnp.logical_or(i == 0, blk_expert_ref[i] != blk_expert_ref[jnp.maximum(i - 1, 0)])

    @pl.when(jnp.logical_and(used, new_expert))
    def _():
        rows = 128

        def cast(c, carry):
            r = pl.multiple_of(c * rows, rows)
            wgu_bf[pl.ds(r, rows), :] = wgu_ref[pl.ds(r, rows), :].astype(BF16)
            wd_bf[pl.ds(r, rows), :] = wd_ref[pl.ds(r, rows), :].astype(BF16)
            return carry

        lax.fori_loop(0, D_MODEL // rows, cast, 0)

    @pl.when(used)
    def _():
        xb = _unpack_rows(xs_ref[...])
        gu = jnp.dot(xb, wgu_bf[...], preferred_element_type=F32) + bgu_ref[...]
        gate = jnp.minimum(gu[:, :EXPERT_FF], SWIGLU_LIMIT)
        up = jnp.clip(gu[:, EXPERT_FF:], -SWIGLU_LIMIT, SWIGLU_LIMIT)
        h = (up + 1.0) * gate * _sigmoid(SWIGLU_ALPHA * gate)
        y = jnp.dot(h.astype(BF16), wd_bf[...], preferred_element_type=F32) + bd_ref[...]
        y_ref[...] = _pack_rows(y.astype(BF16).astype(F32))


def expert_mlp(xs, blk_expert, n_used, layer, w_gu, b_gu, w_down, b_down):
    padded_rows = xs.shape[0]
    blk = MOE_BLOCK
    rows_in = lambda i, be, nu: (jnp.minimum(i, nu[0] - 1), 0)
    per_expert = lambda i, be, nu: (layer, be[i], 0, 0)
    return pl.pallas_call(
        _expert_kernel,
        out_shape=jax.ShapeDtypeStruct((padded_rows, PACKED), U32),
        grid_spec=pltpu.PrefetchScalarGridSpec(
            num_scalar_prefetch=2,
            grid=(padded_rows // blk,),
            in_specs=[pl.BlockSpec((blk, PACKED), rows_in),
                      pl.BlockSpec((None, None, D_MODEL, 2 * EXPERT_FF), per_expert),
                      pl.BlockSpec((None, None, 1, 2 * EXPERT_FF), per_expert),
                      pl.BlockSpec((None, None, EXPERT_FF, D_MODEL), per_expert),
                      pl.BlockSpec((None, None, 1, D_MODEL), per_expert)],
            out_specs=pl.BlockSpec((blk, PACKED), lambda i, be, nu: (i, 0)),
            scratch_shapes=[pltpu.VMEM((D_MODEL, 2 * EXPERT_FF), BF16), pltpu.VMEM((EXPERT_FF, D_MODEL), BF16)]),
        compiler_params=_cparams(1),
        name="expert_mlp",
    )(blk_expert, n_used, xs, w_gu, b_gu, w_down, b_down)


def _combine_kernel(dst_ref, nslot_ref, x_ref, info_ref, g_ref, b_ref, y_ref, out_ref, slots_ref, sems):
    i = pl.program_id(0)
    tm = x_ref.shape[0]
    cur = i % 2

    def fetch(tile, side):
        return lambda s: pltpu.make_async_copy(
            y_ref.at[pl.ds(pl.multiple_of(dst_ref[tile * N_SLOTS + s], SUBLANES), CHUNK)],
            slots_ref.at[side, pl.ds(pl.multiple_of(s * CHUNK, CHUNK), CHUNK)], sems.at[side])

    @pl.when(i == 0)
    def _():
        slots_ref[...] = jnp.zeros_like(slots_ref)
        _start_all(nslot_ref[0], fetch(0, 0))

    @pl.when(i + 1 < pl.num_programs(0))
    def _():
        _start_all(nslot_ref[i + 1], fetch(i + 1, 1 - cur))

    _wait_all(nslot_ref[i], fetch(i, cur))

    info = info_ref[...]
    ffn = jnp.zeros((tm, D_MODEL), F32)
    local = lax.broadcasted_iota(jnp.int32, (tm, tm), 1).astype(F32).astype(BF16)
    zero = jnp.zeros((tm, tm), BF16)
    gates = [info[:, INFO_GATE + k:INFO_GATE + k + 1].astype(BF16) for k in range(TOP_K)]
    for c in range(SLOT_ROWS // tm):
        w = zero
        for k in range(TOP_K):
            target = (info[:, INFO_SLOT + k:INFO_SLOT + k + 1] - float(c * tm)).astype(BF16)
            w = w + jnp.where(local == target, gates[k], zero)
        rows = _unpack_rows(slots_ref[cur, c * tm:(c + 1) * tm, :])
        ffn = ffn + jnp.dot(w, rows, preferred_element_type=F32)
    out_ref[...] = _layer_norm(ALPHA * x_ref[...] + ffn, g_ref[...], b_ref[...])


def combine(x1, info, dst, nslot, y, ln_g, ln_b):
    n = x1.shape[0]
    tm = ROUTE_ROWS
    row = lambda i, *_: (i, 0)
    full = lambda i, *_: (0, 0)
    return pl.pallas_call(
        _combine_kernel,
        out_shape=jax.ShapeDtypeStruct((n, D_MODEL), F32),
        grid_spec=pltpu.PrefetchScalarGridSpec(
            num_scalar_prefetch=2,
            grid=(n // tm,),
            in_specs=[pl.BlockSpec((tm, D_MODEL), row),
                      pl.BlockSpec((tm, LANES), row),
                      pl.BlockSpec((1, D_MODEL), full),
                      pl.BlockSpec((1, D_MODEL), full),
                      pl.BlockSpec(memory_space=pl.ANY)],
            out_specs=pl.BlockSpec((tm, D_MODEL), row),
            scratch_shapes=[pltpu.VMEM((2, SLOT_ROWS, PACKED), U32), pltpu.SemaphoreType.DMA((2,))]),
        compiler_params=_cparams(1),
        name="moe_combine",
    )(dst, nslot, x1, info, ln_g.astype(F32)[None, :], ln_b.astype(F32)[None, :], y)


def moe_sublayer(x1, logits, layer, w_gu, b_gu, w_down, b_down, ln_g, ln_b):
    n = x1.shape[0]
    blk = MOE_BLOCK
    n_tiles = n // ROUTE_ROWS
    padded_rows = n * TOP_K + (SUBLANES - 1) * N_EXPERTS * n_tiles + N_EXPERTS * (blk + CHUNK)
    padded_rows = -(-padded_rows // blk) * blk
    info, tab, cnt = route(logits)

    i32 = lambda t: t.astype(jnp.int32)
    tab = tab.reshape(n_tiles, SUBLANES, LANES)[:, :, :N_EXPERTS]
    run, base, chunks = i32(tab[:, TAB_RUN]), i32(tab[:, TAB_BASE]), i32(tab[:, TAB_CHUNKS])
    counts = i32(cnt[0, :N_EXPERTS])
    padded = (counts + CHUNK + blk - 1) // blk * blk
    seg_end = jnp.cumsum(padded)
    seg_start = seg_end - padded
    slot = jnp.arange(N_SLOTS, dtype=jnp.int32)
    ends = base + chunks
    slot_expert = jnp.minimum(jnp.sum(i32(ends[:, None, :] <= slot[None, :, None]), -1), N_EXPERTS - 1)
    onehot = i32(slot_expert[:, :, None] == jnp.arange(N_EXPERTS, dtype=jnp.int32))
    pick = lambda v: jnp.sum(onehot * v[:, None, :], -1)
    dst = (pick(seg_start[None, :] + run) + (slot[None, :] - pick(base)) * CHUNK).reshape(-1)
    nslot = ends[:, N_EXPERTS - 1]
    total = seg_end[N_EXPERTS - 1]
    n_blocks = padded_rows // blk
    blk_row = jnp.arange(n_blocks, dtype=jnp.int32) * blk
    blk_expert = jnp.minimum(jnp.sum(i32(seg_end[None, :] <= blk_row[:, None]), -1), N_EXPERTS - 1)
    n_used = (total // blk).reshape(1)
    tail0 = (seg_start + counts) // blk * blk
    tail1 = jnp.where(tail0 + blk < seg_end, tail0 + blk, -1)
    zdst = jnp.concatenate([tail0, tail1, jnp.where(blk_row >= total, blk_row, -1)])

    xs = dispatch(x1, info, dst, nslot, zdst, padded_rows)
    y = expert_mlp(xs, blk_expert, n_used, layer, w_gu.astype(F32), b_gu.astype(F32)[:, :, None, :],
                   w_down.astype(F32), b_down.astype(F32)[:, :, None, :])
    return combine(x1, info, dst, nslot, y, ln_g, ln_b)


def kernel(x, w_in, ssd_conv_w, ssd_conv_b, ssd_dt_bias, ssd_a_log, ssd_d, ssd_norm_g, sc_conv_w, w_out,
           ln1_g, ln1_b, router_w, router_b, exp_w_gu, exp_b_gu, exp_w_down, exp_b_down, ln2_g, ln2_b):
    batch, seq, _ = x.shape
    n = batch * seq
    cos_t, sin_t = rope_tables(seq)
    xf = x.reshape(n, D_MODEL).astype(F32)
    a_end = 3 * ATTN_WIDTH
    s_end = a_end + SSD_WIDTH + SSD_CONV_DIM
    d_end = s_end + SSD_HEADS
    for layer in range(w_in.shape[0]):
        w = w_in[layer]
        w_attn = w[:, :a_end].astype(BF16)
        w_ssd = w[:, a_end:s_end].astype(BF16)
        w_dt = jnp.pad(w[:, s_end:d_end], ((0, 0), (0, LANES - SSD_HEADS))).astype(BF16)
        w_sc = w[:, d_end:].astype(BF16)
        (q1, k1, v1, q4, k4, v4, q16, k16, v16, z, xbc, dt_raw, sb, u) = in_projection(
            xf, w_attn, w_ssd, w_dt, w_sc, cos_t, sin_t, batch, seq)
        qkv = ((q1.reshape(batch, 1, seq, ATTN_WIDTH), k1.reshape(batch, 1, seq, ATTN_WIDTH),
                v1.reshape(batch, 1, seq, ATTN_WIDTH)), (q4, k4, v4), (q16, k16, v16))
        attn = [dilated_attention(*t, batch, seq, d) for t, d in zip(qkv, DILATIONS)]
        o_list = [attn[0][0].reshape(n, ATTN_WIDTH), attn[1][0], attn[2][0]]
        lse_list = [attn[0][1].reshape(n, LANES), attn[1][1], attn[2][1]]
        ssd = ssd_mixer(z, xbc, dt_raw, ssd_conv_w[layer], ssd_conv_b[layer], ssd_dt_bias[layer],
                        ssd_a_log[layer], ssd_d[layer], ssd_norm_g[layer], batch, seq)
        x1, logits = out_projection(o_list, lse_list, ssd, sb, u, xf,
                                    sc_conv_w[layer], w_out[layer], ln1_g[layer], ln1_b[layer],
                                    router_w[layer], router_b[layer], seq)
        xf = moe_sublayer(x1, logits, layer, exp_w_gu, exp_b_gu, exp_w_down, exp_b_down,
                          ln2_g[layer], ln2_b[layer])
    return xf.reshape(batch, seq, D_MODEL).astype(x.dtype)
```

```python
import functools

import jax
import jax.numpy as jnp
from jax import lax
from jax.experimental import pallas as pl
from jax.experimental.pallas import tpu as pltpu

F32 = jnp.float32
BF16 = jnp.bfloat16
U32 = jnp.uint32

D_MODEL = 1024
HEAD_DIM = 64
ATTN_HEADS = 8
ATTN_WIDTH = 512
DILATIONS = (1, 4, 16)
ATTN_SPAN = 128
ATTN_BLOCK = 128
ATTN_Q_ROWS = 1024
ROPE_THETA = 10000.0

SSD_HEADS = 8
SSD_WIDTH = 512
SSD_GROUPS = 2
SSD_STATE = 128
SSD_CONV = 4
SSD_CHUNK = 128
SSD_STEP_ROWS = 512
SSD_CONV_DIM = 1024

SC_WIDTH = 512
SC_CONV = 3
MIX_WIDTH = 1536

N_EXPERTS = 32
TOP_K = 4
EXPERT_FF = 1024
SWIGLU_ALPHA = 1.702
SWIGLU_LIMIT = 7.0

DEPTH = 2
ALPHA = (2.0 * DEPTH) ** 0.25
LN_EPS = 1e-5
RMS_EPS = 1e-5

LANES = 128
SUBLANES = 8
HALO = 16
NEG = -1e30
VMEM_LIMIT = 56 * 1024 * 1024

PROJ_ROWS = 512
ROUTE_ROWS = 256
ROUTE_STEP_TILES = 4
MOE_BLOCK = 512
CHUNK = 16
WAIT_GROUP = 8
N_SLOTS = N_EXPERTS + ROUTE_ROWS * TOP_K // CHUNK
SLOT_ROWS = N_SLOTS * CHUNK
PACKED = D_MODEL // 2
HI_MASK = 0xFFFF0000


def _cparams(n_axes):
    return pltpu.CompilerParams(dimension_semantics=("arbitrary",) * n_axes,
                                vmem_limit_bytes=VMEM_LIMIT)


def _sigmoid(x):
    return 1.0 / (1.0 + jnp.exp(-x))


def _pair_cols(lo, tile, h):
    return jnp.where(lo, tile[:, h:h + 1], tile[:, h + 1:h + 2])


def _pack_rows(v):
    return pltpu.bitcast(v[:, :PACKED], U32) | (pltpu.bitcast(v[:, PACKED:], U32) >> 16)


def _unpack_rows(w):
    a = pltpu.bitcast(w & U32(HI_MASK), F32).astype(BF16)
    b = pltpu.bitcast(w << 16, F32).astype(BF16)
    return jnp.concatenate([a, b], axis=1)


def _rope_kernel(inv_ref, cos_ref, sin_ref):
    rows = cos_ref.shape[0]
    base = pl.program_id(0) * rows
    pos = (lax.broadcasted_iota(jnp.int32, cos_ref.shape, 0) + base).astype(F32)
    lane = lax.broadcasted_iota(jnp.int32, cos_ref.shape, 1)
    ang = pos * inv_ref[...]
    cos_ref[...] = jnp.cos(ang)
    sin_ref[...] = jnp.where((lane & 32) == 0, -jnp.sin(ang), jnp.sin(ang))


def rope_tables(seq):
    half = HEAD_DIM // 2
    inv = ROPE_THETA ** (-jnp.arange(half, dtype=F32) / half)
    inv = jnp.tile(inv, LANES // half)[None, :]
    rows = 512
    return pl.pallas_call(
        _rope_kernel,
        out_shape=(jax.ShapeDtypeStruct((seq, LANES), F32),) * 2,
        grid=(seq // rows,),
        in_specs=[pl.BlockSpec((1, LANES), lambda i: (0, 0))],
        out_specs=(pl.BlockSpec((rows, LANES), lambda i: (i, 0)),) * 2,
        compiler_params=_cparams(1),
        name="rope_tables",
    )(inv)


def _inproj_kernel(x_ref, wa_ref, ws_ref, wdt_ref, wc_ref, cos_ref, sin_ref,
                   q1_ref, k1_ref, v1_ref, q4_ref, k4_ref, v4_ref, q16_ref, k16_ref, v16_ref,
                   z_ref, xbc_ref, dt_ref, sb_ref, u_ref, perm_ref):
    tm = x_ref.shape[0]
    xb = x_ref[...].astype(BF16)
    cos = cos_ref[...]
    sin = sin_ref[...]
    lane = lax.broadcasted_iota(jnp.int32, cos.shape, 1)
    first_half = (lane & 32) == 0
    n_slab = ATTN_WIDTH // LANES

    def proj(w_ref, lo, hi):
        return jnp.dot(xb, w_ref[:, lo:hi], preferred_element_type=F32)

    def rope(t):
        rot = jnp.where(first_half, pltpu.roll(t, LANES - 32, 1), pltpu.roll(t, 32, 1))
        return t * cos + rot * sin

    def emit(slabs, nat_ref, strided_refs):
        for j, t in enumerate(slabs):
            perm_ref[j] = t
            nat_ref[:, j * LANES:(j + 1) * LANES] = t.astype(BF16)
        for d, ref in strided_refs:
            for r in range(d):
                for j in range(n_slab):
                    ref[r, :, j * LANES:(j + 1) * LANES] = (
                        perm_ref[j, pl.ds(r, tm // d, stride=d), :].astype(BF16))

    w = ATTN_WIDTH
    q = proj(wa_ref, 0, w)
    emit([rope(q[:, j * LANES:(j + 1) * LANES]) * (HEAD_DIM ** -0.5) for j in range(n_slab)],
         q1_ref, ((4, q4_ref), (16, q16_ref)))
    k = proj(wa_ref, w, 2 * w)
    emit([rope(k[:, j * LANES:(j + 1) * LANES]) for j in range(n_slab)], k1_ref, ((4, k4_ref), (16, k16_ref)))
    v = proj(wa_ref, 2 * w, 3 * w)
    emit([v[:, j * LANES:(j + 1) * LANES] for j in range(n_slab)], v1_ref, ((4, v4_ref), (16, v16_ref)))
    z_ref[...] = proj(ws_ref, 0, SSD_WIDTH).astype(BF16)
    xbc_ref[...] = proj(ws_ref, SSD_WIDTH, SSD_WIDTH + SSD_CONV_DIM).astype(BF16)
    dt_ref[...] = jnp.dot(xb, wdt_ref[...], preferred_element_type=F32)
    sb_ref[...] = proj(wc_ref, 0, SC_WIDTH).astype(BF16)
    c = proj(wc_ref, SC_WIDTH, 2 * SC_WIDTH)
    h = proj(wc_ref, 2 * SC_WIDTH, 3 * SC_WIDTH)
    u_ref[...] = (c * h).astype(BF16)


def in_projection(x, w_attn, w_ssd, w_dt, w_sc, cos_t, sin_t, batch, seq):
    n = x.shape[0]
    tm = PROJ_ROWS
    tps = seq // tm
    row = lambda i: (i, 0)
    full = lambda i: (0, 0)
    tab = lambda i: (i % tps, 0)
    strided = lambda i: (i // tps, 0, i % tps, 0)
    w = ATTN_WIDTH
    nat = jax.ShapeDtypeStruct((n, w), BF16)
    nat_spec = pl.BlockSpec((tm, w), row)
    shapes, specs = [nat] * 3, [nat_spec] * 3
    for d in DILATIONS[1:]:
        shapes += [jax.ShapeDtypeStruct((batch, d, seq // d, w), BF16)] * 3
        specs += [pl.BlockSpec((None, d, tm // d, w), strided)] * 3
    widths = (SSD_WIDTH, SSD_CONV_DIM, LANES, SC_WIDTH, SC_WIDTH)
    dtypes = (BF16, BF16, F32, BF16, BF16)
    shapes += [jax.ShapeDtypeStruct((n, wd), dt) for wd, dt in zip(widths, dtypes)]
    specs += [pl.BlockSpec((tm, wd), row) for wd in widths]
    return pl.pallas_call(
        _inproj_kernel,
        out_shape=tuple(shapes),
        grid=(n // tm,),
        in_specs=[pl.BlockSpec((tm, D_MODEL), row),
                  pl.BlockSpec(w_attn.shape, full),
                  pl.BlockSpec(w_ssd.shape, full),
                  pl.BlockSpec(w_dt.shape, full),
                  pl.BlockSpec(w_sc.shape, full),
                  pl.BlockSpec((tm, LANES), tab),
                  pl.BlockSpec((tm, LANES), tab)],
        out_specs=tuple(specs),
        scratch_shapes=[pltpu.VMEM((w // LANES, tm, LANES), F32)],
        compiler_params=_cparams(1),
        name="in_projection",
    )(x, w_attn, w_ssd, w_dt, w_sc, cos_t, sin_t)


def _attn_kernel(q_ref, kc_ref, kp_ref, vc_ref, vp_ref, o_ref, lse_ref):
    blk = ATTN_BLOCK
    n = pl.program_id(2)
    lane = lax.broadcasted_iota(jnp.int32, (blk, LANES), 1)
    lo = lane < HEAD_DIM
    qi = lax.broadcasted_iota(jnp.int32, (2 * blk, 2 * blk), 0) & (blk - 1)
    kj = lax.broadcasted_iota(jnp.int32, (2 * blk, 2 * blk), 1)
    dist = qi + blk - kj
    in_band = jnp.where(dist >= 0, jnp.where(dist <= ATTN_SPAN, 1, 0), 0)
    band = in_band > 0
    band_first = (in_band * jnp.where(kj >= blk, 1, jnp.where(n > 0, 1, 0))) > 0
    n_res, tq = q_ref.shape[0], q_ref.shape[1]
    for r, j in ((r, j) for r in range(n_res) for j in range(tq // blk)):
        rows = slice(j * blk, (j + 1) * blk)
        prev = slice((j - 1) * blk, j * blk)
        valid = band_first if j == 0 else band
        lse_tile = jnp.zeros((blk, LANES), F32)
        for hp in range(ATTN_WIDTH // LANES):
            sl = slice(hp * LANES, (hp + 1) * LANES)
            q2 = q_ref[r, rows, sl].astype(F32)
            qq = jnp.concatenate([jnp.where(lo, q2, 0.0), jnp.where(lo, 0.0, q2)], axis=0).astype(BF16)
            k_prev = kp_ref[r, :, sl] if j == 0 else kc_ref[r, prev, sl]
            v_prev = vp_ref[r, :, sl] if j == 0 else vc_ref[r, prev, sl]
            k2 = jnp.concatenate([k_prev, kc_ref[r, rows, sl]], axis=0)
            v2 = jnp.concatenate([v_prev, vc_ref[r, rows, sl]], axis=0)
            s = lax.dot_general(qq, k2, (((1,), (1,)), ((), ())), preferred_element_type=F32)
            s = jnp.where(valid, s, NEG)
            m = jnp.max(s, axis=1, keepdims=True)
            p = jnp.exp(s - m)
            l = jnp.sum(p, axis=1, keepdims=True)
            pv = jnp.dot(p.astype(BF16), v2, preferred_element_type=F32)
            o = pv / l
            o_ref[r, rows, sl] = jnp.where(lo, o[:blk], o[blk:]).astype(BF16)
            lse = m + jnp.log(l)
            lse_tile = jnp.where(lane == 2 * hp, lse[:blk],
                                 jnp.where(lane == 2 * hp + 1, lse[blk:], lse_tile))
        lse_ref[r, rows, :] = lse_tile


def dilated_attention(q, k, v, batch, seq, dilation):
    length = seq // dilation
    w = ATTN_WIDTH
    blk = ATTN_BLOCK
    tq = min(ATTN_Q_ROWS, length)
    n_res = ATTN_Q_ROWS // tq
    cur = lambda b, r, n: (b, r, n, 0)
    prev = lambda b, r, n: (b, r, jnp.maximum(n * (tq // blk) - 1, 0), 0)
    big = pl.BlockSpec((None, n_res, tq, w), cur)
    small = pl.BlockSpec((None, n_res, blk, w), prev)
    return pl.pallas_call(
        _attn_kernel,
        out_shape=(jax.ShapeDtypeStruct((batch, dilation, length, w), BF16),
                   jax.ShapeDtypeStruct((batch, dilation, length, LANES), F32)),
        grid=(batch, dilation // n_res, length // tq),
        in_specs=[big, big, small, big, small],
        out_specs=(big, pl.BlockSpec((None, n_res, tq, LANES), cur)),
        compiler_params=_cparams(3),
        name=f"dilated_attention_d{dilation}",
    )(q, k, k, v, v)


def _ssd_kernel(z_ref, xbc_ref, halo_ref, dt_ref, shift_ref, cw_ref, cb_ref, dtb_ref, alog_ref, dskip_ref,
                g_ref, y_ref, st_ref):
    ch = SSD_CHUNK
    c = pl.program_id(1)

    @pl.when(c == 0)
    def _():
        st_ref[...] = jnp.zeros_like(st_ref)

    first_halo = halo_ref[...]
    first_halo = jnp.where(c > 0, first_halo, jnp.zeros_like(first_halo))
    for j in range(xbc_ref.shape[0] // ch):
        rows = slice(j * ch, (j + 1) * ch)
        halo = first_halo if j == 0 else xbc_ref[j * ch - HALO:j * ch, :]
        y_ref[rows, :] = _ssd_chunk(xbc_ref[rows, :], halo, z_ref[rows, :], dt_ref[rows, :], shift_ref, cw_ref,
                                    cb_ref, dtb_ref, alog_ref, dskip_ref, g_ref, st_ref)


def _ssd_chunk(xbc_in, halo, z_in, dt_in, shift_ref, cw_ref, cb_ref, dtb_ref, alog_ref, dskip_ref, g_ref, st_ref):
    ch = SSD_CHUNK
    ext = jnp.concatenate([halo, xbc_in], axis=0)
    conv = cb_ref[...] + cw_ref[SSD_CONV - 1:SSD_CONV, :] * xbc_in.astype(F32)
    for t in range(SSD_CONV - 1):
        conv = conv + cw_ref[t:t + 1, :] * jnp.dot(shift_ref[t], ext, preferred_element_type=F32)
    xbc = conv * _sigmoid(conv)
    xh = xbc[:, :SSD_WIDTH]
    bm = xbc[:, SSD_WIDTH:SSD_WIDTH + SSD_GROUPS * SSD_STATE]
    cm = xbc[:, SSD_WIDTH + SSD_GROUPS * SSD_STATE:]

    lane = lax.broadcasted_iota(jnp.int32, (ch, LANES), 1)
    row = lax.broadcasted_iota(jnp.int32, (ch, LANES), 0)
    lo = lane < HEAD_DIM
    lo_row = lo[0:1, :]
    head_lane = lane < SSD_HEADS

    xdt = dt_in + dtb_ref[...]
    dtv = jnp.maximum(xdt, 0.0) + jnp.log(1.0 + jnp.exp(-jnp.abs(xdt)))
    a = jnp.where(head_lane[0:1, :], -jnp.exp(alog_ref[...]), 0.0)
    acum = dtv * a
    shift = 1
    while shift < ch:
        acum = acum + jnp.where(row >= shift, pltpu.roll(acum, shift, 0), 0.0)
        shift *= 2
    acum_t = acum.T
    tot = acum[ch - 1:ch, :]
    e_in = jnp.exp(acum)
    e_out = jnp.exp(tot - acum)
    e_tot = jnp.exp(tot)
    causal = row >= lane

    ys = []
    for g in range(SSD_GROUPS):
        bg = bm[:, g * SSD_STATE:(g + 1) * SSD_STATE]
        cg = cm[:, g * SSD_STATE:(g + 1) * SSD_STATE].astype(BF16)
        cb = lax.dot_general(cg, bg.astype(BF16), (((1,), (1,)), ((), ())),
                             preferred_element_type=F32)
        bg_t = bg.T.astype(BF16)
        for pp in range(2):
            pair = 2 * g + pp
            ha = 2 * pair
            sl = slice(pair * LANES, (pair + 1) * LANES)
            xp = xh[:, sl]
            xdt_pair = xp * _pair_cols(lo, dtv, ha)
            xb16 = xdt_pair.astype(BF16)
            diag = []
            for h in (ha, ha + 1):
                seg = acum[:, h:h + 1] - acum_t[h:h + 1, :]
                decay = jnp.exp(jnp.where(causal, seg, NEG))
                diag.append(jnp.dot((cb * decay).astype(BF16), xb16, preferred_element_type=F32))
            y = jnp.where(lo, diag[0], diag[1])
            state = st_ref[pair]
            y = y + jnp.dot(cg, state.astype(BF16), preferred_element_type=F32) * _pair_cols(lo, e_in, ha)
            xout = (xdt_pair * _pair_cols(lo, e_out, ha)).astype(BF16)
            st_ref[pair] = (state * _pair_cols(lo_row, e_tot, ha)
                            + jnp.dot(bg_t, xout, preferred_element_type=F32))
            ys.append(y + dskip_ref[:, sl] * xp)
    y = jnp.concatenate(ys, axis=1)
    zz = z_in.astype(F32)
    y = y * (zz * _sigmoid(zz))
    gw = SSD_WIDTH // SSD_GROUPS
    outs = []
    for g in range(SSD_GROUPS):
        yg = y[:, g * gw:(g + 1) * gw]
        ms = jnp.mean(yg * yg, axis=1, keepdims=True)
        outs.append(yg * lax.rsqrt(ms + RMS_EPS))
    return (jnp.concatenate(outs, axis=1) * g_ref[...]).astype(BF16)


def ssd_mixer(z, xbc, dt_raw, conv_w, conv_b, dt_bias, a_log, d_skip, norm_g, batch, seq):
    n = batch * seq
    ch = SSD_CHUNK
    step = SSD_STEP_ROWS
    nc = seq // step
    pad = lambda t: jnp.pad(t.astype(F32), (0, LANES - t.shape[0]))[None, :]
    cur = lambda b, c: (b * nc + c, 0)
    halo = lambda b, c: (jnp.maximum((b * nc + c) * (step // HALO) - 1, 0), 0)
    full = lambda b, c: (0, 0)
    r_idx = lax.broadcasted_iota(jnp.int32, (SSD_CONV - 1, ch, HALO + ch), 1)
    j_idx = lax.broadcasted_iota(jnp.int32, (SSD_CONV - 1, ch, HALO + ch), 2)
    t_idx = lax.broadcasted_iota(jnp.int32, (SSD_CONV - 1, ch, HALO + ch), 0)
    shift = jnp.where(j_idx == r_idx + HALO - (SSD_CONV - 1) + t_idx, 1.0, 0.0).astype(BF16)
    return pl.pallas_call(
        _ssd_kernel,
        out_shape=jax.ShapeDtypeStruct((n, SSD_WIDTH), BF16),
        grid=(batch, nc),
        in_specs=[pl.BlockSpec((step, SSD_WIDTH), cur),
                  pl.BlockSpec((step, SSD_CONV_DIM), cur),
                  pl.BlockSpec((HALO, SSD_CONV_DIM), halo),
                  pl.BlockSpec((step, LANES), cur),
                  pl.BlockSpec(shift.shape, lambda b, c: (0, 0, 0)),
                  pl.BlockSpec((SSD_CONV, SSD_CONV_DIM), full),
                  pl.BlockSpec((1, SSD_CONV_DIM), full),
                  pl.BlockSpec((1, LANES), full),
                  pl.BlockSpec((1, LANES), full),
                  pl.BlockSpec((1, SSD_WIDTH), full),
                  pl.BlockSpec((1, SSD_WIDTH), full)],
        out_specs=pl.BlockSpec((step, SSD_WIDTH), cur),
        scratch_shapes=[pltpu.VMEM((SSD_HEADS // 2, SSD_STATE, LANES), F32)],
        compiler_params=_cparams(2),
        name="ssd_mixer",
    )(z, xbc, xbc, dt_raw, shift, conv_w.astype(F32), conv_b.astype(F32)[None, :], pad(dt_bias), pad(a_log),
      jnp.repeat(d_skip.astype(F32), HEAD_DIM)[None, :], norm_g.astype(F32)[None, :])


def _layer_norm(r, g, b):
    mu = jnp.mean(r, axis=1, keepdims=True)
    d = r - mu
    var = jnp.mean(d * d, axis=1, keepdims=True)
    return d * lax.rsqrt(var + LN_EPS) * g + b


def _outproj_kernel(tiles_per_seq, o1_ref, o4_ref, o16_ref, l1_ref, l4_ref, l16_ref, ssd_ref, sb_ref,
                    u_ref, uh_ref, x_ref, cw_ref, wout_ref, g_ref, b_ref, spread_ref, rw_ref,
                    rb_ref, x1_ref, logit_ref, ext_ref, operm_ref, lperm_ref):
    tm = x_ref.shape[0]
    i = pl.program_id(0)
    n_slab = ATTN_WIDTH // LANES

    for idx, (d, o_ref, l_ref) in enumerate(((4, o4_ref, l4_ref), (16, o16_ref, l16_ref))):
        for r in range(d):
            rows = pl.ds(r, tm // d, stride=d)
            lperm_ref[idx, rows, :] = l_ref[r]
            for j in range(n_slab):
                operm_ref[idx * n_slab + j, rows, :] = o_ref[r, :, j * LANES:(j + 1) * LANES].astype(F32)

    lses = (l1_ref[...], lperm_ref[0], lperm_ref[1])
    top = jnp.maximum(jnp.maximum(lses[0], lses[1]), lses[2])
    es = [jnp.exp(l - top) for l in lses]
    den = es[0] + es[1] + es[2]
    wide = []
    for e in es:
        w = e / den
        w_hi = w.astype(BF16)
        w_lo = (w - w_hi.astype(F32)).astype(BF16)
        wide.append(jnp.dot(w_hi, spread_ref[...], preferred_element_type=F32)
                    + jnp.dot(w_lo, spread_ref[...], preferred_element_type=F32))
    attn = []
    for hp in range(n_slab):
        sl = slice(hp * LANES, (hp + 1) * LANES)
        acc = wide[0][:, sl] * o1_ref[:, sl].astype(F32)
        acc = acc + wide[1][:, sl] * operm_ref[hp]
        acc = acc + wide[2][:, sl] * operm_ref[n_slab + hp]
        attn.append(acc.astype(BF16))

    seq_start = (i % tiles_per_seq) == 0
    ext_ref[0:HALO, :] = jnp.where(seq_start, 0.0, uh_ref[...].astype(F32))
    ext_ref[HALO:HALO + tm, :] = u_ref[...].astype(F32)
    conv = jnp.zeros((tm, SC_WIDTH), F32)
    for t in range(SC_CONV):
        off = HALO - (SC_CONV - 1) + t
        conv = conv + cw_ref[t:t + 1, :] * ext_ref[off:off + tm, :]
    gated = (sb_ref[...].astype(F32) * conv).astype(BF16)

    mixed = jnp.concatenate(attn + [ssd_ref[...], gated], axis=1)
    mix = jnp.dot(mixed, wout_ref[...], preferred_element_type=F32)
    x1 = _layer_norm(ALPHA * x_ref[...] + mix, g_ref[...], b_ref[...])
    x1_ref[...] = x1

    xh = x1.astype(BF16)
    xm = (x1 - xh.astype(F32)).astype(BF16)
    ph = jnp.dot(xh, rw_ref[...], preferred_element_type=F32)
    pm = jnp.dot(xm, rw_ref[...], preferred_element_type=F32)
    logit_ref[...] = pm + pltpu.roll(ph, LANES - N_EXPERTS, 1) + ph + rb_ref[...]


def out_projection(o_list, lse_list, ssd, sb, u, x, sc_conv_w, w_out, ln_g, ln_b, router_w, router_b, seq):
    n = x.shape[0]
    tm = PROJ_ROWS
    tps = seq // tm
    row = lambda i: (i, 0)
    full = lambda i: (0, 0)
    strided = lambda i: (i // tps, 0, i % tps, 0)
    halo = lambda i: (jnp.maximum(i * (tm // HALO) - 1, 0), 0)
    rw_f = router_w.astype(F32)
    rw_hi = rw_f.astype(BF16)
    rw_mid = (rw_f - rw_hi.astype(F32)).astype(BF16)
    rw = jnp.pad(jnp.concatenate([rw_hi, rw_mid], axis=1), ((0, 0), (0, LANES - 2 * N_EXPERTS)))
    rb = jnp.pad(router_b.astype(F32), (0, LANES - N_EXPERTS))[None, :]
    spread = jnp.where(lax.broadcasted_iota(jnp.int32, (LANES, ATTN_WIDTH), 0)
                       == lax.broadcasted_iota(jnp.int32, (LANES, ATTN_WIDTH), 1) // HEAD_DIM, 1.0, 0.0).astype(BF16)
    wide = lambda wd: pl.BlockSpec((tm, wd), row)
    perm = lambda d, wd: pl.BlockSpec((None, d, tm // d, wd), strided)
    const = lambda a: pl.BlockSpec(a.shape, full)
    wout = w_out.astype(BF16)
    cw = sc_conv_w.astype(F32)
    g = ln_g.astype(F32)[None, :]
    b = ln_b.astype(F32)[None, :]
    n_slab = ATTN_WIDTH // LANES
    return pl.pallas_call(
        functools.partial(_outproj_kernel, tps),
        out_shape=(jax.ShapeDtypeStruct((n, D_MODEL), F32), jax.ShapeDtypeStruct((n, LANES), F32)),
        grid=(n // tm,),
        in_specs=[wide(ATTN_WIDTH), perm(4, ATTN_WIDTH), perm(16, ATTN_WIDTH),
                  wide(LANES), perm(4, LANES), perm(16, LANES),
                  wide(SSD_WIDTH), wide(SC_WIDTH), wide(SC_WIDTH), pl.BlockSpec((HALO, SC_WIDTH), halo),
                  wide(D_MODEL), const(cw), const(wout), const(g), const(b),
                  const(spread), const(rw), const(rb)],
        out_specs=(wide(D_MODEL), wide(LANES)),
        scratch_shapes=[pltpu.VMEM((HALO + tm, SC_WIDTH), F32),
                        pltpu.VMEM((2 * n_slab, tm, LANES), F32),
                        pltpu.VMEM((2, tm, LANES), F32)],
        compiler_params=_cparams(1),
        name="out_projection",
    )(*o_list, *lse_list, ssd, sb, u, u, x, cw, wout, g, b, spread, rw, rb)


INFO_IDX, INFO_SLOT, INFO_GATE = 0, TOP_K, 2 * TOP_K
TAB_RUN, TAB_BASE, TAB_CHUNKS = 0, 1, 2


def _route_kernel(logit_ref, upper_ref, info_ref, tab_ref, cnt_ref, run_ref):
    @pl.when(pl.program_id(0) == 0)
    def _():
        run_ref[...] = jnp.zeros_like(run_ref)

    tm = ROUTE_ROWS
    for j in range(logit_ref.shape[0] // tm):
        info, tab = _route_tile(logit_ref[j * tm:(j + 1) * tm, :], upper_ref, run_ref)
        info_ref[j * tm:(j + 1) * tm, :] = info
        tab_ref[j * SUBLANES:(j + 1) * SUBLANES, :] = tab
    cnt_ref[...] = jnp.broadcast_to(run_ref[...], cnt_ref.shape)


def _route_tile(logits, upper_ref, run_ref):
    tm = logits.shape[0]
    ne = N_EXPERTS
    rec_rows = 2 * SUBLANES
    work = logits.T[:ne, :]
    eidx = lax.broadcasted_iota(jnp.int32, (ne, tm), 0).astype(F32)
    sel = jnp.zeros((ne, tm), F32)
    vals, idxs, hits = [], [], []
    for _ in range(TOP_K):
        m = jnp.max(work, axis=0, keepdims=True)
        idx = jnp.min(jnp.where(work == m, eidx, float(ne)), axis=0, keepdims=True)
        hit = eidx == idx
        work = jnp.where(hit, NEG, work)
        sel = sel + jnp.where(hit, 1.0, 0.0)
        vals.append(m)
        idxs.append(idx)
        hits.append(hit)
    es = [jnp.exp(v - vals[0]) for v in vals]
    den = es[0] + es[1] + es[2] + es[3]
    before = jnp.dot(sel.astype(BF16), upper_ref[...], preferred_element_type=F32)
    cnt = jnp.sum(sel, axis=1, keepdims=True)
    chunks = jnp.floor((cnt + (CHUNK - 1)) * (1.0 / CHUNK))
    row = lax.broadcasted_iota(jnp.int32, (ne, LANES), 0)
    chunks_w = jnp.broadcast_to(chunks, (ne, LANES))
    incl = chunks_w
    shift = 1
    while shift < ne:
        incl = incl + jnp.where(row >= shift, pltpu.roll(incl, shift, 0), 0.0)
        shift *= 2
    base = (incl - chunks_w)[:, 0:1]

    sub = lax.broadcasted_iota(jnp.int32, (rec_rows, tm), 0)
    rec = jnp.zeros((rec_rows, tm), F32)
    for k in range(TOP_K):
        rank = jnp.sum(jnp.where(hits[k], before, 0.0), axis=0, keepdims=True)
        first = jnp.sum(jnp.where(hits[k], base, 0.0), axis=0, keepdims=True)
        rec = jnp.where(sub == INFO_IDX + k, idxs[k], rec)
        rec = jnp.where(sub == INFO_SLOT + k, first * CHUNK + rank, rec)
        rec = jnp.where(sub == INFO_GATE + k, es[k] / den, rec)
    info = jnp.concatenate([rec, jnp.zeros((LANES - rec_rows, tm), F32)], axis=0).T

    lane = lax.broadcasted_iota(jnp.int32, (ne, LANES), 1)
    cnt8 = jnp.floor((cnt + (SUBLANES - 1)) * (1.0 / SUBLANES)) * SUBLANES
    cols = jnp.where(lane == 0, cnt8, jnp.where(lane == 1, base, jnp.where(lane == 2, chunks, 0.0)))
    rows = jnp.concatenate([cols, jnp.zeros((LANES - ne, LANES), F32)], axis=0).T
    sub8 = lax.broadcasted_iota(jnp.int32, (SUBLANES, LANES), 0)
    tab = jnp.where(sub8 == TAB_RUN, run_ref[...],
                    jnp.where(sub8 == TAB_BASE, rows[1:2, :], jnp.where(sub8 == TAB_CHUNKS, rows[2:3, :], 0.0)))
    run_ref[...] = run_ref[...] + rows[0:1, :]
    return info, tab


def route(logits):
    n = logits.shape[0]
    tm = ROUTE_ROWS
    r = lax.broadcasted_iota(jnp.int32, (tm, tm), 0)
    c = lax.broadcasted_iota(jnp.int32, (tm, tm), 1)
    upper = jnp.where(r < c, 1.0, 0.0).astype(BF16)
    return pl.pallas_call(
        _route_kernel,
        out_shape=(jax.ShapeDtypeStruct((n, LANES), F32),
                   jax.ShapeDtypeStruct((n // tm * SUBLANES, LANES), F32),
                   jax.ShapeDtypeStruct((SUBLANES, LANES), F32)),
        grid=(n // (tm * ROUTE_STEP_TILES),),
        in_specs=[pl.BlockSpec((tm * ROUTE_STEP_TILES, LANES), lambda i: (i, 0)),
                  pl.BlockSpec((tm, tm), lambda i: (0, 0))],
        out_specs=(pl.BlockSpec((tm * ROUTE_STEP_TILES, LANES), lambda i: (i, 0)),
                   pl.BlockSpec((SUBLANES * ROUTE_STEP_TILES, LANES), lambda i: (i, 0)),
                   pl.BlockSpec((SUBLANES, LANES), lambda i: (0, 0))),
        scratch_shapes=[pltpu.VMEM((1, LANES), F32)],
        compiler_params=_cparams(1),
        name="moe_route",
    )(logits, upper)


def _start_all(n_chunks, make_copy):
    def start(s, carry):
        make_copy(s).start()
        return carry

    lax.fori_loop(0, n_chunks, start, 0)


def _wait_all(n_chunks, make_copy, group_copy):
    groups = n_chunks // WAIT_GROUP

    def wait_group(g, carry):
        group_copy.wait()
        return carry

    def wait(s, carry):
        make_copy(s).wait()
        return carry

    lax.fori_loop(0, groups, wait_group, 0)
    lax.fori_loop(groups * WAIT_GROUP, n_chunks, wait, 0)


def _dispatch_kernel(dst_ref, nslot_ref, zdst_ref, x_ref, info_ref, buf_ref,
                     slots_ref, zero_ref, sems, zsem):
    i = pl.program_id(0)
    tm = x_ref.shape[0]
    cur = i % 2

    def put(tile, side):
        return lambda s: pltpu.make_async_copy(
            slots_ref.at[side, pl.ds(pl.multiple_of(s * CHUNK, CHUNK), CHUNK)],
            buf_ref.at[pl.ds(pl.multiple_of(dst_ref[tile * N_SLOTS + s], SUBLANES), CHUNK)], sems.at[side])

    @pl.when(i == 0)
    def _():
        zero_ref[...] = jnp.zeros_like(zero_ref)

        def zero_copy(e):
            row = pl.multiple_of(jnp.maximum(zdst_ref[e], 0), MOE_BLOCK)
            return pltpu.make_async_copy(zero_ref, buf_ref.at[pl.ds(row, MOE_BLOCK)], zsem)

        def start(e, carry):
            @pl.when(zdst_ref[e] >= 0)
            def _():
                zero_copy(e).start()
            return carry

        def wait(e, carry):
            @pl.when(zdst_ref[e] >= 0)
            def _():
                zero_copy(e).wait()
            return carry

        lax.fori_loop(0, zdst_ref.shape[0], start, 0)
        lax.fori_loop(0, zdst_ref.shape[0], wait, 0)

    xb = x_ref[...].astype(BF16)
    slot_t = info_ref[...].T[INFO_SLOT:INFO_SLOT + TOP_K, :]
    local = lax.broadcasted_iota(jnp.int32, (tm, tm), 0).astype(F32).astype(BF16)
    one = jnp.ones((tm, tm), BF16)
    zero = jnp.zeros((tm, tm), BF16)
    for c in range(SLOT_ROWS // tm):
        target = (slot_t - float(c * tm)).astype(BF16)
        onehot = zero
        for k in range(TOP_K):
            onehot = onehot + jnp.where(local == target[k:k + 1, :], one, zero)
        rows = jnp.dot(onehot, xb, preferred_element_type=F32)
        slots_ref[cur, c * tm:(c + 1) * tm, :] = _pack_rows(rows)

    def group(side):
        rows = WAIT_GROUP * CHUNK
        return pltpu.make_async_copy(slots_ref.at[side, pl.ds(0, rows)], buf_ref.at[pl.ds(0, rows)], sems.at[side])

    @pl.when(i > 0)
    def _():
        _wait_all(nslot_ref[i - 1], put(i - 1, 1 - cur), group(1 - cur))

    _start_all(nslot_ref[i], put(i, cur))

    @pl.when(i == pl.num_programs(0) - 1)
    def _():
        _wait_all(nslot_ref[i], put(i, cur), group(cur))


def dispatch(x1, info, dst, nslot, zdst, padded_rows):
    n = x1.shape[0]
    tm = ROUTE_ROWS
    row = lambda i, *_: (i, 0)
    return pl.pallas_call(
        _dispatch_kernel,
        out_shape=jax.ShapeDtypeStruct((padded_rows, PACKED), U32),
        grid_spec=pltpu.PrefetchScalarGridSpec(
            num_scalar_prefetch=3,
            grid=(n // tm,),
            in_specs=[pl.BlockSpec((tm, D_MODEL), row), pl.BlockSpec((tm, LANES), row)],
            out_specs=pl.BlockSpec(memory_space=pl.ANY),
            scratch_shapes=[pltpu.VMEM((2, SLOT_ROWS, PACKED), U32), pltpu.VMEM((MOE_BLOCK, PACKED), U32),
                            pltpu.SemaphoreType.DMA((2,)), pltpu.SemaphoreType.DMA(())]),
        compiler_params=_cparams(1),
        name="moe_dispatch",
    )(dst, nslot, zdst, x1, info)


def _expert_kernel(blk_expert_ref, n_used_ref, xs_ref, wgu_ref, bgu_ref, wd_ref, bd_ref, y_ref):
    del blk_expert_ref
    used = pl.program_id(0) < n_used_ref[0]

    @pl.when(jnp.logical_not(used))
    def _():
        y_ref[...] = jnp.zeros_like(y_ref)

    @pl.when(used)
    def _():
        xb = _unpack_rows(xs_ref[...])
        gu = jnp.dot(xb, wgu_ref[...].astype(BF16), preferred_element_type=F32) + bgu_ref[...]
        gate = jnp.minimum(gu[:, :EXPERT_FF], SWIGLU_LIMIT)
        up = jnp.clip(gu[:, EXPERT_FF:], -SWIGLU_LIMIT, SWIGLU_LIMIT)
        h = (up + 1.0) * gate * _sigmoid(SWIGLU_ALPHA * gate)
        y = jnp.dot(h.astype(BF16), wd_ref[...].astype(BF16), preferred_element_type=F32) + bd_ref[...]
        y_ref[...] = _pack_rows(y.astype(BF16).astype(F32))


def expert_mlp(xs, blk_expert, n_used, layer, w_gu, b_gu, w_down, b_down):
    padded_rows = xs.shape[0]
    blk = MOE_BLOCK
    rows_in = lambda i, be, nu: (jnp.minimum(i, nu[0] - 1), 0)
    per_expert = lambda i, be, nu: (layer, be[i], 0, 0)
    return pl.pallas_call(
        _expert_kernel,
        out_shape=jax.ShapeDtypeStruct((padded_rows, PACKED), U32),
        grid_spec=pltpu.PrefetchScalarGridSpec(
            num_scalar_prefetch=2,
            grid=(padded_rows // blk,),
            in_specs=[pl.BlockSpec((blk, PACKED), rows_in),
                      pl.BlockSpec((None, None, D_MODEL, 2 * EXPERT_FF), per_expert),
                      pl.BlockSpec((None, None, 1, 2 * EXPERT_FF), per_expert),
                      pl.BlockSpec((None, None, EXPERT_FF, D_MODEL), per_expert),
                      pl.BlockSpec((None, None, 1, D_MODEL), per_expert)],
            out_specs=pl.BlockSpec((blk, PACKED), lambda i, be, nu: (i, 0))),
        compiler_params=_cparams(1),
        name="expert_mlp",
    )(blk_expert, n_used, xs, w_gu, b_gu, w_down, b_down)


def _combine_kernel(dst_ref, nslot_ref, x_ref, info_ref, g_ref, b_ref, y_ref, out_ref, slots_ref, sems):
    i = pl.program_id(0)
    tm = x_ref.shape[0]
    cur = i % 2

    def fetch(tile, side):
        return lambda s: pltpu.make_async_copy(
            y_ref.at[pl.ds(pl.multiple_of(dst_ref[tile * N_SLOTS + s], SUBLANES), CHUNK)],
            slots_ref.at[side, pl.ds(pl.multiple_of(s * CHUNK, CHUNK), CHUNK)], sems.at[side])

    @pl.when(i == 0)
    def _():
        slots_ref[...] = jnp.zeros_like(slots_ref)
        _start_all(nslot_ref[0], fetch(0, 0))

    @pl.when(i + 1 < pl.num_programs(0))
    def _():
        _start_all(nslot_ref[i + 1], fetch(i + 1, 1 - cur))

    group_rows = WAIT_GROUP * CHUNK
    _wait_all(nslot_ref[i], fetch(i, cur), pltpu.make_async_copy(
        y_ref.at[pl.ds(0, group_rows)], slots_ref.at[cur, pl.ds(0, group_rows)], sems.at[cur]))

    info = info_ref[...]
    ffn = jnp.zeros((tm, D_MODEL), F32)
    local = lax.broadcasted_iota(jnp.int32, (tm, tm), 1).astype(F32).astype(BF16)
    zero = jnp.zeros((tm, tm), BF16)
    gates = [info[:, INFO_GATE + k:INFO_GATE + k + 1].astype(BF16) for k in range(TOP_K)]
    for c in range(SLOT_ROWS // tm):
        w = zero
        for k in range(TOP_K):
            target = (info[:, INFO_SLOT + k:INFO_SLOT + k + 1] - float(c * tm)).astype(BF16)
            w = w + jnp.where(local == target, gates[k], zero)
        rows = _unpack_rows(slots_ref[cur, c * tm:(c + 1) * tm, :])
        ffn = ffn + jnp.dot(w, rows, preferred_element_type=F32)
    out_ref[...] = _layer_norm(ALPHA * x_ref[...] + ffn, g_ref[...], b_ref[...])


def combine(x1, info, dst, nslot, y, ln_g, ln_b):
    n = x1.shape[0]
    tm = ROUTE_ROWS
    row = lambda i, *_: (i, 0)
    full = lambda i, *_: (0, 0)
    return pl.pallas_call(
        _combine_kernel,
        out_shape=jax.ShapeDtypeStruct((n, D_MODEL), F32),
        grid_spec=pltpu.PrefetchScalarGridSpec(
            num_scalar_prefetch=2,
            grid=(n // tm,),
            in_specs=[pl.BlockSpec((tm, D_MODEL), row),
                      pl.BlockSpec((tm, LANES), row),
                      pl.BlockSpec((1, D_MODEL), full),
                      pl.BlockSpec((1, D_MODEL), full),
                      pl.BlockSpec(memory_space=pl.ANY)],
            out_specs=pl.BlockSpec((tm, D_MODEL), row),
            scratch_shapes=[pltpu.VMEM((2, SLOT_ROWS, PACKED), U32), pltpu.SemaphoreType.DMA((2,))]),
        compiler_params=_cparams(1),
        name="moe_combine",
    )(dst, nslot, x1, info, ln_g.astype(F32)[None, :], ln_b.astype(F32)[None, :], y)


def moe_sublayer(x1, logits, layer, w_gu, b_gu, w_down, b_down, ln_g, ln_b):
    n = x1.shape[0]
    blk = MOE_BLOCK
    n_tiles = n // ROUTE_ROWS
    padded_rows = n * TOP_K + (SUBLANES - 1) * N_EXPERTS * n_tiles + N_EXPERTS * (blk + CHUNK)
    padded_rows = -(-padded_rows // blk) * blk
    info, tab, cnt = route(logits)

    i32 = lambda t: t.astype(jnp.int32)
    tab = tab.reshape(n_tiles, SUBLANES, LANES)[:, :, :N_EXPERTS]
    run, base, chunks = i32(tab[:, TAB_RUN]), i32(tab[:, TAB_BASE]), i32(tab[:, TAB_CHUNKS])
    counts = i32(cnt[0, :N_EXPERTS])
    padded = (counts + CHUNK + blk - 1) // blk * blk
    seg_end = jnp.cumsum(padded)
    seg_start = seg_end - padded
    slot = jnp.arange(N_SLOTS, dtype=jnp.int32)
    ends = base + chunks
    slot_expert = jnp.minimum(jnp.sum(i32(ends[:, None, :] <= slot[None, :, None]), -1), N_EXPERTS - 1)
    onehot = i32(slot_expert[:, :, None] == jnp.arange(N_EXPERTS, dtype=jnp.int32))
    pick = lambda v: jnp.sum(onehot * v[:, None, :], -1)
    dst = (pick(seg_start[None, :] + run) + (slot[None, :] - pick(base)) * CHUNK).reshape(-1)
    nslot = ends[:, N_EXPERTS - 1]
    total = seg_end[N_EXPERTS - 1]
    n_blocks = padded_rows // blk
    blk_row = jnp.arange(n_blocks, dtype=jnp.int32) * blk
    blk_expert = jnp.minimum(jnp.sum(i32(seg_end[None, :] <= blk_row[:, None]), -1), N_EXPERTS - 1)
    n_used = (total // blk).reshape(1)
    tail0 = (seg_start + counts) // blk * blk
    tail1 = jnp.where(tail0 + blk < seg_end, tail0 + blk, -1)
    zdst = jnp.concatenate([tail0, tail1, jnp.where(blk_row >= total, blk_row, -1)])

    xs = dispatch(x1, info, dst, nslot, zdst, padded_rows)
    y = expert_mlp(xs, blk_expert, n_used, layer, w_gu.astype(F32), b_gu.astype(F32)[:, :, None, :],
                   w_down.astype(F32), b_down.astype(F32)[:, :, None, :])
    return combine(x1, info, dst, nslot, y, ln_g, ln_b)


def kernel(x, w_in, ssd_conv_w, ssd_conv_b, ssd_dt_bias, ssd_a_log, ssd_d, ssd_norm_g, sc_conv_w, w_out,
           ln1_g, ln1_b, router_w, router_b, exp_w_gu, exp_b_gu, exp_w_down, exp_b_down, ln2_g, ln2_b):
    batch, seq, _ = x.shape
    n = batch * seq
    cos_t, sin_t = rope_tables(seq)
    xf = x.reshape(n, D_MODEL).astype(F32)
    a_end = 3 * ATTN_WIDTH
    s_end = a_end + SSD_WIDTH + SSD_CONV_DIM
    d_end = s_end + SSD_HEADS
    for layer in range(w_in.shape[0]):
        w = w_in[layer]
        w_attn = w[:, :a_end].astype(BF16)
        w_ssd = w[:, a_end:s_end].astype(BF16)
        w_dt = jnp.pad(w[:, s_end:d_end], ((0, 0), (0, LANES - SSD_HEADS))).astype(BF16)
        w_sc = w[:, d_end:].astype(BF16)
        (q1, k1, v1, q4, k4, v4, q16, k16, v16, z, xbc, dt_raw, sb, u) = in_projection(
            xf, w_attn, w_ssd, w_dt, w_sc, cos_t, sin_t, batch, seq)
        qkv = ((q1.reshape(batch, 1, seq, ATTN_WIDTH), k1.reshape(batch, 1, seq, ATTN_WIDTH),
                v1.reshape(batch, 1, seq, ATTN_WIDTH)), (q4, k4, v4), (q16, k16, v16))
        attn = [dilated_attention(*t, batch, seq, d) for t, d in zip(qkv, DILATIONS)]
        o_list = [attn[0][0].reshape(n, ATTN_WIDTH), attn[1][0], attn[2][0]]
        lse_list = [attn[0][1].reshape(n, LANES), attn[1][1], attn[2][1]]
        ssd = ssd_mixer(z, xbc, dt_raw, ssd_conv_w[layer], ssd_conv_b[layer], ssd_dt_bias[layer],
                        ssd_a_log[layer], ssd_d[layer], ssd_norm_g[layer], batch, seq)
        x1, logits = out_projection(o_list, lse_list, ssd, sb, u, xf,
                                    sc_conv_w[layer], w_out[layer], ln1_g[layer], ln1_b[layer],
                                    router_w[layer], router_b[layer], seq)
        xf = moe_sublayer(x1, logits, layer, exp_w_gu, exp_b_gu, exp_w_down, exp_b_down,
                          ln2_g[layer], ln2_b[layer])
    return xf.reshape(batch, seq, D_MODEL).astype(x.dtype)
```

```python
import functools

import jax
import jax.numpy as jnp
from jax import lax
from jax.experimental import pallas as pl
from jax.experimental.pallas import tpu as pltpu

F32 = jnp.float32
BF16 = jnp.bfloat16
U32 = jnp.uint32

D_MODEL = 1024
HEAD_DIM = 64
ATTN_HEADS = 8
ATTN_WIDTH = 512
DILATIONS = (1, 4, 16)
ATTN_SPAN = 128
ATTN_BLOCK = 128
ATTN_Q_ROWS = 1024
ROPE_THETA = 10000.0

SSD_HEADS = 8
SSD_WIDTH = 512
SSD_GROUPS = 2
SSD_STATE = 128
SSD_CONV = 4
SSD_CHUNK = 128
SSD_STEP_ROWS = 1024
SSD_CONV_DIM = 1024

SC_WIDTH = 512
SC_CONV = 3
MIX_WIDTH = 1536

N_EXPERTS = 32
TOP_K = 4
EXPERT_FF = 1024
SWIGLU_ALPHA = 1.702
SWIGLU_LIMIT = 7.0

DEPTH = 2
ALPHA = (2.0 * DEPTH) ** 0.25
LN_EPS = 1e-5
RMS_EPS = 1e-5

LANES = 128
SUBLANES = 8
HALO = 16
NEG = -1e30
VMEM_LIMIT = 56 * 1024 * 1024

PROJ_ROWS = 512
ROUTE_ROWS = 256
ROUTE_STEP_TILES = 4
MOE_BLOCK = 512
CHUNK = 16
WAIT_GROUP = 8
N_SLOTS = N_EXPERTS + ROUTE_ROWS * TOP_K // CHUNK
SLOT_ROWS = N_SLOTS * CHUNK
PACKED = D_MODEL // 2
HI_MASK = 0xFFFF0000


def _cparams(n_axes):
    return pltpu.CompilerParams(dimension_semantics=("arbitrary",) * n_axes,
                                vmem_limit_bytes=VMEM_LIMIT)


def _sigmoid(x):
    return 1.0 / (1.0 + jnp.exp(-x))


def _pair_cols(lo, tile, h):
    return jnp.where(lo, tile[:, h:h + 1], tile[:, h + 1:h + 2])


def _pack_rows(v):
    return pltpu.bitcast(v[:, :PACKED], U32) | (pltpu.bitcast(v[:, PACKED:], U32) >> 16)


def _unpack_rows(w):
    a = pltpu.bitcast(w & U32(HI_MASK), F32).astype(BF16)
    b = pltpu.bitcast(w << 16, F32).astype(BF16)
    return jnp.concatenate([a, b], axis=1)


def _rope_kernel(inv_ref, cos_ref, sin_ref):
    rows = cos_ref.shape[0]
    base = pl.program_id(0) * rows
    pos = (lax.broadcasted_iota(jnp.int32, cos_ref.shape, 0) + base).astype(F32)
    lane = lax.broadcasted_iota(jnp.int32, cos_ref.shape, 1)
    ang = pos * inv_ref[...]
    cos_ref[...] = jnp.cos(ang)
    sin_ref[...] = jnp.where((lane & 32) == 0, -jnp.sin(ang), jnp.sin(ang))


def rope_tables(seq):
    half = HEAD_DIM // 2
    inv = ROPE_THETA ** (-jnp.arange(half, dtype=F32) / half)
    inv = jnp.tile(inv, LANES // half)[None, :]
    rows = 512
    return pl.pallas_call(
        _rope_kernel,
        out_shape=(jax.ShapeDtypeStruct((seq, LANES), F32),) * 2,
        grid=(seq // rows,),
        in_specs=[pl.BlockSpec((1, LANES), lambda i: (0, 0))],
        out_specs=(pl.BlockSpec((rows, LANES), lambda i: (i, 0)),) * 2,
        compiler_params=_cparams(1),
        name="rope_tables",
    )(inv)


def _inproj_kernel(x_ref, wa_ref, ws_ref, wdt_ref, wc_ref, cos_ref, sin_ref,
                   q1_ref, k1_ref, v1_ref, q4_ref, k4_ref, v4_ref, q16_ref, k16_ref, v16_ref,
                   z_ref, xbc_ref, dt_ref, sb_ref, u_ref, perm_ref):
    tm = x_ref.shape[0]
    xb = x_ref[...].astype(BF16)
    cos = cos_ref[...]
    sin = sin_ref[...]
    lane = lax.broadcasted_iota(jnp.int32, cos.shape, 1)
    first_half = (lane & 32) == 0
    n_slab = ATTN_WIDTH // LANES

    def proj(w_ref, lo, hi):
        return jnp.dot(xb, w_ref[:, lo:hi], preferred_element_type=F32)

    def rope(t):
        rot = jnp.where(first_half, pltpu.roll(t, LANES - 32, 1), pltpu.roll(t, 32, 1))
        return t * cos + rot * sin

    def emit(slabs, nat_ref, strided_refs):
        for j, t in enumerate(slabs):
            perm_ref[j] = t
            nat_ref[:, j * LANES:(j + 1) * LANES] = t.astype(BF16)
        for d, ref in strided_refs:
            for r in range(d):
                for j in range(n_slab):
                    ref[r, :, j * LANES:(j + 1) * LANES] = (
                        perm_ref[j, pl.ds(r, tm // d, stride=d), :].astype(BF16))

    w = ATTN_WIDTH
    q = proj(wa_ref, 0, w)
    emit([rope(q[:, j * LANES:(j + 1) * LANES]) * (HEAD_DIM ** -0.5) for j in range(n_slab)],
         q1_ref, ((4, q4_ref), (16, q16_ref)))
    k = proj(wa_ref, w, 2 * w)
    emit([rope(k[:, j * LANES:(j + 1) * LANES]) for j in range(n_slab)], k1_ref, ((4, k4_ref), (16, k16_ref)))
    v = proj(wa_ref, 2 * w, 3 * w)
    emit([v[:, j * LANES:(j + 1) * LANES] for j in range(n_slab)], v1_ref, ((4, v4_ref), (16, v16_ref)))
    z_ref[...] = proj(ws_ref, 0, SSD_WIDTH).astype(BF16)
    xbc_ref[...] = proj(ws_ref, SSD_WIDTH, SSD_WIDTH + SSD_CONV_DIM).astype(BF16)
    dt_ref[...] = jnp.dot(xb, wdt_ref[...], preferred_element_type=F32)
    sb_ref[...] = proj(wc_ref, 0, SC_WIDTH).astype(BF16)
    c = proj(wc_ref, SC_WIDTH, 2 * SC_WIDTH)
    h = proj(wc_ref, 2 * SC_WIDTH, 3 * SC_WIDTH)
    u_ref[...] = (c * h).astype(BF16)


def in_projection(x, w_attn, w_ssd, w_dt, w_sc, cos_t, sin_t, batch, seq):
    n = x.shape[0]
    tm = PROJ_ROWS
    tps = seq // tm
    row = lambda i: (i, 0)
    full = lambda i: (0, 0)
    tab = lambda i: (i % tps, 0)
    strided = lambda i: (i // tps, 0, i % tps, 0)
    w = ATTN_WIDTH
    nat = jax.ShapeDtypeStruct((n, w), BF16)
    nat_spec = pl.BlockSpec((tm, w), row)
    shapes, specs = [nat] * 3, [nat_spec] * 3
    for d in DILATIONS[1:]:
        shapes += [jax.ShapeDtypeStruct((batch, d, seq // d, w), BF16)] * 3
        specs += [pl.BlockSpec((None, d, tm // d, w), strided)] * 3
    widths = (SSD_WIDTH, SSD_CONV_DIM, LANES, SC_WIDTH, SC_WIDTH)
    dtypes = (BF16, BF16, F32, BF16, BF16)
    shapes += [jax.ShapeDtypeStruct((n, wd), dt) for wd, dt in zip(widths, dtypes)]
    specs += [pl.BlockSpec((tm, wd), row) for wd in widths]
    return pl.pallas_call(
        _inproj_kernel,
        out_shape=tuple(shapes),
        grid=(n // tm,),
        in_specs=[pl.BlockSpec((tm, D_MODEL), row),
                  pl.BlockSpec(w_attn.shape, full),
                  pl.BlockSpec(w_ssd.shape, full),
                  pl.BlockSpec(w_dt.shape, full),
                  pl.BlockSpec(w_sc.shape, full),
                  pl.BlockSpec((tm, LANES), tab),
                  pl.BlockSpec((tm, LANES), tab)],
        out_specs=tuple(specs),
        scratch_shapes=[pltpu.VMEM((w // LANES, tm, LANES), F32)],
        compiler_params=_cparams(1),
        name="in_projection",
    )(x, w_attn, w_ssd, w_dt, w_sc, cos_t, sin_t)


def _attn_kernel(q_ref, kc_ref, kp_ref, vc_ref, vp_ref, o_ref, lse_ref):
    blk = ATTN_BLOCK
    n = pl.program_id(2)
    lane = lax.broadcasted_iota(jnp.int32, (blk, LANES), 1)
    lo = lane < HEAD_DIM
    qi = lax.broadcasted_iota(jnp.int32, (2 * blk, 2 * blk), 0) & (blk - 1)
    kj = lax.broadcasted_iota(jnp.int32, (2 * blk, 2 * blk), 1)
    dist = qi + blk - kj
    in_band = jnp.where(dist >= 0, jnp.where(dist <= ATTN_SPAN, 1, 0), 0)
    bias = jnp.where(in_band > 0, 0.0, NEG)
    bias_first = jnp.where((in_band * jnp.where(kj >= blk, 1, jnp.where(n > 0, 1, 0))) > 0, 0.0, NEG)
    n_res, tq = q_ref.shape[0], q_ref.shape[1]
    for r, j in ((r, j) for r in range(n_res) for j in range(tq // blk)):
        rows = slice(j * blk, (j + 1) * blk)
        prev = slice((j - 1) * blk, j * blk)
        mask_bias = bias_first if j == 0 else bias
        lse_tile = jnp.zeros((blk, LANES), F32)
        for hp in range(ATTN_WIDTH // LANES):
            sl = slice(hp * LANES, (hp + 1) * LANES)
            q2 = q_ref[r, rows, sl].astype(F32)
            qq = jnp.concatenate([jnp.where(lo, q2, 0.0), jnp.where(lo, 0.0, q2)], axis=0).astype(BF16)
            k_prev = kp_ref[r, :, sl] if j == 0 else kc_ref[r, prev, sl]
            v_prev = vp_ref[r, :, sl] if j == 0 else vc_ref[r, prev, sl]
            k2 = jnp.concatenate([k_prev, kc_ref[r, rows, sl]], axis=0)
            v2 = jnp.concatenate([v_prev, vc_ref[r, rows, sl]], axis=0)
            s = lax.dot_general(qq, k2, (((1,), (1,)), ((), ())), preferred_element_type=F32) + mask_bias
            m = jnp.max(s, axis=1, keepdims=True)
            p = jnp.exp(s - m)
            l = jnp.sum(p, axis=1, keepdims=True)
            pv = jnp.dot(p.astype(BF16), v2, preferred_element_type=F32)
            o = pv / l
            o_ref[r, rows, sl] = jnp.where(lo, o[:blk], o[blk:]).astype(BF16)
            lse = m + jnp.log(l)
            lse_tile = jnp.where(lane == 2 * hp, lse[:blk],
                                 jnp.where(lane == 2 * hp + 1, lse[blk:], lse_tile))
        lse_ref[r, rows, :] = lse_tile


def dilated_attention(q, k, v, batch, seq, dilation):
    length = seq // dilation
    w = ATTN_WIDTH
    blk = ATTN_BLOCK
    tq = min(ATTN_Q_ROWS, length)
    n_res = ATTN_Q_ROWS // tq
    cur = lambda b, r, n: (b, r, n, 0)
    prev = lambda b, r, n: (b, r, jnp.maximum(n * (tq // blk) - 1, 0), 0)
    big = pl.BlockSpec((None, n_res, tq, w), cur)
    small = pl.BlockSpec((None, n_res, blk, w), prev)
    return pl.pallas_call(
        _attn_kernel,
        out_shape=(jax.ShapeDtypeStruct((batch, dilation, length, w), BF16),
                   jax.ShapeDtypeStruct((batch, dilation, length, LANES), F32)),
        grid=(batch, dilation // n_res, length // tq),
        in_specs=[big, big, small, big, small],
        out_specs=(big, pl.BlockSpec((None, n_res, tq, LANES), cur)),
        compiler_params=_cparams(3),
        name=f"dilated_attention_d{dilation}",
    )(q, k, k, v, v)


def _ssd_kernel(z_ref, xbc_ref, halo_ref, dt_ref, shift_ref, cw_ref, cb_ref, dtb_ref, alog_ref, dskip_ref,
                g_ref, y_ref, st_ref):
    ch = SSD_CHUNK
    c = pl.program_id(1)

    @pl.when(c == 0)
    def _():
        st_ref[...] = jnp.zeros_like(st_ref)

    first_halo = halo_ref[...]
    first_halo = jnp.where(c > 0, first_halo, jnp.zeros_like(first_halo))
    for j in range(xbc_ref.shape[0] // ch):
        rows = slice(j * ch, (j + 1) * ch)
        halo = first_halo if j == 0 else xbc_ref[j * ch - HALO:j * ch, :]
        y_ref[rows, :] = _ssd_chunk(xbc_ref[rows, :], halo, z_ref[rows, :], dt_ref[rows, :], shift_ref, cw_ref,
                                    cb_ref, dtb_ref, alog_ref, dskip_ref, g_ref, st_ref)


def _ssd_chunk(xbc_in, halo, z_in, dt_in, shift_ref, cw_ref, cb_ref, dtb_ref, alog_ref, dskip_ref, g_ref, st_ref):
    ch = SSD_CHUNK
    ext = jnp.concatenate([halo, xbc_in], axis=0)
    conv = cb_ref[...] + cw_ref[SSD_CONV - 1:SSD_CONV, :] * xbc_in.astype(F32)
    for t in range(SSD_CONV - 1):
        conv = conv + cw_ref[t:t + 1, :] * jnp.dot(shift_ref[t], ext, preferred_element_type=F32)
    xbc = conv * _sigmoid(conv)
    xh = xbc[:, :SSD_WIDTH]
    bm = xbc[:, SSD_WIDTH:SSD_WIDTH + SSD_GROUPS * SSD_STATE]
    cm = xbc[:, SSD_WIDTH + SSD_GROUPS * SSD_STATE:]

    lane = lax.broadcasted_iota(jnp.int32, (ch, LANES), 1)
    row = lax.broadcasted_iota(jnp.int32, (ch, LANES), 0)
    lo = lane < HEAD_DIM
    lo_row = lo[0:1, :]
    head_lane = lane < SSD_HEADS

    xdt = dt_in + dtb_ref[...]
    dtv = jnp.maximum(xdt, 0.0) + jnp.log(1.0 + jnp.exp(-jnp.abs(xdt)))
    a = jnp.where(head_lane[0:1, :], -jnp.exp(alog_ref[...]), 0.0)
    acum = dtv * a
    shift = 1
    while shift < ch:
        acum = acum + jnp.where(row >= shift, pltpu.roll(acum, shift, 0), 0.0)
        shift *= 2
    acum_t = acum.T
    tot = acum[ch - 1:ch, :]
    e_in = jnp.exp(acum)
    e_out = jnp.exp(tot - acum)
    e_tot = jnp.exp(tot)
    causal = row >= lane

    ys = []
    for g in range(SSD_GROUPS):
        bg = bm[:, g * SSD_STATE:(g + 1) * SSD_STATE]
        cg = cm[:, g * SSD_STATE:(g + 1) * SSD_STATE].astype(BF16)
        cb = lax.dot_general(cg, bg.astype(BF16), (((1,), (1,)), ((), ())),
                             preferred_element_type=F32)
        bg_t = bg.T.astype(BF16)
        for pp in range(2):
            pair = 2 * g + pp
            ha = 2 * pair
            sl = slice(pair * LANES, (pair + 1) * LANES)
            xp = xh[:, sl]
            xdt_pair = xp * _pair_cols(lo, dtv, ha)
            xb16 = xdt_pair.astype(BF16)
            diag = []
            for h in (ha, ha + 1):
                seg = acum[:, h:h + 1] - acum_t[h:h + 1, :]
                decay = jnp.exp(jnp.where(causal, seg, NEG))
                diag.append(jnp.dot((cb * decay).astype(BF16), xb16, preferred_element_type=F32))
            y = jnp.where(lo, diag[0], diag[1])
            state = st_ref[pair]
            y = y + jnp.dot(cg, state.astype(BF16), preferred_element_type=F32) * _pair_cols(lo, e_in, ha)
            xout = (xdt_pair * _pair_cols(lo, e_out, ha)).astype(BF16)
            st_ref[pair] = (state * _pair_cols(lo_row, e_tot, ha)
                            + jnp.dot(bg_t, xout, preferred_element_type=F32))
            ys.append(y + dskip_ref[:, sl] * xp)
    y = jnp.concatenate(ys, axis=1)
    zz = z_in.astype(F32)
    y = y * (zz * _sigmoid(zz))
    gw = SSD_WIDTH // SSD_GROUPS
    outs = []
    for g in range(SSD_GROUPS):
        yg = y[:, g * gw:(g + 1) * gw]
        ms = jnp.mean(yg * yg, axis=1, keepdims=True)
        outs.append(yg * lax.rsqrt(ms + RMS_EPS))
    return (jnp.concatenate(outs, axis=1) * g_ref[...]).astype(BF16)


def ssd_mixer(z, xbc, dt_raw, conv_w, conv_b, dt_bias, a_log, d_skip, norm_g, batch, seq):
    n = batch * seq
    ch = SSD_CHUNK
    step = SSD_STEP_ROWS
    nc = seq // step
    pad = lambda t: jnp.pad(t.astype(F32), (0, LANES - t.shape[0]))[None, :]
    cur = lambda b, c: (b * nc + c, 0)
    halo = lambda b, c: (jnp.maximum((b * nc + c) * (step // HALO) - 1, 0), 0)
    full = lambda b, c: (0, 0)
    r_idx = lax.broadcasted_iota(jnp.int32, (SSD_CONV - 1, ch, HALO + ch), 1)
    j_idx = lax.broadcasted_iota(jnp.int32, (SSD_CONV - 1, ch, HALO + ch), 2)
    t_idx = lax.broadcasted_iota(jnp.int32, (SSD_CONV - 1, ch, HALO + ch), 0)
    shift = jnp.where(j_idx == r_idx + HALO - (SSD_CONV - 1) + t_idx, 1.0, 0.0).astype(BF16)
    return pl.pallas_call(
        _ssd_kernel,
        out_shape=jax.ShapeDtypeStruct((n, SSD_WIDTH), BF16),
        grid=(batch, nc),
        in_specs=[pl.BlockSpec((step, SSD_WIDTH), cur),
                  pl.BlockSpec((step, SSD_CONV_DIM), cur),
                  pl.BlockSpec((HALO, SSD_CONV_DIM), halo),
                  pl.BlockSpec((step, LANES), cur),
                  pl.BlockSpec(shift.shape, lambda b, c: (0, 0, 0)),
                  pl.BlockSpec((SSD_CONV, SSD_CONV_DIM), full),
                  pl.BlockSpec((1, SSD_CONV_DIM), full),
                  pl.BlockSpec((1, LANES), full),
                  pl.BlockSpec((1, LANES), full),
                  pl.BlockSpec((1, SSD_WIDTH), full),
                  pl.BlockSpec((1, SSD_WIDTH), full)],
        out_specs=pl.BlockSpec((step, SSD_WIDTH), cur),
        scratch_shapes=[pltpu.VMEM((SSD_HEADS // 2, SSD_STATE, LANES), F32)],
        compiler_params=_cparams(2),
        name="ssd_mixer",
    )(z, xbc, xbc, dt_raw, shift, conv_w.astype(F32), conv_b.astype(F32)[None, :], pad(dt_bias), pad(a_log),
      jnp.repeat(d_skip.astype(F32), HEAD_DIM)[None, :], norm_g.astype(F32)[None, :])


def _layer_norm(r, g, b):
    mu = jnp.mean(r, axis=1, keepdims=True)
    d = r - mu
    var = jnp.mean(d * d, axis=1, keepdims=True)
    return d * lax.rsqrt(var + LN_EPS) * g + b


def _outproj_kernel(tiles_per_seq, o1_ref, o4_ref, o16_ref, l1_ref, l4_ref, l16_ref, ssd_ref, sb_ref,
                    u_ref, uh_ref, x_ref, cw_ref, wout_ref, g_ref, b_ref, spread_ref, rw_ref,
                    rb_ref, x1_ref, logit_ref, ext_ref, operm_ref, lperm_ref):
    tm = x_ref.shape[0]
    i = pl.program_id(0)
    n_slab = ATTN_WIDTH // LANES

    for idx, (d, o_ref, l_ref) in enumerate(((4, o4_ref, l4_ref), (16, o16_ref, l16_ref))):
        for r in range(d):
            rows = pl.ds(r, tm // d, stride=d)
            lperm_ref[idx, rows, :] = l_ref[r]
            for j in range(n_slab):
                operm_ref[idx * n_slab + j, rows, :] = o_ref[r, :, j * LANES:(j + 1) * LANES].astype(F32)

    lses = (l1_ref[...], lperm_ref[0], lperm_ref[1])
    top = jnp.maximum(jnp.maximum(lses[0], lses[1]), lses[2])
    es = [jnp.exp(l - top) for l in lses]
    den = es[0] + es[1] + es[2]
    wide = []
    for e in es:
        w = e / den
        w_hi = w.astype(BF16)
        w_lo = (w - w_hi.astype(F32)).astype(BF16)
        wide.append(jnp.dot(w_hi, spread_ref[...], preferred_element_type=F32)
                    + jnp.dot(w_lo, spread_ref[...], preferred_element_type=F32))
    attn = []
    for hp in range(n_slab):
        sl = slice(hp * LANES, (hp + 1) * LANES)
        acc = wide[0][:, sl] * o1_ref[:, sl].astype(F32)
        acc = acc + wide[1][:, sl] * operm_ref[hp]
        acc = acc + wide[2][:, sl] * operm_ref[n_slab + hp]
        attn.append(acc.astype(BF16))

    seq_start = (i % tiles_per_seq) == 0
    ext_ref[0:HALO, :] = jnp.where(seq_start, 0.0, uh_ref[...].astype(F32))
    ext_ref[HALO:HALO + tm, :] = u_ref[...].astype(F32)
    conv = jnp.zeros((tm, SC_WIDTH), F32)
    for t in range(SC_CONV):
        off = HALO - (SC_CONV - 1) + t
        conv = conv + cw_ref[t:t + 1, :] * ext_ref[off:off + tm, :]
    gated = (sb_ref[...].astype(F32) * conv).astype(BF16)

    mixed = jnp.concatenate(attn + [ssd_ref[...], gated], axis=1)
    mix = jnp.dot(mixed, wout_ref[...], preferred_element_type=F32)
    x1 = _layer_norm(ALPHA * x_ref[...] + mix, g_ref[...], b_ref[...])
    x1_ref[...] = x1

    xh = x1.astype(BF16)
    xm = (x1 - xh.astype(F32)).astype(BF16)
    ph = jnp.dot(xh, rw_ref[...], preferred_element_type=F32)
    pm = jnp.dot(xm, rw_ref[...], preferred_element_type=F32)
    logit_ref[...] = pm + pltpu.roll(ph, LANES - N_EXPERTS, 1) + ph + rb_ref[...]


def out_projection(o_list, lse_list, ssd, sb, u, x, sc_conv_w, w_out, ln_g, ln_b, router_w, router_b, seq):
    n = x.shape[0]
    tm = PROJ_ROWS
    tps = seq // tm
    row = lambda i: (i, 0)
    full = lambda i: (0, 0)
    strided = lambda i: (i // tps, 0, i % tps, 0)
    halo = lambda i: (jnp.maximum(i * (tm // HALO) - 1, 0), 0)
    rw_f = router_w.astype(F32)
    rw_hi = rw_f.astype(BF16)
    rw_mid = (rw_f - rw_hi.astype(F32)).astype(BF16)
    rw = jnp.pad(jnp.concatenate([rw_hi, rw_mid], axis=1), ((0, 0), (0, LANES - 2 * N_EXPERTS)))
    rb = jnp.pad(router_b.astype(F32), (0, LANES - N_EXPERTS))[None, :]
    spread = jnp.where(lax.broadcasted_iota(jnp.int32, (LANES, ATTN_WIDTH), 0)
                       == lax.broadcasted_iota(jnp.int32, (LANES, ATTN_WIDTH), 1) // HEAD_DIM, 1.0, 0.0).astype(BF16)
    wide = lambda wd: pl.BlockSpec((tm, wd), row)
    perm = lambda d, wd: pl.BlockSpec((None, d, tm // d, wd), strided)
    const = lambda a: pl.BlockSpec(a.shape, full)
    wout = w_out.astype(BF16)
    cw = sc_conv_w.astype(F32)
    g = ln_g.astype(F32)[None, :]
    b = ln_b.astype(F32)[None, :]
    n_slab = ATTN_WIDTH // LANES
    return pl.pallas_call(
        functools.partial(_outproj_kernel, tps),
        out_shape=(jax.ShapeDtypeStruct((n, D_MODEL), F32), jax.ShapeDtypeStruct((n, LANES), F32)),
        grid=(n // tm,),
        in_specs=[wide(ATTN_WIDTH), perm(4, ATTN_WIDTH), perm(16, ATTN_WIDTH),
                  wide(LANES), perm(4, LANES), perm(16, LANES),
                  wide(SSD_WIDTH), wide(SC_WIDTH), wide(SC_WIDTH), pl.BlockSpec((HALO, SC_WIDTH), halo),
                  wide(D_MODEL), const(cw), const(wout), const(g), const(b),
                  const(spread), const(rw), const(rb)],
        out_specs=(wide(D_MODEL), wide(LANES)),
        scratch_shapes=[pltpu.VMEM((HALO + tm, SC_WIDTH), F32),
                        pltpu.VMEM((2 * n_slab, tm, LANES), F32),
                        pltpu.VMEM((2, tm, LANES), F32)],
        compiler_params=_cparams(1),
        name="out_projection",
    )(*o_list, *lse_list, ssd, sb, u, u, x, cw, wout, g, b, spread, rw, rb)


INFO_IDX, INFO_SLOT, INFO_GATE = 0, TOP_K, 2 * TOP_K
TAB_RUN, TAB_BASE, TAB_CHUNKS = 0, 1, 2


def _route_kernel(logit_ref, upper_ref, info_ref, tab_ref, cnt_ref, run_ref):
    @pl.when(pl.program_id(0) == 0)
    def _():
        run_ref[...] = jnp.zeros_like(run_ref)

    tm = ROUTE_ROWS
    for j in range(logit_ref.shape[0] // tm):
        info, tab = _route_tile(logit_ref[j * tm:(j + 1) * tm, :], upper_ref, run_ref)
        info_ref[j * tm:(j + 1) * tm, :] = info
        tab_ref[j * SUBLANES:(j + 1) * SUBLANES, :] = tab
    cnt_ref[...] = jnp.broadcast_to(run_ref[...], cnt_ref.shape)


def _route_tile(logits, upper_ref, run_ref):
    tm = logits.shape[0]
    ne = N_EXPERTS
    rec_rows = 2 * SUBLANES
    work = logits.T[:ne, :]
    eidx = lax.broadcasted_iota(jnp.int32, (ne, tm), 0).astype(F32)
    sel = jnp.zeros((ne, tm), F32)
    vals, idxs, hits = [], [], []
    for _ in range(TOP_K):
        m = jnp.max(work, axis=0, keepdims=True)
        idx = jnp.min(jnp.where(work == m, eidx, float(ne)), axis=0, keepdims=True)
        hit = eidx == idx
        work = jnp.where(hit, NEG, work)
        sel = sel + jnp.where(hit, 1.0, 0.0)
        vals.append(m)
        idxs.append(idx)
        hits.append(hit)
    es = [jnp.exp(v - vals[0]) for v in vals]
    den = es[0] + es[1] + es[2] + es[3]
    before = jnp.dot(sel.astype(BF16), upper_ref[...], preferred_element_type=F32)
    cnt = jnp.sum(sel, axis=1, keepdims=True)
    chunks = jnp.floor((cnt + (CHUNK - 1)) * (1.0 / CHUNK))
    row = lax.broadcasted_iota(jnp.int32, (ne, LANES), 0)
    chunks_w = jnp.broadcast_to(chunks, (ne, LANES))
    incl = chunks_w
    shift = 1
    while shift < ne:
        incl = incl + jnp.where(row >= shift, pltpu.roll(incl, shift, 0), 0.0)
        shift *= 2
    base = (incl - chunks_w)[:, 0:1]

    sub = lax.broadcasted_iota(jnp.int32, (rec_rows, tm), 0)
    rec = jnp.zeros((rec_rows, tm), F32)
    for k in range(TOP_K):
        rank = jnp.sum(jnp.where(hits[k], before, 0.0), axis=0, keepdims=True)
        first = jnp.sum(jnp.where(hits[k], base, 0.0), axis=0, keepdims=True)
        rec = jnp.where(sub == INFO_IDX + k, idxs[k], rec)
        rec = jnp.where(sub == INFO_SLOT + k, first * CHUNK + rank, rec)
        rec = jnp.where(sub == INFO_GATE + k, es[k] / den, rec)
    info = jnp.concatenate([rec, jnp.zeros((LANES - rec_rows, tm), F32)], axis=0).T

    lane = lax.broadcasted_iota(jnp.int32, (ne, LANES), 1)
    cnt8 = jnp.floor((cnt + (SUBLANES - 1)) * (1.0 / SUBLANES)) * SUBLANES
    cols = jnp.where(lane == 0, cnt8, jnp.where(lane == 1, base, jnp.where(lane == 2, chunks, 0.0)))
    rows = jnp.concatenate([cols, jnp.zeros((LANES - ne, LANES), F32)], axis=0).T
    sub8 = lax.broadcasted_iota(jnp.int32, (SUBLANES, LANES), 0)
    tab = jnp.where(sub8 == TAB_RUN, run_ref[...],
                    jnp.where(sub8 == TAB_BASE, rows[1:2, :], jnp.where(sub8 == TAB_CHUNKS, rows[2:3, :], 0.0)))
    run_ref[...] = run_ref[...] + rows[0:1, :]
    return info, tab


def route(logits):
    n = logits.shape[0]
    tm = ROUTE_ROWS
    r = lax.broadcasted_iota(jnp.int32, (tm, tm), 0)
    c = lax.broadcasted_iota(jnp.int32, (tm, tm), 1)
    upper = jnp.where(r < c, 1.0, 0.0).astype(BF16)
    return pl.pallas_call(
        _route_kernel,
        out_shape=(jax.ShapeDtypeStruct((n, LANES), F32),
                   jax.ShapeDtypeStruct((n // tm * SUBLANES, LANES), F32),
                   jax.ShapeDtypeStruct((SUBLANES, LANES), F32)),
        grid=(n // (tm * ROUTE_STEP_TILES),),
        in_specs=[pl.BlockSpec((tm * ROUTE_STEP_TILES, LANES), lambda i: (i, 0)),
                  pl.BlockSpec((tm, tm), lambda i: (0, 0))],
        out_specs=(pl.BlockSpec((tm * ROUTE_STEP_TILES, LANES), lambda i: (i, 0)),
                   pl.BlockSpec((SUBLANES * ROUTE_STEP_TILES, LANES), lambda i: (i, 0)),
                   pl.BlockSpec((SUBLANES, LANES), lambda i: (0, 0))),
        scratch_shapes=[pltpu.VMEM((1, LANES), F32)],
        compiler_params=_cparams(1),
        name="moe_route",
    )(logits, upper)


def _start_all(n_chunks, make_copy):
    def start(s, carry):
        make_copy(s).start()
        return carry

    lax.fori_loop(0, n_chunks, start, 0)


def _wait_all(n_chunks, make_copy, group_copy):
    groups = n_chunks // WAIT_GROUP

    def wait_group(g, carry):
        group_copy.wait()
        return carry

    def wait(s, carry):
        make_copy(s).wait()
        return carry

    lax.fori_loop(0, groups, wait_group, 0)
    lax.fori_loop(groups * WAIT_GROUP, n_chunks, wait, 0)


def _dispatch_kernel(dst_ref, nslot_ref, zdst_ref, x_ref, info_ref, buf_ref,
                     slots_ref, zero_ref, sems, zsem):
    i = pl.program_id(0)
    tm = x_ref.shape[0]
    cur = i % 2

    def put(tile, side):
        return lambda s: pltpu.make_async_copy(
            slots_ref.at[side, pl.ds(pl.multiple_of(s * CHUNK, CHUNK), CHUNK)],
            buf_ref.at[pl.ds(pl.multiple_of(dst_ref[tile * N_SLOTS + s], SUBLANES), CHUNK)], sems.at[side])

    @pl.when(i == 0)
    def _():
        zero_ref[...] = jnp.zeros_like(zero_ref)

        def zero_copy(e):
            row = pl.multiple_of(jnp.maximum(zdst_ref[e], 0), MOE_BLOCK)
            return pltpu.make_async_copy(zero_ref, buf_ref.at[pl.ds(row, MOE_BLOCK)], zsem)

        def start(e, carry):
            @pl.when(zdst_ref[e] >= 0)
            def _():
                zero_copy(e).start()
            return carry

        def wait(e, carry):
            @pl.when(zdst_ref[e] >= 0)
            def _():
                zero_copy(e).wait()
            return carry

        lax.fori_loop(0, zdst_ref.shape[0], start, 0)
        lax.fori_loop(0, zdst_ref.shape[0], wait, 0)

    xb = x_ref[...].astype(BF16)
    slot_t = info_ref[...].T[INFO_SLOT:INFO_SLOT + TOP_K, :]
    local = lax.broadcasted_iota(jnp.int32, (tm, tm), 0).astype(F32).astype(BF16)
    one = jnp.ones((tm, tm), BF16)
    zero = jnp.zeros((tm, tm), BF16)
    for c in range(SLOT_ROWS // tm):
        target = (slot_t - float(c * tm)).astype(BF16)
        onehot = zero
        for k in range(TOP_K):
            onehot = onehot + jnp.where(local == target[k:k + 1, :], one, zero)
        rows = jnp.dot(onehot, xb, preferred_element_type=F32)
        slots_ref[cur, c * tm:(c + 1) * tm, :] = _pack_rows(rows)

    def group(side):
        rows = WAIT_GROUP * CHUNK
        return pltpu.make_async_copy(slots_ref.at[side, pl.ds(0, rows)], buf_ref.at[pl.ds(0, rows)], sems.at[side])

    @pl.when(i > 0)
    def _():
        _wait_all(nslot_ref[i - 1], put(i - 1, 1 - cur), group(1 - cur))

    _start_all(nslot_ref[i], put(i, cur))

    @pl.when(i == pl.num_programs(0) - 1)
    def _():
        _wait_all(nslot_ref[i], put(i, cur), group(cur))


def dispatch(x1, info, dst, nslot, zdst, padded_rows):
    n = x1.shape[0]
    tm = ROUTE_ROWS
    row = lambda i, *_: (i, 0)
    return pl.pallas_call(
        _dispatch_kernel,
        out_shape=jax.ShapeDtypeStruct((padded_rows, PACKED), U32),
        grid_spec=pltpu.PrefetchScalarGridSpec(
            num_scalar_prefetch=3,
            grid=(n // tm,),
            in_specs=[pl.BlockSpec((tm, D_MODEL), row), pl.BlockSpec((tm, LANES), row)],
            out_specs=pl.BlockSpec(memory_space=pl.ANY),
            scratch_shapes=[pltpu.VMEM((2, SLOT_ROWS, PACKED), U32), pltpu.VMEM((MOE_BLOCK, PACKED), U32),
                            pltpu.SemaphoreType.DMA((2,)), pltpu.SemaphoreType.DMA(())]),
        compiler_params=_cparams(1),
        name="moe_dispatch",
    )(dst, nslot, zdst, x1, info)


def _expert_kernel(blk_expert_ref, n_used_ref, xs_ref, wgu_ref, bgu_ref, wd_ref, bd_ref, y_ref):
    del blk_expert_ref
    used = pl.program_id(0) < n_used_ref[0]

    @pl.when(jnp.logical_not(used))
    def _():
        y_ref[...] = jnp.zeros_like(y_ref)

    @pl.when(used)
    def _():
        xb = _unpack_rows(xs_ref[...])
        gu = jnp.dot(xb, wgu_ref[...].astype(BF16), preferred_element_type=F32) + bgu_ref[...]
        gate = jnp.minimum(gu[:, :EXPERT_FF], SWIGLU_LIMIT)
        up = jnp.clip(gu[:, EXPERT_FF:], -SWIGLU_LIMIT, SWIGLU_LIMIT)
        h = (up + 1.0) * gate * _sigmoid(SWIGLU_ALPHA * gate)
        y = jnp.dot(h.astype(BF16), wd_ref[...].astype(BF16), preferred_element_type=F32) + bd_ref[...]
        y_ref[...] = _pack_rows(y.astype(BF16).astype(F32))


def expert_mlp(xs, blk_expert, n_used, layer, w_gu, b_gu, w_down, b_down):
    padded_rows = xs.shape[0]
    blk = MOE_BLOCK
    rows_in = lambda i, be, nu: (jnp.minimum(i, nu[0] - 1), 0)
    per_expert = lambda i, be, nu: (layer, be[i], 0, 0)
    return pl.pallas_call(
        _expert_kernel,
        out_shape=jax.ShapeDtypeStruct((padded_rows, PACKED), U32),
        grid_spec=pltpu.PrefetchScalarGridSpec(
            num_scalar_prefetch=2,
            grid=(padded_rows // blk,),
            in_specs=[pl.BlockSpec((blk, PACKED), rows_in),
                      pl.BlockSpec((None, None, D_MODEL, 2 * EXPERT_FF), per_expert),
                      pl.BlockSpec((None, None, 1, 2 * EXPERT_FF), per_expert),
                      pl.BlockSpec((None, None, EXPERT_FF, D_MODEL), per_expert),
                      pl.BlockSpec((None, None, 1, D_MODEL), per_expert)],
            out_specs=pl.BlockSpec((blk, PACKED), lambda i, be, nu: (i, 0))),
        compiler_params=_cparams(1),
        name="expert_mlp",
    )(blk_expert, n_used, xs, w_gu, b_gu, w_down, b_down)


def _combine_kernel(dst_ref, nslot_ref, x_ref, info_ref, g_ref, b_ref, y_ref, out_ref, slots_ref, sems):
    i = pl.program_id(0)
    tm = x_ref.shape[0]
    cur = i % 2

    def fetch(tile, side):
        return lambda s: pltpu.make_async_copy(
            y_ref.at[pl.ds(pl.multiple_of(dst_ref[tile * N_SLOTS + s], SUBLANES), CHUNK)],
            slots_ref.at[side, pl.ds(pl.multiple_of(s * CHUNK, CHUNK), CHUNK)], sems.at[side])

    @pl.when(i == 0)
    def _():
        slots_ref[...] = jnp.zeros_like(slots_ref)
        _start_all(nslot_ref[0], fetch(0, 0))

    @pl.when(i + 1 < pl.num_programs(0))
    def _():
        _start_all(nslot_ref[i + 1], fetch(i + 1, 1 - cur))

    group_rows = WAIT_GROUP * CHUNK
    _wait_all(nslot_ref[i], fetch(i, cur), pltpu.make_async_copy(
        y_ref.at[pl.ds(0, group_rows)], slots_ref.at[cur, pl.ds(0, group_rows)], sems.at[cur]))

    info = info_ref[...]
    ffn = jnp.zeros((tm, D_MODEL), F32)
    local = lax.broadcasted_iota(jnp.int32, (tm, tm), 1).astype(F32).astype(BF16)
    zero = jnp.zeros((tm, tm), BF16)
    gates = [info[:, INFO_GATE + k:INFO_GATE + k + 1].astype(BF16) for k in range(TOP_K)]
    for c in range(SLOT_ROWS // tm):
        w = zero
        for k in range(TOP_K):
            target = (info[:, INFO_SLOT + k:INFO_SLOT + k + 1] - float(c * tm)).astype(BF16)
            w = w + jnp.where(local == target, gates[k], zero)
        rows = _unpack_rows(slots_ref[cur, c * tm:(c + 1) * tm, :])
        ffn = ffn + jnp.dot(w, rows, preferred_element_type=F32)
    out_ref[...] = _layer_norm(ALPHA * x_ref[...] + ffn, g_ref[...], b_ref[...])


def combine(x1, info, dst, nslot, y, ln_g, ln_b):
    n = x1.shape[0]
    tm = ROUTE_ROWS
    row = lambda i, *_: (i, 0)
    full = lambda i, *_: (0, 0)
    return pl.pallas_call(
        _combine_kernel,
        out_shape=jax.ShapeDtypeStruct((n, D_MODEL), F32),
        grid_spec=pltpu.PrefetchScalarGridSpec(
            num_scalar_prefetch=2,
            grid=(n // tm,),
            in_specs=[pl.BlockSpec((tm, D_MODEL), row),
                      pl.BlockSpec((tm, LANES), row),
                      pl.BlockSpec((1, D_MODEL), full),
                      pl.BlockSpec((1, D_MODEL), full),
                      pl.BlockSpec(memory_space=pl.ANY)],
            out_specs=pl.BlockSpec((tm, D_MODEL), row),
            scratch_shapes=[pltpu.VMEM((2, SLOT_ROWS, PACKED), U32), pltpu.SemaphoreType.DMA((2,))]),
        compiler_params=_cparams(1),
        name="moe_combine",
    )(dst, nslot, x1, info, ln_g.astype(F32)[None, :], ln_b.astype(F32)[None, :], y)


def moe_sublayer(x1, logits, layer, w_gu, b_gu, w_down, b_down, ln_g, ln_b):
    n = x1.shape[0]
    blk = MOE_BLOCK
    n_tiles = n // ROUTE_ROWS
    padded_rows = n * TOP_K + (SUBLANES - 1) * N_EXPERTS * n_tiles + N_EXPERTS * (blk + CHUNK)
    padded_rows = -(-padded_rows // blk) * blk
    info, tab, cnt = route(logits)

    i32 = lambda t: t.astype(jnp.int32)
    tab = tab.reshape(n_tiles, SUBLANES, LANES)[:, :, :N_EXPERTS]
    run, base, chunks = i32(tab[:, TAB_RUN]), i32(tab[:, TAB_BASE]), i32(tab[:, TAB_CHUNKS])
    counts = i32(cnt[0, :N_EXPERTS])
    padded = (counts + CHUNK + blk - 1) // blk * blk
    seg_end = jnp.cumsum(padded)
    seg_start = seg_end - padded
    slot = jnp.arange(N_SLOTS, dtype=jnp.int32)
    ends = base + chunks
    slot_expert = jnp.minimum(jnp.sum(i32(ends[:, None, :] <= slot[None, :, None]), -1), N_EXPERTS - 1)
    onehot = i32(slot_expert[:, :, None] == jnp.arange(N_EXPERTS, dtype=jnp.int32))
    pick = lambda v: jnp.sum(onehot * v[:, None, :], -1)
    dst = (pick(seg_start[None, :] + run) + (slot[None, :] - pick(base)) * CHUNK).reshape(-1)
    nslot = ends[:, N_EXPERTS - 1]
    total = seg_end[N_EXPERTS - 1]
    n_blocks = padded_rows // blk
    blk_row = jnp.arange(n_blocks, dtype=jnp.int32) * blk
    blk_expert = jnp.minimum(jnp.sum(i32(seg_end[None, :] <= blk_row[:, None]), -1), N_EXPERTS - 1)
    n_used = (total // blk).reshape(1)
    tail0 = (seg_start + counts) // blk * blk
    tail1 = jnp.where(tail0 + blk < seg_end, tail0 + blk, -1)
    zdst = jnp.concatenate([tail0, tail1, jnp.where(blk_row >= total, blk_row, -1)])

    xs = dispatch(x1, info, dst, nslot, zdst, padded_rows)
    y = expert_mlp(xs, blk_expert, n_used, layer, w_gu.astype(F32), b_gu.astype(F32)[:, :, None, :],
                   w_down.astype(F32), b_down.astype(F32)[:, :, None, :])
    return combine(x1, info, dst, nslot, y, ln_g, ln_b)


def kernel(x, w_in, ssd_conv_w, ssd_conv_b, ssd_dt_bias, ssd_a_log, ssd_d, ssd_norm_g, sc_conv_w, w_out,
           ln1_g, ln1_b, router_w, router_b, exp_w_gu, exp_b_gu, exp_w_down, exp_b_down, ln2_g, ln2_b):
    batch, seq, _ = x.shape
    n = batch * seq
    cos_t, sin_t = rope_tables(seq)
    xf = x.reshape(n, D_MODEL).astype(F32)
    a_end = 3 * ATTN_WIDTH
    s_end = a_end + SSD_WIDTH + SSD_CONV_DIM
    d_end = s_end + SSD_HEADS
    for layer in range(w_in.shape[0]):
        w = w_in[layer]
        w_attn = w[:, :a_end].astype(BF16)
        w_ssd = w[:, a_end:s_end].astype(BF16)
        w_dt = jnp.pad(w[:, s_end:d_end], ((0, 0), (0, LANES - SSD_HEADS))).astype(BF16)
        w_sc = w[:, d_end:].astype(BF16)
        (q1, k1, v1, q4, k4, v4, q16, k16, v16, z, xbc, dt_raw, sb, u) = in_projection(
            xf, w_attn, w_ssd, w_dt, w_sc, cos_t, sin_t, batch, seq)
        qkv = ((q1.reshape(batch, 1, seq, ATTN_WIDTH), k1.reshape(batch, 1, seq, ATTN_WIDTH),
                v1.reshape(batch, 1, seq, ATTN_WIDTH)), (q4, k4, v4), (q16, k16, v16))
        attn = [dilated_attention(*t, batch, seq, d) for t, d in zip(qkv, DILATIONS)]
        o_list = [attn[0][0].reshape(n, ATTN_WIDTH), attn[1][0], attn[2][0]]
        lse_list = [attn[0][1].reshape(n, LANES), attn[1][1], attn[2][1]]
        ssd = ssd_mixer(z, xbc, dt_raw, ssd_conv_w[layer], ssd_conv_b[layer], ssd_dt_bias[layer],
                        ssd_a_log[layer], ssd_d[layer], ssd_norm_g[layer], batch, seq)
        x1, logits = out_projection(o_list, lse_list, ssd, sb, u, xf,
                                    sc_conv_w[layer], w_out[layer], ln1_g[layer], ln1_b[layer],
                                    router_w[layer], router_b[layer], seq)
        xf = moe_sublayer(x1, logits, layer, exp_w_gu, exp_b_gu, exp_w_down, exp_b_down,
                          ln2_g[layer], ln2_b[layer])
    return xf.reshape(batch, seq, D_MODEL).astype(x.dtype)
```

```python
import functools

import jax
import jax.numpy as jnp
from jax import lax
from jax.experimental import pallas as pl
from jax.experimental.pallas import tpu as pltpu

F32 = jnp.float32
BF16 = jnp.bfloat16
U32 = jnp.uint32

D_MODEL = 1024
HEAD_DIM = 64
ATTN_HEADS = 8
ATTN_WIDTH = 512
DILATIONS = (1, 4, 16)
ATTN_SPAN = 128
ATTN_BLOCK = 128
ATTN_Q_ROWS = 1024
ROPE_THETA = 10000.0

SSD_HEADS = 8
SSD_WIDTH = 512
SSD_GROUPS = 2
SSD_STATE = 128
SSD_CONV = 4
SSD_CHUNK = 128
SSD_STEP_ROWS = 1024
SSD_CONV_DIM = 1024

SC_WIDTH = 512
SC_CONV = 3
MIX_WIDTH = 1536

N_EXPERTS = 32
TOP_K = 4
EXPERT_FF = 1024
SWIGLU_ALPHA = 1.702
SWIGLU_LIMIT = 7.0

DEPTH = 2
ALPHA = (2.0 * DEPTH) ** 0.25
LN_EPS = 1e-5
RMS_EPS = 1e-5

LANES = 128
SUBLANES = 8
HALO = 16
NEG = -1e30
VMEM_LIMIT = 56 * 1024 * 1024

PROJ_ROWS = 512
ROUTE_ROWS = 256
ROUTE_STEP_TILES = 4
MOE_BLOCK = 512
CHUNK = 16
WAIT_GROUP = 8
N_SLOTS = N_EXPERTS + ROUTE_ROWS * TOP_K // CHUNK
SLOT_ROWS = N_SLOTS * CHUNK
MAX_PAIRS = N_SLOTS // 2
PACKED = D_MODEL // 2
HI_MASK = 0xFFFF0000


def _cparams(n_axes):
    return pltpu.CompilerParams(dimension_semantics=("arbitrary",) * n_axes,
                                vmem_limit_bytes=VMEM_LIMIT)


def _sigmoid(x):
    return 1.0 / (1.0 + jnp.exp(-x))


def _pair_cols(lo, tile, h):
    return jnp.where(lo, tile[:, h:h + 1], tile[:, h + 1:h + 2])


def _pack_rows(v):
    return pltpu.bitcast(v[:, :PACKED], U32) | (pltpu.bitcast(v[:, PACKED:], U32) >> 16)


def _unpack_rows(w):
    a = pltpu.bitcast(w & U32(HI_MASK), F32).astype(BF16)
    b = pltpu.bitcast(w << 16, F32).astype(BF16)
    return jnp.concatenate([a, b], axis=1)


def _rope_kernel(inv_ref, cos_ref, sin_ref):
    rows = cos_ref.shape[0]
    base = pl.program_id(0) * rows
    pos = (lax.broadcasted_iota(jnp.int32, cos_ref.shape, 0) + base).astype(F32)
    lane = lax.broadcasted_iota(jnp.int32, cos_ref.shape, 1)
    ang = pos * inv_ref[...]
    cos_ref[...] = jnp.cos(ang)
    sin_ref[...] = jnp.where((lane & 32) == 0, -jnp.sin(ang), jnp.sin(ang))


def rope_tables(seq):
    half = HEAD_DIM // 2
    inv = ROPE_THETA ** (-jnp.arange(half, dtype=F32) / half)
    inv = jnp.tile(inv, LANES // half)[None, :]
    rows = 512
    return pl.pallas_call(
        _rope_kernel,
        out_shape=(jax.ShapeDtypeStruct((seq, LANES), F32),) * 2,
        grid=(seq // rows,),
        in_specs=[pl.BlockSpec((1, LANES), lambda i: (0, 0))],
        out_specs=(pl.BlockSpec((rows, LANES), lambda i: (i, 0)),) * 2,
        compiler_params=_cparams(1),
        name="rope_tables",
    )(inv)


def _inproj_kernel(x_ref, wa_ref, ws_ref, wdt_ref, wc_ref, cos_ref, sin_ref,
                   q1_ref, k1_ref, v1_ref, q4_ref, k4_ref, v4_ref, q16_ref, k16_ref, v16_ref,
                   z_ref, xbc_ref, dt_ref, sb_ref, u_ref, perm_ref):
    tm = x_ref.shape[0]
    xb = x_ref[...].astype(BF16)
    cos = cos_ref[...]
    sin = sin_ref[...]
    lane = lax.broadcasted_iota(jnp.int32, cos.shape, 1)
    first_half = (lane & 32) == 0
    n_slab = ATTN_WIDTH // LANES

    def proj(w_ref, lo, hi):
        return jnp.dot(xb, w_ref[:, lo:hi], preferred_element_type=F32)

    def rope(t):
        rot = jnp.where(first_half, pltpu.roll(t, LANES - 32, 1), pltpu.roll(t, 32, 1))
        return t * cos + rot * sin

    def emit(slabs, nat_ref, strided_refs):
        for j, t in enumerate(slabs):
            perm_ref[j] = t
            nat_ref[:, j * LANES:(j + 1) * LANES] = t.astype(BF16)
        for d, ref in strided_refs:
            for r in range(d):
                for j in range(n_slab):
                    ref[r, :, j * LANES:(j + 1) * LANES] = (
                        perm_ref[j, pl.ds(r, tm // d, stride=d), :].astype(BF16))

    w = ATTN_WIDTH
    q = proj(wa_ref, 0, w)
    emit([rope(q[:, j * LANES:(j + 1) * LANES]) * (HEAD_DIM ** -0.5) for j in range(n_slab)],
         q1_ref, ((4, q4_ref), (16, q16_ref)))
    k = proj(wa_ref, w, 2 * w)
    emit([rope(k[:, j * LANES:(j + 1) * LANES]) for j in range(n_slab)], k1_ref, ((4, k4_ref), (16, k16_ref)))
    v = proj(wa_ref, 2 * w, 3 * w)
    emit([v[:, j * LANES:(j + 1) * LANES] for j in range(n_slab)], v1_ref, ((4, v4_ref), (16, v16_ref)))
    z_ref[...] = proj(ws_ref, 0, SSD_WIDTH).astype(BF16)
    xbc_ref[...] = proj(ws_ref, SSD_WIDTH, SSD_WIDTH + SSD_CONV_DIM).astype(BF16)
    dt_ref[...] = jnp.dot(xb, wdt_ref[...], preferred_element_type=F32)
    sb_ref[...] = proj(wc_ref, 0, SC_WIDTH).astype(BF16)
    c = proj(wc_ref, SC_WIDTH, 2 * SC_WIDTH)
    h = proj(wc_ref, 2 * SC_WIDTH, 3 * SC_WIDTH)
    u_ref[...] = (c * h).astype(BF16)


def in_projection(x, w_attn, w_ssd, w_dt, w_sc, cos_t, sin_t, batch, seq):
    n = x.shape[0]
    tm = PROJ_ROWS
    tps = seq // tm
    row = lambda i: (i, 0)
    full = lambda i: (0, 0)
    tab = lambda i: (i % tps, 0)
    strided = lambda i: (i // tps, 0, i % tps, 0)
    w = ATTN_WIDTH
    nat = jax.ShapeDtypeStruct((n, w), BF16)
    nat_spec = pl.BlockSpec((tm, w), row)
    shapes, specs = [nat] * 3, [nat_spec] * 3
    for d in DILATIONS[1:]:
        shapes += [jax.ShapeDtypeStruct((batch, d, seq // d, w), BF16)] * 3
        specs += [pl.BlockSpec((None, d, tm // d, w), strided)] * 3
    widths = (SSD_WIDTH, SSD_CONV_DIM, LANES, SC_WIDTH, SC_WIDTH)
    dtypes = (BF16, BF16, F32, BF16, BF16)
    shapes += [jax.ShapeDtypeStruct((n, wd), dt) for wd, dt in zip(widths, dtypes)]
    specs += [pl.BlockSpec((tm, wd), row) for wd in widths]
    return pl.pallas_call(
        _inproj_kernel,
        out_shape=tuple(shapes),
        grid=(n // tm,),
        in_specs=[pl.BlockSpec((tm, D_MODEL), row),
                  pl.BlockSpec(w_attn.shape, full),
                  pl.BlockSpec(w_ssd.shape, full),
                  pl.BlockSpec(w_dt.shape, full),
                  pl.BlockSpec(w_sc.shape, full),
                  pl.BlockSpec((tm, LANES), tab),
                  pl.BlockSpec((tm, LANES), tab)],
        out_specs=tuple(specs),
        scratch_shapes=[pltpu.VMEM((w // LANES, tm, LANES), F32)],
        compiler_params=_cparams(1),
        name="in_projection",
    )(x, w_attn, w_ssd, w_dt, w_sc, cos_t, sin_t)


def _attn_kernel(q_ref, kc_ref, kp_ref, vc_ref, vp_ref, o_ref, lse_ref):
    blk = ATTN_BLOCK
    n = pl.program_id(2)
    lane = lax.broadcasted_iota(jnp.int32, (blk, LANES), 1)
    lo = lane < HEAD_DIM
    qi = lax.broadcasted_iota(jnp.int32, (2 * blk, 2 * blk), 0) & (blk - 1)
    kj = lax.broadcasted_iota(jnp.int32, (2 * blk, 2 * blk), 1)
    dist = qi + blk - kj
    in_band = jnp.where(dist >= 0, jnp.where(dist <= ATTN_SPAN, 1, 0), 0)
    bias = jnp.where(in_band > 0, 0.0, NEG)
    bias_first = jnp.where((in_band * jnp.where(kj >= blk, 1, jnp.where(n > 0, 1, 0))) > 0, 0.0, NEG)
    n_res, tq = q_ref.shape[0], q_ref.shape[1]
    for r, j in ((r, j) for r in range(n_res) for j in range(tq // blk)):
        rows = slice(j * blk, (j + 1) * blk)
        prev = slice((j - 1) * blk, j * blk)
        mask_bias = bias_first if j == 0 else bias
        lse_tile = jnp.zeros((blk, LANES), F32)
        for hp in range(ATTN_WIDTH // LANES):
            sl = slice(hp * LANES, (hp + 1) * LANES)
            q2 = q_ref[r, rows, sl].astype(F32)
            qq = jnp.concatenate([jnp.where(lo, q2, 0.0), jnp.where(lo, 0.0, q2)], axis=0).astype(BF16)
            k_prev = kp_ref[r, :, sl] if j == 0 else kc_ref[r, prev, sl]
            v_prev = vp_ref[r, :, sl] if j == 0 else vc_ref[r, prev, sl]
            k2 = jnp.concatenate([k_prev, kc_ref[r, rows, sl]], axis=0)
            v2 = jnp.concatenate([v_prev, vc_ref[r, rows, sl]], axis=0)
            s = lax.dot_general(qq, k2, (((1,), (1,)), ((), ())), preferred_element_type=F32) + mask_bias
            m = jnp.max(s, axis=1, keepdims=True)
            p = jnp.exp(s - m)
            l = jnp.sum(p, axis=1, keepdims=True)
            pv = jnp.dot(p.astype(BF16), v2, preferred_element_type=F32)
            o = pv / l
            o_ref[r, rows, sl] = jnp.where(lo, o[:blk], o[blk:]).astype(BF16)
            lse = m + jnp.log(l)
            lse_tile = jnp.where(lane == 2 * hp, lse[:blk],
                                 jnp.where(lane == 2 * hp + 1, lse[blk:], lse_tile))
        lse_ref[r, rows, :] = lse_tile


def dilated_attention(q, k, v, batch, seq, dilation):
    length = seq // dilation
    w = ATTN_WIDTH
    blk = ATTN_BLOCK
    tq = min(ATTN_Q_ROWS, length)
    n_res = ATTN_Q_ROWS // tq
    cur = lambda b, r, n: (b, r, n, 0)
    prev = lambda b, r, n: (b, r, jnp.maximum(n * (tq // blk) - 1, 0), 0)
    big = pl.BlockSpec((None, n_res, tq, w), cur)
    small = pl.BlockSpec((None, n_res, blk, w), prev)
    return pl.pallas_call(
        _attn_kernel,
        out_shape=(jax.ShapeDtypeStruct((batch, dilation, length, w), BF16),
                   jax.ShapeDtypeStruct((batch, dilation, length, LANES), F32)),
        grid=(batch, dilation // n_res, length // tq),
        in_specs=[big, big, small, big, small],
        out_specs=(big, pl.BlockSpec((None, n_res, tq, LANES), cur)),
        compiler_params=_cparams(3),
        name=f"dilated_attention_d{dilation}",
    )(q, k, k, v, v)


def _ssd_kernel(z_ref, xbc_ref, halo_ref, dt_ref, shift_ref, cw_ref, cb_ref, dtb_ref, alog_ref, dskip_ref,
                g_ref, y_ref, st_ref):
    ch = SSD_CHUNK
    c = pl.program_id(1)

    @pl.when(c == 0)
    def _():
        st_ref[...] = jnp.zeros_like(st_ref)

    first_halo = halo_ref[...]
    first_halo = jnp.where(c > 0, first_halo, jnp.zeros_like(first_halo))
    for j in range(xbc_ref.shape[0] // ch):
        rows = slice(j * ch, (j + 1) * ch)
        halo = first_halo if j == 0 else xbc_ref[j * ch - HALO:j * ch, :]
        y_ref[rows, :] = _ssd_chunk(xbc_ref[rows, :], halo, z_ref[rows, :], dt_ref[rows, :], shift_ref, cw_ref,
                                    cb_ref, dtb_ref, alog_ref, dskip_ref, g_ref, st_ref)


def _ssd_chunk(xbc_in, halo, z_in, dt_in, shift_ref, cw_ref, cb_ref, dtb_ref, alog_ref, dskip_ref, g_ref, st_ref):
    ch = SSD_CHUNK
    ext = jnp.concatenate([halo, xbc_in], axis=0)
    conv = cb_ref[...] + cw_ref[SSD_CONV - 1:SSD_CONV, :] * xbc_in.astype(F32)
    for t in range(SSD_CONV - 1):
        conv = conv + cw_ref[t:t + 1, :] * jnp.dot(shift_ref[t], ext, preferred_element_type=F32)
    xbc = conv * _sigmoid(conv)
    xh = xbc[:, :SSD_WIDTH]
    bm = xbc[:, SSD_WIDTH:SSD_WIDTH + SSD_GROUPS * SSD_STATE]
    cm = xbc[:, SSD_WIDTH + SSD_GROUPS * SSD_STATE:]

    lane = lax.broadcasted_iota(jnp.int32, (ch, LANES), 1)
    row = lax.broadcasted_iota(jnp.int32, (ch, LANES), 0)
    lo = lane < HEAD_DIM
    lo_row = lo[0:1, :]
    head_lane = lane < SSD_HEADS

    xdt = dt_in + dtb_ref[...]
    dtv = jnp.maximum(xdt, 0.0) + jnp.log(1.0 + jnp.exp(-jnp.abs(xdt)))
    a = jnp.where(head_lane[0:1, :], -jnp.exp(alog_ref[...]), 0.0)
    acum = dtv * a
    shift = 1
    while shift < ch:
        acum = acum + jnp.where(row >= shift, pltpu.roll(acum, shift, 0), 0.0)
        shift *= 2
    acum_t = acum.T
    tot = acum[ch - 1:ch, :]
    e_in = jnp.exp(acum)
    e_out = jnp.exp(tot - acum)
    e_tot = jnp.exp(tot)
    causal = row >= lane

    ys = []
    for g in range(SSD_GROUPS):
        bg = bm[:, g * SSD_STATE:(g + 1) * SSD_STATE]
        cg = cm[:, g * SSD_STATE:(g + 1) * SSD_STATE].astype(BF16)
        cb = lax.dot_general(cg, bg.astype(BF16), (((1,), (1,)), ((), ())),
                             preferred_element_type=F32)
        bg_t = bg.T.astype(BF16)
        for pp in range(2):
            pair = 2 * g + pp
            ha = 2 * pair
            sl = slice(pair * LANES, (pair + 1) * LANES)
            xp = xh[:, sl]
            xdt_pair = xp * _pair_cols(lo, dtv, ha)
            xb16 = xdt_pair.astype(BF16)
            diag = []
            for h in (ha, ha + 1):
                seg = acum[:, h:h + 1] - acum_t[h:h + 1, :]
                decay = jnp.exp(jnp.where(causal, seg, NEG))
                diag.append(jnp.dot((cb * decay).astype(BF16), xb16, preferred_element_type=F32))
            y = jnp.where(lo, diag[0], diag[1])
            state = st_ref[pair]
            y = y + jnp.dot(cg, state.astype(BF16), preferred_element_type=F32) * _pair_cols(lo, e_in, ha)
            xout = (xdt_pair * _pair_cols(lo, e_out, ha)).astype(BF16)
            st_ref[pair] = (state * _pair_cols(lo_row, e_tot, ha)
                            + jnp.dot(bg_t, xout, preferred_element_type=F32))
            ys.append(y + dskip_ref[:, sl] * xp)
    y = jnp.concatenate(ys, axis=1)
    zz = z_in.astype(F32)
    y = y * (zz * _sigmoid(zz))
    gw = SSD_WIDTH // SSD_GROUPS
    outs = []
    for g in range(SSD_GROUPS):
        yg = y[:, g * gw:(g + 1) * gw]
        ms = jnp.mean(yg * yg, axis=1, keepdims=True)
        outs.append(yg * lax.rsqrt(ms + RMS_EPS))
    return (jnp.concatenate(outs, axis=1) * g_ref[...]).astype(BF16)


def ssd_mixer(z, xbc, dt_raw, conv_w, conv_b, dt_bias, a_log, d_skip, norm_g, batch, seq):
    n = batch * seq
    ch = SSD_CHUNK
    step = SSD_STEP_ROWS
    nc = seq // step
    pad = lambda t: jnp.pad(t.astype(F32), (0, LANES - t.shape[0]))[None, :]
    cur = lambda b, c: (b * nc + c, 0)
    halo = lambda b, c: (jnp.maximum((b * nc + c) * (step // HALO) - 1, 0), 0)
    full = lambda b, c: (0, 0)
    r_idx = lax.broadcasted_iota(jnp.int32, (SSD_CONV - 1, ch, HALO + ch), 1)
    j_idx = lax.broadcasted_iota(jnp.int32, (SSD_CONV - 1, ch, HALO + ch), 2)
    t_idx = lax.broadcasted_iota(jnp.int32, (SSD_CONV - 1, ch, HALO + ch), 0)
    shift = jnp.where(j_idx == r_idx + HALO - (SSD_CONV - 1) + t_idx, 1.0, 0.0).astype(BF16)
    return pl.pallas_call(
        _ssd_kernel,
        out_shape=jax.ShapeDtypeStruct((n, SSD_WIDTH), BF16),
        grid=(batch, nc),
        in_specs=[pl.BlockSpec((step, SSD_WIDTH), cur),
                  pl.BlockSpec((step, SSD_CONV_DIM), cur),
                  pl.BlockSpec((HALO, SSD_CONV_DIM), halo),
                  pl.BlockSpec((step, LANES), cur),
                  pl.BlockSpec(shift.shape, lambda b, c: (0, 0, 0)),
                  pl.BlockSpec((SSD_CONV, SSD_CONV_DIM), full),
                  pl.BlockSpec((1, SSD_CONV_DIM), full),
                  pl.BlockSpec((1, LANES), full),
                  pl.BlockSpec((1, LANES), full),
                  pl.BlockSpec((1, SSD_WIDTH), full),
                  pl.BlockSpec((1, SSD_WIDTH), full)],
        out_specs=pl.BlockSpec((step, SSD_WIDTH), cur),
        scratch_shapes=[pltpu.VMEM((SSD_HEADS // 2, SSD_STATE, LANES), F32)],
        compiler_params=_cparams(2),
        name="ssd_mixer",
    )(z, xbc, xbc, dt_raw, shift, conv_w.astype(F32), conv_b.astype(F32)[None, :], pad(dt_bias), pad(a_log),
      jnp.repeat(d_skip.astype(F32), HEAD_DIM)[None, :], norm_g.astype(F32)[None, :])


def _layer_norm(r, g, b):
    mu = jnp.mean(r, axis=1, keepdims=True)
    d = r - mu
    var = jnp.mean(d * d, axis=1, keepdims=True)
    return d * lax.rsqrt(var + LN_EPS) * g + b


def _outproj_kernel(tiles_per_seq, o1_ref, o4_ref, o16_ref, l1_ref, l4_ref, l16_ref, ssd_ref, sb_ref,
                    u_ref, uh_ref, x_ref, cw_ref, wout_ref, g_ref, b_ref, spread_ref, rw_ref,
                    rb_ref, x1_ref, logit_ref, ext_ref, operm_ref, lperm_ref):
    tm = x_ref.shape[0]
    i = pl.program_id(0)
    n_slab = ATTN_WIDTH // LANES

    for idx, (d, o_ref, l_ref) in enumerate(((4, o4_ref, l4_ref), (16, o16_ref, l16_ref))):
        for r in range(d):
            rows = pl.ds(r, tm // d, stride=d)
            lperm_ref[idx, rows, :] = l_ref[r]
            for j in range(n_slab):
                operm_ref[idx * n_slab + j, rows, :] = o_ref[r, :, j * LANES:(j + 1) * LANES].astype(F32)

    lses = (l1_ref[...], lperm_ref[0], lperm_ref[1])
    top = jnp.maximum(jnp.maximum(lses[0], lses[1]), lses[2])
    es = [jnp.exp(l - top) for l in lses]
    den = es[0] + es[1] + es[2]
    wide = []
    for e in es:
        w = e / den
        w_hi = w.astype(BF16)
        w_lo = (w - w_hi.astype(F32)).astype(BF16)
        wide.append(jnp.dot(jnp.concatenate([w_hi, w_lo], axis=1), spread_ref[...], preferred_element_type=F32))
    attn = []
    for hp in range(n_slab):
        sl = slice(hp * LANES, (hp + 1) * LANES)
        acc = wide[0][:, sl] * o1_ref[:, sl].astype(F32)
        acc = acc + wide[1][:, sl] * operm_ref[hp]
        acc = acc + wide[2][:, sl] * operm_ref[n_slab + hp]
        attn.append(acc.astype(BF16))

    seq_start = (i % tiles_per_seq) == 0
    ext_ref[0:HALO, :] = jnp.where(seq_start, 0.0, uh_ref[...].astype(F32))
    ext_ref[HALO:HALO + tm, :] = u_ref[...].astype(F32)
    conv = jnp.zeros((tm, SC_WIDTH), F32)
    for t in range(SC_CONV):
        off = HALO - (SC_CONV - 1) + t
        conv = conv + cw_ref[t:t + 1, :] * ext_ref[off:off + tm, :]
    gated = (sb_ref[...].astype(F32) * conv).astype(BF16)

    mixed = jnp.concatenate(attn + [ssd_ref[...], gated], axis=1)
    mix = jnp.dot(mixed, wout_ref[...], preferred_element_type=F32)
    x1 = _layer_norm(ALPHA * x_ref[...] + mix, g_ref[...], b_ref[...])
    x1_ref[...] = x1

    xh = x1.astype(BF16)
    xm = (x1 - xh.astype(F32)).astype(BF16)
    ph = jnp.dot(xh, rw_ref[...], preferred_element_type=F32)
    pm = jnp.dot(xm, rw_ref[...], preferred_element_type=F32)
    logit_ref[...] = pm + pltpu.roll(ph, LANES - N_EXPERTS, 1) + ph + rb_ref[...]


def out_projection(o_list, lse_list, ssd, sb, u, x, sc_conv_w, w_out, ln_g, ln_b, router_w, router_b, seq):
    n = x.shape[0]
    tm = PROJ_ROWS
    tps = seq // tm
    row = lambda i: (i, 0)
    full = lambda i: (0, 0)
    strided = lambda i: (i // tps, 0, i % tps, 0)
    halo = lambda i: (jnp.maximum(i * (tm // HALO) - 1, 0), 0)
    rw_f = router_w.astype(F32)
    rw_hi = rw_f.astype(BF16)
    rw_mid = (rw_f - rw_hi.astype(F32)).astype(BF16)
    rw = jnp.pad(jnp.concatenate([rw_hi, rw_mid], axis=1), ((0, 0), (0, LANES - 2 * N_EXPERTS)))
    rb = jnp.pad(router_b.astype(F32), (0, LANES - N_EXPERTS))[None, :]
    spread = jnp.where(lax.broadcasted_iota(jnp.int32, (2 * LANES, ATTN_WIDTH), 0) % LANES
                       == lax.broadcasted_iota(jnp.int32, (2 * LANES, ATTN_WIDTH), 1) // HEAD_DIM,
                       1.0, 0.0).astype(BF16)
    wide = lambda wd: pl.BlockSpec((tm, wd), row)
    perm = lambda d, wd: pl.BlockSpec((None, d, tm // d, wd), strided)
    const = lambda a: pl.BlockSpec(a.shape, full)
    wout = w_out.astype(BF16)
    cw = sc_conv_w.astype(F32)
    g = ln_g.astype(F32)[None, :]
    b = ln_b.astype(F32)[None, :]
    n_slab = ATTN_WIDTH // LANES
    return pl.pallas_call(
        functools.partial(_outproj_kernel, tps),
        out_shape=(jax.ShapeDtypeStruct((n, D_MODEL), F32), jax.ShapeDtypeStruct((n, LANES), F32)),
        grid=(n // tm,),
        in_specs=[wide(ATTN_WIDTH), perm(4, ATTN_WIDTH), perm(16, ATTN_WIDTH),
                  wide(LANES), perm(4, LANES), perm(16, LANES),
                  wide(SSD_WIDTH), wide(SC_WIDTH), wide(SC_WIDTH), pl.BlockSpec((HALO, SC_WIDTH), halo),
                  wide(D_MODEL), const(cw), const(wout), const(g), const(b),
                  const(spread), const(rw), const(rb)],
        out_specs=(wide(D_MODEL), wide(LANES)),
        scratch_shapes=[pltpu.VMEM((HALO + tm, SC_WIDTH), F32),
                        pltpu.VMEM((2 * n_slab, tm, LANES), F32),
                        pltpu.VMEM((2, tm, LANES), F32)],
        compiler_params=_cparams(1),
        name="out_projection",
    )(*o_list, *lse_list, ssd, sb, u, u, x, cw, wout, g, b, spread, rw, rb)


INFO_IDX, INFO_SLOT, INFO_GATE = 0, TOP_K, 2 * TOP_K
TAB_RUN, TAB_BASE, TAB_CHUNKS = 0, 1, 2


def _route_kernel(logit_ref, upper_ref, info_ref, tab_ref, cnt_ref, run_ref):
    @pl.when(pl.program_id(0) == 0)
    def _():
        run_ref[...] = jnp.zeros_like(run_ref)

    tm = ROUTE_ROWS
    for j in range(logit_ref.shape[0] // tm):
        info, tab = _route_tile(logit_ref[j * tm:(j + 1) * tm, :], upper_ref, run_ref)
        info_ref[j * tm:(j + 1) * tm, :] = info
        tab_ref[j * SUBLANES:(j + 1) * SUBLANES, :] = tab
    cnt_ref[...] = jnp.broadcast_to(run_ref[...], cnt_ref.shape)


def _route_tile(logits, upper_ref, run_ref):
    tm = logits.shape[0]
    ne = N_EXPERTS
    rec_rows = 2 * SUBLANES
    work = logits.T[:ne, :]
    eidx = lax.broadcasted_iota(jnp.int32, (ne, tm), 0).astype(F32)
    sel = jnp.zeros((ne, tm), F32)
    vals, idxs, hits = [], [], []
    for _ in range(TOP_K):
        m = jnp.max(work, axis=0, keepdims=True)
        idx = jnp.min(jnp.where(work == m, eidx, float(ne)), axis=0, keepdims=True)
        hit = eidx == idx
        work = jnp.where(hit, NEG, work)
        sel = sel + jnp.where(hit, 1.0, 0.0)
        vals.append(m)
        idxs.append(idx)
        hits.append(hit)
    es = [jnp.exp(v - vals[0]) for v in vals]
    den = es[0] + es[1] + es[2] + es[3]
    before = jnp.dot(sel.astype(BF16), upper_ref[...], preferred_element_type=F32)
    cnt = jnp.sum(sel, axis=1, keepdims=True)
    chunks = jnp.floor((cnt + (CHUNK - 1)) * (1.0 / CHUNK))
    row = lax.broadcasted_iota(jnp.int32, (ne, LANES), 0)
    chunks_w = jnp.broadcast_to(chunks, (ne, LANES))
    incl = chunks_w
    shift = 1
    while shift < ne:
        incl = incl + jnp.where(row >= shift, pltpu.roll(incl, shift, 0), 0.0)
        shift *= 2
    base = (incl - chunks_w)[:, 0:1]

    sub = lax.broadcasted_iota(jnp.int32, (rec_rows, tm), 0)
    rec = jnp.zeros((rec_rows, tm), F32)
    for k in range(TOP_K):
        rank = jnp.sum(jnp.where(hits[k], before, 0.0), axis=0, keepdims=True)
        first = jnp.sum(jnp.where(hits[k], base, 0.0), axis=0, keepdims=True)
        rec = jnp.where(sub == INFO_IDX + k, idxs[k], rec)
        rec = jnp.where(sub == INFO_SLOT + k, first * CHUNK + rank, rec)
        rec = jnp.where(sub == INFO_GATE + k, es[k] / den, rec)
    info = jnp.concatenate([rec, jnp.zeros((LANES - rec_rows, tm), F32)], axis=0).T

    lane = lax.broadcasted_iota(jnp.int32, (ne, LANES), 1)
    cnt8 = jnp.floor((cnt + (SUBLANES - 1)) * (1.0 / SUBLANES)) * SUBLANES
    cols = jnp.where(lane == 0, cnt8, jnp.where(lane == 1, base, jnp.where(lane == 2, chunks, 0.0)))
    rows = jnp.concatenate([cols, jnp.zeros((LANES - ne, LANES), F32)], axis=0).T
    sub8 = lax.broadcasted_iota(jnp.int32, (SUBLANES, LANES), 0)
    tab = jnp.where(sub8 == TAB_RUN, run_ref[...],
                    jnp.where(sub8 == TAB_BASE, rows[1:2, :], jnp.where(sub8 == TAB_CHUNKS, rows[2:3, :], 0.0)))
    run_ref[...] = run_ref[...] + rows[0:1, :]
    return info, tab


def route(logits):
    n = logits.shape[0]
    tm = ROUTE_ROWS
    r = lax.broadcasted_iota(jnp.int32, (tm, tm), 0)
    c = lax.broadcasted_iota(jnp.int32, (tm, tm), 1)
    upper = jnp.where(r < c, 1.0, 0.0).astype(BF16)
    return pl.pallas_call(
        _route_kernel,
        out_shape=(jax.ShapeDtypeStruct((n, LANES), F32),
                   jax.ShapeDtypeStruct((n // tm * SUBLANES, LANES), F32),
                   jax.ShapeDtypeStruct((SUBLANES, LANES), F32)),
        grid=(n // (tm * ROUTE_STEP_TILES),),
        in_specs=[pl.BlockSpec((tm * ROUTE_STEP_TILES, LANES), lambda i: (i, 0)),
                  pl.BlockSpec((tm, tm), lambda i: (0, 0))],
        out_specs=(pl.BlockSpec((tm * ROUTE_STEP_TILES, LANES), lambda i: (i, 0)),
                   pl.BlockSpec((SUBLANES * ROUTE_STEP_TILES, LANES), lambda i: (i, 0)),
                   pl.BlockSpec((SUBLANES, LANES), lambda i: (0, 0))),
        scratch_shapes=[pltpu.VMEM((1, LANES), F32)],
        compiler_params=_cparams(1),
        name="moe_route",
    )(logits, upper)


def _start_all(n_chunks, make_copy):
    def start(s, carry):
        make_copy(s).start()
        return carry

    lax.fori_loop(0, n_chunks, start, 0)


def _wait_all(n_chunks, make_wait):
    groups = n_chunks // WAIT_GROUP

    def wait_group(g, carry):
        make_wait(WAIT_GROUP).wait()
        return carry

    def wait(s, carry):
        make_wait(1).wait()
        return carry

    lax.fori_loop(0, groups, wait_group, 0)
    lax.fori_loop(groups * WAIT_GROUP, n_chunks, wait, 0)


def _dispatch_kernel(pslot_ref, pdst_ref, npair_ref, sslot_ref, sdst_ref, nsingle_ref, nslot_ref, zdst_ref,
                     x_ref, info_ref, buf_ref, slots_ref, zero_ref, sems, zsem):
    i = pl.program_id(0)
    tm = x_ref.shape[0]
    cur = i % 2

    def put(tile, side, slot_tab, dst_tab, width, chunks):
        return lambda s: pltpu.make_async_copy(
            slots_ref.at[side, pl.ds(pl.multiple_of(slot_tab[tile * width + s] * CHUNK, CHUNK), chunks * CHUNK)],
            buf_ref.at[pl.ds(pl.multiple_of(dst_tab[tile * width + s], SUBLANES), chunks * CHUNK)],
            sems.at[side])

    def start_tile(tile, side):
        _start_all(npair_ref[tile], put(tile, side, pslot_ref, pdst_ref, MAX_PAIRS, 2))
        _start_all(nsingle_ref[tile], put(tile, side, sslot_ref, sdst_ref, N_EXPERTS, 1))

    def landed(side):
        return lambda k: pltpu.make_async_copy(slots_ref.at[side, pl.ds(0, k * CHUNK)],
                                               buf_ref.at[pl.ds(0, k * CHUNK)], sems.at[side])

    @pl.when(i == 0)
    def _():
        zero_ref[...] = jnp.zeros_like(zero_ref)

        def zero_copy(e):
            row = pl.multiple_of(jnp.maximum(zdst_ref[e], 0), MOE_BLOCK)
            return pltpu.make_async_copy(zero_ref, buf_ref.at[pl.ds(row, MOE_BLOCK)], zsem)

        def start(e, carry):
            @pl.when(zdst_ref[e] >= 0)
            def _():
                zero_copy(e).start()
            return carry

        def wait(e, carry):
            @pl.when(zdst_ref[e] >= 0)
            def _():
                zero_copy(e).wait()
            return carry

        lax.fori_loop(0, zdst_ref.shape[0], start, 0)
        lax.fori_loop(0, zdst_ref.shape[0], wait, 0)

    xb = x_ref[...].astype(BF16)
    slot_t = info_ref[...].T[INFO_SLOT:INFO_SLOT + TOP_K, :]
    local = lax.broadcasted_iota(jnp.int32, (tm, tm), 0).astype(F32).astype(BF16)
    one = jnp.ones((tm, tm), BF16)
    zero = jnp.zeros((tm, tm), BF16)
    for c in range(SLOT_ROWS // tm):
        target = (slot_t - float(c * tm)).astype(BF16)
        onehot = zero
        for k in range(TOP_K):
            onehot = onehot + jnp.where(local == target[k:k + 1, :], one, zero)
        rows = jnp.dot(onehot, xb, preferred_element_type=F32)
        slots_ref[cur, c * tm:(c + 1) * tm, :] = _pack_rows(rows)

    @pl.when(i > 0)
    def _():
        _wait_all(nslot_ref[i - 1], landed(1 - cur))

    start_tile(i, cur)

    @pl.when(i == pl.num_programs(0) - 1)
    def _():
        _wait_all(nslot_ref[i], landed(cur))


def dispatch(x1, info, copies, zdst, padded_rows):
    n = x1.shape[0]
    tm = ROUTE_ROWS
    row = lambda i, *_: (i, 0)
    return pl.pallas_call(
        _dispatch_kernel,
        out_shape=jax.ShapeDtypeStruct((padded_rows, PACKED), U32),
        grid_spec=pltpu.PrefetchScalarGridSpec(
            num_scalar_prefetch=len(copies) + 1,
            grid=(n // tm,),
            in_specs=[pl.BlockSpec((tm, D_MODEL), row), pl.BlockSpec((tm, LANES), row)],
            out_specs=pl.BlockSpec(memory_space=pl.ANY),
            scratch_shapes=[pltpu.VMEM((2, SLOT_ROWS, PACKED), U32), pltpu.VMEM((MOE_BLOCK, PACKED), U32),
                            pltpu.SemaphoreType.DMA((2,)), pltpu.SemaphoreType.DMA(())]),
        compiler_params=_cparams(1),
        name="moe_dispatch",
    )(*copies, zdst, x1, info)


def _expert_kernel(blk_expert_ref, n_used_ref, xs_ref, wgu_ref, bgu_ref, wd_ref, bd_ref, y_ref):
    del blk_expert_ref
    used = pl.program_id(0) < n_used_ref[0]

    @pl.when(jnp.logical_not(used))
    def _():
        y_ref[...] = jnp.zeros_like(y_ref)

    @pl.when(used)
    def _():
        xb = _unpack_rows(xs_ref[...])
        gu = jnp.dot(xb, wgu_ref[...].astype(BF16), preferred_element_type=F32) + bgu_ref[...]
        gate = jnp.minimum(gu[:, :EXPERT_FF], SWIGLU_LIMIT)
        up = jnp.clip(gu[:, EXPERT_FF:], -SWIGLU_LIMIT, SWIGLU_LIMIT)
        h = (up + 1.0) * gate * _sigmoid(SWIGLU_ALPHA * gate)
        y = jnp.dot(h.astype(BF16), wd_ref[...].astype(BF16), preferred_element_type=F32) + bd_ref[...]
        y_ref[...] = _pack_rows(y.astype(BF16).astype(F32))


def expert_mlp(xs, blk_expert, n_used, layer, w_gu, b_gu, w_down, b_down):
    padded_rows = xs.shape[0]
    blk = MOE_BLOCK
    rows_in = lambda i, be, nu: (jnp.minimum(i, nu[0] - 1), 0)
    per_expert = lambda i, be, nu: (layer, be[i], 0, 0)
    return pl.pallas_call(
        _expert_kernel,
        out_shape=jax.ShapeDtypeStruct((padded_rows, PACKED), U32),
        grid_spec=pltpu.PrefetchScalarGridSpec(
            num_scalar_prefetch=2,
            grid=(padded_rows // blk,),
            in_specs=[pl.BlockSpec((blk, PACKED), rows_in),
                      pl.BlockSpec((None, None, D_MODEL, 2 * EXPERT_FF), per_expert),
                      pl.BlockSpec((None, None, 1, 2 * EXPERT_FF), per_expert),
                      pl.BlockSpec((None, None, EXPERT_FF, D_MODEL), per_expert),
                      pl.BlockSpec((None, None, 1, D_MODEL), per_expert)],
            out_specs=pl.BlockSpec((blk, PACKED), lambda i, be, nu: (i, 0))),
        compiler_params=_cparams(1),
        name="expert_mlp",
    )(blk_expert, n_used, xs, w_gu, b_gu, w_down, b_down)


def _combine_kernel(pslot_ref, pdst_ref, npair_ref, sslot_ref, sdst_ref, nsingle_ref, nslot_ref,
                    x_ref, info_ref, g_ref, b_ref, y_ref, out_ref, slots_ref, sems):
    i = pl.program_id(0)
    tm = x_ref.shape[0]
    cur = i % 2

    def fetch(tile, side, slot_tab, dst_tab, width, chunks):
        return lambda s: pltpu.make_async_copy(
            y_ref.at[pl.ds(pl.multiple_of(dst_tab[tile * width + s], SUBLANES), chunks * CHUNK)],
            slots_ref.at[side, pl.ds(pl.multiple_of(slot_tab[tile * width + s] * CHUNK, CHUNK), chunks * CHUNK)],
            sems.at[side])

    def start_tile(tile, side):
        _start_all(npair_ref[tile], fetch(tile, side, pslot_ref, pdst_ref, MAX_PAIRS, 2))
        _start_all(nsingle_ref[tile], fetch(tile, side, sslot_ref, sdst_ref, N_EXPERTS, 1))

    @pl.when(i == 0)
    def _():
        slots_ref[...] = jnp.zeros_like(slots_ref)
        start_tile(0, 0)

    @pl.when(i + 1 < pl.num_programs(0))
    def _():
        start_tile(i + 1, 1 - cur)

    _wait_all(nslot_ref[i], lambda k: pltpu.make_async_copy(
        y_ref.at[pl.ds(0, k * CHUNK)], slots_ref.at[cur, pl.ds(0, k * CHUNK)], sems.at[cur]))

    info = info_ref[...]
    ffn = jnp.zeros((tm, D_MODEL), F32)
    local = lax.broadcasted_iota(jnp.int32, (tm, tm), 1).astype(F32).astype(BF16)
    zero = jnp.zeros((tm, tm), BF16)
    gates = [info[:, INFO_GATE + k:INFO_GATE + k + 1].astype(BF16) for k in range(TOP_K)]
    for c in range(SLOT_ROWS // tm):
        w = zero
        for k in range(TOP_K):
            target = (info[:, INFO_SLOT + k:INFO_SLOT + k + 1] - float(c * tm)).astype(BF16)
            w = w + jnp.where(local == target, gates[k], zero)
        rows = _unpack_rows(slots_ref[cur, c * tm:(c + 1) * tm, :])
        ffn = ffn + jnp.dot(w, rows, preferred_element_type=F32)
    out_ref[...] = _layer_norm(ALPHA * x_ref[...] + ffn, g_ref[...], b_ref[...])


def combine(x1, info, copies, y, ln_g, ln_b):
    n = x1.shape[0]
    tm = ROUTE_ROWS
    row = lambda i, *_: (i, 0)
    full = lambda i, *_: (0, 0)
    return pl.pallas_call(
        _combine_kernel,
        out_shape=jax.ShapeDtypeStruct((n, D_MODEL), F32),
        grid_spec=pltpu.PrefetchScalarGridSpec(
            num_scalar_prefetch=len(copies),
            grid=(n // tm,),
            in_specs=[pl.BlockSpec((tm, D_MODEL), row),
                      pl.BlockSpec((tm, LANES), row),
                      pl.BlockSpec((1, D_MODEL), full),
                      pl.BlockSpec((1, D_MODEL), full),
                      pl.BlockSpec(memory_space=pl.ANY)],
            out_specs=pl.BlockSpec((tm, D_MODEL), row),
            scratch_shapes=[pltpu.VMEM((2, SLOT_ROWS, PACKED), U32), pltpu.SemaphoreType.DMA((2,))]),
        compiler_params=_cparams(1),
        name="moe_combine",
    )(*copies, x1, info, ln_g.astype(F32)[None, :], ln_b.astype(F32)[None, :], y)


def moe_sublayer(x1, logits, layer, w_gu, b_gu, w_down, b_down, ln_g, ln_b):
    n = x1.shape[0]
    blk = MOE_BLOCK
    n_tiles = n // ROUTE_ROWS
    padded_rows = n * TOP_K + (SUBLANES - 1) * N_EXPERTS * n_tiles + N_EXPERTS * (blk + CHUNK)
    padded_rows = -(-padded_rows // blk) * blk
    info, tab, cnt = route(logits)

    i32 = lambda t: t.astype(jnp.int32)
    tab = tab.reshape(n_tiles, SUBLANES, LANES)[:, :, :N_EXPERTS]
    run, base, chunks = i32(tab[:, TAB_RUN]), i32(tab[:, TAB_BASE]), i32(tab[:, TAB_CHUNKS])
    counts = i32(cnt[0, :N_EXPERTS])
    padded = (counts + CHUNK + blk - 1) // blk * blk
    seg_end = jnp.cumsum(padded)
    seg_start = seg_end - padded
    row0 = seg_start[None, :] + run

    def copy_list(per_expert, width, first_chunk):
        ends = jnp.cumsum(per_expert, axis=1)
        item = jnp.arange(width, dtype=jnp.int32)
        owner = jnp.minimum(jnp.sum(i32(ends[:, None, :] <= item[None, :, None]), -1), N_EXPERTS - 1)
        onehot = i32(owner[:, :, None] == jnp.arange(N_EXPERTS, dtype=jnp.int32))
        pick = lambda v: jnp.sum(onehot * v[:, None, :], -1)
        chunk = first_chunk(item[None, :] - pick(ends - per_expert), pick)
        return (pick(base) + chunk).reshape(-1), (pick(row0) + chunk * CHUNK).reshape(-1), ends[:, -1]

    pairs = copy_list(chunks // 2, MAX_PAIRS, lambda q, pick: 2 * q)
    singles = copy_list(chunks % 2, N_EXPERTS, lambda q, pick: pick(chunks) - 1)
    copies = (*pairs, *singles, jnp.sum(chunks, axis=1))
    total = seg_end[N_EXPERTS - 1]
    n_blocks = padded_rows // blk
    blk_row = jnp.arange(n_blocks, dtype=jnp.int32) * blk
    blk_expert = jnp.minimum(jnp.sum(i32(seg_end[None, :] <= blk_row[:, None]), -1), N_EXPERTS - 1)
    n_used = (total // blk).reshape(1)
    tail0 = (seg_start + counts) // blk * blk
    tail1 = jnp.where(tail0 + blk < seg_end, tail0 + blk, -1)
    zdst = jnp.concatenate([tail0, tail1, jnp.where(blk_row >= total, blk_row, -1)])

    xs = dispatch(x1, info, copies, zdst, padded_rows)
    y = expert_mlp(xs, blk_expert, n_used, layer, w_gu.astype(F32), b_gu.astype(F32)[:, :, None, :],
                   w_down.astype(F32), b_down.astype(F32)[:, :, None, :])
    return combine(x1, info, copies, y, ln_g, ln_b)


def kernel(x, w_in, ssd_conv_w, ssd_conv_b, ssd_dt_bias, ssd_a_log, ssd_d, ssd_norm_g, sc_conv_w, w_out,
           ln1_g, ln1_b, router_w, router_b, exp_w_gu, exp_b_gu, exp_w_down, exp_b_down, ln2_g, ln2_b):
    batch, seq, _ = x.shape
    n = batch * seq
    cos_t, sin_t = rope_tables(seq)
    xf = x.reshape(n, D_MODEL).astype(F32)
    a_end = 3 * ATTN_WIDTH
    s_end = a_end + SSD_WIDTH + SSD_CONV_DIM
    d_end = s_end + SSD_HEADS
    for layer in range(w_in.shape[0]):
        w = w_in[layer]
        w_attn = w[:, :a_end].astype(BF16)
        w_ssd = w[:, a_end:s_end].astype(BF16)
        w_dt = jnp.pad(w[:, s_end:d_end], ((0, 0), (0, LANES - SSD_HEADS))).astype(BF16)
        w_sc = w[:, d_end:].astype(BF16)
        (q1, k1, v1, q4, k4, v4, q16, k16, v16, z, xbc, dt_raw, sb, u) = in_projection(
            xf, w_attn, w_ssd, w_dt, w_sc, cos_t, sin_t, batch, seq)
        qkv = ((q1.reshape(batch, 1, seq, ATTN_WIDTH), k1.reshape(batch, 1, seq, ATTN_WIDTH),
                v1.reshape(batch, 1, seq, ATTN_WIDTH)), (q4, k4, v4), (q16, k16, v16))
        attn = [dilated_attention(*t, batch, seq, d) for t, d in zip(qkv, DILATIONS)]
        o_list = [attn[0][0].reshape(n, ATTN_WIDTH), attn[1][0], attn[2][0]]
        lse_list = [attn[0][1].reshape(n, LANES), attn[1][1], attn[2][1]]
        ssd = ssd_mixer(z, xbc, dt_raw, ssd_conv_w[layer], ssd_conv_b[layer], ssd_dt_bias[layer],
                        ssd_a_log[layer], ssd_d[layer], ssd_norm_g[layer], batch, seq)
        x1, logits = out_projection(o_list, lse_list, ssd, sb, u, xf,
                                    sc_conv_w[layer], w_out[layer], ln1_g[layer], ln1_b[layer],
                                    router_w[layer], router_b[layer], seq)
        xf = moe_sublayer(x1, logits, layer, exp_w_gu, exp_b_gu, exp_w_down, exp_b_down,
                          ln2_g[layer], ln2_b[layer])
    return xf.reshape(batch, seq, D_MODEL).astype(x.dtype)
```

```python
import functools

import jax
import jax.numpy as jnp
from jax import lax
from jax.experimental import pallas as pl
from jax.experimental.pallas import tpu as pltpu

F32 = jnp.float32
BF16 = jnp.bfloat16
U32 = jnp.uint32

D_MODEL = 1024
HEAD_DIM = 64
ATTN_WIDTH = 512
DILATIONS = (1, 4, 16)
ATTN_SPAN = 128
ATTN_BLOCK = 128
ATTN_Q_ROWS = 1024
ROPE_THETA = 10000.0

SSD_HEADS = 8
SSD_WIDTH = 512
SSD_GROUPS = 2
SSD_STATE = 128
SSD_CONV = 4
SSD_CHUNK = 128
SSD_STEP_ROWS = 1024
SSD_CONV_DIM = 1024

SC_WIDTH = 512
SC_CONV = 3

N_EXPERTS = 32
TOP_K = 4
EXPERT_FF = 1024
SWIGLU_ALPHA = 1.702
SWIGLU_LIMIT = 7.0

DEPTH = 2
ALPHA = (2.0 * DEPTH) ** 0.25
LN_EPS = 1e-5
RMS_EPS = 1e-5

LANES = 128
SUBLANES = 8
HALO = 16
NEG = -1e30
V7X_VMEM_BYTES = 64 * 1024 * 1024
VMEM_LIMIT = V7X_VMEM_BYTES * 7 // 8
BF16_BITS = 16

PROJ_ROWS = 1024
OUT_PROJ_ROWS = 1024
ROUTE_ROWS = 256
ROUTE_STEP_TILES = 4
MOE_BLOCK = 512
CHUNK = 16
WAIT_GROUP = 8
N_SLOTS = N_EXPERTS + ROUTE_ROWS * TOP_K // CHUNK
SLOT_ROWS = N_SLOTS * CHUNK
MAX_PAIRS = N_SLOTS // 2
PACKED = D_MODEL // 2
HI_MASK = 0xFFFFFFFF ^ ((1 << BF16_BITS) - 1)


def _cparams(n_axes):
    return pltpu.CompilerParams(dimension_semantics=("arbitrary",) * n_axes,
                                vmem_limit_bytes=VMEM_LIMIT)


def _sigmoid(x):
    return 1.0 / (1.0 + jnp.exp(-x))


def _pair_cols(lo, tile, h):
    return jnp.where(lo, tile[:, h:h + 1], tile[:, h + 1:h + 2])


def _pack_rows(v):
    return pltpu.bitcast(v[:, :PACKED], U32) | (pltpu.bitcast(v[:, PACKED:], U32) >> BF16_BITS)


def _unpack_rows(w):
    a = pltpu.bitcast(w & U32(HI_MASK), F32).astype(BF16)
    b = pltpu.bitcast(w << BF16_BITS, F32).astype(BF16)
    return jnp.concatenate([a, b], axis=1)


def _rope_kernel(inv_ref, cos_ref, sin_ref):
    rows = cos_ref.shape[0]
    base = pl.program_id(0) * rows
    pos = (lax.broadcasted_iota(jnp.int32, cos_ref.shape, 0) + base).astype(F32)
    lane = lax.broadcasted_iota(jnp.int32, cos_ref.shape, 1)
    ang = pos * inv_ref[...]
    cos_ref[...] = jnp.cos(ang)
    sin_ref[...] = jnp.where((lane & 32) == 0, -jnp.sin(ang), jnp.sin(ang))


def rope_tables(seq):
    half = HEAD_DIM // 2
    inv = ROPE_THETA ** (-jnp.arange(half, dtype=F32) / half)
    inv = jnp.tile(inv, LANES // half)[None, :]
    rows = PROJ_ROWS
    return pl.pallas_call(
        _rope_kernel,
        out_shape=(jax.ShapeDtypeStruct((seq, LANES), F32),) * 2,
        grid=(seq // rows,),
        in_specs=[pl.BlockSpec((1, LANES), lambda i: (0, 0))],
        out_specs=(pl.BlockSpec((rows, LANES), lambda i: (i, 0)),) * 2,
        compiler_params=_cparams(1),
        name="rope_tables",
    )(inv)


def _inproj_kernel(x_ref, wa_ref, ws_ref, wdt_ref, wc_ref, cos_ref, sin_ref,
                   q1_ref, k1_ref, v1_ref, q4_ref, k4_ref, v4_ref, q16_ref, k16_ref, v16_ref,
                   z_ref, xbc_ref, dt_ref, sb_ref, u_ref, perm_ref):
    tm = x_ref.shape[0]
    xb = x_ref[...].astype(BF16)
    cos = cos_ref[...]
    sin = sin_ref[...]
    lane = lax.broadcasted_iota(jnp.int32, cos.shape, 1)
    first_half = (lane & 32) == 0
    n_slab = ATTN_WIDTH // LANES

    def proj(w_ref, lo, hi):
        return jnp.dot(xb, w_ref[:, lo:hi], preferred_element_type=F32)

    def rope(t):
        rot = jnp.where(first_half, pltpu.roll(t, LANES - 32, 1), pltpu.roll(t, 32, 1))
        return t * cos + rot * sin

    def emit(slabs, nat_ref, strided_refs):
        for j, t in enumerate(slabs):
            perm_ref[j] = t
            nat_ref[:, j * LANES:(j + 1) * LANES] = t.astype(BF16)
        for d, ref in strided_refs:
            for r in range(d):
                for j in range(n_slab):
                    ref[r, :, j * LANES:(j + 1) * LANES] = (
                        perm_ref[j, pl.ds(r, tm // d, stride=d), :].astype(BF16))

    w = ATTN_WIDTH
    q = proj(wa_ref, 0, w)
    emit([rope(q[:, j * LANES:(j + 1) * LANES]) * (HEAD_DIM ** -0.5) for j in range(n_slab)],
         q1_ref, ((4, q4_ref), (16, q16_ref)))
    k = proj(wa_ref, w, 2 * w)
    emit([rope(k[:, j * LANES:(j + 1) * LANES]) for j in range(n_slab)], k1_ref, ((4, k4_ref), (16, k16_ref)))
    v = proj(wa_ref, 2 * w, 3 * w)
    emit([v[:, j * LANES:(j + 1) * LANES] for j in range(n_slab)], v1_ref, ((4, v4_ref), (16, v16_ref)))
    z_ref[...] = proj(ws_ref, 0, SSD_WIDTH).astype(BF16)
    xbc_ref[...] = proj(ws_ref, SSD_WIDTH, SSD_WIDTH + SSD_CONV_DIM).astype(BF16)
    dt_ref[...] = jnp.dot(xb, wdt_ref[...], preferred_element_type=F32)
    sb_ref[...] = proj(wc_ref, 0, SC_WIDTH).astype(BF16)
    c = proj(wc_ref, SC_WIDTH, 2 * SC_WIDTH)
    h = proj(wc_ref, 2 * SC_WIDTH, 3 * SC_WIDTH)
    u_ref[...] = (c * h).astype(BF16)


def in_projection(x, w_attn, w_ssd, w_dt, w_sc, cos_t, sin_t, batch, seq):
    n = x.shape[0]
    tm = PROJ_ROWS
    tps = seq // tm
    row = lambda i: (i, 0)
    full = lambda i: (0, 0)
    tab = lambda i: (i % tps, 0)
    strided = lambda i: (i // tps, 0, i % tps, 0)
    w = ATTN_WIDTH
    nat = jax.ShapeDtypeStruct((n, w), BF16)
    nat_spec = pl.BlockSpec((tm, w), row)
    shapes, specs = [nat] * 3, [nat_spec] * 3
    for d in DILATIONS[1:]:
        shapes += [jax.ShapeDtypeStruct((batch, d, seq // d, w), BF16)] * 3
        specs += [pl.BlockSpec((None, d, tm // d, w), strided)] * 3
    widths = (SSD_WIDTH, SSD_CONV_DIM, LANES, SC_WIDTH, SC_WIDTH)
    dtypes = (BF16, BF16, F32, BF16, BF16)
    shapes += [jax.ShapeDtypeStruct((n, wd), dt) for wd, dt in zip(widths, dtypes)]
    specs += [pl.BlockSpec((tm, wd), row) for wd in widths]
    return pl.pallas_call(
        _inproj_kernel,
        out_shape=tuple(shapes),
        grid=(n // tm,),
        in_specs=[pl.BlockSpec((tm, D_MODEL), row),
                  pl.BlockSpec(w_attn.shape, full, pipeline_mode=pl.Buffered(1)),
                  pl.BlockSpec(w_ssd.shape, full, pipeline_mode=pl.Buffered(1)),
                  pl.BlockSpec(w_dt.shape, full, pipeline_mode=pl.Buffered(1)),
                  pl.BlockSpec(w_sc.shape, full, pipeline_mode=pl.Buffered(1)),
                  pl.BlockSpec((tm, LANES), tab),
                  pl.BlockSpec((tm, LANES), tab)],
        out_specs=tuple(specs),
        scratch_shapes=[pltpu.VMEM((w // LANES, tm, LANES), F32)],
        compiler_params=_cparams(1),
        name="in_projection",
    )(x, w_attn, w_ssd, w_dt, w_sc, cos_t, sin_t)


def _attn_kernel(q_ref, kc_ref, kp_ref, vc_ref, vp_ref, o_ref, lse_ref):
    blk = ATTN_BLOCK
    n = pl.program_id(2)
    lane = lax.broadcasted_iota(jnp.int32, (blk, LANES), 1)
    lo = lane < HEAD_DIM
    qi = lax.broadcasted_iota(jnp.int32, (2 * blk, 2 * blk), 0) & (blk - 1)
    kj = lax.broadcasted_iota(jnp.int32, (2 * blk, 2 * blk), 1)
    dist = qi + blk - kj
    in_band = jnp.where(dist >= 0, jnp.where(dist <= ATTN_SPAN, 1, 0), 0)
    bias = jnp.where(in_band > 0, 0.0, NEG)
    bias_first = jnp.where((in_band * jnp.where(kj >= blk, 1, jnp.where(n > 0, 1, 0))) > 0, 0.0, NEG)
    n_res, tq = q_ref.shape[0], q_ref.shape[1]
    for r, j in ((r, j) for r in range(n_res) for j in range(tq // blk)):
        rows = slice(j * blk, (j + 1) * blk)
        prev = slice((j - 1) * blk, j * blk)
        mask_bias = bias_first if j == 0 else bias
        lse_tile = jnp.zeros((blk, LANES), F32)
        for hp in range(ATTN_WIDTH // LANES):
            sl = slice(hp * LANES, (hp + 1) * LANES)
            q2 = q_ref[r, rows, sl].astype(F32)
            qq = jnp.concatenate([jnp.where(lo, q2, 0.0), jnp.where(lo, 0.0, q2)], axis=0).astype(BF16)
            k_prev = kp_ref[r, :, sl] if j == 0 else kc_ref[r, prev, sl]
            v_prev = vp_ref[r, :, sl] if j == 0 else vc_ref[r, prev, sl]
            k2 = jnp.concatenate([k_prev, kc_ref[r, rows, sl]], axis=0)
            v2 = jnp.concatenate([v_prev, vc_ref[r, rows, sl]], axis=0)
            s = lax.dot_general(qq, k2, (((1,), (1,)), ((), ())), preferred_element_type=F32) + mask_bias
            m = jnp.max(s, axis=1, keepdims=True)
            p = jnp.exp(s - m)
            l = jnp.sum(p, axis=1, keepdims=True)
            pv = jnp.dot(p.astype(BF16), v2, preferred_element_type=F32)
            o = pv / l
            o_ref[r, rows, sl] = jnp.where(lo, o[:blk], o[blk:]).astype(BF16)
            lse = m + jnp.log(l)
            lse_tile = jnp.where(lane == 2 * hp, lse[:blk],
                                 jnp.where(lane == 2 * hp + 1, lse[blk:], lse_tile))
        lse_ref[r, rows, :] = lse_tile


def dilated_attention(q, k, v, batch, seq, dilation):
    length = seq // dilation
    w = ATTN_WIDTH
    blk = ATTN_BLOCK
    tq = min(ATTN_Q_ROWS, length)
    n_res = ATTN_Q_ROWS // tq
    cur = lambda b, r, n: (b, r, n, 0)
    prev = lambda b, r, n: (b, r, jnp.maximum(n * (tq // blk) - 1, 0), 0)
    big = pl.BlockSpec((None, n_res, tq, w), cur)
    small = pl.BlockSpec((None, n_res, blk, w), prev)
    return pl.pallas_call(
        _attn_kernel,
        out_shape=(jax.ShapeDtypeStruct((batch, dilation, length, w), BF16),
                   jax.ShapeDtypeStruct((batch, dilation, length, LANES), F32)),
        grid=(batch, dilation // n_res, length // tq),
        in_specs=[big, big, small, big, small],
        out_specs=(big, pl.BlockSpec((None, n_res, tq, LANES), cur)),
        compiler_params=_cparams(3),
        name=f"dilated_attention_d{dilation}",
    )(q, k, k, v, v)


def _ssd_kernel(z_ref, xbc_ref, halo_ref, dt_ref, shift_ref, cw_ref, cb_ref, dtb_ref, alog_ref, dskip_ref,
                g_ref, y_ref, st_ref):
    ch = SSD_CHUNK
    c = pl.program_id(1)

    @pl.when(c == 0)
    def _():
        st_ref[...] = jnp.zeros_like(st_ref)

    first_halo = halo_ref[...]
    first_halo = jnp.where(c > 0, first_halo, jnp.zeros_like(first_halo))
    for j in range(xbc_ref.shape[0] // ch):
        rows = slice(j * ch, (j + 1) * ch)
        halo = first_halo if j == 0 else xbc_ref[j * ch - HALO:j * ch, :]
        y_ref[rows, :] = _ssd_chunk(xbc_ref[rows, :], halo, z_ref[rows, :], dt_ref[rows, :], shift_ref, cw_ref,
                                    cb_ref, dtb_ref, alog_ref, dskip_ref, g_ref, st_ref)


def _ssd_chunk(xbc_in, halo, z_in, dt_in, shift_ref, cw_ref, cb_ref, dtb_ref, alog_ref, dskip_ref, g_ref, st_ref):
    ch = SSD_CHUNK
    ext = jnp.concatenate([halo, xbc_in], axis=0)
    conv = cb_ref[...] + cw_ref[SSD_CONV - 1:SSD_CONV, :] * xbc_in.astype(F32)
    for t in range(SSD_CONV - 1):
        conv = conv + cw_ref[t:t + 1, :] * jnp.dot(shift_ref[t], ext, preferred_element_type=F32)
    xbc = conv * _sigmoid(conv)
    xh = xbc[:, :SSD_WIDTH]
    bm = xbc[:, SSD_WIDTH:SSD_WIDTH + SSD_GROUPS * SSD_STATE]
    cm = xbc[:, SSD_WIDTH + SSD_GROUPS * SSD_STATE:]

    lane = lax.broadcasted_iota(jnp.int32, (ch, LANES), 1)
    row = lax.broadcasted_iota(jnp.int32, (ch, LANES), 0)
    lo = lane < HEAD_DIM
    lo_row = lo[0:1, :]
    head_lane = lane < SSD_HEADS

    xdt = dt_in + dtb_ref[...]
    dtv = jnp.maximum(xdt, 0.0) + jnp.log(1.0 + jnp.exp(-jnp.abs(xdt)))
    a = jnp.where(head_lane[0:1, :], -jnp.exp(alog_ref[...]), 0.0)
    acum = dtv * a
    shift = 1
    while shift < ch:
        acum = acum + jnp.where(row >= shift, pltpu.roll(acum, shift, 0), 0.0)
        shift *= 2
    acum_t = acum.T
    tot = acum[ch - 1:ch, :]
    e_in = jnp.exp(acum)
    e_out = jnp.exp(tot - acum)
    e_tot = jnp.exp(tot)
    causal = row >= lane

    ys = []
    for g in range(SSD_GROUPS):
        bg = bm[:, g * SSD_STATE:(g + 1) * SSD_STATE]
        cg = cm[:, g * SSD_STATE:(g + 1) * SSD_STATE].astype(BF16)
        cb = lax.dot_general(cg, bg.astype(BF16), (((1,), (1,)), ((), ())),
                             preferred_element_type=F32)
        bg_t = bg.T.astype(BF16)
        for pp in range(2):
            pair = 2 * g + pp
            ha = 2 * pair
            sl = slice(pair * LANES, (pair + 1) * LANES)
            xp = xh[:, sl]
            xdt_pair = xp * _pair_cols(lo, dtv, ha)
            xb16 = xdt_pair.astype(BF16)
            diag = []
            for h in (ha, ha + 1):
                seg = acum[:, h:h + 1] - acum_t[h:h + 1, :]
                decay = jnp.exp(jnp.where(causal, seg, NEG))
                diag.append(jnp.dot((cb * decay).astype(BF16), xb16, preferred_element_type=F32))
            y = jnp.where(lo, diag[0], diag[1])
            state = st_ref[pair]
            y = y + jnp.dot(cg, state.astype(BF16), preferred_element_type=F32) * _pair_cols(lo, e_in, ha)
            xout = (xdt_pair * _pair_cols(lo, e_out, ha)).astype(BF16)
            st_ref[pair] = (state * _pair_cols(lo_row, e_tot, ha)
                            + jnp.dot(bg_t, xout, preferred_element_type=F32))
            ys.append(y + dskip_ref[:, sl] * xp)
    y = jnp.concatenate(ys, axis=1)
    zz = z_in.astype(F32)
    y = y * (zz * _sigmoid(zz))
    gw = SSD_WIDTH // SSD_GROUPS
    outs = []
    for g in range(SSD_GROUPS):
        yg = y[:, g * gw:(g + 1) * gw]
        ms = jnp.mean(yg * yg, axis=1, keepdims=True)
        outs.append(yg * lax.rsqrt(ms + RMS_EPS))
    return (jnp.concatenate(outs, axis=1) * g_ref[...]).astype(BF16)


def ssd_mixer(z, xbc, dt_raw, conv_w, conv_b, dt_bias, a_log, d_skip, norm_g, batch, seq):
    n = batch * seq
    ch = SSD_CHUNK
    step = SSD_STEP_ROWS
    nc = seq // step
    pad = lambda t: jnp.pad(t.astype(F32), (0, LANES - t.shape[0]))[None, :]
    cur = lambda b, c: (b * nc + c, 0)
    halo = lambda b, c: (jnp.maximum((b * nc + c) * (step // HALO) - 1, 0), 0)
    full = lambda b, c: (0, 0)
    r_idx = lax.broadcasted_iota(jnp.int32, (SSD_CONV - 1, ch, HALO + ch), 1)
    j_idx = lax.broadcasted_iota(jnp.int32, (SSD_CONV - 1, ch, HALO + ch), 2)
    t_idx = lax.broadcasted_iota(jnp.int32, (SSD_CONV - 1, ch, HALO + ch), 0)
    shift = jnp.where(j_idx == r_idx + HALO - (SSD_CONV - 1) + t_idx, 1.0, 0.0).astype(BF16)
    return pl.pallas_call(
        _ssd_kernel,
        out_shape=jax.ShapeDtypeStruct((n, SSD_WIDTH), BF16),
        grid=(batch, nc),
        in_specs=[pl.BlockSpec((step, SSD_WIDTH), cur),
                  pl.BlockSpec((step, SSD_CONV_DIM), cur),
                  pl.BlockSpec((HALO, SSD_CONV_DIM), halo),
                  pl.BlockSpec((step, LANES), cur),
                  pl.BlockSpec(shift.shape, lambda b, c: (0, 0, 0)),
                  pl.BlockSpec((SSD_CONV, SSD_CONV_DIM), full),
                  pl.BlockSpec((1, SSD_CONV_DIM), full),
                  pl.BlockSpec((1, LANES), full),
                  pl.BlockSpec((1, LANES), full),
                  pl.BlockSpec((1, SSD_WIDTH), full),
                  pl.BlockSpec((1, SSD_WIDTH), full)],
        out_specs=pl.BlockSpec((step, SSD_WIDTH), cur),
        scratch_shapes=[pltpu.VMEM((SSD_HEADS // 2, SSD_STATE, LANES), F32)],
        compiler_params=_cparams(2),
        name="ssd_mixer",
    )(z, xbc, xbc, dt_raw, shift, conv_w.astype(F32), conv_b.astype(F32)[None, :], pad(dt_bias), pad(a_log),
      jnp.repeat(d_skip.astype(F32), HEAD_DIM)[None, :], norm_g.astype(F32)[None, :])


def _layer_norm(r, g, b):
    mu = jnp.mean(r, axis=1, keepdims=True)
    d = r - mu
    var = jnp.mean(d * d, axis=1, keepdims=True)
    return d * lax.rsqrt(var + LN_EPS) * g + b


def _outproj_kernel(tiles_per_seq, o1_ref, o4_ref, o16_ref, l1_ref, l4_ref, l16_ref, ssd_ref, sb_ref,
                    u_ref, uh_ref, x_ref, cw_ref, wout_ref, g_ref, b_ref, spread_ref, rw_ref,
                    rb_ref, x1_ref, logit_ref, ext_ref, operm_ref, lperm_ref):
    tm = x_ref.shape[0]
    i = pl.program_id(0)
    n_slab = ATTN_WIDTH // LANES

    for idx, (d, o_ref, l_ref) in enumerate(((4, o4_ref, l4_ref), (16, o16_ref, l16_ref))):
        for r in range(d):
            rows = pl.ds(r, tm // d, stride=d)
            lperm_ref[idx, rows, :] = l_ref[r]
            for j in range(n_slab):
                operm_ref[idx * n_slab + j, rows, :] = o_ref[r, :, j * LANES:(j + 1) * LANES].astype(F32)

    lses = (l1_ref[...], lperm_ref[0], lperm_ref[1])
    top = jnp.maximum(jnp.maximum(lses[0], lses[1]), lses[2])
    es = [jnp.exp(l - top) for l in lses]
    den = es[0] + es[1] + es[2]
    wide = []
    for e in es:
        w = e / den
        w_hi = w.astype(BF16)
        w_lo = (w - w_hi.astype(F32)).astype(BF16)
        wide.append(jnp.dot(jnp.concatenate([w_hi, w_lo], axis=1), spread_ref[...], preferred_element_type=F32))
    attn = []
    for hp in range(n_slab):
        sl = slice(hp * LANES, (hp + 1) * LANES)
        acc = wide[0][:, sl] * o1_ref[:, sl].astype(F32)
        acc = acc + wide[1][:, sl] * operm_ref[hp]
        acc = acc + wide[2][:, sl] * operm_ref[n_slab + hp]
        attn.append(acc.astype(BF16))

    seq_start = (i % tiles_per_seq) == 0
    ext_ref[0:HALO, :] = jnp.where(seq_start, 0.0, uh_ref[...].astype(F32))
    ext_ref[HALO:HALO + tm, :] = u_ref[...].astype(F32)
    conv = jnp.zeros((tm, SC_WIDTH), F32)
    for t in range(SC_CONV):
        off = HALO - (SC_CONV - 1) + t
        conv = conv + cw_ref[t:t + 1, :] * ext_ref[off:off + tm, :]
    gated = (sb_ref[...].astype(F32) * conv).astype(BF16)

    mixed = jnp.concatenate(attn + [ssd_ref[...], gated], axis=1)
    mix = jnp.dot(mixed, wout_ref[...], preferred_element_type=F32)
    x1 = _layer_norm(ALPHA * x_ref[...] + mix, g_ref[...], b_ref[...])
    x1_ref[...] = x1

    xh = x1.astype(BF16)
    xm = (x1 - xh.astype(F32)).astype(BF16)
    ph = jnp.dot(xh, rw_ref[...], preferred_element_type=F32)
    pm = jnp.dot(xm, rw_ref[...], preferred_element_type=F32)
    logit_ref[...] = pm + pltpu.roll(ph, LANES - N_EXPERTS, 1) + ph + rb_ref[...]


def out_projection(o_list, lse_list, ssd, sb, u, x, sc_conv_w, w_out, ln_g, ln_b, router_w, router_b, seq):
    n = x.shape[0]
    tm = OUT_PROJ_ROWS
    tps = seq // tm
    row = lambda i: (i, 0)
    full = lambda i: (0, 0)
    strided = lambda i: (i // tps, 0, i % tps, 0)
    halo = lambda i: (jnp.maximum(i * (tm // HALO) - 1, 0), 0)
    rw_f = router_w.astype(F32)
    rw_hi = rw_f.astype(BF16)
    rw_mid = (rw_f - rw_hi.astype(F32)).astype(BF16)
    rw = jnp.pad(jnp.concatenate([rw_hi, rw_mid], axis=1), ((0, 0), (0, LANES - 2 * N_EXPERTS)))
    rb = jnp.pad(router_b.astype(F32), (0, LANES - N_EXPERTS))[None, :]
    spread = jnp.where(lax.broadcasted_iota(jnp.int32, (2 * LANES, ATTN_WIDTH), 0) % LANES
                       == lax.broadcasted_iota(jnp.int32, (2 * LANES, ATTN_WIDTH), 1) // HEAD_DIM,
                       1.0, 0.0).astype(BF16)
    wide = lambda wd: pl.BlockSpec((tm, wd), row)
    perm = lambda d, wd: pl.BlockSpec((None, d, tm // d, wd), strided)
    const = lambda a: pl.BlockSpec(a.shape, full)
    wout = w_out.astype(BF16)
    cw = sc_conv_w.astype(F32)
    g = ln_g.astype(F32)[None, :]
    b = ln_b.astype(F32)[None, :]
    n_slab = ATTN_WIDTH // LANES
    return pl.pallas_call(
        functools.partial(_outproj_kernel, tps),
        out_shape=(jax.ShapeDtypeStruct((n, D_MODEL), F32), jax.ShapeDtypeStruct((n, LANES), F32)),
        grid=(n // tm,),
        in_specs=[wide(ATTN_WIDTH), perm(4, ATTN_WIDTH), perm(16, ATTN_WIDTH),
                  wide(LANES), perm(4, LANES), perm(16, LANES),
                  wide(SSD_WIDTH), wide(SC_WIDTH), wide(SC_WIDTH), pl.BlockSpec((HALO, SC_WIDTH), halo),
                  wide(D_MODEL), const(cw), const(wout), const(g), const(b),
                  const(spread), const(rw), const(rb)],
        out_specs=(wide(D_MODEL), wide(LANES)),
        scratch_shapes=[pltpu.VMEM((HALO + tm, SC_WIDTH), F32),
                        pltpu.VMEM((2 * n_slab, tm, LANES), F32),
                        pltpu.VMEM((2, tm, LANES), F32)],
        compiler_params=_cparams(1),
        name="out_projection",
    )(*o_list, *lse_list, ssd, sb, u, u, x, cw, wout, g, b, spread, rw, rb)


INFO_IDX, INFO_SLOT, INFO_GATE = 0, TOP_K, 2 * TOP_K
TAB_RUN, TAB_BASE, TAB_CHUNKS = 0, 1, 2


def _route_kernel(logit_ref, upper_ref, info_ref, tab_ref, cnt_ref, run_ref):
    @pl.when(pl.program_id(0) == 0)
    def _():
        run_ref[...] = jnp.zeros_like(run_ref)

    tm = ROUTE_ROWS
    for j in range(logit_ref.shape[0] // tm):
        info, tab = _route_tile(logit_ref[j * tm:(j + 1) * tm, :], upper_ref, run_ref)
        info_ref[j * tm:(j + 1) * tm, :] = info
        tab_ref[j * SUBLANES:(j + 1) * SUBLANES, :] = tab
    cnt_ref[...] = jnp.broadcast_to(run_ref[...], cnt_ref.shape)


def _route_tile(logits, upper_ref, run_ref):
    tm = logits.shape[0]
    ne = N_EXPERTS
    rec_rows = 2 * SUBLANES
    work = logits.T[:ne, :]
    eidx = lax.broadcasted_iota(jnp.int32, (ne, tm), 0).astype(F32)
    sel = jnp.zeros((ne, tm), F32)
    vals, idxs, hits = [], [], []
    for _ in range(TOP_K):
        m = jnp.max(work, axis=0, keepdims=True)
        idx = jnp.min(jnp.where(work == m, eidx, float(ne)), axis=0, keepdims=True)
        hit = eidx == idx
        work = jnp.where(hit, NEG, work)
        sel = sel + jnp.where(hit, 1.0, 0.0)
        vals.append(m)
        idxs.append(idx)
        hits.append(hit)
    es = [jnp.exp(v - vals[0]) for v in vals]
    den = es[0] + es[1] + es[2] + es[3]
    before = jnp.dot(sel.astype(BF16), upper_ref[...], preferred_element_type=F32)
    cnt = jnp.sum(sel, axis=1, keepdims=True)
    chunks = jnp.floor((cnt + (CHUNK - 1)) * (1.0 / CHUNK))
    row = lax.broadcasted_iota(jnp.int32, (ne, LANES), 0)
    chunks_w = jnp.broadcast_to(chunks, (ne, LANES))
    incl = chunks_w
    shift = 1
    while shift < ne:
        incl = incl + jnp.where(row >= shift, pltpu.roll(incl, shift, 0), 0.0)
        shift *= 2
    base = (incl - chunks_w)[:, 0:1]

    sub = lax.broadcasted_iota(jnp.int32, (rec_rows, tm), 0)
    rec = jnp.zeros((rec_rows, tm), F32)
    for k in range(TOP_K):
        rank = jnp.sum(jnp.where(hits[k], before, 0.0), axis=0, keepdims=True)
        first = jnp.sum(jnp.where(hits[k], base, 0.0), axis=0, keepdims=True)
        rec = jnp.where(sub == INFO_IDX + k, idxs[k], rec)
        rec = jnp.where(sub == INFO_SLOT + k, first * CHUNK + rank, rec)
        rec = jnp.where(sub == INFO_GATE + k, es[k] / den, rec)
    info = jnp.concatenate([rec, jnp.zeros((LANES - rec_rows, tm), F32)], axis=0).T

    lane = lax.broadcasted_iota(jnp.int32, (ne, LANES), 1)
    cnt8 = jnp.floor((cnt + (SUBLANES - 1)) * (1.0 / SUBLANES)) * SUBLANES
    cols = jnp.where(lane == 0, cnt8, jnp.where(lane == 1, base, jnp.where(lane == 2, chunks, 0.0)))
    rows = jnp.concatenate([cols, jnp.zeros((LANES - ne, LANES), F32)], axis=0).T
    sub8 = lax.broadcasted_iota(jnp.int32, (SUBLANES, LANES), 0)
    tab = jnp.where(sub8 == TAB_RUN, run_ref[...],
                    jnp.where(sub8 == TAB_BASE, rows[1:2, :], jnp.where(sub8 == TAB_CHUNKS, rows[2:3, :], 0.0)))
    run_ref[...] = run_ref[...] + rows[0:1, :]
    return info, tab


def route(logits):
    n = logits.shape[0]
    tm = ROUTE_ROWS
    r = lax.broadcasted_iota(jnp.int32, (tm, tm), 0)
    c = lax.broadcasted_iota(jnp.int32, (tm, tm), 1)
    upper = jnp.where(r < c, 1.0, 0.0).astype(BF16)
    return pl.pallas_call(
        _route_kernel,
        out_shape=(jax.ShapeDtypeStruct((n, LANES), F32),
                   jax.ShapeDtypeStruct((n // tm * SUBLANES, LANES), F32),
                   jax.ShapeDtypeStruct((SUBLANES, LANES), F32)),
        grid=(n // (tm * ROUTE_STEP_TILES),),
        in_specs=[pl.BlockSpec((tm * ROUTE_STEP_TILES, LANES), lambda i: (i, 0)),
                  pl.BlockSpec((tm, tm), lambda i: (0, 0))],
        out_specs=(pl.BlockSpec((tm * ROUTE_STEP_TILES, LANES), lambda i: (i, 0)),
                   pl.BlockSpec((SUBLANES * ROUTE_STEP_TILES, LANES), lambda i: (i, 0)),
                   pl.BlockSpec((SUBLANES, LANES), lambda i: (0, 0))),
        scratch_shapes=[pltpu.VMEM((1, LANES), F32)],
        compiler_params=_cparams(1),
        name="moe_route",
    )(logits, upper)


def _start_all(n_chunks, make_copy):
    def start(s, carry):
        make_copy(s).start()
        return carry

    lax.fori_loop(0, n_chunks, start, 0)


def _wait_all(n_chunks, make_wait):
    groups = n_chunks // WAIT_GROUP

    def wait_group(g, carry):
        make_wait(WAIT_GROUP).wait()
        return carry

    def wait(s, carry):
        make_wait(1).wait()
        return carry

    lax.fori_loop(0, groups, wait_group, 0)
    lax.fori_loop(groups * WAIT_GROUP, n_chunks, wait, 0)


def _dispatch_kernel(pslot_ref, pdst_ref, npair_ref, sslot_ref, sdst_ref, nsingle_ref, nslot_ref, zdst_ref,
                     x_ref, info_ref, buf_ref, slots_ref, zero_ref, sems, zsem):
    i = pl.program_id(0)
    tm = x_ref.shape[0]
    cur = i % 2

    def put(tile, side, slot_tab, dst_tab, width, chunks):
        return lambda s: pltpu.make_async_copy(
            slots_ref.at[side, pl.ds(pl.multiple_of(slot_tab[tile * width + s] * CHUNK, CHUNK), chunks * CHUNK)],
            buf_ref.at[pl.ds(pl.multiple_of(dst_tab[tile * width + s], SUBLANES), chunks * CHUNK)],
            sems.at[side])

    def start_tile(tile, side):
        _start_all(npair_ref[tile], put(tile, side, pslot_ref, pdst_ref, MAX_PAIRS, 2))
        _start_all(nsingle_ref[tile], put(tile, side, sslot_ref, sdst_ref, N_EXPERTS, 1))

    def landed(side):
        return lambda k: pltpu.make_async_copy(slots_ref.at[side, pl.ds(0, k * CHUNK)],
                                               buf_ref.at[pl.ds(0, k * CHUNK)], sems.at[side])

    @pl.when(i == 0)
    def _():
        zero_ref[...] = jnp.zeros_like(zero_ref)

        def zero_copy(e):
            row = pl.multiple_of(jnp.maximum(zdst_ref[e], 0), MOE_BLOCK)
            return pltpu.make_async_copy(zero_ref, buf_ref.at[pl.ds(row, MOE_BLOCK)], zsem)

        def start(e, carry):
            @pl.when(zdst_ref[e] >= 0)
            def _():
                zero_copy(e).start()
            return carry

        def wait(e, carry):
            @pl.when(zdst_ref[e] >= 0)
            def _():
                zero_copy(e).wait()
            return carry

        lax.fori_loop(0, zdst_ref.shape[0], start, 0)
        lax.fori_loop(0, zdst_ref.shape[0], wait, 0)

    xb = x_ref[...].astype(BF16)
    slot_t = info_ref[...].T[INFO_SLOT:INFO_SLOT + TOP_K, :]
    local = lax.broadcasted_iota(jnp.int32, (tm, tm), 0).astype(F32).astype(BF16)
    one = jnp.ones((tm, tm), BF16)
    zero = jnp.zeros((tm, tm), BF16)
    for c in range(SLOT_ROWS // tm):
        target = (slot_t - float(c * tm)).astype(BF16)
        onehot = zero
        for k in range(TOP_K):
            onehot = onehot + jnp.where(local == target[k:k + 1, :], one, zero)
        rows = jnp.dot(onehot, xb, preferred_element_type=F32)
        slots_ref[cur, c * tm:(c + 1) * tm, :] = _pack_rows(rows)

    @pl.when(i > 0)
    def _():
        _wait_all(nslot_ref[i - 1], landed(1 - cur))

    start_tile(i, cur)

    @pl.when(i == pl.num_programs(0) - 1)
    def _():
        _wait_all(nslot_ref[i], landed(cur))


def dispatch(x1, info, copies, zdst, padded_rows):
    n = x1.shape[0]
    tm = ROUTE_ROWS
    row = lambda i, *_: (i, 0)
    return pl.pallas_call(
        _dispatch_kernel,
        out_shape=jax.ShapeDtypeStruct((padded_rows, PACKED), U32),
        grid_spec=pltpu.PrefetchScalarGridSpec(
            num_scalar_prefetch=len(copies) + 1,
            grid=(n // tm,),
            in_specs=[pl.BlockSpec((tm, D_MODEL), row), pl.BlockSpec((tm, LANES), row)],
            out_specs=pl.BlockSpec(memory_space=pl.ANY),
            scratch_shapes=[pltpu.VMEM((2, SLOT_ROWS, PACKED), U32), pltpu.VMEM((MOE_BLOCK, PACKED), U32),
                            pltpu.SemaphoreType.DMA((2,)), pltpu.SemaphoreType.DMA(())]),
        compiler_params=_cparams(1),
        name="moe_dispatch",
    )(*copies, zdst, x1, info)


def _expert_kernel(blk_expert_ref, n_used_ref, xs_ref, wgu_ref, bgu_ref, wd_ref, bd_ref, y_ref):
    del blk_expert_ref
    used = pl.program_id(0) < n_used_ref[0]

    @pl.when(jnp.logical_not(used))
    def _():
        y_ref[...] = jnp.zeros_like(y_ref)

    @pl.when(used)
    def _():
        xb = _unpack_rows(xs_ref[...])
        gu = jnp.dot(xb, wgu_ref[...].astype(BF16), preferred_element_type=F32) + bgu_ref[...]
        gate = jnp.minimum(gu[:, :EXPERT_FF], SWIGLU_LIMIT)
        up = jnp.clip(gu[:, EXPERT_FF:], -SWIGLU_LIMIT, SWIGLU_LIMIT)
        h = (up + 1.0) * gate * _sigmoid(SWIGLU_ALPHA * gate)
        y = jnp.dot(h.astype(BF16), wd_ref[...].astype(BF16), preferred_element_type=F32) + bd_ref[...]
        y_ref[...] = _pack_rows(y.astype(BF16).astype(F32))


def expert_mlp(xs, blk_expert, n_used, layer, w_gu, b_gu, w_down, b_down):
    padded_rows = xs.shape[0]
    blk = MOE_BLOCK
    rows_in = lambda i, be, nu: (jnp.minimum(i, nu[0] - 1), 0)
    per_expert = lambda i, be, nu: (layer, be[i], 0, 0)
    return pl.pallas_call(
        _expert_kernel,
        out_shape=jax.ShapeDtypeStruct((padded_rows, PACKED), U32),
        grid_spec=pltpu.PrefetchScalarGridSpec(
            num_scalar_prefetch=2,
            grid=(padded_rows // blk,),
            in_specs=[pl.BlockSpec((blk, PACKED), rows_in),
                      pl.BlockSpec((None, None, D_MODEL, 2 * EXPERT_FF), per_expert),
                      pl.BlockSpec((None, None, 1, 2 * EXPERT_FF), per_expert),
                      pl.BlockSpec((None, None, EXPERT_FF, D_MODEL), per_expert),
                      pl.BlockSpec((None, None, 1, D_MODEL), per_expert)],
            out_specs=pl.BlockSpec((blk, PACKED), lambda i, be, nu: (i, 0))),
        compiler_params=_cparams(1),
        name="expert_mlp",
    )(blk_expert, n_used, xs, w_gu, b_gu, w_down, b_down)


def _combine_kernel(pslot_ref, pdst_ref, npair_ref, sslot_ref, sdst_ref, nsingle_ref, nslot_ref,
                    x_ref, info_ref, g_ref, b_ref, y_ref, out_ref, slots_ref, sems):
    i = pl.program_id(0)
    tm = x_ref.shape[0]
    cur = i % 2

    def fetch(tile, side, slot_tab, dst_tab, width, chunks):
        return lambda s: pltpu.make_async_copy(
            y_ref.at[pl.ds(pl.multiple_of(dst_tab[tile * width + s], SUBLANES), chunks * CHUNK)],
            slots_ref.at[side, pl.ds(pl.multiple_of(slot_tab[tile * width + s] * CHUNK, CHUNK), chunks * CHUNK)],
            sems.at[side])

    def start_tile(tile, side):
        _start_all(npair_ref[tile], fetch(tile, side, pslot_ref, pdst_ref, MAX_PAIRS, 2))
        _start_all(nsingle_ref[tile], fetch(tile, side, sslot_ref, sdst_ref, N_EXPERTS, 1))

    @pl.when(i == 0)
    def _():
        slots_ref[...] = jnp.zeros_like(slots_ref)
        start_tile(0, 0)

    @pl.when(i + 1 < pl.num_programs(0))
    def _():
        start_tile(i + 1, 1 - cur)

    _wait_all(nslot_ref[i], lambda k: pltpu.make_async_copy(
        y_ref.at[pl.ds(0, k * CHUNK)], slots_ref.at[cur, pl.ds(0, k * CHUNK)], sems.at[cur]))

    info = info_ref[...]
    ffn = jnp.zeros((tm, D_MODEL), F32)
    local = lax.broadcasted_iota(jnp.int32, (tm, tm), 1).astype(F32).astype(BF16)
    zero = jnp.zeros((tm, tm), BF16)
    gates = [info[:, INFO_GATE + k:INFO_GATE + k + 1].astype(BF16) for k in range(TOP_K)]
    for c in range(SLOT_ROWS // tm):
        w = zero
        for k in range(TOP_K):
            target = (info[:, INFO_SLOT + k:INFO_SLOT + k + 1] - float(c * tm)).astype(BF16)
            w = w + jnp.where(local == target, gates[k], zero)
        rows = _unpack_rows(slots_ref[cur, c * tm:(c + 1) * tm, :])
        ffn = ffn + jnp.dot(w, rows, preferred_element_type=F32)
    out_ref[...] = _layer_norm(ALPHA * x_ref[...] + ffn, g_ref[...], b_ref[...])


def combine(x1, info, copies, y, ln_g, ln_b):
    n = x1.shape[0]
    tm = ROUTE_ROWS
    row = lambda i, *_: (i, 0)
    full = lambda i, *_: (0, 0)
    return pl.pallas_call(
        _combine_kernel,
        out_shape=jax.ShapeDtypeStruct((n, D_MODEL), F32),
        grid_spec=pltpu.PrefetchScalarGridSpec(
            num_scalar_prefetch=len(copies),
            grid=(n // tm,),
            in_specs=[pl.BlockSpec((tm, D_MODEL), row),
                      pl.BlockSpec((tm, LANES), row),
                      pl.BlockSpec((1, D_MODEL), full),
                      pl.BlockSpec((1, D_MODEL), full),
                      pl.BlockSpec(memory_space=pl.ANY)],
            out_specs=pl.BlockSpec((tm, D_MODEL), row),
            scratch_shapes=[pltpu.VMEM((2, SLOT_ROWS, PACKED), U32), pltpu.SemaphoreType.DMA((2,))]),
        compiler_params=_cparams(1),
        name="moe_combine",
    )(*copies, x1, info, ln_g.astype(F32)[None, :], ln_b.astype(F32)[None, :], y)


def moe_sublayer(x1, logits, layer, w_gu, b_gu, w_down, b_down, ln_g, ln_b):
    n = x1.shape[0]
    blk = MOE_BLOCK
    n_tiles = n // ROUTE_ROWS
    padded_rows = n * TOP_K + (SUBLANES - 1) * N_EXPERTS * n_tiles + N_EXPERTS * (blk + CHUNK)
    padded_rows = -(-padded_rows // blk) * blk
    info, tab, cnt = route(logits)

    i32 = lambda t: t.astype(jnp.int32)
    tab = tab.reshape(n_tiles, SUBLANES, LANES)[:, :, :N_EXPERTS]
    run, base, chunks = i32(tab[:, TAB_RUN]), i32(tab[:, TAB_BASE]), i32(tab[:, TAB_CHUNKS])
    counts = i32(cnt[0, :N_EXPERTS])
    padded = (counts + CHUNK + blk - 1) // blk * blk
    seg_end = jnp.cumsum(padded)
    seg_start = seg_end - padded
    row0 = seg_start[None, :] + run

    def copy_list(per_expert, width, first_chunk):
        ends = jnp.cumsum(per_expert, axis=1)
        item = jnp.arange(width, dtype=jnp.int32)
        owner = jnp.minimum(jnp.sum(i32(ends[:, None, :] <= item[None, :, None]), -1), N_EXPERTS - 1)
        onehot = i32(owner[:, :, None] == jnp.arange(N_EXPERTS, dtype=jnp.int32))
        pick = lambda v: jnp.sum(onehot * v[:, None, :], -1)
        chunk = first_chunk(item[None, :] - pick(ends - per_expert), pick)
        return (pick(base) + chunk).reshape(-1), (pick(row0) + chunk * CHUNK).reshape(-1), ends[:, -1]

    pairs = copy_list(chunks // 2, MAX_PAIRS, lambda q, pick: 2 * q)
    singles = copy_list(chunks % 2, N_EXPERTS, lambda q, pick: pick(chunks) - 1)
    copies = (*pairs, *singles, jnp.sum(chunks, axis=1))
    total = seg_end[N_EXPERTS - 1]
    n_blocks = padded_rows // blk
    blk_row = jnp.arange(n_blocks, dtype=jnp.int32) * blk
    blk_expert = jnp.minimum(jnp.sum(i32(seg_end[None, :] <= blk_row[:, None]), -1), N_EXPERTS - 1)
    n_used = (total // blk).reshape(1)
    tail0 = (seg_start + counts) // blk * blk
    tail1 = jnp.where(tail0 + blk < seg_end, tail0 + blk, -1)
    zdst = jnp.concatenate([tail0, tail1, jnp.where(blk_row >= total, blk_row, -1)])

    xs = dispatch(x1, info, copies, zdst, padded_rows)
    y = expert_mlp(xs, blk_expert, n_used, layer, w_gu.astype(F32), b_gu.astype(F32)[:, :, None, :],
                   w_down.astype(F32), b_down.astype(F32)[:, :, None, :])
    return combine(x1, info, copies, y, ln_g, ln_b)


def kernel(x, w_in, ssd_conv_w, ssd_conv_b, ssd_dt_bias, ssd_a_log, ssd_d, ssd_norm_g, sc_conv_w, w_out,
           ln1_g, ln1_b, router_w, router_b, exp_w_gu, exp_b_gu, exp_w_down, exp_b_down, ln2_g, ln2_b):
    batch, seq, _ = x.shape
    n = batch * seq
    cos_t, sin_t = rope_tables(seq)
    xf = x.reshape(n, D_MODEL).astype(F32)
    a_end = 3 * ATTN_WIDTH
    s_end = a_end + SSD_WIDTH + SSD_CONV_DIM
    d_end = s_end + SSD_HEADS
    for layer in range(w_in.shape[0]):
        w = w_in[layer]
        w_attn = w[:, :a_end].astype(BF16)
        w_ssd = w[:, a_end:s_end].astype(BF16)
        w_dt = jnp.pad(w[:, s_end:d_end], ((0, 0), (0, LANES - SSD_HEADS))).astype(BF16)
        w_sc = w[:, d_end:].astype(BF16)
        (q1, k1, v1, q4, k4, v4, q16, k16, v16, z, xbc, dt_raw, sb, u) = in_projection(
            xf, w_attn, w_ssd, w_dt, w_sc, cos_t, sin_t, batch, seq)
        qkv = ((q1.reshape(batch, 1, seq, ATTN_WIDTH), k1.reshape(batch, 1, seq, ATTN_WIDTH),
                v1.reshape(batch, 1, seq, ATTN_WIDTH)), (q4, k4, v4), (q16, k16, v16))
        attn = [dilated_attention(*t, batch, seq, d) for t, d in zip(qkv, DILATIONS)]
        o_list = [attn[0][0].reshape(n, ATTN_WIDTH), attn[1][0], attn[2][0]]
        lse_list = [attn[0][1].reshape(n, LANES), attn[1][1], attn[2][1]]
        ssd = ssd_mixer(z, xbc, dt_raw, ssd_conv_w[layer], ssd_conv_b[layer], ssd_dt_bias[layer],
                        ssd_a_log[layer], ssd_d[layer], ssd_norm_g[layer], batch, seq)
        x1, logits = out_projection(o_list, lse_list, ssd, sb, u, xf,
                                    sc_conv_w[layer], w_out[layer], ln1_g[layer], ln1_b[layer],
                                    router_w[layer], router_b[layer], seq)
        xf = moe_sublayer(x1, logits, layer, exp_w_gu, exp_b_gu, exp_w_down, exp_b_down,
                          ln2_g[layer], ln2_b[layer])
    return xf.reshape(batch, seq, D_MODEL).astype(x.dtype)
```

```python
import functools

import jax
import jax.numpy as jnp
from jax import lax
from jax.experimental import pallas as pl
from jax.experimental.pallas import tpu as pltpu

F32 = jnp.float32
BF16 = jnp.bfloat16
U32 = jnp.uint32

D_MODEL = 1024
HEAD_DIM = 64
ATTN_WIDTH = 512
DILATIONS = (1, 4, 16)
ATTN_SPAN = 128
ATTN_BLOCK = 128
ATTN_Q_ROWS = 1024
ROPE_THETA = 10000.0

SSD_HEADS = 8
SSD_WIDTH = 512
SSD_GROUPS = 2
SSD_STATE = 128
SSD_CONV = 4
SSD_CHUNK = 128
SSD_STEP_ROWS = 1024
SSD_CONV_DIM = 1024

SC_WIDTH = 512
SC_CONV = 3

N_EXPERTS = 32
TOP_K = 4
EXPERT_FF = 1024
SWIGLU_ALPHA = 1.702
SWIGLU_LIMIT = 7.0

DEPTH = 2
ALPHA = (2.0 * DEPTH) ** 0.25
LN_EPS = 1e-5
RMS_EPS = 1e-5

LANES = 128
SUBLANES = 8
HALO = 16
NEG = -1e30
V7X_VMEM_BYTES = 64 * 1024 * 1024
VMEM_LIMIT = V7X_VMEM_BYTES * 7 // 8
BF16_BITS = 16

PROJ_ROWS = 1024
OUT_PROJ_ROWS = 1024
ROUTE_ROWS = 256
ROUTE_STEP_TILES = 4
MOE_BLOCK = 1024
CHUNK = 16
WAIT_GROUP = 8
N_SLOTS = N_EXPERTS + ROUTE_ROWS * TOP_K // CHUNK
SLOT_ROWS = N_SLOTS * CHUNK
MAX_PAIRS = N_SLOTS // 2
PACKED = D_MODEL // 2
HI_MASK = 0xFFFFFFFF ^ ((1 << BF16_BITS) - 1)


def _cparams(n_axes):
    return pltpu.CompilerParams(dimension_semantics=("arbitrary",) * n_axes,
                                vmem_limit_bytes=VMEM_LIMIT)


def _sigmoid(x):
    return 1.0 / (1.0 + jnp.exp(-x))


def _pair_cols(lo, tile, h):
    return jnp.where(lo, tile[:, h:h + 1], tile[:, h + 1:h + 2])


def _pack_rows(v):
    return pltpu.bitcast(v[:, :PACKED], U32) | (pltpu.bitcast(v[:, PACKED:], U32) >> BF16_BITS)


def _unpack_rows(w):
    a = pltpu.bitcast(w & U32(HI_MASK), F32).astype(BF16)
    b = pltpu.bitcast(w << BF16_BITS, F32).astype(BF16)
    return jnp.concatenate([a, b], axis=1)


def _rope_kernel(inv_ref, cos_ref, sin_ref):
    rows = cos_ref.shape[0]
    base = pl.program_id(0) * rows
    pos = (lax.broadcasted_iota(jnp.int32, cos_ref.shape, 0) + base).astype(F32)
    lane = lax.broadcasted_iota(jnp.int32, cos_ref.shape, 1)
    ang = pos * inv_ref[...]
    cos_ref[...] = jnp.cos(ang)
    sin_ref[...] = jnp.where((lane & 32) == 0, -jnp.sin(ang), jnp.sin(ang))


def rope_tables(seq):
    half = HEAD_DIM // 2
    inv = ROPE_THETA ** (-jnp.arange(half, dtype=F32) / half)
    inv = jnp.tile(inv, LANES // half)[None, :]
    rows = PROJ_ROWS
    return pl.pallas_call(
        _rope_kernel,
        out_shape=(jax.ShapeDtypeStruct((seq, LANES), F32),) * 2,
        grid=(seq // rows,),
        in_specs=[pl.BlockSpec((1, LANES), lambda i: (0, 0))],
        out_specs=(pl.BlockSpec((rows, LANES), lambda i: (i, 0)),) * 2,
        compiler_params=_cparams(1),
        name="rope_tables",
    )(inv)


def _inproj_kernel(x_ref, wa_ref, ws_ref, wdt_ref, wc_ref, cos_ref, sin_ref,
                   q1_ref, k1_ref, v1_ref, q4_ref, k4_ref, v4_ref, q16_ref, k16_ref, v16_ref,
                   z_ref, xbc_ref, dt_ref, sb_ref, u_ref, perm_ref):
    tm = x_ref.shape[0]
    xb = x_ref[...].astype(BF16)
    cos = cos_ref[...]
    sin = sin_ref[...]
    lane = lax.broadcasted_iota(jnp.int32, cos.shape, 1)
    first_half = (lane & 32) == 0
    n_slab = ATTN_WIDTH // LANES

    def proj(w_ref, lo, hi):
        return jnp.dot(xb, w_ref[:, lo:hi], preferred_element_type=F32)

    def rope(t):
        rot = jnp.where(first_half, pltpu.roll(t, LANES - 32, 1), pltpu.roll(t, 32, 1))
        return t * cos + rot * sin

    def emit(slabs, nat_ref, strided_refs):
        for j, t in enumerate(slabs):
            perm_ref[j] = t
            nat_ref[:, j * LANES:(j + 1) * LANES] = t.astype(BF16)
        for d, ref in strided_refs:
            for r in range(d):
                for j in range(n_slab):
                    ref[r, :, j * LANES:(j + 1) * LANES] = (
                        perm_ref[j, pl.ds(r, tm // d, stride=d), :].astype(BF16))

    w = ATTN_WIDTH
    q = proj(wa_ref, 0, w)
    emit([rope(q[:, j * LANES:(j + 1) * LANES]) * (HEAD_DIM ** -0.5) for j in range(n_slab)],
         q1_ref, ((4, q4_ref), (16, q16_ref)))
    k = proj(wa_ref, w, 2 * w)
    emit([rope(k[:, j * LANES:(j + 1) * LANES]) for j in range(n_slab)], k1_ref, ((4, k4_ref), (16, k16_ref)))
    v = proj(wa_ref, 2 * w, 3 * w)
    emit([v[:, j * LANES:(j + 1) * LANES] for j in range(n_slab)], v1_ref, ((4, v4_ref), (16, v16_ref)))
    z_ref[...] = proj(ws_ref, 0, SSD_WIDTH).astype(BF16)
    xbc_ref[...] = proj(ws_ref, SSD_WIDTH, SSD_WIDTH + SSD_CONV_DIM).astype(BF16)
    dt_ref[...] = jnp.dot(xb, wdt_ref[...], preferred_element_type=F32)
    sb_ref[...] = proj(wc_ref, 0, SC_WIDTH).astype(BF16)
    c = proj(wc_ref, SC_WIDTH, 2 * SC_WIDTH)
    h = proj(wc_ref, 2 * SC_WIDTH, 3 * SC_WIDTH)
    u_ref[...] = (c * h).astype(BF16)


def in_projection(x, w_attn, w_ssd, w_dt, w_sc, cos_t, sin_t, batch, seq):
    n = x.shape[0]
    tm = PROJ_ROWS
    tps = seq // tm
    row = lambda i: (i, 0)
    full = lambda i: (0, 0)
    tab = lambda i: (i % tps, 0)
    strided = lambda i: (i // tps, 0, i % tps, 0)
    w = ATTN_WIDTH
    nat = jax.ShapeDtypeStruct((n, w), BF16)
    nat_spec = pl.BlockSpec((tm, w), row)
    shapes, specs = [nat] * 3, [nat_spec] * 3
    for d in DILATIONS[1:]:
        shapes += [jax.ShapeDtypeStruct((batch, d, seq // d, w), BF16)] * 3
        specs += [pl.BlockSpec((None, d, tm // d, w), strided)] * 3
    widths = (SSD_WIDTH, SSD_CONV_DIM, LANES, SC_WIDTH, SC_WIDTH)
    dtypes = (BF16, BF16, F32, BF16, BF16)
    shapes += [jax.ShapeDtypeStruct((n, wd), dt) for wd, dt in zip(widths, dtypes)]
    specs += [pl.BlockSpec((tm, wd), row) for wd in widths]
    return pl.pallas_call(
        _inproj_kernel,
        out_shape=tuple(shapes),
        grid=(n // tm,),
        in_specs=[pl.BlockSpec((tm, D_MODEL), row),
                  pl.BlockSpec(w_attn.shape, full, pipeline_mode=pl.Buffered(1)),
                  pl.BlockSpec(w_ssd.shape, full, pipeline_mode=pl.Buffered(1)),
                  pl.BlockSpec(w_dt.shape, full, pipeline_mode=pl.Buffered(1)),
                  pl.BlockSpec(w_sc.shape, full, pipeline_mode=pl.Buffered(1)),
                  pl.BlockSpec((tm, LANES), tab),
                  pl.BlockSpec((tm, LANES), tab)],
        out_specs=tuple(specs),
        scratch_shapes=[pltpu.VMEM((w // LANES, tm, LANES), F32)],
        compiler_params=_cparams(1),
        name="in_projection",
    )(x, w_attn, w_ssd, w_dt, w_sc, cos_t, sin_t)


def _attn_kernel(q_ref, kc_ref, kp_ref, vc_ref, vp_ref, o_ref, lse_ref):
    blk = ATTN_BLOCK
    n = pl.program_id(2)
    lane = lax.broadcasted_iota(jnp.int32, (blk, LANES), 1)
    lo = lane < HEAD_DIM
    qi = lax.broadcasted_iota(jnp.int32, (2 * blk, 2 * blk), 0) & (blk - 1)
    kj = lax.broadcasted_iota(jnp.int32, (2 * blk, 2 * blk), 1)
    dist = qi + blk - kj
    in_band = jnp.where(dist >= 0, jnp.where(dist <= ATTN_SPAN, 1, 0), 0)
    bias = jnp.where(in_band > 0, 0.0, NEG)
    bias_first = jnp.where((in_band * jnp.where(kj >= blk, 1, jnp.where(n > 0, 1, 0))) > 0, 0.0, NEG)
    n_res, tq = q_ref.shape[0], q_ref.shape[1]
    for r, j in ((r, j) for r in range(n_res) for j in range(tq // blk)):
        rows = slice(j * blk, (j + 1) * blk)
        prev = slice((j - 1) * blk, j * blk)
        mask_bias = bias_first if j == 0 else bias
        lse_tile = jnp.zeros((blk, LANES), F32)
        for hp in range(ATTN_WIDTH // LANES):
            sl = slice(hp * LANES, (hp + 1) * LANES)
            q2 = q_ref[r, rows, sl].astype(F32)
            qq = jnp.concatenate([jnp.where(lo, q2, 0.0), jnp.where(lo, 0.0, q2)], axis=0).astype(BF16)
            k_prev = kp_ref[r, :, sl] if j == 0 else kc_ref[r, prev, sl]
            v_prev = vp_ref[r, :, sl] if j == 0 else vc_ref[r, prev, sl]
            k2 = jnp.concatenate([k_prev, kc_ref[r, rows, sl]], axis=0)
            v2 = jnp.concatenate([v_prev, vc_ref[r, rows, sl]], axis=0)
            s = lax.dot_general(qq, k2, (((1,), (1,)), ((), ())), preferred_element_type=F32) + mask_bias
            m = jnp.max(s, axis=1, keepdims=True)
            p = jnp.exp(s - m)
            l = jnp.sum(p, axis=1, keepdims=True)
            pv = jnp.dot(p.astype(BF16), v2, preferred_element_type=F32)
            o = pv / l
            o_ref[r, rows, sl] = jnp.where(lo, o[:blk], o[blk:]).astype(BF16)
            lse = m + jnp.log(l)
            lse_tile = jnp.where(lane == 2 * hp, lse[:blk],
                                 jnp.where(lane == 2 * hp + 1, lse[blk:], lse_tile))
        lse_ref[r, rows, :] = lse_tile


def dilated_attention(q, k, v, batch, seq, dilation):
    length = seq // dilation
    w = ATTN_WIDTH
    blk = ATTN_BLOCK
    tq = min(ATTN_Q_ROWS, length)
    n_res = ATTN_Q_ROWS // tq
    cur = lambda b, r, n: (b, r, n, 0)
    prev = lambda b, r, n: (b, r, jnp.maximum(n * (tq // blk) - 1, 0), 0)
    big = pl.BlockSpec((None, n_res, tq, w), cur)
    small = pl.BlockSpec((None, n_res, blk, w), prev)
    return pl.pallas_call(
        _attn_kernel,
        out_shape=(jax.ShapeDtypeStruct((batch, dilation, length, w), BF16),
                   jax.ShapeDtypeStruct((batch, dilation, length, LANES), F32)),
        grid=(batch, dilation // n_res, length // tq),
        in_specs=[big, big, small, big, small],
        out_specs=(big, pl.BlockSpec((None, n_res, tq, LANES), cur)),
        compiler_params=_cparams(3),
        name=f"dilated_attention_d{dilation}",
    )(q, k, k, v, v)


def _ssd_kernel(z_ref, xbc_ref, halo_ref, dt_ref, shift_ref, cw_ref, cb_ref, dtb_ref, alog_ref, dskip_ref,
                g_ref, y_ref, st_ref):
    ch = SSD_CHUNK
    c = pl.program_id(1)

    @pl.when(c == 0)
    def _():
        st_ref[...] = jnp.zeros_like(st_ref)

    first_halo = halo_ref[...]
    first_halo = jnp.where(c > 0, first_halo, jnp.zeros_like(first_halo))
    for j in range(xbc_ref.shape[0] // ch):
        rows = slice(j * ch, (j + 1) * ch)
        halo = first_halo if j == 0 else xbc_ref[j * ch - HALO:j * ch, :]
        y_ref[rows, :] = _ssd_chunk(xbc_ref[rows, :], halo, z_ref[rows, :], dt_ref[rows, :], shift_ref, cw_ref,
                                    cb_ref, dtb_ref, alog_ref, dskip_ref, g_ref, st_ref)


def _ssd_chunk(xbc_in, halo, z_in, dt_in, shift_ref, cw_ref, cb_ref, dtb_ref, alog_ref, dskip_ref, g_ref, st_ref):
    ch = SSD_CHUNK
    ext = jnp.concatenate([halo, xbc_in], axis=0)
    conv = cb_ref[...] + cw_ref[SSD_CONV - 1:SSD_CONV, :] * xbc_in.astype(F32)
    for t in range(SSD_CONV - 1):
        conv = conv + cw_ref[t:t + 1, :] * jnp.dot(shift_ref[t], ext, preferred_element_type=F32)
    xbc = conv * _sigmoid(conv)
    xh = xbc[:, :SSD_WIDTH]
    bm = xbc[:, SSD_WIDTH:SSD_WIDTH + SSD_GROUPS * SSD_STATE]
    cm = xbc[:, SSD_WIDTH + SSD_GROUPS * SSD_STATE:]

    lane = lax.broadcasted_iota(jnp.int32, (ch, LANES), 1)
    row = lax.broadcasted_iota(jnp.int32, (ch, LANES), 0)
    lo = lane < HEAD_DIM
    lo_row = lo[0:1, :]
    head_lane = lane < SSD_HEADS

    xdt = dt_in + dtb_ref[...]
    dtv = jnp.maximum(xdt, 0.0) + jnp.log(1.0 + jnp.exp(-jnp.abs(xdt)))
    a = jnp.where(head_lane[0:1, :], -jnp.exp(alog_ref[...]), 0.0)
    acum = dtv * a
    shift = 1
    while shift < ch:
        acum = acum + jnp.where(row >= shift, pltpu.roll(acum, shift, 0), 0.0)
        shift *= 2
    acum_t = acum.T
    tot = acum[ch - 1:ch, :]
    e_in = jnp.exp(acum)
    e_out = jnp.exp(tot - acum)
    e_tot = jnp.exp(tot)
    causal = row >= lane

    ys = []
    for g in range(SSD_GROUPS):
        bg = bm[:, g * SSD_STATE:(g + 1) * SSD_STATE]
        cg = cm[:, g * SSD_STATE:(g + 1) * SSD_STATE].astype(BF16)
        cb = lax.dot_general(cg, bg.astype(BF16), (((1,), (1,)), ((), ())),
                             preferred_element_type=F32)
        bg_t = bg.T.astype(BF16)
        for pp in range(2):
            pair = 2 * g + pp
            ha = 2 * pair
            sl = slice(pair * LANES, (pair + 1) * LANES)
            xp = xh[:, sl]
            xdt_pair = xp * _pair_cols(lo, dtv, ha)
            xb16 = xdt_pair.astype(BF16)
            diag = []
            for h in (ha, ha + 1):
                seg = acum[:, h:h + 1] - acum_t[h:h + 1, :]
                decay = jnp.exp(jnp.where(causal, seg, NEG))
                diag.append(jnp.dot((cb * decay).astype(BF16), xb16, preferred_element_type=F32))
            y = jnp.where(lo, diag[0], diag[1])
            state = st_ref[pair]
            y = y + jnp.dot(cg, state.astype(BF16), preferred_element_type=F32) * _pair_cols(lo, e_in, ha)
            xout = (xdt_pair * _pair_cols(lo, e_out, ha)).astype(BF16)
            st_ref[pair] = (state * _pair_cols(lo_row, e_tot, ha)
                            + jnp.dot(bg_t, xout, preferred_element_type=F32))
            ys.append(y + dskip_ref[:, sl] * xp)
    y = jnp.concatenate(ys, axis=1)
    zz = z_in.astype(F32)
    y = y * (zz * _sigmoid(zz))
    gw = SSD_WIDTH // SSD_GROUPS
    outs = []
    for g in range(SSD_GROUPS):
        yg = y[:, g * gw:(g + 1) * gw]
        ms = jnp.mean(yg * yg, axis=1, keepdims=True)
        outs.append(yg * lax.rsqrt(ms + RMS_EPS))
    return (jnp.concatenate(outs, axis=1) * g_ref[...]).astype(BF16)


def ssd_mixer(z, xbc, dt_raw, conv_w, conv_b, dt_bias, a_log, d_skip, norm_g, batch, seq):
    n = batch * seq
    ch = SSD_CHUNK
    step = SSD_STEP_ROWS
    nc = seq // step
    pad = lambda t: jnp.pad(t.astype(F32), (0, LANES - t.shape[0]))[None, :]
    cur = lambda b, c: (b * nc + c, 0)
    halo = lambda b, c: (jnp.maximum((b * nc + c) * (step // HALO) - 1, 0), 0)
    full = lambda b, c: (0, 0)
    r_idx = lax.broadcasted_iota(jnp.int32, (SSD_CONV - 1, ch, HALO + ch), 1)
    j_idx = lax.broadcasted_iota(jnp.int32, (SSD_CONV - 1, ch, HALO + ch), 2)
    t_idx = lax.broadcasted_iota(jnp.int32, (SSD_CONV - 1, ch, HALO + ch), 0)
    shift = jnp.where(j_idx == r_idx + HALO - (SSD_CONV - 1) + t_idx, 1.0, 0.0).astype(BF16)
    return pl.pallas_call(
        _ssd_kernel,
        out_shape=jax.ShapeDtypeStruct((n, SSD_WIDTH), BF16),
        grid=(batch, nc),
        in_specs=[pl.BlockSpec((step, SSD_WIDTH), cur),
                  pl.BlockSpec((step, SSD_CONV_DIM), cur),
                  pl.BlockSpec((HALO, SSD_CONV_DIM), halo),
                  pl.BlockSpec((step, LANES), cur),
                  pl.BlockSpec(shift.shape, lambda b, c: (0, 0, 0)),
                  pl.BlockSpec((SSD_CONV, SSD_CONV_DIM), full),
                  pl.BlockSpec((1, SSD_CONV_DIM), full),
                  pl.BlockSpec((1, LANES), full),
                  pl.BlockSpec((1, LANES), full),
                  pl.BlockSpec((1, SSD_WIDTH), full),
                  pl.BlockSpec((1, SSD_WIDTH), full)],
        out_specs=pl.BlockSpec((step, SSD_WIDTH), cur),
        scratch_shapes=[pltpu.VMEM((SSD_HEADS // 2, SSD_STATE, LANES), F32)],
        compiler_params=_cparams(2),
        name="ssd_mixer",
    )(z, xbc, xbc, dt_raw, shift, conv_w.astype(F32), conv_b.astype(F32)[None, :], pad(dt_bias), pad(a_log),
      jnp.repeat(d_skip.astype(F32), HEAD_DIM)[None, :], norm_g.astype(F32)[None, :])


def _layer_norm(r, g, b):
    mu = jnp.mean(r, axis=1, keepdims=True)
    d = r - mu
    var = jnp.mean(d * d, axis=1, keepdims=True)
    return d * lax.rsqrt(var + LN_EPS) * g + b


def _outproj_kernel(tiles_per_seq, o1_ref, o4_ref, o16_ref, l1_ref, l4_ref, l16_ref, ssd_ref, sb_ref,
                    u_ref, uh_ref, x_ref, cw_ref, wout_ref, g_ref, b_ref, spread_ref, rw_ref,
                    rb_ref, x1_ref, logit_ref, ext_ref, operm_ref, lperm_ref):
    tm = x_ref.shape[0]
    i = pl.program_id(0)
    n_slab = ATTN_WIDTH // LANES

    for idx, (d, o_ref, l_ref) in enumerate(((4, o4_ref, l4_ref), (16, o16_ref, l16_ref))):
        for r in range(d):
            rows = pl.ds(r, tm // d, stride=d)
            lperm_ref[idx, rows, :] = l_ref[r]
            for j in range(n_slab):
                operm_ref[idx * n_slab + j, rows, :] = o_ref[r, :, j * LANES:(j + 1) * LANES].astype(F32)

    lses = (l1_ref[...], lperm_ref[0], lperm_ref[1])
    top = jnp.maximum(jnp.maximum(lses[0], lses[1]), lses[2])
    es = [jnp.exp(l - top) for l in lses]
    den = es[0] + es[1] + es[2]
    wide = []
    for e in es:
        w = e / den
        w_hi = w.astype(BF16)
        w_lo = (w - w_hi.astype(F32)).astype(BF16)
        wide.append(jnp.dot(jnp.concatenate([w_hi, w_lo], axis=1), spread_ref[...], preferred_element_type=F32))
    attn = []
    for hp in range(n_slab):
        sl = slice(hp * LANES, (hp + 1) * LANES)
        acc = wide[0][:, sl] * o1_ref[:, sl].astype(F32)
        acc = acc + wide[1][:, sl] * operm_ref[hp]
        acc = acc + wide[2][:, sl] * operm_ref[n_slab + hp]
        attn.append(acc.astype(BF16))

    seq_start = (i % tiles_per_seq) == 0
    ext_ref[0:HALO, :] = jnp.where(seq_start, 0.0, uh_ref[...].astype(F32))
    ext_ref[HALO:HALO + tm, :] = u_ref[...].astype(F32)
    conv = jnp.zeros((tm, SC_WIDTH), F32)
    for t in range(SC_CONV):
        off = HALO - (SC_CONV - 1) + t
        conv = conv + cw_ref[t:t + 1, :] * ext_ref[off:off + tm, :]
    gated = (sb_ref[...].astype(F32) * conv).astype(BF16)

    mixed = jnp.concatenate(attn + [ssd_ref[...], gated], axis=1)
    mix = jnp.dot(mixed, wout_ref[...], preferred_element_type=F32)
    x1 = _layer_norm(ALPHA * x_ref[...] + mix, g_ref[...], b_ref[...])
    x1_ref[...] = x1

    xh = x1.astype(BF16)
    xm = (x1 - xh.astype(F32)).astype(BF16)
    ph = jnp.dot(xh, rw_ref[...], preferred_element_type=F32)
    pm = jnp.dot(xm, rw_ref[...], preferred_element_type=F32)
    logit_ref[...] = pm + pltpu.roll(ph, LANES - N_EXPERTS, 1) + ph + rb_ref[...]


def out_projection(o_list, lse_list, ssd, sb, u, x, sc_conv_w, w_out, ln_g, ln_b, router_w, router_b, seq):
    n = x.shape[0]
    tm = OUT_PROJ_ROWS
    tps = seq // tm
    row = lambda i: (i, 0)
    full = lambda i: (0, 0)
    strided = lambda i: (i // tps, 0, i % tps, 0)
    halo = lambda i: (jnp.maximum(i * (tm // HALO) - 1, 0), 0)
    rw_f = router_w.astype(F32)
    rw_hi = rw_f.astype(BF16)
    rw_mid = (rw_f - rw_hi.astype(F32)).astype(BF16)
    rw = jnp.pad(jnp.concatenate([rw_hi, rw_mid], axis=1), ((0, 0), (0, LANES - 2 * N_EXPERTS)))
    rb = jnp.pad(router_b.astype(F32), (0, LANES - N_EXPERTS))[None, :]
    spread = jnp.where(lax.broadcasted_iota(jnp.int32, (2 * LANES, ATTN_WIDTH), 0) % LANES
                       == lax.broadcasted_iota(jnp.int32, (2 * LANES, ATTN_WIDTH), 1) // HEAD_DIM,
                       1.0, 0.0).astype(BF16)
    wide = lambda wd: pl.BlockSpec((tm, wd), row)
    perm = lambda d, wd: pl.BlockSpec((None, d, tm // d, wd), strided)
    const = lambda a: pl.BlockSpec(a.shape, full)
    wout = w_out.astype(BF16)
    cw = sc_conv_w.astype(F32)
    g = ln_g.astype(F32)[None, :]
    b = ln_b.astype(F32)[None, :]
    n_slab = ATTN_WIDTH // LANES
    return pl.pallas_call(
        functools.partial(_outproj_kernel, tps),
        out_shape=(jax.ShapeDtypeStruct((n, D_MODEL), F32), jax.ShapeDtypeStruct((n, LANES), F32)),
        grid=(n // tm,),
        in_specs=[wide(ATTN_WIDTH), perm(4, ATTN_WIDTH), perm(16, ATTN_WIDTH),
                  wide(LANES), perm(4, LANES), perm(16, LANES),
                  wide(SSD_WIDTH), wide(SC_WIDTH), wide(SC_WIDTH), pl.BlockSpec((HALO, SC_WIDTH), halo),
                  wide(D_MODEL), const(cw), const(wout), const(g), const(b),
                  const(spread), const(rw), const(rb)],
        out_specs=(wide(D_MODEL), wide(LANES)),
        scratch_shapes=[pltpu.VMEM((HALO + tm, SC_WIDTH), F32),
                        pltpu.VMEM((2 * n_slab, tm, LANES), F32),
                        pltpu.VMEM((2, tm, LANES), F32)],
        compiler_params=_cparams(1),
        name="out_projection",
    )(*o_list, *lse_list, ssd, sb, u, u, x, cw, wout, g, b, spread, rw, rb)


INFO_IDX, INFO_SLOT, INFO_GATE = 0, TOP_K, 2 * TOP_K
TAB_RUN, TAB_BASE, TAB_CHUNKS = 0, 1, 2


def _route_kernel(logit_ref, upper_ref, info_ref, tab_ref, cnt_ref, run_ref):
    @pl.when(pl.program_id(0) == 0)
    def _():
        run_ref[...] = jnp.zeros_like(run_ref)

    tm = ROUTE_ROWS
    for j in range(logit_ref.shape[0] // tm):
        info, tab = _route_tile(logit_ref[j * tm:(j + 1) * tm, :], upper_ref, run_ref)
        info_ref[j * tm:(j + 1) * tm, :] = info
        tab_ref[j * SUBLANES:(j + 1) * SUBLANES, :] = tab
    cnt_ref[...] = jnp.broadcast_to(run_ref[...], cnt_ref.shape)


def _route_tile(logits, upper_ref, run_ref):
    tm = logits.shape[0]
    ne = N_EXPERTS
    rec_rows = 2 * SUBLANES
    work = logits.T[:ne, :]
    eidx = lax.broadcasted_iota(jnp.int32, (ne, tm), 0).astype(F32)
    sel = jnp.zeros((ne, tm), F32)
    vals, idxs, hits = [], [], []
    for _ in range(TOP_K):
        m = jnp.max(work, axis=0, keepdims=True)
        idx = jnp.min(jnp.where(work == m, eidx, float(ne)), axis=0, keepdims=True)
        hit = eidx == idx
        work = jnp.where(hit, NEG, work)
        sel = sel + jnp.where(hit, 1.0, 0.0)
        vals.append(m)
        idxs.append(idx)
        hits.append(hit)
    es = [jnp.exp(v - vals[0]) for v in vals]
    den = es[0] + es[1] + es[2] + es[3]
    before = jnp.dot(sel.astype(BF16), upper_ref[...], preferred_element_type=F32)
    cnt = jnp.sum(sel, axis=1, keepdims=True)
    chunks = jnp.floor((cnt + (CHUNK - 1)) * (1.0 / CHUNK))
    row = lax.broadcasted_iota(jnp.int32, (ne, LANES), 0)
    chunks_w = jnp.broadcast_to(chunks, (ne, LANES))
    incl = chunks_w
    shift = 1
    while shift < ne:
        incl = incl + jnp.where(row >= shift, pltpu.roll(incl, shift, 0), 0.0)
        shift *= 2
    base = (incl - chunks_w)[:, 0:1]

    sub = lax.broadcasted_iota(jnp.int32, (rec_rows, tm), 0)
    rec = jnp.zeros((rec_rows, tm), F32)
    for k in range(TOP_K):
        rank = jnp.sum(jnp.where(hits[k], before, 0.0), axis=0, keepdims=True)
        first = jnp.sum(jnp.where(hits[k], base, 0.0), axis=0, keepdims=True)
        rec = jnp.where(sub == INFO_IDX + k, idxs[k], rec)
        rec = jnp.where(sub == INFO_SLOT + k, first * CHUNK + rank, rec)
        rec = jnp.where(sub == INFO_GATE + k, es[k] / den, rec)
    info = jnp.concatenate([rec, jnp.zeros((LANES - rec_rows, tm), F32)], axis=0).T

    lane = lax.broadcasted_iota(jnp.int32, (ne, LANES), 1)
    cnt8 = jnp.floor((cnt + (SUBLANES - 1)) * (1.0 / SUBLANES)) * SUBLANES
    cols = jnp.where(lane == 0, cnt8, jnp.where(lane == 1, base, jnp.where(lane == 2, chunks, 0.0)))
    rows = jnp.concatenate([cols, jnp.zeros((LANES - ne, LANES), F32)], axis=0).T
    sub8 = lax.broadcasted_iota(jnp.int32, (SUBLANES, LANES), 0)
    tab = jnp.where(sub8 == TAB_RUN, run_ref[...],
                    jnp.where(sub8 == TAB_BASE, rows[1:2, :], jnp.where(sub8 == TAB_CHUNKS, rows[2:3, :], 0.0)))
    run_ref[...] = run_ref[...] + rows[0:1, :]
    return info, tab


def route(logits):
    n = logits.shape[0]
    tm = ROUTE_ROWS
    r = lax.broadcasted_iota(jnp.int32, (tm, tm), 0)
    c = lax.broadcasted_iota(jnp.int32, (tm, tm), 1)
    upper = jnp.where(r < c, 1.0, 0.0).astype(BF16)
    return pl.pallas_call(
        _route_kernel,
        out_shape=(jax.ShapeDtypeStruct((n, LANES), F32),
                   jax.ShapeDtypeStruct((n // tm * SUBLANES, LANES), F32),
                   jax.ShapeDtypeStruct((SUBLANES, LANES), F32)),
        grid=(n // (tm * ROUTE_STEP_TILES),),
        in_specs=[pl.BlockSpec((tm * ROUTE_STEP_TILES, LANES), lambda i: (i, 0)),
                  pl.BlockSpec((tm, tm), lambda i: (0, 0))],
        out_specs=(pl.BlockSpec((tm * ROUTE_STEP_TILES, LANES), lambda i: (i, 0)),
                   pl.BlockSpec((SUBLANES * ROUTE_STEP_TILES, LANES), lambda i: (i, 0)),
                   pl.BlockSpec((SUBLANES, LANES), lambda i: (0, 0))),
        scratch_shapes=[pltpu.VMEM((1, LANES), F32)],
        compiler_params=_cparams(1),
        name="moe_route",
    )(logits, upper)


def _start_all(n_chunks, make_copy):
    def start(s, carry):
        make_copy(s).start()
        return carry

    lax.fori_loop(0, n_chunks, start, 0)


def _wait_all(n_chunks, make_wait):
    groups = n_chunks // WAIT_GROUP

    def wait_group(g, carry):
        make_wait(WAIT_GROUP).wait()
        return carry

    def wait(s, carry):
        make_wait(1).wait()
        return carry

    lax.fori_loop(0, groups, wait_group, 0)
    lax.fori_loop(groups * WAIT_GROUP, n_chunks, wait, 0)


def _dispatch_kernel(pslot_ref, pdst_ref, npair_ref, sslot_ref, sdst_ref, nsingle_ref, nslot_ref, zdst_ref,
                     x_ref, info_ref, buf_ref, slots_ref, zero_ref, sems, zsem):
    i = pl.program_id(0)
    tm = x_ref.shape[0]
    cur = i % 2

    def put(tile, side, slot_tab, dst_tab, width, chunks):
        return lambda s: pltpu.make_async_copy(
            slots_ref.at[side, pl.ds(pl.multiple_of(slot_tab[tile * width + s] * CHUNK, CHUNK), chunks * CHUNK)],
            buf_ref.at[pl.ds(pl.multiple_of(dst_tab[tile * width + s], SUBLANES), chunks * CHUNK)],
            sems.at[side])

    def start_tile(tile, side):
        _start_all(npair_ref[tile], put(tile, side, pslot_ref, pdst_ref, MAX_PAIRS, 2))
        _start_all(nsingle_ref[tile], put(tile, side, sslot_ref, sdst_ref, N_EXPERTS, 1))

    def landed(side):
        return lambda k: pltpu.make_async_copy(slots_ref.at[side, pl.ds(0, k * CHUNK)],
                                               buf_ref.at[pl.ds(0, k * CHUNK)], sems.at[side])

    @pl.when(i == 0)
    def _():
        zero_ref[...] = jnp.zeros_like(zero_ref)

        def zero_copy(e):
            row = pl.multiple_of(jnp.maximum(zdst_ref[e], 0), MOE_BLOCK)
            return pltpu.make_async_copy(zero_ref, buf_ref.at[pl.ds(row, MOE_BLOCK)], zsem)

        def start(e, carry):
            @pl.when(zdst_ref[e] >= 0)
            def _():
                zero_copy(e).start()
            return carry

        def wait(e, carry):
            @pl.when(zdst_ref[e] >= 0)
            def _():
                zero_copy(e).wait()
            return carry

        lax.fori_loop(0, zdst_ref.shape[0], start, 0)
        lax.fori_loop(0, zdst_ref.shape[0], wait, 0)

    xb = x_ref[...].astype(BF16)
    slot_t = info_ref[...].T[INFO_SLOT:INFO_SLOT + TOP_K, :]
    local = lax.broadcasted_iota(jnp.int32, (tm, tm), 0).astype(F32).astype(BF16)
    one = jnp.ones((tm, tm), BF16)
    zero = jnp.zeros((tm, tm), BF16)
    for c in range(SLOT_ROWS // tm):
        target = (slot_t - float(c * tm)).astype(BF16)
        onehot = zero
        for k in range(TOP_K):
            onehot = onehot + jnp.where(local == target[k:k + 1, :], one, zero)
        rows = jnp.dot(onehot, xb, preferred_element_type=F32)
        slots_ref[cur, c * tm:(c + 1) * tm, :] = _pack_rows(rows)

    @pl.when(i > 0)
    def _():
        _wait_all(nslot_ref[i - 1], landed(1 - cur))

    start_tile(i, cur)

    @pl.when(i == pl.num_programs(0) - 1)
    def _():
        _wait_all(nslot_ref[i], landed(cur))


def dispatch(x1, info, copies, zdst, padded_rows):
    n = x1.shape[0]
    tm = ROUTE_ROWS
    row = lambda i, *_: (i, 0)
    return pl.pallas_call(
        _dispatch_kernel,
        out_shape=jax.ShapeDtypeStruct((padded_rows, PACKED), U32),
        grid_spec=pltpu.PrefetchScalarGridSpec(
            num_scalar_prefetch=len(copies) + 1,
            grid=(n // tm,),
            in_specs=[pl.BlockSpec((tm, D_MODEL), row), pl.BlockSpec((tm, LANES), row)],
            out_specs=pl.BlockSpec(memory_space=pl.ANY),
            scratch_shapes=[pltpu.VMEM((2, SLOT_ROWS, PACKED), U32), pltpu.VMEM((MOE_BLOCK, PACKED), U32),
                            pltpu.SemaphoreType.DMA((2,)), pltpu.SemaphoreType.DMA(())]),
        compiler_params=_cparams(1),
        name="moe_dispatch",
    )(*copies, zdst, x1, info)


def _expert_kernel(blk_expert_ref, n_used_ref, xs_ref, wgu_ref, bgu_ref, wd_ref, bd_ref, y_ref):
    del blk_expert_ref
    used = pl.program_id(0) < n_used_ref[0]

    @pl.when(jnp.logical_not(used))
    def _():
        y_ref[...] = jnp.zeros_like(y_ref)

    @pl.when(used)
    def _():
        xb = _unpack_rows(xs_ref[...])
        gu = jnp.dot(xb, wgu_ref[...].astype(BF16), preferred_element_type=F32) + bgu_ref[...]
        gate = jnp.minimum(gu[:, :EXPERT_FF], SWIGLU_LIMIT)
        up = jnp.clip(gu[:, EXPERT_FF:], -SWIGLU_LIMIT, SWIGLU_LIMIT)
        h = (up + 1.0) * gate * _sigmoid(SWIGLU_ALPHA * gate)
        y = jnp.dot(h.astype(BF16), wd_ref[...].astype(BF16), preferred_element_type=F32) + bd_ref[...]
        y_ref[...] = _pack_rows(y.astype(BF16).astype(F32))


def expert_mlp(xs, blk_expert, n_used, layer, w_gu, b_gu, w_down, b_down):
    padded_rows = xs.shape[0]
    blk = MOE_BLOCK
    rows_in = lambda i, be, nu: (jnp.minimum(i, nu[0] - 1), 0)
    per_expert = lambda i, be, nu: (layer, be[i], 0, 0)
    return pl.pallas_call(
        _expert_kernel,
        out_shape=jax.ShapeDtypeStruct((padded_rows, PACKED), U32),
        grid_spec=pltpu.PrefetchScalarGridSpec(
            num_scalar_prefetch=2,
            grid=(padded_rows // blk,),
            in_specs=[pl.BlockSpec((blk, PACKED), rows_in),
                      pl.BlockSpec((None, None, D_MODEL, 2 * EXPERT_FF), per_expert),
                      pl.BlockSpec((None, None, 1, 2 * EXPERT_FF), per_expert),
                      pl.BlockSpec((None, None, EXPERT_FF, D_MODEL), per_expert),
                      pl.BlockSpec((None, None, 1, D_MODEL), per_expert)],
            out_specs=pl.BlockSpec((blk, PACKED), lambda i, be, nu: (i, 0))),
        compiler_params=_cparams(1),
        name="expert_mlp",
    )(blk_expert, n_used, xs, w_gu, b_gu, w_down, b_down)


def _combine_kernel(pslot_ref, pdst_ref, npair_ref, sslot_ref, sdst_ref, nsingle_ref, nslot_ref,
                    x_ref, info_ref, g_ref, b_ref, y_ref, out_ref, slots_ref, sems):
    i = pl.program_id(0)
    tm = x_ref.shape[0]
    cur = i % 2

    def fetch(tile, side, slot_tab, dst_tab, width, chunks):
        return lambda s: pltpu.make_async_copy(
            y_ref.at[pl.ds(pl.multiple_of(dst_tab[tile * width + s], SUBLANES), chunks * CHUNK)],
            slots_ref.at[side, pl.ds(pl.multiple_of(slot_tab[tile * width + s] * CHUNK, CHUNK), chunks * CHUNK)],
            sems.at[side])

    def start_tile(tile, side):
        _start_all(npair_ref[tile], fetch(tile, side, pslot_ref, pdst_ref, MAX_PAIRS, 2))
        _start_all(nsingle_ref[tile], fetch(tile, side, sslot_ref, sdst_ref, N_EXPERTS, 1))

    @pl.when(i == 0)
    def _():
        slots_ref[...] = jnp.zeros_like(slots_ref)
        start_tile(0, 0)

    @pl.when(i + 1 < pl.num_programs(0))
    def _():
        start_tile(i + 1, 1 - cur)

    _wait_all(nslot_ref[i], lambda k: pltpu.make_async_copy(
        y_ref.at[pl.ds(0, k * CHUNK)], slots_ref.at[cur, pl.ds(0, k * CHUNK)], sems.at[cur]))

    info = info_ref[...]
    ffn = jnp.zeros((tm, D_MODEL), F32)
    local = lax.broadcasted_iota(jnp.int32, (tm, tm), 1).astype(F32).astype(BF16)
    zero = jnp.zeros((tm, tm), BF16)
    gates = [info[:, INFO_GATE + k:INFO_GATE + k + 1].astype(BF16) for k in range(TOP_K)]
    for c in range(SLOT_ROWS // tm):
        w = zero
        for k in range(TOP_K):
            target = (info[:, INFO_SLOT + k:INFO_SLOT + k + 1] - float(c * tm)).astype(BF16)
            w = w + jnp.where(local == target, gates[k], zero)
        rows = _unpack_rows(slots_ref[cur, c * tm:(c + 1) * tm, :])
        ffn = ffn + jnp.dot(w, rows, preferred_element_type=F32)
    out_ref[...] = _layer_norm(ALPHA * x_ref[...] + ffn, g_ref[...], b_ref[...])


def combine(x1, info, copies, y, ln_g, ln_b):
    n = x1.shape[0]
    tm = ROUTE_ROWS
    row = lambda i, *_: (i, 0)
    full = lambda i, *_: (0, 0)
    return pl.pallas_call(
        _combine_kernel,
        out_shape=jax.ShapeDtypeStruct((n, D_MODEL), F32),
        grid_spec=pltpu.PrefetchScalarGridSpec(
            num_scalar_prefetch=len(copies),
            grid=(n // tm,),
            in_specs=[pl.BlockSpec((tm, D_MODEL), row),
                      pl.BlockSpec((tm, LANES), row),
                      pl.BlockSpec((1, D_MODEL), full),
                      pl.BlockSpec((1, D_MODEL), full),
                      pl.BlockSpec(memory_space=pl.ANY)],
            out_specs=pl.BlockSpec((tm, D_MODEL), row),
            scratch_shapes=[pltpu.VMEM((2, SLOT_ROWS, PACKED), U32), pltpu.SemaphoreType.DMA((2,))]),
        compiler_params=_cparams(1),
        name="moe_combine",
    )(*copies, x1, info, ln_g.astype(F32)[None, :], ln_b.astype(F32)[None, :], y)


def moe_sublayer(x1, logits, layer, w_gu, b_gu, w_down, b_down, ln_g, ln_b):
    n = x1.shape[0]
    blk = MOE_BLOCK
    n_tiles = n // ROUTE_ROWS
    padded_rows = n * TOP_K + (SUBLANES - 1) * N_EXPERTS * n_tiles + N_EXPERTS * (blk + CHUNK)
    padded_rows = -(-padded_rows // blk) * blk
    info, tab, cnt = route(logits)

    i32 = lambda t: t.astype(jnp.int32)
    tab = tab.reshape(n_tiles, SUBLANES, LANES)[:, :, :N_EXPERTS]
    run, base, chunks = i32(tab[:, TAB_RUN]), i32(tab[:, TAB_BASE]), i32(tab[:, TAB_CHUNKS])
    counts = i32(cnt[0, :N_EXPERTS])
    padded = (counts + CHUNK + blk - 1) // blk * blk
    seg_end = jnp.cumsum(padded)
    seg_start = seg_end - padded
    row0 = seg_start[None, :] + run

    def copy_list(per_expert, width, first_chunk):
        ends = jnp.cumsum(per_expert, axis=1)
        item = jnp.arange(width, dtype=jnp.int32)
        owner = jnp.minimum(jnp.sum(i32(ends[:, None, :] <= item[None, :, None]), -1), N_EXPERTS - 1)
        onehot = i32(owner[:, :, None] == jnp.arange(N_EXPERTS, dtype=jnp.int32))
        pick = lambda v: jnp.sum(onehot * v[:, None, :], -1)
        chunk = first_chunk(item[None, :] - pick(ends - per_expert), pick)
        return (pick(base) + chunk).reshape(-1), (pick(row0) + chunk * CHUNK).reshape(-1), ends[:, -1]

    pairs = copy_list(chunks // 2, MAX_PAIRS, lambda q, pick: 2 * q)
    singles = copy_list(chunks % 2, N_EXPERTS, lambda q, pick: pick(chunks) - 1)
    copies = (*pairs, *singles, jnp.sum(chunks, axis=1))
    total = seg_end[N_EXPERTS - 1]
    n_blocks = padded_rows // blk
    blk_row = jnp.arange(n_blocks, dtype=jnp.int32) * blk
    blk_expert = jnp.minimum(jnp.sum(i32(seg_end[None, :] <= blk_row[:, None]), -1), N_EXPERTS - 1)
    n_used = (total // blk).reshape(1)
    tail0 = (seg_start + counts) // blk * blk
    tail1 = jnp.where(tail0 + blk < seg_end, tail0 + blk, -1)
    zdst = jnp.concatenate([tail0, tail1, jnp.where(blk_row >= total, blk_row, -1)])

    xs = dispatch(x1, info, copies, zdst, padded_rows)
    y = expert_mlp(xs, blk_expert, n_used, layer, w_gu.astype(F32), b_gu.astype(F32)[:, :, None, :],
                   w_down.astype(F32), b_down.astype(F32)[:, :, None, :])
    return combine(x1, info, copies, y, ln_g, ln_b)


def kernel(x, w_in, ssd_conv_w, ssd_conv_b, ssd_dt_bias, ssd_a_log, ssd_d, ssd_norm_g, sc_conv_w, w_out,
           ln1_g, ln1_b, router_w, router_b, exp_w_gu, exp_b_gu, exp_w_down, exp_b_down, ln2_g, ln2_b):
    batch, seq, _ = x.shape
    n = batch * seq
    cos_t, sin_t = rope_tables(seq)
    xf = x.reshape(n, D_MODEL).astype(F32)
    a_end = 3 * ATTN_WIDTH
    s_end = a_end + SSD_WIDTH + SSD_CONV_DIM
    d_end = s_end + SSD_HEADS
    for layer in range(w_in.shape[0]):
        w = w_in[layer]
        w_attn = w[:, :a_end].astype(BF16)
        w_ssd = w[:, a_end:s_end].astype(BF16)
        w_dt = jnp.pad(w[:, s_end:d_end], ((0, 0), (0, LANES - SSD_HEADS))).astype(BF16)
        w_sc = w[:, d_end:].astype(BF16)
        (q1, k1, v1, q4, k4, v4, q16, k16, v16, z, xbc, dt_raw, sb, u) = in_projection(
            xf, w_attn, w_ssd, w_dt, w_sc, cos_t, sin_t, batch, seq)
        qkv = ((q1.reshape(batch, 1, seq, ATTN_WIDTH), k1.reshape(batch, 1, seq, ATTN_WIDTH),
                v1.reshape(batch, 1, seq, ATTN_WIDTH)), (q4, k4, v4), (q16, k16, v16))
        attn = [dilated_attention(*t, batch, seq, d) for t, d in zip(qkv, DILATIONS)]
        o_list = [attn[0][0].reshape(n, ATTN_WIDTH), attn[1][0], attn[2][0]]
        lse_list = [attn[0][1].reshape(n, LANES), attn[1][1], attn[2][1]]
        ssd = ssd_mixer(z, xbc, dt_raw, ssd_conv_w[layer], ssd_conv_b[layer], ssd_dt_bias[layer],
                        ssd_a_log[layer], ssd_d[layer], ssd_norm_g[layer], batch, seq)
        x1, logits = out_projection(o_list, lse_list, ssd, sb, u, xf,
                                    sc_conv_w[layer], w_out[layer], ln1_g[layer], ln1_b[layer],
                                    router_w[layer], router_b[layer], seq)
        xf = moe_sublayer(x1, logits, layer, exp_w_gu, exp_b_gu, exp_w_down, exp_b_down,
                          ln2_g[layer], ln2_b[layer])
    return xf.reshape(batch, seq, D_MODEL).astype(x.dtype)
```

```python
import functools

import jax
import jax.numpy as jnp
from jax import lax
from jax.experimental import pallas as pl
from jax.experimental.pallas import tpu as pltpu

F32 = jnp.float32
BF16 = jnp.bfloat16
U32 = jnp.uint32

D_MODEL = 1024
HEAD_DIM = 64
ATTN_WIDTH = 512
DILATIONS = (1, 4, 16)
ATTN_SPAN = 128
ATTN_BLOCK = 128
ATTN_Q_ROWS = 1024
ROPE_THETA = 10000.0

SSD_HEADS = 8
SSD_WIDTH = 512
SSD_GROUPS = 2
SSD_STATE = 128
SSD_CONV = 4
SSD_CHUNK = 128
SSD_STEP_ROWS = 1024
SSD_CONV_DIM = 1024

SC_WIDTH = 512
SC_CONV = 3

N_EXPERTS = 32
TOP_K = 4
EXPERT_FF = 1024
SWIGLU_ALPHA = 1.702
SWIGLU_LIMIT = 7.0

DEPTH = 2
ALPHA = (2.0 * DEPTH) ** 0.25
LN_EPS = 1e-5
RMS_EPS = 1e-5

LANES = 128
SUBLANES = 8
HALO = 16
NEG = -1e30
V7X_VMEM_BYTES = 64 * 1024 * 1024
VMEM_LIMIT = V7X_VMEM_BYTES * 7 // 8
BF16_BITS = 16

PROJ_ROWS = 1024
OUT_PROJ_ROWS = 1024
ROUTE_ROWS = 256
ROUTE_STEP_TILES = 4
MOE_BLOCK = 1024
CHUNK = 16
WAIT_GROUP = 8
N_SLOTS = N_EXPERTS + ROUTE_ROWS * TOP_K // CHUNK
SLOT_ROWS = N_SLOTS * CHUNK
MAX_PAIRS = N_SLOTS // 2
PACKED = D_MODEL // 2
HI_MASK = 0xFFFFFFFF ^ ((1 << BF16_BITS) - 1)


def _cparams(n_axes):
    return pltpu.CompilerParams(dimension_semantics=("arbitrary",) * n_axes,
                                vmem_limit_bytes=VMEM_LIMIT)


def _sigmoid(x):
    return 1.0 / (1.0 + jnp.exp(-x))


def _pair_cols(lo, tile, h):
    return jnp.where(lo, tile[:, h:h + 1], tile[:, h + 1:h + 2])


def _pack_rows(v):
    return pltpu.bitcast(v[:, :PACKED], U32) | (pltpu.bitcast(v[:, PACKED:], U32) >> BF16_BITS)


def _unpack_rows(w):
    a = pltpu.bitcast(w & U32(HI_MASK), F32).astype(BF16)
    b = pltpu.bitcast(w << BF16_BITS, F32).astype(BF16)
    return jnp.concatenate([a, b], axis=1)


def _rope_kernel(inv_ref, cos_ref, sin_ref):
    rows = cos_ref.shape[0]
    base = pl.program_id(0) * rows
    pos = (lax.broadcasted_iota(jnp.int32, cos_ref.shape, 0) + base).astype(F32)
    lane = lax.broadcasted_iota(jnp.int32, cos_ref.shape, 1)
    ang = pos * inv_ref[...]
    cos_ref[...] = jnp.cos(ang)
    sin_ref[...] = jnp.where((lane & 32) == 0, -jnp.sin(ang), jnp.sin(ang))


def rope_tables(seq):
    half = HEAD_DIM // 2
    inv = ROPE_THETA ** (-jnp.arange(half, dtype=F32) / half)
    inv = jnp.tile(inv, LANES // half)[None, :]
    rows = PROJ_ROWS
    return pl.pallas_call(
        _rope_kernel,
        out_shape=(jax.ShapeDtypeStruct((seq, LANES), F32),) * 2,
        grid=(seq // rows,),
        in_specs=[pl.BlockSpec((1, LANES), lambda i: (0, 0))],
        out_specs=(pl.BlockSpec((rows, LANES), lambda i: (i, 0)),) * 2,
        compiler_params=_cparams(1),
        name="rope_tables",
    )(inv)


def _inproj_kernel(x_ref, wa_ref, ws_ref, wdt_ref, wc_ref, cos_ref, sin_ref,
                   q1_ref, k1_ref, v1_ref, q4_ref, k4_ref, v4_ref, q16_ref, k16_ref, v16_ref,
                   z_ref, xbc_ref, dt_ref, sb_ref, u_ref, perm_ref):
    tm = x_ref.shape[0]
    xb = x_ref[...].astype(BF16)
    cos = cos_ref[...]
    sin = sin_ref[...]
    lane = lax.broadcasted_iota(jnp.int32, cos.shape, 1)
    first_half = (lane & 32) == 0
    n_slab = ATTN_WIDTH // LANES

    def proj(w_ref, lo, hi):
        return jnp.dot(xb, w_ref[:, lo:hi], preferred_element_type=F32)

    def rope(t):
        rot = jnp.where(first_half, pltpu.roll(t, LANES - 32, 1), pltpu.roll(t, 32, 1))
        return t * cos + rot * sin

    def emit(slabs, nat_ref, strided_refs):
        for j, t in enumerate(slabs):
            perm_ref[j] = t
            nat_ref[:, j * LANES:(j + 1) * LANES] = t.astype(BF16)
        for d, ref in strided_refs:
            for r in range(d):
                for j in range(n_slab):
                    ref[r, :, j * LANES:(j + 1) * LANES] = (
                        perm_ref[j, pl.ds(r, tm // d, stride=d), :].astype(BF16))

    w = ATTN_WIDTH
    q = proj(wa_ref, 0, w)
    emit([rope(q[:, j * LANES:(j + 1) * LANES]) * (HEAD_DIM ** -0.5) for j in range(n_slab)],
         q1_ref, ((4, q4_ref), (16, q16_ref)))
    k = proj(wa_ref, w, 2 * w)
    emit([rope(k[:, j * LANES:(j + 1) * LANES]) for j in range(n_slab)], k1_ref, ((4, k4_ref), (16, k16_ref)))
    v = proj(wa_ref, 2 * w, 3 * w)
    emit([v[:, j * LANES:(j + 1) * LANES] for j in range(n_slab)], v1_ref, ((4, v4_ref), (16, v16_ref)))
    z_ref[...] = proj(ws_ref, 0, SSD_WIDTH).astype(BF16)
    xbc_ref[...] = proj(ws_ref, SSD_WIDTH, SSD_WIDTH + SSD_CONV_DIM).astype(BF16)
    dt_ref[...] = jnp.dot(xb, wdt_ref[...], preferred_element_type=F32)
    sb_ref[...] = proj(wc_ref, 0, SC_WIDTH).astype(BF16)
    c = proj(wc_ref, SC_WIDTH, 2 * SC_WIDTH)
    h = proj(wc_ref, 2 * SC_WIDTH, 3 * SC_WIDTH)
    u_ref[...] = (c * h).astype(BF16)


def in_projection(x, w_attn, w_ssd, w_dt, w_sc, cos_t, sin_t, batch, seq):
    n = x.shape[0]
    tm = PROJ_ROWS
    tps = seq // tm
    row = lambda i: (i, 0)
    full = lambda i: (0, 0)
    tab = lambda i: (i % tps, 0)
    strided = lambda i: (i // tps, 0, i % tps, 0)
    w = ATTN_WIDTH
    nat = jax.ShapeDtypeStruct((n, w), BF16)
    nat_spec = pl.BlockSpec((tm, w), row)
    shapes, specs = [nat] * 3, [nat_spec] * 3
    for d in DILATIONS[1:]:
        shapes += [jax.ShapeDtypeStruct((batch, d, seq // d, w), BF16)] * 3
        specs += [pl.BlockSpec((None, d, tm // d, w), strided)] * 3
    widths = (SSD_WIDTH, SSD_CONV_DIM, LANES, SC_WIDTH, SC_WIDTH)
    dtypes = (BF16, BF16, F32, BF16, BF16)
    shapes += [jax.ShapeDtypeStruct((n, wd), dt) for wd, dt in zip(widths, dtypes)]
    specs += [pl.BlockSpec((tm, wd), row) for wd in widths]
    return pl.pallas_call(
        _inproj_kernel,
        out_shape=tuple(shapes),
        grid=(n // tm,),
        in_specs=[pl.BlockSpec((tm, D_MODEL), row),
                  pl.BlockSpec(w_attn.shape, full, pipeline_mode=pl.Buffered(1)),
                  pl.BlockSpec(w_ssd.shape, full, pipeline_mode=pl.Buffered(1)),
                  pl.BlockSpec(w_dt.shape, full, pipeline_mode=pl.Buffered(1)),
                  pl.BlockSpec(w_sc.shape, full, pipeline_mode=pl.Buffered(1)),
                  pl.BlockSpec((tm, LANES), tab),
                  pl.BlockSpec((tm, LANES), tab)],
        out_specs=tuple(specs),
        scratch_shapes=[pltpu.VMEM((w // LANES, tm, LANES), F32)],
        compiler_params=_cparams(1),
        name="in_projection",
    )(x, w_attn, w_ssd, w_dt, w_sc, cos_t, sin_t)


def _attn_kernel(q_ref, kc_ref, kp_ref, vc_ref, vp_ref, o_ref, lse_ref):
    blk = ATTN_BLOCK
    n = pl.program_id(2)
    lane = lax.broadcasted_iota(jnp.int32, (blk, LANES), 1)
    lo = lane < HEAD_DIM
    qi = lax.broadcasted_iota(jnp.int32, (2 * blk, 2 * blk), 0) & (blk - 1)
    kj = lax.broadcasted_iota(jnp.int32, (2 * blk, 2 * blk), 1)
    dist = qi + blk - kj
    in_band = jnp.where(dist >= 0, jnp.where(dist <= ATTN_SPAN, 1, 0), 0)
    bias = jnp.where(in_band > 0, 0.0, NEG)
    bias_first = jnp.where((in_band * jnp.where(kj >= blk, 1, jnp.where(n > 0, 1, 0))) > 0, 0.0, NEG)
    n_res, tq = q_ref.shape[0], q_ref.shape[1]
    for r, j in ((r, j) for r in range(n_res) for j in range(tq // blk)):
        rows = slice(j * blk, (j + 1) * blk)
        prev = slice((j - 1) * blk, j * blk)
        mask_bias = bias_first if j == 0 else bias
        lse_tile = jnp.zeros((blk, LANES), F32)
        for hp in range(ATTN_WIDTH // LANES):
            sl = slice(hp * LANES, (hp + 1) * LANES)
            q2 = q_ref[r, rows, sl].astype(F32)
            qq = jnp.concatenate([jnp.where(lo, q2, 0.0), jnp.where(lo, 0.0, q2)], axis=0).astype(BF16)
            k_prev = kp_ref[r, :, sl] if j == 0 else kc_ref[r, prev, sl]
            v_prev = vp_ref[r, :, sl] if j == 0 else vc_ref[r, prev, sl]
            k2 = jnp.concatenate([k_prev, kc_ref[r, rows, sl]], axis=0)
            v2 = jnp.concatenate([v_prev, vc_ref[r, rows, sl]], axis=0)
            s = lax.dot_general(qq, k2, (((1,), (1,)), ((), ())), preferred_element_type=F32) + mask_bias
            m = jnp.max(s, axis=1, keepdims=True)
            p = jnp.exp(s - m)
            l = jnp.sum(p, axis=1, keepdims=True)
            pv = jnp.dot(p.astype(BF16), v2, preferred_element_type=F32)
            o = pv / l
            o_ref[r, rows, sl] = jnp.where(lo, o[:blk], o[blk:]).astype(BF16)
            lse = m + jnp.log(l)
            lse_tile = jnp.where(lane == 2 * hp, lse[:blk],
                                 jnp.where(lane == 2 * hp + 1, lse[blk:], lse_tile))
        lse_ref[r, rows, :] = lse_tile


def dilated_attention(q, k, v, batch, seq, dilation):
    length = seq // dilation
    w = ATTN_WIDTH
    blk = ATTN_BLOCK
    tq = min(ATTN_Q_ROWS, length)
    n_res = ATTN_Q_ROWS // tq
    cur = lambda b, r, n: (b, r, n, 0)
    prev = lambda b, r, n: (b, r, jnp.maximum(n * (tq // blk) - 1, 0), 0)
    big = pl.BlockSpec((None, n_res, tq, w), cur)
    small = pl.BlockSpec((None, n_res, blk, w), prev)
    return pl.pallas_call(
        _attn_kernel,
        out_shape=(jax.ShapeDtypeStruct((batch, dilation, length, w), BF16),
                   jax.ShapeDtypeStruct((batch, dilation, length, LANES), F32)),
        grid=(batch, dilation // n_res, length // tq),
        in_specs=[big, big, small, big, small],
        out_specs=(big, pl.BlockSpec((None, n_res, tq, LANES), cur)),
        compiler_params=_cparams(3),
        name=f"dilated_attention_d{dilation}",
    )(q, k, k, v, v)


def _ssd_kernel(z_ref, xbc_ref, halo_ref, dt_ref, shift_ref, cw_ref, cb_ref, dtb_ref, alog_ref, dskip_ref,
                g_ref, y_ref, st_ref):
    ch = SSD_CHUNK
    c = pl.program_id(1)

    @pl.when(c == 0)
    def _():
        st_ref[...] = jnp.zeros_like(st_ref)

    first_halo = halo_ref[...]
    first_halo = jnp.where(c > 0, first_halo, jnp.zeros_like(first_halo))
    for j in range(xbc_ref.shape[0] // ch):
        rows = slice(j * ch, (j + 1) * ch)
        halo = first_halo if j == 0 else xbc_ref[j * ch - HALO:j * ch, :]
        y_ref[rows, :] = _ssd_chunk(xbc_ref[rows, :], halo, z_ref[rows, :], dt_ref[rows, :], shift_ref, cw_ref,
                                    cb_ref, dtb_ref, alog_ref, dskip_ref, g_ref, st_ref)


def _ssd_chunk(xbc_in, halo, z_in, dt_in, shift_ref, cw_ref, cb_ref, dtb_ref, alog_ref, dskip_ref, g_ref, st_ref):
    ch = SSD_CHUNK
    ext = jnp.concatenate([halo, xbc_in], axis=0)
    conv = cb_ref[...] + cw_ref[SSD_CONV - 1:SSD_CONV, :] * xbc_in.astype(F32)
    for t in range(SSD_CONV - 1):
        conv = conv + cw_ref[t:t + 1, :] * jnp.dot(shift_ref[t], ext, preferred_element_type=F32)
    xbc = conv * _sigmoid(conv)
    xh = xbc[:, :SSD_WIDTH]
    bm = xbc[:, SSD_WIDTH:SSD_WIDTH + SSD_GROUPS * SSD_STATE]
    cm = xbc[:, SSD_WIDTH + SSD_GROUPS * SSD_STATE:]

    lane = lax.broadcasted_iota(jnp.int32, (ch, LANES), 1)
    row = lax.broadcasted_iota(jnp.int32, (ch, LANES), 0)
    lo = lane < HEAD_DIM
    lo_row = lo[0:1, :]
    head_lane = lane < SSD_HEADS

    xdt = dt_in + dtb_ref[...]
    dtv = jnp.maximum(xdt, 0.0) + jnp.log(1.0 + jnp.exp(-jnp.abs(xdt)))
    a = jnp.where(head_lane[0:1, :], -jnp.exp(alog_ref[...]), 0.0)
    acum = dtv * a
    shift = 1
    while shift < ch:
        acum = acum + jnp.where(row >= shift, pltpu.roll(acum, shift, 0), 0.0)
        shift *= 2
    acum_t = acum.T
    tot = acum[ch - 1:ch, :]
    e_in = jnp.exp(acum)
    e_out = jnp.exp(tot - acum)
    e_tot = jnp.exp(tot)
    causal = row >= lane

    ys = []
    for g in range(SSD_GROUPS):
        bg = bm[:, g * SSD_STATE:(g + 1) * SSD_STATE]
        cg = cm[:, g * SSD_STATE:(g + 1) * SSD_STATE].astype(BF16)
        cb = lax.dot_general(cg, bg.astype(BF16), (((1,), (1,)), ((), ())),
                             preferred_element_type=F32)
        bg_t = bg.T.astype(BF16)
        for pp in range(2):
            pair = 2 * g + pp
            ha = 2 * pair
            sl = slice(pair * LANES, (pair + 1) * LANES)
            xp = xh[:, sl]
            xdt_pair = xp * _pair_cols(lo, dtv, ha)
            xb16 = xdt_pair.astype(BF16)
            diag = []
            for h in (ha, ha + 1):
                seg = acum[:, h:h + 1] - acum_t[h:h + 1, :]
                decay = jnp.exp(jnp.where(causal, seg, NEG))
                diag.append(jnp.dot((cb * decay).astype(BF16), xb16, preferred_element_type=F32))
            y = jnp.where(lo, diag[0], diag[1])
            state = st_ref[pair]
            y = y + jnp.dot(cg, state.astype(BF16), preferred_element_type=F32) * _pair_cols(lo, e_in, ha)
            xout = (xdt_pair * _pair_cols(lo, e_out, ha)).astype(BF16)
            st_ref[pair] = (state * _pair_cols(lo_row, e_tot, ha)
                            + jnp.dot(bg_t, xout, preferred_element_type=F32))
            ys.append(y + dskip_ref[:, sl] * xp)
    y = jnp.concatenate(ys, axis=1)
    zz = z_in.astype(F32)
    y = y * (zz * _sigmoid(zz))
    gw = SSD_WIDTH // SSD_GROUPS
    outs = []
    for g in range(SSD_GROUPS):
        yg = y[:, g * gw:(g + 1) * gw]
        ms = jnp.mean(yg * yg, axis=1, keepdims=True)
        outs.append(yg * lax.rsqrt(ms + RMS_EPS))
    return (jnp.concatenate(outs, axis=1) * g_ref[...]).astype(BF16)


def ssd_mixer(z, xbc, dt_raw, conv_w, conv_b, dt_bias, a_log, d_skip, norm_g, batch, seq):
    n = batch * seq
    ch = SSD_CHUNK
    step = SSD_STEP_ROWS
    nc = seq // step
    pad = lambda t: jnp.pad(t.astype(F32), (0, LANES - t.shape[0]))[None, :]
    cur = lambda b, c: (b * nc + c, 0)
    halo = lambda b, c: (jnp.maximum((b * nc + c) * (step // HALO) - 1, 0), 0)
    full = lambda b, c: (0, 0)
    r_idx = lax.broadcasted_iota(jnp.int32, (SSD_CONV - 1, ch, HALO + ch), 1)
    j_idx = lax.broadcasted_iota(jnp.int32, (SSD_CONV - 1, ch, HALO + ch), 2)
    t_idx = lax.broadcasted_iota(jnp.int32, (SSD_CONV - 1, ch, HALO + ch), 0)
    shift = jnp.where(j_idx == r_idx + HALO - (SSD_CONV - 1) + t_idx, 1.0, 0.0).astype(BF16)
    return pl.pallas_call(
        _ssd_kernel,
        out_shape=jax.ShapeDtypeStruct((n, SSD_WIDTH), BF16),
        grid=(batch, nc),
        in_specs=[pl.BlockSpec((step, SSD_WIDTH), cur),
                  pl.BlockSpec((step, SSD_CONV_DIM), cur),
                  pl.BlockSpec((HALO, SSD_CONV_DIM), halo),
                  pl.BlockSpec((step, LANES), cur),
                  pl.BlockSpec(shift.shape, lambda b, c: (0, 0, 0)),
                  pl.BlockSpec((SSD_CONV, SSD_CONV_DIM), full),
                  pl.BlockSpec((1, SSD_CONV_DIM), full),
                  pl.BlockSpec((1, LANES), full),
                  pl.BlockSpec((1, LANES), full),
                  pl.BlockSpec((1, SSD_WIDTH), full),
                  pl.BlockSpec((1, SSD_WIDTH), full)],
        out_specs=pl.BlockSpec((step, SSD_WIDTH), cur),
        scratch_shapes=[pltpu.VMEM((SSD_HEADS // 2, SSD_STATE, LANES), F32)],
        compiler_params=_cparams(2),
        name="ssd_mixer",
    )(z, xbc, xbc, dt_raw, shift, conv_w.astype(F32), conv_b.astype(F32)[None, :], pad(dt_bias), pad(a_log),
      jnp.repeat(d_skip.astype(F32), HEAD_DIM)[None, :], norm_g.astype(F32)[None, :])


def _layer_norm(r, g, b):
    mu = jnp.mean(r, axis=1, keepdims=True)
    d = r - mu
    var = jnp.mean(d * d, axis=1, keepdims=True)
    return d * lax.rsqrt(var + LN_EPS) * g + b


def _outproj_kernel(tiles_per_seq, o1_ref, o4_ref, o16_ref, l1_ref, l4_ref, l16_ref, ssd_ref, sb_ref,
                    u_ref, uh_ref, x_ref, cw_ref, wout_ref, g_ref, b_ref, spread_ref, rw_ref,
                    rb_ref, x1_ref, logit_ref, ext_ref, operm_ref, lperm_ref):
    tm = x_ref.shape[0]
    i = pl.program_id(0)
    n_slab = ATTN_WIDTH // LANES

    for idx, (d, o_ref, l_ref) in enumerate(((4, o4_ref, l4_ref), (16, o16_ref, l16_ref))):
        for r in range(d):
            rows = pl.ds(r, tm // d, stride=d)
            lperm_ref[idx, rows, :] = l_ref[r]
            for j in range(n_slab):
                operm_ref[idx * n_slab + j, rows, :] = o_ref[r, :, j * LANES:(j + 1) * LANES].astype(F32)

    lses = (l1_ref[...], lperm_ref[0], lperm_ref[1])
    top = jnp.maximum(jnp.maximum(lses[0], lses[1]), lses[2])
    es = [jnp.exp(l - top) for l in lses]
    den = es[0] + es[1] + es[2]
    wide = []
    for e in es:
        w = e / den
        w_hi = w.astype(BF16)
        w_lo = (w - w_hi.astype(F32)).astype(BF16)
        wide.append(jnp.dot(jnp.concatenate([w_hi, w_lo], axis=1), spread_ref[...], preferred_element_type=F32))
    attn = []
    for hp in range(n_slab):
        sl = slice(hp * LANES, (hp + 1) * LANES)
        acc = wide[0][:, sl] * o1_ref[:, sl].astype(F32)
        acc = acc + wide[1][:, sl] * operm_ref[hp]
        acc = acc + wide[2][:, sl] * operm_ref[n_slab + hp]
        attn.append(acc.astype(BF16))

    seq_start = (i % tiles_per_seq) == 0
    ext_ref[0:HALO, :] = jnp.where(seq_start, 0.0, uh_ref[...].astype(F32))
    ext_ref[HALO:HALO + tm, :] = u_ref[...].astype(F32)
    conv = jnp.zeros((tm, SC_WIDTH), F32)
    for t in range(SC_CONV):
        off = HALO - (SC_CONV - 1) + t
        conv = conv + cw_ref[t:t + 1, :] * ext_ref[off:off + tm, :]
    gated = (sb_ref[...].astype(F32) * conv).astype(BF16)

    mixed = jnp.concatenate(attn + [ssd_ref[...], gated], axis=1)
    mix = jnp.dot(mixed, wout_ref[...], preferred_element_type=F32)
    x1 = _layer_norm(ALPHA * x_ref[...] + mix, g_ref[...], b_ref[...])
    x1_ref[...] = x1

    xh = x1.astype(BF16)
    xm = (x1 - xh.astype(F32)).astype(BF16)
    ph = jnp.dot(xh, rw_ref[...], preferred_element_type=F32)
    pm = jnp.dot(xm, rw_ref[...], preferred_element_type=F32)
    logit_ref[...] = pm + pltpu.roll(ph, LANES - N_EXPERTS, 1) + ph + rb_ref[...]


def out_projection(o_list, lse_list, ssd, sb, u, x, sc_conv_w, w_out, ln_g, ln_b, router_w, router_b, seq):
    n = x.shape[0]
    tm = OUT_PROJ_ROWS
    tps = seq // tm
    row = lambda i: (i, 0)
    full = lambda i: (0, 0)
    strided = lambda i: (i // tps, 0, i % tps, 0)
    halo = lambda i: (jnp.maximum(i * (tm // HALO) - 1, 0), 0)
    rw_f = router_w.astype(F32)
    rw_hi = rw_f.astype(BF16)
    rw_mid = (rw_f - rw_hi.astype(F32)).astype(BF16)
    rw = jnp.pad(jnp.concatenate([rw_hi, rw_mid], axis=1), ((0, 0), (0, LANES - 2 * N_EXPERTS)))
    rb = jnp.pad(router_b.astype(F32), (0, LANES - N_EXPERTS))[None, :]
    spread = jnp.where(lax.broadcasted_iota(jnp.int32, (2 * LANES, ATTN_WIDTH), 0) % LANES
                       == lax.broadcasted_iota(jnp.int32, (2 * LANES, ATTN_WIDTH), 1) // HEAD_DIM,
                       1.0, 0.0).astype(BF16)
    wide = lambda wd: pl.BlockSpec((tm, wd), row)
    perm = lambda d, wd: pl.BlockSpec((None, d, tm // d, wd), strided)
    const = lambda a: pl.BlockSpec(a.shape, full)
    wout = w_out.astype(BF16)
    cw = sc_conv_w.astype(F32)
    g = ln_g.astype(F32)[None, :]
    b = ln_b.astype(F32)[None, :]
    n_slab = ATTN_WIDTH // LANES
    return pl.pallas_call(
        functools.partial(_outproj_kernel, tps),
        out_shape=(jax.ShapeDtypeStruct((n, D_MODEL), F32), jax.ShapeDtypeStruct((n, LANES), F32)),
        grid=(n // tm,),
        in_specs=[wide(ATTN_WIDTH), perm(4, ATTN_WIDTH), perm(16, ATTN_WIDTH),
                  wide(LANES), perm(4, LANES), perm(16, LANES),
                  wide(SSD_WIDTH), wide(SC_WIDTH), wide(SC_WIDTH), pl.BlockSpec((HALO, SC_WIDTH), halo),
                  wide(D_MODEL), const(cw), const(wout), const(g), const(b),
                  const(spread), const(rw), const(rb)],
        out_specs=(wide(D_MODEL), wide(LANES)),
        scratch_shapes=[pltpu.VMEM((HALO + tm, SC_WIDTH), F32),
                        pltpu.VMEM((2 * n_slab, tm, LANES), F32),
                        pltpu.VMEM((2, tm, LANES), F32)],
        compiler_params=_cparams(1),
        name="out_projection",
    )(*o_list, *lse_list, ssd, sb, u, u, x, cw, wout, g, b, spread, rw, rb)


INFO_IDX, INFO_SLOT, INFO_GATE = 0, TOP_K, 2 * TOP_K
TAB_RUN, TAB_BASE, TAB_CHUNKS = 0, 1, 2


def _route_kernel(logit_ref, upper_ref, info_ref, tab_ref, cnt_ref, run_ref):
    @pl.when(pl.program_id(0) == 0)
    def _():
        run_ref[...] = jnp.zeros_like(run_ref)

    tm = ROUTE_ROWS
    for j in range(logit_ref.shape[0] // tm):
        info, tab = _route_tile(logit_ref[j * tm:(j + 1) * tm, :], upper_ref, run_ref)
        info_ref[j * tm:(j + 1) * tm, :] = info
        tab_ref[j * SUBLANES:(j + 1) * SUBLANES, :] = tab
    cnt_ref[...] = jnp.broadcast_to(run_ref[...], cnt_ref.shape)


def _route_tile(logits, upper_ref, run_ref):
    tm = logits.shape[0]
    ne = N_EXPERTS
    rec_rows = 2 * SUBLANES
    work = logits.T[:ne, :]
    eidx = lax.broadcasted_iota(jnp.int32, (ne, tm), 0).astype(F32)
    sel = jnp.zeros((ne, tm), F32)
    vals, idxs, hits = [], [], []
    for _ in range(TOP_K):
        m = jnp.max(work, axis=0, keepdims=True)
        idx = jnp.min(jnp.where(work == m, eidx, float(ne)), axis=0, keepdims=True)
        hit = eidx == idx
        work = jnp.where(hit, NEG, work)
        sel = sel + jnp.where(hit, 1.0, 0.0)
        vals.append(m)
        idxs.append(idx)
        hits.append(hit)
    es = [jnp.exp(v - vals[0]) for v in vals]
    den = es[0] + es[1] + es[2] + es[3]
    before = jnp.dot(sel.astype(BF16), upper_ref[...], preferred_element_type=F32)
    cnt = jnp.sum(sel, axis=1, keepdims=True)
    chunks = jnp.floor((cnt + (CHUNK - 1)) * (1.0 / CHUNK))
    row = lax.broadcasted_iota(jnp.int32, (ne, LANES), 0)
    chunks_w = jnp.broadcast_to(chunks, (ne, LANES))
    incl = chunks_w
    shift = 1
    while shift < ne:
        incl = incl + jnp.where(row >= shift, pltpu.roll(incl, shift, 0), 0.0)
        shift *= 2
    base = (incl - chunks_w)[:, 0:1]

    sub = lax.broadcasted_iota(jnp.int32, (rec_rows, tm), 0)
    rec = jnp.zeros((rec_rows, tm), F32)
    for k in range(TOP_K):
        rank = jnp.sum(jnp.where(hits[k], before, 0.0), axis=0, keepdims=True)
        first = jnp.sum(jnp.where(hits[k], base, 0.0), axis=0, keepdims=True)
        rec = jnp.where(sub == INFO_IDX + k, idxs[k], rec)
        rec = jnp.where(sub == INFO_SLOT + k, first * CHUNK + rank, rec)
        rec = jnp.where(sub == INFO_GATE + k, es[k] / den, rec)
    info = jnp.concatenate([rec, jnp.zeros((LANES - rec_rows, tm), F32)], axis=0).T

    lane = lax.broadcasted_iota(jnp.int32, (ne, LANES), 1)
    cnt8 = jnp.floor((cnt + (SUBLANES - 1)) * (1.0 / SUBLANES)) * SUBLANES
    cols = jnp.where(lane == 0, cnt8, jnp.where(lane == 1, base, jnp.where(lane == 2, chunks, 0.0)))
    rows = jnp.concatenate([cols, jnp.zeros((LANES - ne, LANES), F32)], axis=0).T
    sub8 = lax.broadcasted_iota(jnp.int32, (SUBLANES, LANES), 0)
    tab = jnp.where(sub8 == TAB_RUN, run_ref[...],
                    jnp.where(sub8 == TAB_BASE, rows[1:2, :], jnp.where(sub8 == TAB_CHUNKS, rows[2:3, :], 0.0)))
    run_ref[...] = run_ref[...] + rows[0:1, :]
    return info, tab


def route(logits):
    n = logits.shape[0]
    tm = ROUTE_ROWS
    r = lax.broadcasted_iota(jnp.int32, (tm, tm), 0)
    c = lax.broadcasted_iota(jnp.int32, (tm, tm), 1)
    upper = jnp.where(r < c, 1.0, 0.0).astype(BF16)
    return pl.pallas_call(
        _route_kernel,
        out_shape=(jax.ShapeDtypeStruct((n, LANES), F32),
                   jax.ShapeDtypeStruct((n // tm * SUBLANES, LANES), F32),
                   jax.ShapeDtypeStruct((SUBLANES, LANES), F32)),
        grid=(n // (tm * ROUTE_STEP_TILES),),
        in_specs=[pl.BlockSpec((tm * ROUTE_STEP_TILES, LANES), lambda i: (i, 0)),
                  pl.BlockSpec((tm, tm), lambda i: (0, 0))],
        out_specs=(pl.BlockSpec((tm * ROUTE_STEP_TILES, LANES), lambda i: (i, 0)),
                   pl.BlockSpec((SUBLANES * ROUTE_STEP_TILES, LANES), lambda i: (i, 0)),
                   pl.BlockSpec((SUBLANES, LANES), lambda i: (0, 0))),
        scratch_shapes=[pltpu.VMEM((1, LANES), F32)],
        compiler_params=_cparams(1),
        name="moe_route",
    )(logits, upper)


def _start_all(n_chunks, make_copy):
    def start(s, carry):
        make_copy(s).start()
        return carry

    lax.fori_loop(0, n_chunks, start, 0)


def _wait_all(n_chunks, make_wait):
    groups = n_chunks // WAIT_GROUP

    def wait_group(g, carry):
        make_wait(WAIT_GROUP).wait()
        return carry

    def wait(s, carry):
        make_wait(1).wait()
        return carry

    lax.fori_loop(0, groups, wait_group, 0)
    lax.fori_loop(groups * WAIT_GROUP, n_chunks, wait, 0)


def _dispatch_kernel(pslot_ref, pdst_ref, npair_ref, sslot_ref, sdst_ref, nsingle_ref, nslot_ref, zdst_ref,
                     x_ref, info_ref, buf_ref, slots_ref, zero_ref, sems, zsem):
    i = pl.program_id(0)
    tm = x_ref.shape[0]
    cur = i % 2

    def put(tile, side, slot_tab, dst_tab, width, chunks):
        return lambda s: pltpu.make_async_copy(
            slots_ref.at[side, pl.ds(pl.multiple_of(slot_tab[tile * width + s] * CHUNK, CHUNK), chunks * CHUNK)],
            buf_ref.at[pl.ds(pl.multiple_of(dst_tab[tile * width + s], SUBLANES), chunks * CHUNK)],
            sems.at[side])

    def start_tile(tile, side):
        _start_all(npair_ref[tile], put(tile, side, pslot_ref, pdst_ref, MAX_PAIRS, 2))
        _start_all(nsingle_ref[tile], put(tile, side, sslot_ref, sdst_ref, N_EXPERTS, 1))

    def landed(side):
        return lambda k: pltpu.make_async_copy(slots_ref.at[side, pl.ds(0, k * CHUNK)],
                                               buf_ref.at[pl.ds(0, k * CHUNK)], sems.at[side])

    @pl.when(i == 0)
    def _():
        zero_ref[...] = jnp.zeros_like(zero_ref)

        def zero_copy(e):
            row = pl.multiple_of(jnp.maximum(zdst_ref[e], 0), MOE_BLOCK)
            return pltpu.make_async_copy(zero_ref, buf_ref.at[pl.ds(row, MOE_BLOCK)], zsem)

        def start(e, carry):
            @pl.when(zdst_ref[e] >= 0)
            def _():
                zero_copy(e).start()
            return carry

        def wait(e, carry):
            @pl.when(zdst_ref[e] >= 0)
            def _():
                zero_copy(e).wait()
            return carry

        lax.fori_loop(0, zdst_ref.shape[0], start, 0)
        lax.fori_loop(0, zdst_ref.shape[0], wait, 0)

    xb = x_ref[...].astype(BF16)
    slot_t = info_ref[...].T[INFO_SLOT:INFO_SLOT + TOP_K, :]
    local = lax.broadcasted_iota(jnp.int32, (tm, tm), 0).astype(F32).astype(BF16)
    one = jnp.ones((tm, tm), BF16)
    zero = jnp.zeros((tm, tm), BF16)
    for c in range(SLOT_ROWS // tm):
        target = (slot_t - float(c * tm)).astype(BF16)
        onehot = zero
        for k in range(TOP_K):
            onehot = onehot + jnp.where(local == target[k:k + 1, :], one, zero)
        rows = jnp.dot(onehot, xb, preferred_element_type=F32)
        slots_ref[cur, c * tm:(c + 1) * tm, :] = _pack_rows(rows)

    @pl.when(i > 0)
    def _():
        _wait_all(nslot_ref[i - 1], landed(1 - cur))

    start_tile(i, cur)

    @pl.when(i == pl.num_programs(0) - 1)
    def _():
        _wait_all(nslot_ref[i], landed(cur))


def dispatch(x1, info, copies, zdst, padded_rows):
    n = x1.shape[0]
    tm = ROUTE_ROWS
    row = lambda i, *_: (i, 0)
    return pl.pallas_call(
        _dispatch_kernel,
        out_shape=jax.ShapeDtypeStruct((padded_rows, PACKED), U32),
        grid_spec=pltpu.PrefetchScalarGridSpec(
            num_scalar_prefetch=len(copies) + 1,
            grid=(n // tm,),
            in_specs=[pl.BlockSpec((tm, D_MODEL), row), pl.BlockSpec((tm, LANES), row)],
            out_specs=pl.BlockSpec(memory_space=pl.ANY),
            scratch_shapes=[pltpu.VMEM((2, SLOT_ROWS, PACKED), U32), pltpu.VMEM((MOE_BLOCK, PACKED), U32),
                            pltpu.SemaphoreType.DMA((2,)), pltpu.SemaphoreType.DMA(())]),
        compiler_params=_cparams(1),
        name="moe_dispatch",
    )(*copies, zdst, x1, info)


def _expert_kernel(blk_expert_ref, n_used_ref, blk_rows_ref, xs_ref, wgu_ref, bgu_ref, wd_ref, bd_ref, y_ref):
    del blk_expert_ref
    i = pl.program_id(0)
    used = i < n_used_ref[0]
    half = MOE_BLOCK // 2
    full = blk_rows_ref[i] > half

    def mlp(rows):
        xb = _unpack_rows(xs_ref[rows, :])
        gu = jnp.dot(xb, wgu_ref[...].astype(BF16), preferred_element_type=F32) + bgu_ref[...]
        gate = jnp.minimum(gu[:, :EXPERT_FF], SWIGLU_LIMIT)
        up = jnp.clip(gu[:, EXPERT_FF:], -SWIGLU_LIMIT, SWIGLU_LIMIT)
        h = (up + 1.0) * gate * _sigmoid(SWIGLU_ALPHA * gate)
        y = jnp.dot(h.astype(BF16), wd_ref[...].astype(BF16), preferred_element_type=F32) + bd_ref[...]
        y_ref[rows, :] = _pack_rows(y.astype(BF16).astype(F32))

    @pl.when(jnp.logical_not(used))
    def _():
        y_ref[...] = jnp.zeros_like(y_ref)

    @pl.when(jnp.logical_and(used, full))
    def _():
        mlp(slice(0, MOE_BLOCK))

    @pl.when(jnp.logical_and(used, jnp.logical_not(full)))
    def _():
        mlp(slice(0, half))
        y_ref[half:, :] = jnp.zeros((half, PACKED), U32)


def expert_mlp(xs, blk_expert, n_used, blk_rows, layer, w_gu, b_gu, w_down, b_down):
    padded_rows = xs.shape[0]
    blk = MOE_BLOCK
    rows_in = lambda i, be, nu, br: (jnp.minimum(i, nu[0] - 1), 0)
    per_expert = lambda i, be, nu, br: (layer, be[i], 0, 0)
    return pl.pallas_call(
        _expert_kernel,
        out_shape=jax.ShapeDtypeStruct((padded_rows, PACKED), U32),
        grid_spec=pltpu.PrefetchScalarGridSpec(
            num_scalar_prefetch=3,
            grid=(padded_rows // blk,),
            in_specs=[pl.BlockSpec((blk, PACKED), rows_in),
                      pl.BlockSpec((None, None, D_MODEL, 2 * EXPERT_FF), per_expert),
                      pl.BlockSpec((None, None, 1, 2 * EXPERT_FF), per_expert),
                      pl.BlockSpec((None, None, EXPERT_FF, D_MODEL), per_expert),
                      pl.BlockSpec((None, None, 1, D_MODEL), per_expert)],
            out_specs=pl.BlockSpec((blk, PACKED), lambda i, be, nu, br: (i, 0))),
        compiler_params=_cparams(1),
        name="expert_mlp",
    )(blk_expert, n_used, blk_rows, xs, w_gu, b_gu, w_down, b_down)


def _combine_kernel(pslot_ref, pdst_ref, npair_ref, sslot_ref, sdst_ref, nsingle_ref, nslot_ref,
                    x_ref, info_ref, g_ref, b_ref, y_ref, out_ref, slots_ref, sems):
    i = pl.program_id(0)
    tm = x_ref.shape[0]
    cur = i % 2

    def fetch(tile, side, slot_tab, dst_tab, width, chunks):
        return lambda s: pltpu.make_async_copy(
            y_ref.at[pl.ds(pl.multiple_of(dst_tab[tile * width + s], SUBLANES), chunks * CHUNK)],
            slots_ref.at[side, pl.ds(pl.multiple_of(slot_tab[tile * width + s] * CHUNK, CHUNK), chunks * CHUNK)],
            sems.at[side])

    def start_tile(tile, side):
        _start_all(npair_ref[tile], fetch(tile, side, pslot_ref, pdst_ref, MAX_PAIRS, 2))
        _start_all(nsingle_ref[tile], fetch(tile, side, sslot_ref, sdst_ref, N_EXPERTS, 1))

    @pl.when(i == 0)
    def _():
        slots_ref[...] = jnp.zeros_like(slots_ref)
        start_tile(0, 0)

    @pl.when(i + 1 < pl.num_programs(0))
    def _():
        start_tile(i + 1, 1 - cur)

    _wait_all(nslot_ref[i], lambda k: pltpu.make_async_copy(
        y_ref.at[pl.ds(0, k * CHUNK)], slots_ref.at[cur, pl.ds(0, k * CHUNK)], sems.at[cur]))

    info = info_ref[...]
    ffn = jnp.zeros((tm, D_MODEL), F32)
    local = lax.broadcasted_iota(jnp.int32, (tm, tm), 1).astype(F32).astype(BF16)
    zero = jnp.zeros((tm, tm), BF16)
    gates = [info[:, INFO_GATE + k:INFO_GATE + k + 1].astype(BF16) for k in range(TOP_K)]
    for c in range(SLOT_ROWS // tm):
        w = zero
        for k in range(TOP_K):
            target = (info[:, INFO_SLOT + k:INFO_SLOT + k + 1] - float(c * tm)).astype(BF16)
            w = w + jnp.where(local == target, gates[k], zero)
        rows = _unpack_rows(slots_ref[cur, c * tm:(c + 1) * tm, :])
        ffn = ffn + jnp.dot(w, rows, preferred_element_type=F32)
    out_ref[...] = _layer_norm(ALPHA * x_ref[...] + ffn, g_ref[...], b_ref[...])


def combine(x1, info, copies, y, ln_g, ln_b):
    n = x1.shape[0]
    tm = ROUTE_ROWS
    row = lambda i, *_: (i, 0)
    full = lambda i, *_: (0, 0)
    return pl.pallas_call(
        _combine_kernel,
        out_shape=jax.ShapeDtypeStruct((n, D_MODEL), F32),
        grid_spec=pltpu.PrefetchScalarGridSpec(
            num_scalar_prefetch=len(copies),
            grid=(n // tm,),
            in_specs=[pl.BlockSpec((tm, D_MODEL), row),
                      pl.BlockSpec((tm, LANES), row),
                      pl.BlockSpec((1, D_MODEL), full),
                      pl.BlockSpec((1, D_MODEL), full),
                      pl.BlockSpec(memory_space=pl.ANY)],
            out_specs=pl.BlockSpec((tm, D_MODEL), row),
            scratch_shapes=[pltpu.VMEM((2, SLOT_ROWS, PACKED), U32), pltpu.SemaphoreType.DMA((2,))]),
        compiler_params=_cparams(1),
        name="moe_combine",
    )(*copies, x1, info, ln_g.astype(F32)[None, :], ln_b.astype(F32)[None, :], y)


def moe_sublayer(x1, logits, layer, w_gu, b_gu, w_down, b_down, ln_g, ln_b):
    n = x1.shape[0]
    blk = MOE_BLOCK
    n_tiles = n // ROUTE_ROWS
    padded_rows = n * TOP_K + (SUBLANES - 1) * N_EXPERTS * n_tiles + N_EXPERTS * (blk + CHUNK)
    padded_rows = -(-padded_rows // blk) * blk
    info, tab, cnt = route(logits)

    i32 = lambda t: t.astype(jnp.int32)
    tab = tab.reshape(n_tiles, SUBLANES, LANES)[:, :, :N_EXPERTS]
    run, base, chunks = i32(tab[:, TAB_RUN]), i32(tab[:, TAB_BASE]), i32(tab[:, TAB_CHUNKS])
    counts = i32(cnt[0, :N_EXPERTS])
    padded = (counts + CHUNK + blk - 1) // blk * blk
    seg_end = jnp.cumsum(padded)
    seg_start = seg_end - padded
    row0 = seg_start[None, :] + run

    def copy_list(per_expert, width, first_chunk):
        ends = jnp.cumsum(per_expert, axis=1)
        item = jnp.arange(width, dtype=jnp.int32)
        owner = jnp.minimum(jnp.sum(i32(ends[:, None, :] <= item[None, :, None]), -1), N_EXPERTS - 1)
        onehot = i32(owner[:, :, None] == jnp.arange(N_EXPERTS, dtype=jnp.int32))
        pick = lambda v: jnp.sum(onehot * v[:, None, :], -1)
        chunk = first_chunk(item[None, :] - pick(ends - per_expert), pick)
        return (pick(base) + chunk).reshape(-1), (pick(row0) + chunk * CHUNK).reshape(-1), ends[:, -1]

    pairs = copy_list(chunks // 2, MAX_PAIRS, lambda q, pick: 2 * q)
    singles = copy_list(chunks % 2, N_EXPERTS, lambda q, pick: pick(chunks) - 1)
    copies = (*pairs, *singles, jnp.sum(chunks, axis=1))
    total = seg_end[N_EXPERTS - 1]
    n_blocks = padded_rows // blk
    blk_row = jnp.arange(n_blocks, dtype=jnp.int32) * blk
    blk_expert = jnp.minimum(jnp.sum(i32(seg_end[None, :] <= blk_row[:, None]), -1), N_EXPERTS - 1)
    onehot_blk = i32(blk_expert[:, None] == jnp.arange(N_EXPERTS, dtype=jnp.int32))
    n_used = (total // blk).reshape(1)
    tail0 = (seg_start + counts) // blk * blk
    tail1 = jnp.where(tail0 + blk < seg_end, tail0 + blk, -1)
    zdst = jnp.concatenate([tail0, tail1, jnp.where(blk_row >= total, blk_row, -1)])

    xs = dispatch(x1, info, copies, zdst, padded_rows)
    seg_rows_end = jnp.sum(onehot_blk * (seg_start + counts)[None, :], -1)
    blk_rows = jnp.clip(seg_rows_end - blk_row, 0, blk)
    y = expert_mlp(xs, blk_expert, n_used, blk_rows, layer, w_gu.astype(F32), b_gu.astype(F32)[:, :, None, :],
                   w_down.astype(F32), b_down.astype(F32)[:, :, None, :])
    return combine(x1, info, copies, y, ln_g, ln_b)


def kernel(x, w_in, ssd_conv_w, ssd_conv_b, ssd_dt_bias, ssd_a_log, ssd_d, ssd_norm_g, sc_conv_w, w_out,
           ln1_g, ln1_b, router_w, router_b, exp_w_gu, exp_b_gu, exp_w_down, exp_b_down, ln2_g, ln2_b):
    batch, seq, _ = x.shape
    n = batch * seq
    cos_t, sin_t = rope_tables(seq)
    xf = x.reshape(n, D_MODEL).astype(F32)
    a_end = 3 * ATTN_WIDTH
    s_end = a_end + SSD_WIDTH + SSD_CONV_DIM
    d_end = s_end + SSD_HEADS
    for layer in range(w_in.shape[0]):
        w = w_in[layer]
        w_attn = w[:, :a_end].astype(BF16)
        w_ssd = w[:, a_end:s_end].astype(BF16)
        w_dt = jnp.pad(w[:, s_end:d_end], ((0, 0), (0, LANES - SSD_HEADS))).astype(BF16)
        w_sc = w[:, d_end:].astype(BF16)
        (q1, k1, v1, q4, k4, v4, q16, k16, v16, z, xbc, dt_raw, sb, u) = in_projection(
            xf, w_attn, w_ssd, w_dt, w_sc, cos_t, sin_t, batch, seq)
        qkv = ((q1.reshape(batch, 1, seq, ATTN_WIDTH), k1.reshape(batch, 1, seq, ATTN_WIDTH),
                v1.reshape(batch, 1, seq, ATTN_WIDTH)), (q4, k4, v4), (q16, k16, v16))
        attn = [dilated_attention(*t, batch, seq, d) for t, d in zip(qkv, DILATIONS)]
        o_list = [attn[0][0].reshape(n, ATTN_WIDTH), attn[1][0], attn[2][0]]
        lse_list = [attn[0][1].reshape(n, LANES), attn[1][1], attn[2][1]]
        ssd = ssd_mixer(z, xbc, dt_raw, ssd_conv_w[layer], ssd_conv_b[layer], ssd_dt_bias[layer],
                        ssd_a_log[layer], ssd_d[layer], ssd_norm_g[layer], batch, seq)
        x1, logits = out_projection(o_list, lse_list, ssd, sb, u, xf,
                                    sc_conv_w[layer], w_out[layer], ln1_g[layer], ln1_b[layer],
                                    router_w[layer], router_b[layer], seq)
        xf = moe_sublayer(x1, logits, layer, exp_w_gu, exp_b_gu, exp_w_down, exp_b_down,
                          ln2_g[layer], ln2_b[layer])
    return xf.reshape(batch, seq, D_MODEL).astype(x.dtype)
```

```python
import functools

import jax
import jax.numpy as jnp
from jax import lax
from jax.experimental import pallas as pl
from jax.experimental.pallas import tpu as pltpu

F32 = jnp.float32
BF16 = jnp.bfloat16
U32 = jnp.uint32

D_MODEL = 1024
HEAD_DIM = 64
ATTN_WIDTH = 512
DILATIONS = (1, 4, 16)
ATTN_SPAN = 128
ATTN_BLOCK = 128
ATTN_Q_ROWS = 1024
ROPE_THETA = 10000.0

SSD_HEADS = 8
SSD_WIDTH = 512
SSD_GROUPS = 2
SSD_STATE = 128
SSD_CONV = 4
SSD_CHUNK = 128
SSD_STEP_ROWS = 1024
SSD_CONV_DIM = 1024

SC_WIDTH = 512
SC_CONV = 3

N_EXPERTS = 32
TOP_K = 4
EXPERT_FF = 1024
SWIGLU_ALPHA = 1.702
SWIGLU_LIMIT = 7.0

DEPTH = 2
ALPHA = (2.0 * DEPTH) ** 0.25
LN_EPS = 1e-5
RMS_EPS = 1e-5

LANES = 128
SUBLANES = 8
HALO = 16
NEG = -1e30
V7X_VMEM_BYTES = 64 * 1024 * 1024
VMEM_LIMIT = V7X_VMEM_BYTES * 7 // 8
BF16_BITS = 16

PROJ_ROWS = 1024
OUT_PROJ_ROWS = 1024
ROUTE_ROWS = 256
ROUTE_STEP_TILES = 4
MOE_BLOCK = 1024
CHUNK = 16
WAIT_GROUP = 8
N_SLOTS = N_EXPERTS + ROUTE_ROWS * TOP_K // CHUNK
SLOT_ROWS = N_SLOTS * CHUNK
MAX_PAIRS = N_SLOTS // 2
PACKED = D_MODEL // 2
HI_MASK = 0xFFFFFFFF ^ ((1 << BF16_BITS) - 1)


def _cparams(n_axes):
    return pltpu.CompilerParams(dimension_semantics=("arbitrary",) * n_axes,
                                vmem_limit_bytes=VMEM_LIMIT)


def _sigmoid(x):
    return 1.0 / (1.0 + jnp.exp(-x))


def _pair_cols(lo, tile, h):
    return jnp.where(lo, tile[:, h:h + 1], tile[:, h + 1:h + 2])


def _pack_rows(v):
    return pltpu.bitcast(v[:, :PACKED], U32) | (pltpu.bitcast(v[:, PACKED:], U32) >> BF16_BITS)


def _unpack_rows(w):
    a = pltpu.bitcast(w & U32(HI_MASK), F32).astype(BF16)
    b = pltpu.bitcast(w << BF16_BITS, F32).astype(BF16)
    return jnp.concatenate([a, b], axis=1)


def _rope_kernel(inv_ref, cos_ref, sin_ref):
    rows = cos_ref.shape[0]
    base = pl.program_id(0) * rows
    pos = (lax.broadcasted_iota(jnp.int32, cos_ref.shape, 0) + base).astype(F32)
    lane = lax.broadcasted_iota(jnp.int32, cos_ref.shape, 1)
    ang = pos * inv_ref[...]
    cos_ref[...] = jnp.cos(ang)
    sin_ref[...] = jnp.where((lane & 32) == 0, -jnp.sin(ang), jnp.sin(ang))


def rope_tables(seq):
    half = HEAD_DIM // 2
    inv = ROPE_THETA ** (-jnp.arange(half, dtype=F32) / half)
    inv = jnp.tile(inv, LANES // half)[None, :]
    rows = PROJ_ROWS
    return pl.pallas_call(
        _rope_kernel,
        out_shape=(jax.ShapeDtypeStruct((seq, LANES), F32),) * 2,
        grid=(seq // rows,),
        in_specs=[pl.BlockSpec((1, LANES), lambda i: (0, 0))],
        out_specs=(pl.BlockSpec((rows, LANES), lambda i: (i, 0)),) * 2,
        compiler_params=_cparams(1),
        name="rope_tables",
    )(inv)


def _inproj_kernel(x_ref, wa_ref, ws_ref, wdt_ref, wc_ref, cos_ref, sin_ref,
                   q1_ref, k1_ref, v1_ref, q4_ref, k4_ref, v4_ref, q16_ref, k16_ref, v16_ref,
                   z_ref, xbc_ref, dt_ref, sb_ref, u_ref, perm_ref):
    tm = x_ref.shape[0]
    xb = x_ref[...].astype(BF16)
    cos = cos_ref[...]
    sin = sin_ref[...]
    lane = lax.broadcasted_iota(jnp.int32, cos.shape, 1)
    first_half = (lane & 32) == 0
    n_slab = ATTN_WIDTH // LANES

    def proj(w_ref, lo, hi):
        return jnp.dot(xb, w_ref[:, lo:hi], preferred_element_type=F32)

    def rope(t):
        rot = jnp.where(first_half, pltpu.roll(t, LANES - 32, 1), pltpu.roll(t, 32, 1))
        return t * cos + rot * sin

    def emit(slabs, nat_ref, strided_refs):
        for j, t in enumerate(slabs):
            perm_ref[j] = t
            nat_ref[:, j * LANES:(j + 1) * LANES] = t.astype(BF16)
        for d, ref in strided_refs:
            for r in range(d):
                for j in range(n_slab):
                    ref[r, :, j * LANES:(j + 1) * LANES] = (
                        perm_ref[j, pl.ds(r, tm // d, stride=d), :].astype(BF16))

    w = ATTN_WIDTH
    q = proj(wa_ref, 0, w)
    emit([rope(q[:, j * LANES:(j + 1) * LANES]) * (HEAD_DIM ** -0.5) for j in range(n_slab)],
         q1_ref, ((4, q4_ref), (16, q16_ref)))
    k = proj(wa_ref, w, 2 * w)
    emit([rope(k[:, j * LANES:(j + 1) * LANES]) for j in range(n_slab)], k1_ref, ((4, k4_ref), (16, k16_ref)))
    v = proj(wa_ref, 2 * w, 3 * w)
    emit([v[:, j * LANES:(j + 1) * LANES] for j in range(n_slab)], v1_ref, ((4, v4_ref), (16, v16_ref)))
    z_ref[...] = proj(ws_ref, 0, SSD_WIDTH).astype(BF16)
    xbc_ref[...] = proj(ws_ref, SSD_WIDTH, SSD_WIDTH + SSD_CONV_DIM).astype(BF16)
    dt_ref[...] = jnp.dot(xb, wdt_ref[...], preferred_element_type=F32)
    sb_ref[...] = proj(wc_ref, 0, SC_WIDTH).astype(BF16)
    c = proj(wc_ref, SC_WIDTH, 2 * SC_WIDTH)
    h = proj(wc_ref, 2 * SC_WIDTH, 3 * SC_WIDTH)
    u_ref[...] = (c * h).astype(BF16)


def in_projection(x, w_attn, w_ssd, w_dt, w_sc, cos_t, sin_t, batch, seq):
    n = x.shape[0]
    tm = PROJ_ROWS
    tps = seq // tm
    row = lambda i: (i, 0)
    full = lambda i: (0, 0)
    tab = lambda i: (i % tps, 0)
    strided = lambda i: (i // tps, 0, i % tps, 0)
    w = ATTN_WIDTH
    nat = jax.ShapeDtypeStruct((n, w), BF16)
    nat_spec = pl.BlockSpec((tm, w), row)
    shapes, specs = [nat] * 3, [nat_spec] * 3
    for d in DILATIONS[1:]:
        shapes += [jax.ShapeDtypeStruct((batch, d, seq // d, w), BF16)] * 3
        specs += [pl.BlockSpec((None, d, tm // d, w), strided)] * 3
    widths = (SSD_WIDTH, SSD_CONV_DIM, LANES, SC_WIDTH, SC_WIDTH)
    dtypes = (BF16, BF16, F32, BF16, BF16)
    shapes += [jax.ShapeDtypeStruct((n, wd), dt) for wd, dt in zip(widths, dtypes)]
    specs += [pl.BlockSpec((tm, wd), row) for wd in widths]
    return pl.pallas_call(
        _inproj_kernel,
        out_shape=tuple(shapes),
        grid=(n // tm,),
        in_specs=[pl.BlockSpec((tm, D_MODEL), row),
                  pl.BlockSpec(w_attn.shape, full, pipeline_mode=pl.Buffered(1)),
                  pl.BlockSpec(w_ssd.shape, full, pipeline_mode=pl.Buffered(1)),
                  pl.BlockSpec(w_dt.shape, full, pipeline_mode=pl.Buffered(1)),
                  pl.BlockSpec(w_sc.shape, full, pipeline_mode=pl.Buffered(1)),
                  pl.BlockSpec((tm, LANES), tab),
                  pl.BlockSpec((tm, LANES), tab)],
        out_specs=tuple(specs),
        scratch_shapes=[pltpu.VMEM((w // LANES, tm, LANES), F32)],
        compiler_params=_cparams(1),
        name="in_projection",
    )(x, w_attn, w_ssd, w_dt, w_sc, cos_t, sin_t)


def _attn_kernel(q_ref, kc_ref, kp_ref, vc_ref, vp_ref, o_ref, lse_ref):
    blk = ATTN_BLOCK
    n = pl.program_id(2)
    lane = lax.broadcasted_iota(jnp.int32, (blk, LANES), 1)
    lo = lane < HEAD_DIM
    qi = lax.broadcasted_iota(jnp.int32, (2 * blk, 2 * blk), 0) & (blk - 1)
    kj = lax.broadcasted_iota(jnp.int32, (2 * blk, 2 * blk), 1)
    dist = qi + blk - kj
    in_band = jnp.where(dist >= 0, jnp.where(dist <= ATTN_SPAN, 1, 0), 0)
    bias = jnp.where(in_band > 0, 0.0, NEG)
    bias_first = jnp.where((in_band * jnp.where(kj >= blk, 1, jnp.where(n > 0, 1, 0))) > 0, 0.0, NEG)
    n_res, tq = q_ref.shape[0], q_ref.shape[1]
    for r, j in ((r, j) for r in range(n_res) for j in range(tq // blk)):
        rows = slice(j * blk, (j + 1) * blk)
        prev = slice((j - 1) * blk, j * blk)
        mask_bias = bias_first if j == 0 else bias
        lse_tile = jnp.zeros((blk, LANES), F32)
        for hp in range(ATTN_WIDTH // LANES):
            sl = slice(hp * LANES, (hp + 1) * LANES)
            q2 = q_ref[r, rows, sl].astype(F32)
            qq = jnp.concatenate([jnp.where(lo, q2, 0.0), jnp.where(lo, 0.0, q2)], axis=0).astype(BF16)
            k_prev = kp_ref[r, :, sl] if j == 0 else kc_ref[r, prev, sl]
            v_prev = vp_ref[r, :, sl] if j == 0 else vc_ref[r, prev, sl]
            k2 = jnp.concatenate([k_prev, kc_ref[r, rows, sl]], axis=0)
            v2 = jnp.concatenate([v_prev, vc_ref[r, rows, sl]], axis=0)
            s = lax.dot_general(qq, k2, (((1,), (1,)), ((), ())), preferred_element_type=F32) + mask_bias
            m = jnp.max(s, axis=1, keepdims=True)
            p = jnp.exp(s - m)
            l = jnp.sum(p, axis=1, keepdims=True)
            pv = jnp.dot(p.astype(BF16), v2, preferred_element_type=F32)
            o = pv / l
            o_ref[r, rows, sl] = jnp.where(lo, o[:blk], o[blk:]).astype(BF16)
            lse = m + jnp.log(l)
            lse_tile = jnp.where(lane == 2 * hp, lse[:blk],
                                 jnp.where(lane == 2 * hp + 1, lse[blk:], lse_tile))
        lse_ref[r, rows, :] = lse_tile


def dilated_attention(q, k, v, batch, seq, dilation):
    length = seq // dilation
    w = ATTN_WIDTH
    blk = ATTN_BLOCK
    tq = min(ATTN_Q_ROWS, length)
    n_res = ATTN_Q_ROWS // tq
    cur = lambda b, r, n: (b, r, n, 0)
    prev = lambda b, r, n: (b, r, jnp.maximum(n * (tq // blk) - 1, 0), 0)
    big = pl.BlockSpec((None, n_res, tq, w), cur)
    small = pl.BlockSpec((None, n_res, blk, w), prev)
    return pl.pallas_call(
        _attn_kernel,
        out_shape=(jax.ShapeDtypeStruct((batch, dilation, length, w), BF16),
                   jax.ShapeDtypeStruct((batch, dilation, length, LANES), F32)),
        grid=(batch, dilation // n_res, length // tq),
        in_specs=[big, big, small, big, small],
        out_specs=(big, pl.BlockSpec((None, n_res, tq, LANES), cur)),
        compiler_params=_cparams(3),
        name=f"dilated_attention_d{dilation}",
    )(q, k, k, v, v)


def _ssd_kernel(z_ref, xbc_ref, halo_ref, dt_ref, shift_ref, cw_ref, cb_ref, dtb_ref, alog_ref, dskip_ref,
                g_ref, y_ref, st_ref):
    ch = SSD_CHUNK
    c = pl.program_id(1)

    @pl.when(c == 0)
    def _():
        st_ref[...] = jnp.zeros_like(st_ref)

    first_halo = halo_ref[...]
    first_halo = jnp.where(c > 0, first_halo, jnp.zeros_like(first_halo))
    for j in range(xbc_ref.shape[0] // ch):
        rows = slice(j * ch, (j + 1) * ch)
        halo = first_halo if j == 0 else xbc_ref[j * ch - HALO:j * ch, :]
        y_ref[rows, :] = _ssd_chunk(xbc_ref[rows, :], halo, z_ref[rows, :], dt_ref[rows, :], shift_ref, cw_ref,
                                    cb_ref, dtb_ref, alog_ref, dskip_ref, g_ref, st_ref)


def _ssd_chunk(xbc_in, halo, z_in, dt_in, shift_ref, cw_ref, cb_ref, dtb_ref, alog_ref, dskip_ref, g_ref, st_ref):
    ch = SSD_CHUNK
    ext = jnp.concatenate([halo, xbc_in], axis=0)
    conv = cb_ref[...] + cw_ref[SSD_CONV - 1:SSD_CONV, :] * xbc_in.astype(F32)
    for t in range(SSD_CONV - 1):
        conv = conv + cw_ref[t:t + 1, :] * jnp.dot(shift_ref[t], ext, preferred_element_type=F32)
    xbc = conv * _sigmoid(conv)
    xh = xbc[:, :SSD_WIDTH]
    bm = xbc[:, SSD_WIDTH:SSD_WIDTH + SSD_GROUPS * SSD_STATE]
    cm = xbc[:, SSD_WIDTH + SSD_GROUPS * SSD_STATE:]

    lane = lax.broadcasted_iota(jnp.int32, (ch, LANES), 1)
    row = lax.broadcasted_iota(jnp.int32, (ch, LANES), 0)
    lo = lane < HEAD_DIM
    lo_row = lo[0:1, :]
    head_lane = lane < SSD_HEADS

    xdt = dt_in + dtb_ref[...]
    dtv = jnp.maximum(xdt, 0.0) + jnp.log(1.0 + jnp.exp(-jnp.abs(xdt)))
    a = jnp.where(head_lane[0:1, :], -jnp.exp(alog_ref[...]), 0.0)
    acum = dtv * a
    shift = 1
    while shift < ch:
        acum = acum + jnp.where(row >= shift, pltpu.roll(acum, shift, 0), 0.0)
        shift *= 2
    acum_t = acum.T
    tot = acum[ch - 1:ch, :]
    e_in = jnp.exp(acum)
    e_out = jnp.exp(tot - acum)
    e_tot = jnp.exp(tot)
    causal = row >= lane

    ys = []
    for g in range(SSD_GROUPS):
        bg = bm[:, g * SSD_STATE:(g + 1) * SSD_STATE]
        cg = cm[:, g * SSD_STATE:(g + 1) * SSD_STATE].astype(BF16)
        cb = lax.dot_general(cg, bg.astype(BF16), (((1,), (1,)), ((), ())),
                             preferred_element_type=F32)
        bg_t = bg.T.astype(BF16)
        for pp in range(2):
            pair = 2 * g + pp
            ha = 2 * pair
            sl = slice(pair * LANES, (pair + 1) * LANES)
            xp = xh[:, sl]
            xdt_pair = xp * _pair_cols(lo, dtv, ha)
            xb16 = xdt_pair.astype(BF16)
            diag = []
            for h in (ha, ha + 1):
                seg = acum[:, h:h + 1] - acum_t[h:h + 1, :]
                decay = jnp.exp(jnp.where(causal, seg, NEG))
                diag.append(jnp.dot((cb * decay).astype(BF16), xb16, preferred_element_type=F32))
            y = jnp.where(lo, diag[0], diag[1])
            state = st_ref[pair]
            y = y + jnp.dot(cg, state.astype(BF16), preferred_element_type=F32) * _pair_cols(lo, e_in, ha)
            xout = (xdt_pair * _pair_cols(lo, e_out, ha)).astype(BF16)
            st_ref[pair] = (state * _pair_cols(lo_row, e_tot, ha)
                            + jnp.dot(bg_t, xout, preferred_element_type=F32))
            ys.append(y + dskip_ref[:, sl] * xp)
    y = jnp.concatenate(ys, axis=1)
    zz = z_in.astype(F32)
    y = y * (zz * _sigmoid(zz))
    gw = SSD_WIDTH // SSD_GROUPS
    outs = []
    for g in range(SSD_GROUPS):
        yg = y[:, g * gw:(g + 1) * gw]
        ms = jnp.mean(yg * yg, axis=1, keepdims=True)
        outs.append(yg * lax.rsqrt(ms + RMS_EPS))
    return (jnp.concatenate(outs, axis=1) * g_ref[...]).astype(BF16)


def ssd_mixer(z, xbc, dt_raw, conv_w, conv_b, dt_bias, a_log, d_skip, norm_g, batch, seq):
    n = batch * seq
    ch = SSD_CHUNK
    step = SSD_STEP_ROWS
    nc = seq // step
    pad = lambda t: jnp.pad(t.astype(F32), (0, LANES - t.shape[0]))[None, :]
    cur = lambda b, c: (b * nc + c, 0)
    halo = lambda b, c: (jnp.maximum((b * nc + c) * (step // HALO) - 1, 0), 0)
    full = lambda b, c: (0, 0)
    r_idx = lax.broadcasted_iota(jnp.int32, (SSD_CONV - 1, ch, HALO + ch), 1)
    j_idx = lax.broadcasted_iota(jnp.int32, (SSD_CONV - 1, ch, HALO + ch), 2)
    t_idx = lax.broadcasted_iota(jnp.int32, (SSD_CONV - 1, ch, HALO + ch), 0)
    shift = jnp.where(j_idx == r_idx + HALO - (SSD_CONV - 1) + t_idx, 1.0, 0.0).astype(BF16)
    return pl.pallas_call(
        _ssd_kernel,
        out_shape=jax.ShapeDtypeStruct((n, SSD_WIDTH), BF16),
        grid=(batch, nc),
        in_specs=[pl.BlockSpec((step, SSD_WIDTH), cur),
                  pl.BlockSpec((step, SSD_CONV_DIM), cur),
                  pl.BlockSpec((HALO, SSD_CONV_DIM), halo),
                  pl.BlockSpec((step, LANES), cur),
                  pl.BlockSpec(shift.shape, lambda b, c: (0, 0, 0)),
                  pl.BlockSpec((SSD_CONV, SSD_CONV_DIM), full),
                  pl.BlockSpec((1, SSD_CONV_DIM), full),
                  pl.BlockSpec((1, LANES), full),
                  pl.BlockSpec((1, LANES), full),
                  pl.BlockSpec((1, SSD_WIDTH), full),
                  pl.BlockSpec((1, SSD_WIDTH), full)],
        out_specs=pl.BlockSpec((step, SSD_WIDTH), cur),
        scratch_shapes=[pltpu.VMEM((SSD_HEADS // 2, SSD_STATE, LANES), F32)],
        compiler_params=_cparams(2),
        name="ssd_mixer",
    )(z, xbc, xbc, dt_raw, shift, conv_w.astype(F32), conv_b.astype(F32)[None, :], pad(dt_bias), pad(a_log),
      jnp.repeat(d_skip.astype(F32), HEAD_DIM)[None, :], norm_g.astype(F32)[None, :])


def _layer_norm(r, g, b):
    mu = jnp.mean(r, axis=1, keepdims=True)
    d = r - mu
    var = jnp.mean(d * d, axis=1, keepdims=True)
    return d * lax.rsqrt(var + LN_EPS) * g + b


def _outproj_kernel(tiles_per_seq, o1_ref, o4_ref, o16_ref, l1_ref, l4_ref, l16_ref, ssd_ref, sb_ref,
                    u_ref, uh_ref, x_ref, cw_ref, wout_ref, g_ref, b_ref, spread_ref, rw_ref,
                    rb_ref, x1_ref, logit_ref, ext_ref, operm_ref, lperm_ref):
    tm = x_ref.shape[0]
    i = pl.program_id(0)
    n_slab = ATTN_WIDTH // LANES

    for idx, (d, o_ref, l_ref) in enumerate(((4, o4_ref, l4_ref), (16, o16_ref, l16_ref))):
        for r in range(d):
            rows = pl.ds(r, tm // d, stride=d)
            lperm_ref[idx, rows, :] = l_ref[r]
            for j in range(n_slab):
                operm_ref[idx * n_slab + j, rows, :] = o_ref[r, :, j * LANES:(j + 1) * LANES].astype(F32)

    lses = (l1_ref[...], lperm_ref[0], lperm_ref[1])
    top = jnp.maximum(jnp.maximum(lses[0], lses[1]), lses[2])
    es = [jnp.exp(l - top) for l in lses]
    den = es[0] + es[1] + es[2]
    wide = []
    for e in es:
        w = e / den
        w_hi = w.astype(BF16)
        w_lo = (w - w_hi.astype(F32)).astype(BF16)
        wide.append(jnp.dot(jnp.concatenate([w_hi, w_lo], axis=1), spread_ref[...], preferred_element_type=F32))
    attn = []
    for hp in range(n_slab):
        sl = slice(hp * LANES, (hp + 1) * LANES)
        acc = wide[0][:, sl] * o1_ref[:, sl].astype(F32)
        acc = acc + wide[1][:, sl] * operm_ref[hp]
        acc = acc + wide[2][:, sl] * operm_ref[n_slab + hp]
        attn.append(acc.astype(BF16))

    seq_start = (i % tiles_per_seq) == 0
    ext_ref[0:HALO, :] = jnp.where(seq_start, 0.0, uh_ref[...].astype(F32))
    ext_ref[HALO:HALO + tm, :] = u_ref[...].astype(F32)
    conv = jnp.zeros((tm, SC_WIDTH), F32)
    for t in range(SC_CONV):
        off = HALO - (SC_CONV - 1) + t
        conv = conv + cw_ref[t:t + 1, :] * ext_ref[off:off + tm, :]
    gated = (sb_ref[...].astype(F32) * conv).astype(BF16)

    mixed = jnp.concatenate(attn + [ssd_ref[...], gated], axis=1)
    mix = jnp.dot(mixed, wout_ref[...], preferred_element_type=F32)
    x1 = _layer_norm(ALPHA * x_ref[...] + mix, g_ref[...], b_ref[...])
    x1_ref[...] = x1

    xh = x1.astype(BF16)
    xm = (x1 - xh.astype(F32)).astype(BF16)
    ph = jnp.dot(xh, rw_ref[...], preferred_element_type=F32)
    pm = jnp.dot(xm, rw_ref[...], preferred_element_type=F32)
    logit_ref[...] = pm + pltpu.roll(ph, LANES - N_EXPERTS, 1) + ph + rb_ref[...]


def out_projection(o_list, lse_list, ssd, sb, u, x, sc_conv_w, w_out, ln_g, ln_b, router_w, router_b, seq):
    n = x.shape[0]
    tm = OUT_PROJ_ROWS
    tps = seq // tm
    row = lambda i: (i, 0)
    full = lambda i: (0, 0)
    strided = lambda i: (i // tps, 0, i % tps, 0)
    halo = lambda i: (jnp.maximum(i * (tm // HALO) - 1, 0), 0)
    rw_f = router_w.astype(F32)
    rw_hi = rw_f.astype(BF16)
    rw_mid = (rw_f - rw_hi.astype(F32)).astype(BF16)
    rw = jnp.pad(jnp.concatenate([rw_hi, rw_mid], axis=1), ((0, 0), (0, LANES - 2 * N_EXPERTS)))
    rb = jnp.pad(router_b.astype(F32), (0, LANES - N_EXPERTS))[None, :]
    spread = jnp.where(lax.broadcasted_iota(jnp.int32, (2 * LANES, ATTN_WIDTH), 0) % LANES
                       == lax.broadcasted_iota(jnp.int32, (2 * LANES, ATTN_WIDTH), 1) // HEAD_DIM,
                       1.0, 0.0).astype(BF16)
    wide = lambda wd: pl.BlockSpec((tm, wd), row)
    perm = lambda d, wd: pl.BlockSpec((None, d, tm // d, wd), strided)
    const = lambda a: pl.BlockSpec(a.shape, full)
    wout = w_out.astype(BF16)
    cw = sc_conv_w.astype(F32)
    g = ln_g.astype(F32)[None, :]
    b = ln_b.astype(F32)[None, :]
    n_slab = ATTN_WIDTH // LANES
    return pl.pallas_call(
        functools.partial(_outproj_kernel, tps),
        out_shape=(jax.ShapeDtypeStruct((n, D_MODEL), F32), jax.ShapeDtypeStruct((n, LANES), F32)),
        grid=(n // tm,),
        in_specs=[wide(ATTN_WIDTH), perm(4, ATTN_WIDTH), perm(16, ATTN_WIDTH),
                  wide(LANES), perm(4, LANES), perm(16, LANES),
                  wide(SSD_WIDTH), wide(SC_WIDTH), wide(SC_WIDTH), pl.BlockSpec((HALO, SC_WIDTH), halo),
                  wide(D_MODEL), const(cw), const(wout), const(g), const(b),
                  const(spread), const(rw), const(rb)],
        out_specs=(wide(D_MODEL), wide(LANES)),
        scratch_shapes=[pltpu.VMEM((HALO + tm, SC_WIDTH), F32),
                        pltpu.VMEM((2 * n_slab, tm, LANES), F32),
                        pltpu.VMEM((2, tm, LANES), F32)],
        compiler_params=_cparams(1),
        name="out_projection",
    )(*o_list, *lse_list, ssd, sb, u, u, x, cw, wout, g, b, spread, rw, rb)


INFO_IDX, INFO_SLOT, INFO_GATE = 0, TOP_K, 2 * TOP_K
TAB_RUN, TAB_BASE, TAB_CHUNKS = 0, 1, 2


def _route_kernel(logit_ref, upper_ref, info_ref, tab_ref, cnt_ref, run_ref):
    @pl.when(pl.program_id(0) == 0)
    def _():
        run_ref[...] = jnp.zeros_like(run_ref)

    tm = ROUTE_ROWS
    for j in range(logit_ref.shape[0] // tm):
        info, tab = _route_tile(logit_ref[j * tm:(j + 1) * tm, :], upper_ref, run_ref)
        info_ref[j * tm:(j + 1) * tm, :] = info
        tab_ref[j * SUBLANES:(j + 1) * SUBLANES, :] = tab
    cnt_ref[...] = jnp.broadcast_to(run_ref[...], cnt_ref.shape)


def _route_tile(logits, upper_ref, run_ref):
    tm = logits.shape[0]
    ne = N_EXPERTS
    rec_rows = 2 * SUBLANES
    work = logits.T[:ne, :]
    eidx = lax.broadcasted_iota(jnp.int32, (ne, tm), 0).astype(F32)
    sel = jnp.zeros((ne, tm), F32)
    vals, idxs, hits = [], [], []
    for _ in range(TOP_K):
        m = jnp.max(work, axis=0, keepdims=True)
        idx = jnp.min(jnp.where(work == m, eidx, float(ne)), axis=0, keepdims=True)
        hit = eidx == idx
        work = jnp.where(hit, NEG, work)
        sel = sel + jnp.where(hit, 1.0, 0.0)
        vals.append(m)
        idxs.append(idx)
        hits.append(hit)
    es = [jnp.exp(v - vals[0]) for v in vals]
    den = es[0] + es[1] + es[2] + es[3]
    before = jnp.dot(sel.astype(BF16), upper_ref[...], preferred_element_type=F32)
    cnt = jnp.sum(sel, axis=1, keepdims=True)
    chunks = jnp.floor((cnt + (CHUNK - 1)) * (1.0 / CHUNK))
    row = lax.broadcasted_iota(jnp.int32, (ne, LANES), 0)
    chunks_w = jnp.broadcast_to(chunks, (ne, LANES))
    incl = chunks_w
    shift = 1
    while shift < ne:
        incl = incl + jnp.where(row >= shift, pltpu.roll(incl, shift, 0), 0.0)
        shift *= 2
    base = (incl - chunks_w)[:, 0:1]

    sub = lax.broadcasted_iota(jnp.int32, (rec_rows, tm), 0)
    rec = jnp.zeros((rec_rows, tm), F32)
    for k in range(TOP_K):
        rank = jnp.sum(jnp.where(hits[k], before, 0.0), axis=0, keepdims=True)
        first = jnp.sum(jnp.where(hits[k], base, 0.0), axis=0, keepdims=True)
        rec = jnp.where(sub == INFO_IDX + k, idxs[k], rec)
        rec = jnp.where(sub == INFO_SLOT + k, first * CHUNK + rank, rec)
        rec = jnp.where(sub == INFO_GATE + k, es[k] / den, rec)
    info = jnp.concatenate([rec, jnp.zeros((LANES - rec_rows, tm), F32)], axis=0).T

    lane = lax.broadcasted_iota(jnp.int32, (ne, LANES), 1)
    cnt8 = jnp.floor((cnt + (SUBLANES - 1)) * (1.0 / SUBLANES)) * SUBLANES
    cols = jnp.where(lane == 0, cnt8, jnp.where(lane == 1, base, jnp.where(lane == 2, chunks, 0.0)))
    rows = jnp.concatenate([cols, jnp.zeros((LANES - ne, LANES), F32)], axis=0).T
    sub8 = lax.broadcasted_iota(jnp.int32, (SUBLANES, LANES), 0)
    tab = jnp.where(sub8 == TAB_RUN, run_ref[...],
                    jnp.where(sub8 == TAB_BASE, rows[1:2, :], jnp.where(sub8 == TAB_CHUNKS, rows[2:3, :], 0.0)))
    run_ref[...] = run_ref[...] + rows[0:1, :]
    return info, tab


def route(logits):
    n = logits.shape[0]
    tm = ROUTE_ROWS
    r = lax.broadcasted_iota(jnp.int32, (tm, tm), 0)
    c = lax.broadcasted_iota(jnp.int32, (tm, tm), 1)
    upper = jnp.where(r < c, 1.0, 0.0).astype(BF16)
    return pl.pallas_call(
        _route_kernel,
        out_shape=(jax.ShapeDtypeStruct((n, LANES), F32),
                   jax.ShapeDtypeStruct((n // tm * SUBLANES, LANES), F32),
                   jax.ShapeDtypeStruct((SUBLANES, LANES), F32)),
        grid=(n // (tm * ROUTE_STEP_TILES),),
        in_specs=[pl.BlockSpec((tm * ROUTE_STEP_TILES, LANES), lambda i: (i, 0)),
                  pl.BlockSpec((tm, tm), lambda i: (0, 0))],
        out_specs=(pl.BlockSpec((tm * ROUTE_STEP_TILES, LANES), lambda i: (i, 0)),
                   pl.BlockSpec((SUBLANES * ROUTE_STEP_TILES, LANES), lambda i: (i, 0)),
                   pl.BlockSpec((SUBLANES, LANES), lambda i: (0, 0))),
        scratch_shapes=[pltpu.VMEM((1, LANES), F32)],
        compiler_params=_cparams(1),
        name="moe_route",
    )(logits, upper)


def _start_all(n_copies, make_copy):
    def starter(priority):
        def start(h, carry):
            make_copy(2 * h + priority).start(priority=priority)
            return carry
        return start

    lax.fori_loop(0, (n_copies + 1) // 2, starter(0), 0)
    lax.fori_loop(0, n_copies // 2, starter(1), 0)


def _wait_all(n_chunks, make_wait):
    groups = n_chunks // WAIT_GROUP

    def wait_group(g, carry):
        make_wait(WAIT_GROUP).wait()
        return carry

    def wait(s, carry):
        make_wait(1).wait()
        return carry

    lax.fori_loop(0, groups, wait_group, 0)
    lax.fori_loop(groups * WAIT_GROUP, n_chunks, wait, 0)


def _dispatch_kernel(pslot_ref, pdst_ref, npair_ref, sslot_ref, sdst_ref, nsingle_ref, nslot_ref, zdst_ref,
                     x_ref, info_ref, buf_ref, slots_ref, zero_ref, sems, zsem):
    i = pl.program_id(0)
    tm = x_ref.shape[0]
    cur = i % 2

    def put(tile, side, slot_tab, dst_tab, width, chunks):
        return lambda s: pltpu.make_async_copy(
            slots_ref.at[side, pl.ds(pl.multiple_of(slot_tab[tile * width + s] * CHUNK, CHUNK), chunks * CHUNK)],
            buf_ref.at[pl.ds(pl.multiple_of(dst_tab[tile * width + s], SUBLANES), chunks * CHUNK)],
            sems.at[side])

    def start_tile(tile, side):
        _start_all(npair_ref[tile], put(tile, side, pslot_ref, pdst_ref, MAX_PAIRS, 2))
        _start_all(nsingle_ref[tile], put(tile, side, sslot_ref, sdst_ref, N_EXPERTS, 1))

    def landed(side):
        return lambda k: pltpu.make_async_copy(slots_ref.at[side, pl.ds(0, k * CHUNK)],
                                               buf_ref.at[pl.ds(0, k * CHUNK)], sems.at[side])

    @pl.when(i == 0)
    def _():
        zero_ref[...] = jnp.zeros_like(zero_ref)

        def zero_copy(e):
            row = pl.multiple_of(jnp.maximum(zdst_ref[e], 0), MOE_BLOCK)
            return pltpu.make_async_copy(zero_ref, buf_ref.at[pl.ds(row, MOE_BLOCK)], zsem)

        def start(e, carry):
            @pl.when(zdst_ref[e] >= 0)
            def _():
                zero_copy(e).start()
            return carry

        def wait(e, carry):
            @pl.when(zdst_ref[e] >= 0)
            def _():
                zero_copy(e).wait()
            return carry

        lax.fori_loop(0, zdst_ref.shape[0], start, 0)
        lax.fori_loop(0, zdst_ref.shape[0], wait, 0)

    xb = x_ref[...].astype(BF16)
    slot_t = info_ref[...].T[INFO_SLOT:INFO_SLOT + TOP_K, :]
    local = lax.broadcasted_iota(jnp.int32, (tm, tm), 0).astype(F32).astype(BF16)
    one = jnp.ones((tm, tm), BF16)
    zero = jnp.zeros((tm, tm), BF16)
    for c in range(SLOT_ROWS // tm):
        target = (slot_t - float(c * tm)).astype(BF16)
        onehot = zero
        for k in range(TOP_K):
            onehot = onehot + jnp.where(local == target[k:k + 1, :], one, zero)
        rows = jnp.dot(onehot, xb, preferred_element_type=F32)
        slots_ref[cur, c * tm:(c + 1) * tm, :] = _pack_rows(rows)

    @pl.when(i > 0)
    def _():
        _wait_all(nslot_ref[i - 1], landed(1 - cur))

    start_tile(i, cur)

    @pl.when(i == pl.num_programs(0) - 1)
    def _():
        _wait_all(nslot_ref[i], landed(cur))


def dispatch(x1, info, copies, zdst, padded_rows):
    n = x1.shape[0]
    tm = ROUTE_ROWS
    row = lambda i, *_: (i, 0)
    return pl.pallas_call(
        _dispatch_kernel,
        out_shape=jax.ShapeDtypeStruct((padded_rows, PACKED), U32),
        grid_spec=pltpu.PrefetchScalarGridSpec(
            num_scalar_prefetch=len(copies) + 1,
            grid=(n // tm,),
            in_specs=[pl.BlockSpec((tm, D_MODEL), row), pl.BlockSpec((tm, LANES), row)],
            out_specs=pl.BlockSpec(memory_space=pl.ANY),
            scratch_shapes=[pltpu.VMEM((2, SLOT_ROWS, PACKED), U32), pltpu.VMEM((MOE_BLOCK, PACKED), U32),
                            pltpu.SemaphoreType.DMA((2,)), pltpu.SemaphoreType.DMA(())]),
        compiler_params=_cparams(1),
        name="moe_dispatch",
    )(*copies, zdst, x1, info)


def _expert_kernel(blk_expert_ref, n_used_ref, blk_rows_ref, xs_ref, wgu_ref, bgu_ref, wd_ref, bd_ref, y_ref):
    del blk_expert_ref
    i = pl.program_id(0)
    used = i < n_used_ref[0]
    half = MOE_BLOCK // 2
    full = blk_rows_ref[i] > half

    def mlp(rows):
        xb = _unpack_rows(xs_ref[rows, :])
        gu = jnp.dot(xb, wgu_ref[...].astype(BF16), preferred_element_type=F32) + bgu_ref[...]
        gate = jnp.minimum(gu[:, :EXPERT_FF], SWIGLU_LIMIT)
        up = jnp.clip(gu[:, EXPERT_FF:], -SWIGLU_LIMIT, SWIGLU_LIMIT)
        h = (up + 1.0) * gate * _sigmoid(SWIGLU_ALPHA * gate)
        y = jnp.dot(h.astype(BF16), wd_ref[...].astype(BF16), preferred_element_type=F32) + bd_ref[...]
        y_ref[rows, :] = _pack_rows(y.astype(BF16).astype(F32))

    @pl.when(jnp.logical_not(used))
    def _():
        y_ref[...] = jnp.zeros_like(y_ref)

    @pl.when(jnp.logical_and(used, full))
    def _():
        mlp(slice(0, MOE_BLOCK))

    @pl.when(jnp.logical_and(used, jnp.logical_not(full)))
    def _():
        mlp(slice(0, half))
        y_ref[half:, :] = jnp.zeros((half, PACKED), U32)


def expert_mlp(xs, blk_expert, n_used, blk_rows, layer, w_gu, b_gu, w_down, b_down):
    padded_rows = xs.shape[0]
    blk = MOE_BLOCK
    rows_in = lambda i, be, nu, br: (jnp.minimum(i, nu[0] - 1), 0)
    per_expert = lambda i, be, nu, br: (layer, be[i], 0, 0)
    return pl.pallas_call(
        _expert_kernel,
        out_shape=jax.ShapeDtypeStruct((padded_rows, PACKED), U32),
        grid_spec=pltpu.PrefetchScalarGridSpec(
            num_scalar_prefetch=3,
            grid=(padded_rows // blk,),
            in_specs=[pl.BlockSpec((blk, PACKED), rows_in),
                      pl.BlockSpec((None, None, D_MODEL, 2 * EXPERT_FF), per_expert),
                      pl.BlockSpec((None, None, 1, 2 * EXPERT_FF), per_expert),
                      pl.BlockSpec((None, None, EXPERT_FF, D_MODEL), per_expert),
                      pl.BlockSpec((None, None, 1, D_MODEL), per_expert)],
            out_specs=pl.BlockSpec((blk, PACKED), lambda i, be, nu, br: (i, 0))),
        compiler_params=_cparams(1),
        name="expert_mlp",
    )(blk_expert, n_used, blk_rows, xs, w_gu, b_gu, w_down, b_down)


def _combine_kernel(pslot_ref, pdst_ref, npair_ref, sslot_ref, sdst_ref, nsingle_ref, nslot_ref,
                    x_ref, info_ref, g_ref, b_ref, y_ref, out_ref, slots_ref, sems):
    i = pl.program_id(0)
    tm = x_ref.shape[0]
    cur = i % 2

    def fetch(tile, side, slot_tab, dst_tab, width, chunks):
        return lambda s: pltpu.make_async_copy(
            y_ref.at[pl.ds(pl.multiple_of(dst_tab[tile * width + s], SUBLANES), chunks * CHUNK)],
            slots_ref.at[side, pl.ds(pl.multiple_of(slot_tab[tile * width + s] * CHUNK, CHUNK), chunks * CHUNK)],
            sems.at[side])

    def start_tile(tile, side):
        _start_all(npair_ref[tile], fetch(tile, side, pslot_ref, pdst_ref, MAX_PAIRS, 2))
        _start_all(nsingle_ref[tile], fetch(tile, side, sslot_ref, sdst_ref, N_EXPERTS, 1))

    @pl.when(i == 0)
    def _():
        slots_ref[...] = jnp.zeros_like(slots_ref)
        start_tile(0, 0)

    @pl.when(i + 1 < pl.num_programs(0))
    def _():
        start_tile(i + 1, 1 - cur)

    _wait_all(nslot_ref[i], lambda k: pltpu.make_async_copy(
        y_ref.at[pl.ds(0, k * CHUNK)], slots_ref.at[cur, pl.ds(0, k * CHUNK)], sems.at[cur]))

    info = info_ref[...]
    ffn = jnp.zeros((tm, D_MODEL), F32)
    local = lax.broadcasted_iota(jnp.int32, (tm, tm), 1).astype(F32).astype(BF16)
    zero = jnp.zeros((tm, tm), BF16)
    gates = [info[:, INFO_GATE + k:INFO_GATE + k + 1].astype(BF16) for k in range(TOP_K)]
    for c in range(SLOT_ROWS // tm):
        w = zero
        for k in range(TOP_K):
            target = (info[:, INFO_SLOT + k:INFO_SLOT + k + 1] - float(c * tm)).astype(BF16)
            w = w + jnp.where(local == target, gates[k], zero)
        rows = _unpack_rows(slots_ref[cur, c * tm:(c + 1) * tm, :])
        ffn = ffn + jnp.dot(w, rows, preferred_element_type=F32)
    out_ref[...] = _layer_norm(ALPHA * x_ref[...] + ffn, g_ref[...], b_ref[...])


def combine(x1, info, copies, y, ln_g, ln_b):
    n = x1.shape[0]
    tm = ROUTE_ROWS
    row = lambda i, *_: (i, 0)
    full = lambda i, *_: (0, 0)
    return pl.pallas_call(
        _combine_kernel,
        out_shape=jax.ShapeDtypeStruct((n, D_MODEL), F32),
        grid_spec=pltpu.PrefetchScalarGridSpec(
            num_scalar_prefetch=len(copies),
            grid=(n // tm,),
            in_specs=[pl.BlockSpec((tm, D_MODEL), row),
                      pl.BlockSpec((tm, LANES), row),
                      pl.BlockSpec((1, D_MODEL), full),
                      pl.BlockSpec((1, D_MODEL), full),
                      pl.BlockSpec(memory_space=pl.ANY)],
            out_specs=pl.BlockSpec((tm, D_MODEL), row),
            scratch_shapes=[pltpu.VMEM((2, SLOT_ROWS, PACKED), U32), pltpu.SemaphoreType.DMA((2,))]),
        compiler_params=_cparams(1),
        name="moe_combine",
    )(*copies, x1, info, ln_g.astype(F32)[None, :], ln_b.astype(F32)[None, :], y)


def moe_sublayer(x1, logits, layer, w_gu, b_gu, w_down, b_down, ln_g, ln_b):
    n = x1.shape[0]
    blk = MOE_BLOCK
    n_tiles = n // ROUTE_ROWS
    padded_rows = n * TOP_K + (SUBLANES - 1) * N_EXPERTS * n_tiles + N_EXPERTS * (blk + CHUNK)
    padded_rows = -(-padded_rows // blk) * blk
    info, tab, cnt = route(logits)

    i32 = lambda t: t.astype(jnp.int32)
    tab = tab.reshape(n_tiles, SUBLANES, LANES)[:, :, :N_EXPERTS]
    run, base, chunks = i32(tab[:, TAB_RUN]), i32(tab[:, TAB_BASE]), i32(tab[:, TAB_CHUNKS])
    counts = i32(cnt[0, :N_EXPERTS])
    padded = (counts + CHUNK + blk - 1) // blk * blk
    seg_end = jnp.cumsum(padded)
    seg_start = seg_end - padded
    row0 = seg_start[None, :] + run

    def copy_list(per_expert, width, first_chunk):
        ends = jnp.cumsum(per_expert, axis=1)
        item = jnp.arange(width, dtype=jnp.int32)
        owner = jnp.minimum(jnp.sum(i32(ends[:, None, :] <= item[None, :, None]), -1), N_EXPERTS - 1)
        onehot = i32(owner[:, :, None] == jnp.arange(N_EXPERTS, dtype=jnp.int32))
        pick = lambda v: jnp.sum(onehot * v[:, None, :], -1)
        chunk = first_chunk(item[None, :] - pick(ends - per_expert), pick)
        return (pick(base) + chunk).reshape(-1), (pick(row0) + chunk * CHUNK).reshape(-1), ends[:, -1]

    pairs = copy_list(chunks // 2, MAX_PAIRS, lambda q, pick: 2 * q)
    singles = copy_list(chunks % 2, N_EXPERTS, lambda q, pick: pick(chunks) - 1)
    copies = (*pairs, *singles, jnp.sum(chunks, axis=1))
    total = seg_end[N_EXPERTS - 1]
    n_blocks = padded_rows // blk
    blk_row = jnp.arange(n_blocks, dtype=jnp.int32) * blk
    blk_expert = jnp.minimum(jnp.sum(i32(seg_end[None, :] <= blk_row[:, None]), -1), N_EXPERTS - 1)
    onehot_blk = i32(blk_expert[:, None] == jnp.arange(N_EXPERTS, dtype=jnp.int32))
    n_used = (total // blk).reshape(1)
    tail0 = (seg_start + counts) // blk * blk
    tail1 = jnp.where(tail0 + blk < seg_end, tail0 + blk, -1)
    zdst = jnp.concatenate([tail0, tail1, jnp.where(blk_row >= total, blk_row, -1)])

    xs = dispatch(x1, info, copies, zdst, padded_rows)
    seg_rows_end = jnp.sum(onehot_blk * (seg_start + counts)[None, :], -1)
    blk_rows = jnp.clip(seg_rows_end - blk_row, 0, blk)
    y = expert_mlp(xs, blk_expert, n_used, blk_rows, layer, w_gu.astype(F32), b_gu.astype(F32)[:, :, None, :],
                   w_down.astype(F32), b_down.astype(F32)[:, :, None, :])
    return combine(x1, info, copies, y, ln_g, ln_b)


def kernel(x, w_in, ssd_conv_w, ssd_conv_b, ssd_dt_bias, ssd_a_log, ssd_d, ssd_norm_g, sc_conv_w, w_out,
           ln1_g, ln1_b, router_w, router_b, exp_w_gu, exp_b_gu, exp_w_down, exp_b_down, ln2_g, ln2_b):
    batch, seq, _ = x.shape
    n = batch * seq
    cos_t, sin_t = rope_tables(seq)
    xf = x.reshape(n, D_MODEL).astype(F32)
    a_end = 3 * ATTN_WIDTH
    s_end = a_end + SSD_WIDTH + SSD_CONV_DIM
    d_end = s_end + SSD_HEADS
    for layer in range(w_in.shape[0]):
        w = w_in[layer]
        w_attn = w[:, :a_end].astype(BF16)
        w_ssd = w[:, a_end:s_end].astype(BF16)
        w_dt = jnp.pad(w[:, s_end:d_end], ((0, 0), (0, LANES - SSD_HEADS))).astype(BF16)
        w_sc = w[:, d_end:].astype(BF16)
        (q1, k1, v1, q4, k4, v4, q16, k16, v16, z, xbc, dt_raw, sb, u) = in_projection(
            xf, w_attn, w_ssd, w_dt, w_sc, cos_t, sin_t, batch, seq)
        qkv = ((q1.reshape(batch, 1, seq, ATTN_WIDTH), k1.reshape(batch, 1, seq, ATTN_WIDTH),
                v1.reshape(batch, 1, seq, ATTN_WIDTH)), (q4, k4, v4), (q16, k16, v16))
        attn = [dilated_attention(*t, batch, seq, d) for t, d in zip(qkv, DILATIONS)]
        o_list = [attn[0][0].reshape(n, ATTN_WIDTH), attn[1][0], attn[2][0]]
        lse_list = [attn[0][1].reshape(n, LANES), attn[1][1], attn[2][1]]
        ssd = ssd_mixer(z, xbc, dt_raw, ssd_conv_w[layer], ssd_conv_b[layer], ssd_dt_bias[layer],
                        ssd_a_log[layer], ssd_d[layer], ssd_norm_g[layer], batch, seq)
        x1, logits = out_projection(o_list, lse_list, ssd, sb, u, xf,
                                    sc_conv_w[layer], w_out[layer], ln1_g[layer], ln1_b[layer],
                                    router_w[layer], router_b[layer], seq)
        xf = moe_sublayer(x1, logits, layer, exp_w_gu, exp_b_gu, exp_w_down, exp_b_down,
                          ln2_g[layer], ln2_b[layer])
    return xf.reshape(batch, seq, D_MODEL).astype(x.dtype)
```
